```python
import math
import jax
import jax.numpy as jnp
from jax import lax
import numpy as np

D_MODEL = 2048
BATCH = 2
SEQ = 4096
DEPTH = 1
DEC_BATCH = 8
DEC_SEQ = 1
PAST_LEN = 16384
PAGE_SIZE = 128

N_HEADS = 16
HEAD_DIM = 64
N_KV_HEADS = 2
HEADS_PER_KV = N_HEADS // N_KV_HEADS
ATTN_WIDTH = N_HEADS * HEAD_DIM
CMP_BLOCK = 32
CMP_STRIDE = 16
CMP_HID = 2 * HEAD_DIM
SEL_BLOCK = 64
SEL_TOPK = 16
WINDOW = 512
Q_BLOCK = 128
GM_WIDTH = D_MODEL // 2
GM_GROUPS = 8
GM_GROUP_DIM = GM_WIDTH // GM_GROUPS
CHUNK = 128
N_BUCKETS = 32
REL_MAX_DIST = 128
N_EXPERTS = 32
TOP_K = 4
D_FF = D_MODEL
SWIGLU_LIMIT = 7.0
SWIGLU_ALPHA = 1.702
RMS_EPS = 1e-5
LN_EPS = 1e-5
NEG_INF = -1e30
KV_COLS = 2 * N_KV_HEADS * HEAD_DIM
GATE_COLS = 3 * N_HEADS
IN_WIDTH = ATTN_WIDTH + 3 * KV_COLS + GATE_COLS + 2 * GM_WIDTH + 2 * D_MODEL

kernel_name = 'nsa_gmlp_gated_moe_step'


def rmsnorm(x, g):
    xf = x.astype(jnp.float32)
    return (xf * lax.rsqrt(jnp.mean(xf * xf, axis=-1, keepdims=True) + RMS_EPS)).astype(x.dtype) * g


def t5_bucket(d):
    d = jnp.maximum(d, 0)
    exact = N_BUCKETS // 2
    far = exact + (jnp.log(jnp.maximum(d, exact).astype(jnp.float32) / exact)
                   / math.log(REL_MAX_DIST / exact) * (N_BUCKETS - exact)).astype(jnp.int32)
    return jnp.where(d < exact, d, jnp.minimum(far, N_BUCKETS - 1))


def head_bias(rel_bias, d):
    b = rel_bias.astype(jnp.float32)[t5_bucket(d)]
    return jnp.moveaxis(b, -1, 0).reshape(N_KV_HEADS, HEADS_PER_KV, *d.shape)


def masked_softmax(logits, mask):
    return jax.nn.softmax(jnp.where(mask, logits, NEG_INF), axis=-1)


def gather_pages(pool, page_table):
    rows = pool[page_table]
    return rows.reshape(page_table.shape[0], page_table.shape[1] * pool.shape[1], *pool.shape[2:])


def compress_blocks(kv, pe, w1, w2):
    B, L = kv.shape[0], kv.shape[1]
    r = CMP_BLOCK // CMP_STRIDE
    n_chunk = L // CMP_STRIDE
    n_cmp = n_chunk - r + 1
    ch = kv[:, :n_chunk * CMP_STRIDE].reshape(B, n_chunk, CMP_STRIDE, 2, N_KV_HEADS, HEAD_DIM)
    blocks = jnp.concatenate([ch[:, i:i + n_cmp] for i in range(r)], axis=2)
    blocks = blocks + jnp.transpose(pe, (1, 0, 2))[:, :, None, :]
    flat = blocks.transpose(0, 1, 3, 4, 2, 5).reshape(B, n_cmp, 2, N_KV_HEADS, CMP_BLOCK * HEAD_DIM)
    hid = jax.nn.gelu(jnp.einsum('bnsgf,sfh->bnsgh', flat, w1), approximate=False)
    return jnp.einsum('bnsgh,shd->bnsgd', hid, w2)


def sel_overlap(n_cmp, n_slc):
    cs = np.arange(n_cmp)[:, None] * CMP_STRIDE
    ss = np.arange(n_slc)[None, :] * SEL_BLOCK
    ov = np.minimum(cs + CMP_BLOCK, ss + SEL_BLOCK) - np.maximum(cs, ss)
    return (np.maximum(ov, 0) / CMP_STRIDE).astype(np.float32)


def nsa_attention(q, kv_cmp, kv_slc, kv_win, gate, rel_bias, pe_cmp, w_cmp1, w_cmp2):
    B, T = q.shape[0], q.shape[1]
    L, Lw = kv_cmp.shape[1], kv_win.shape[1]
    q0, pw0 = L - T, L - Lw
    G, HPG = N_KV_HEADS, HEADS_PER_KV
    scale = HEAD_DIM ** -0.5
    qg = q.reshape(B, T, G, HPG, HEAD_DIM)
    t_pos = q0 + jnp.arange(T)

    kvc = compress_blocks(kv_cmp, pe_cmp, w_cmp1, w_cmp2)
    n_cmp = kvc.shape[1]
    d_c = t_pos[:, None] - (jnp.arange(n_cmp) * CMP_STRIDE + CMP_BLOCK - 1)[None, :]
    mask_c = d_c >= 0
    logits_c = jnp.einsum('btghd,bngd->bghtn', qg, kvc[:, :, 0]).astype(jnp.float32) * scale + head_bias(rel_bias, d_c)
    p_c = masked_softmax(logits_c, mask_c) * jnp.any(mask_c, axis=-1, keepdims=True)
    o_cmp = jnp.einsum('bghtn,bngd->btghd', p_c.astype(kvc.dtype), kvc[:, :, 1])

    n_slc = -(-L // SEL_BLOCK)
    p_slc = jnp.einsum('bgtn,nj->bgtj', p_c.sum(axis=2), jnp.asarray(sel_overlap(n_cmp, n_slc)))
    jt = (t_pos // SEL_BLOCK)[:, None]
    jj = jnp.arange(n_slc)[None, :]
    forced = (jj == 0) | (jj == jt) | (jj == jt - 1)
    score = jnp.where(forced, jnp.inf, jnp.where(jj <= jt, p_slc, -jnp.inf))
    k_sel = min(SEL_TOPK, n_slc)
    top_v, top_i = lax.top_k(score, k_sel)
    sel_ok = top_v > NEG_INF

    qb = min(Q_BLOCK, T)
    nq = -(-T // qb)
    pad_t = nq * qb - T
    q_blk = jnp.pad(qg, ((0, 0), (0, pad_t), (0, 0), (0, 0), (0, 0))).reshape(B, nq, qb, G, HPG, HEAD_DIM).swapaxes(0, 1)
    i_blk = jnp.pad(top_i, ((0, 0), (0, 0), (0, pad_t), (0, 0))).reshape(B, G, nq, qb, k_sel).transpose(2, 0, 1, 3, 4)
    ok_blk = jnp.pad(sel_ok, ((0, 0), (0, 0), (0, pad_t), (0, 0))).reshape(B, G, nq, qb, k_sel).transpose(2, 0, 1, 3, 4)
    t_blk = (q0 + jnp.arange(nq * qb)).reshape(nq, qb)
    w_start = (Lw - T) + jnp.arange(nq) * qb

    kvs = jnp.pad(kv_slc, ((0, 0), (0, n_slc * SEL_BLOCK - L), (0, 0), (0, 0), (0, 0)))
    kvs = kvs.reshape(B, n_slc, SEL_BLOCK, 2, G, HEAD_DIM).transpose(0, 4, 1, 2, 3, 5)
    kvw = jnp.pad(kv_win, ((0, 0), (WINDOW, qb), (0, 0), (0, 0), (0, 0)))
    s_w = WINDOW + qb
    tbl_g = rel_bias.astype(jnp.float32).reshape(N_BUCKETS, G, HPG).transpose(1, 0, 2)
    b_ix = jnp.arange(B)[:, None, None]
    g_ix = jnp.arange(G)[None, :, None]

    def block_fn(args):
        qx, ix, ok, tp, ws = args
        kv = kvs[b_ix, g_ix, ix.reshape(B, G, qb * k_sel)].reshape(B, G, qb, k_sel * SEL_BLOCK, 2, HEAD_DIM)
        kpos = (ix[..., None] * SEL_BLOCK + jnp.arange(SEL_BLOCK)).reshape(B, G, qb, k_sel * SEL_BLOCK)
        d = tp[None, None, :, None] - kpos
        mask = jnp.repeat(ok, SEL_BLOCK, axis=-1) & (d >= 0)
        bias = jnp.moveaxis(tbl_g[g_ix[..., None], t5_bucket(d)], -1, 2)
        logits = jnp.einsum('btghd,bgtsd->bghts', qx, kv[..., 0, :]).astype(jnp.float32) * scale + bias
        p = masked_softmax(logits, mask[:, :, None])
        o_s = jnp.einsum('bghts,bgtsd->btghd', p.astype(kv.dtype), kv[..., 1, :])
        slab = lax.dynamic_slice_in_dim(kvw, ws, s_w, axis=1)
        kidx = ws + jnp.arange(s_w)
        dw = tp[:, None] - (pw0 - WINDOW + kidx)[None, :]
        mask_w = ((kidx >= WINDOW) & (kidx < WINDOW + Lw))[None, :] & (dw >= 0) & (dw <= WINDOW)
        logits_w = jnp.einsum('btghd,bsgd->bghts', qx, slab[:, :, 0]).astype(jnp.float32) * scale + head_bias(rel_bias, dw)
        o_w = jnp.einsum('bghts,bsgd->btghd', masked_softmax(logits_w, mask_w).astype(slab.dtype), slab[:, :, 1])
        return o_s, o_w

    o_slc, o_win = lax.map(block_fn, (q_blk, i_blk, ok_blk, t_blk, w_start))

    def unblock(o):
        return o.swapaxes(0, 1).reshape(B, nq * qb, G, HPG, HEAD_DIM)[:, :T]

    g = jax.nn.sigmoid(gate).reshape(B, T, G, HPG, 3)
    o = g[..., 0:1] * o_cmp + g[..., 1:2] * unblock(o_slc) + g[..., 2:3] * unblock(o_win)
    return o.reshape(B, T, ATTN_WIDTH)


def spatial_gating(u, v, ln_g, ln_b, w_s, b_s):
    B, T = v.shape[0], v.shape[1]
    vf = v.astype(jnp.float32)
    mu = jnp.mean(vf, axis=-1, keepdims=True)
    var = jnp.mean(jnp.square(vf - mu), axis=-1, keepdims=True)
    vn = ((vf - mu) * lax.rsqrt(var + LN_EPS)).astype(v.dtype) * ln_g + ln_b
    nc = -(-T // CHUNK)
    vc = jnp.pad(vn, ((0, 0), (0, nc * CHUNK - T), (0, 0))).reshape(B, nc, CHUNK, GM_GROUPS, GM_GROUP_DIM)
    s = jnp.einsum('gij,bcjgd->bcigd', jnp.tril(w_s), vc) + b_s.T[:, :, None]
    return u * s.reshape(B, nc * CHUNK, GM_WIDTH)[:, :T], vn


def moe_ffn(h, w_router, b_router, w_gate, b_gate, w_up, b_up, w_down, b_down):
    logits = (h @ w_router + b_router).astype(jnp.float32)
    top_v, top_i = lax.top_k(logits, TOP_K)
    wts = jax.nn.softmax(top_v, axis=-1)
    combine = jnp.einsum('nk,nke->ne', wts, jax.nn.one_hot(top_i, N_EXPERTS, dtype=jnp.float32)).astype(h.dtype)
    out = jnp.zeros_like(h)
    for e in range(N_EXPERTS):
        g = jnp.minimum(h @ w_gate[e] + b_gate[e], SWIGLU_LIMIT)
        u = jnp.clip(h @ w_up[e] + b_up[e], -SWIGLU_LIMIT, SWIGLU_LIMIT)
        act = (u + 1) * (g * jax.nn.sigmoid(SWIGLU_ALPHA * g))
        out = out + combine[:, e:e + 1] * (act @ w_down[e] + b_down[e])
    return out


def in_cuts():
    widths = (ATTN_WIDTH, KV_COLS, KV_COLS, KV_COLS, GATE_COLS, 2 * GM_WIDTH)
    return [sum(widths[:i + 1]) for i in range(len(widths))]


def setup_inputs(seed: int = 0) -> dict:
    key = jax.random.key(seed)
    keys = jax.random.split(key, 32)

    def nrm(i, shape, scale):
        return jax.random.normal(keys[i], shape, jnp.float32) * scale

    n_pages = PAST_LEN // PAGE_SIZE
    n_used = DEC_BATCH * n_pages
    n_phys = n_used + max(1, n_used // 4)
    win_buf = min(WINDOW, PAST_LEN)
    kv_tail = (2, N_KV_HEADS, HEAD_DIM)
    page_table = jax.random.permutation(keys[0], n_phys)[:n_used].reshape(DEC_BATCH, n_pages).astype(jnp.int32)
    return {
        'x_prompt': nrm(1, (BATCH, SEQ, D_MODEL), 1.0),
        'x_sample': nrm(2, (DEC_BATCH, DEC_SEQ, D_MODEL), 1.0),
        'cache_cmp_kv': nrm(3, (DEPTH, n_phys, PAGE_SIZE) + kv_tail, 1.0),
        'cache_slc_kv': nrm(4, (DEPTH, n_phys, PAGE_SIZE) + kv_tail, 1.0),
        'state_win_kv': nrm(5, (DEPTH, DEC_BATCH, win_buf) + kv_tail, 1.0),
        'page_table': page_table,
        'rel_bias': nrm(6, (N_BUCKETS, N_HEADS), 0.5),
        'norm_mix': 1.0 + nrm(7, (DEPTH, D_MODEL), 0.05),
        'w_in': nrm(8, (DEPTH, D_MODEL, IN_WIDTH), D_MODEL ** -0.5),
        'pe_cmp': nrm(9, (DEPTH, 2, CMP_BLOCK, HEAD_DIM), 0.1),
        'w_cmp1': nrm(10, (DEPTH, 2, CMP_BLOCK * HEAD_DIM, CMP_HID), (CMP_BLOCK * HEAD_DIM) ** -0.5),
        'w_cmp2': nrm(11, (DEPTH, 2, CMP_HID, HEAD_DIM), CMP_HID ** -0.5),
        'gm_ln_g': 1.0 + nrm(12, (DEPTH, GM_WIDTH), 0.05),
        'gm_ln_b': nrm(13, (DEPTH, GM_WIDTH), 0.02),
        'gm_w_s': nrm(14, (DEPTH, GM_GROUPS, CHUNK, CHUNK), CHUNK ** -0.5),
        'gm_b_s': 1.0 + nrm(15, (DEPTH, GM_GROUPS, CHUNK), 0.1),
        'w_br_attn': nrm(16, (DEPTH, ATTN_WIDTH, D_MODEL), ATTN_WIDTH ** -0.5),
        'w_br_gmlp': nrm(17, (DEPTH, GM_WIDTH, D_MODEL), GM_WIDTH ** -0.5),
        'w_out': nrm(18, (DEPTH, D_MODEL, D_MODEL), D_MODEL ** -0.5),
        'norm_ffn': 1.0 + nrm(19, (DEPTH, D_MODEL), 0.05),
        'w_router': nrm(20, (DEPTH, D_MODEL, N_EXPERTS), D_MODEL ** -0.5),
        'b_router': nrm(21, (DEPTH, N_EXPERTS), 0.01),
        'w_gate': nrm(22, (DEPTH, N_EXPERTS, D_MODEL, D_FF), D_MODEL ** -0.5),
        'b_gate': nrm(23, (DEPTH, N_EXPERTS, D_FF), 0.01),
        'w_up': nrm(24, (DEPTH, N_EXPERTS, D_MODEL, D_FF), D_MODEL ** -0.5),
        'b_up': nrm(25, (DEPTH, N_EXPERTS, D_FF), 0.01),
        'w_down': nrm(26, (DEPTH, N_EXPERTS, D_FF, D_MODEL), D_FF ** -0.5),
        'b_down': nrm(27, (DEPTH, N_EXPERTS, D_MODEL), 0.01),
        'norm_final': 1.0 + nrm(28, (D_MODEL,), 0.05),
    }


def reference(x_prompt, x_sample, cache_cmp_kv, cache_slc_kv, state_win_kv, page_table, rel_bias,
              norm_mix, w_in, pe_cmp, w_cmp1, w_cmp2, gm_ln_g, gm_ln_b, gm_w_s, gm_b_s,
              w_br_attn, w_br_gmlp, w_out, norm_ffn, w_router, b_router, w_gate, b_gate,
              w_up, b_up, w_down, b_down, norm_final):
    B, T = x_prompt.shape[0], x_prompt.shape[1]
    Bs, Ts = x_sample.shape[0], x_sample.shape[1]
    n_p = B * T
    win_buf = state_win_kv.shape[2]
    kv_tail = (2, N_KV_HEADS, HEAD_DIM)
    cuts = in_cuts()
    h = jnp.concatenate([x_prompt.reshape(n_p, D_MODEL), x_sample.reshape(Bs * Ts, D_MODEL)], axis=0)

    def prompt(a, *tail):
        return a[:n_p].reshape(B, T, *tail)

    def sample(a, *tail):
        return a[n_p:].reshape(Bs, Ts, *tail)

    cmp_p, cmp_s, slc_p, slc_s, win_p, win_s, gmv_s = [], [], [], [], [], [], []
    for l in range(DEPTH):
        a = rmsnorm(h, norm_mix[l])
        q, kc, ks, kw, ng, uv, mg = jnp.split(a @ w_in[l], cuts, axis=-1)

        kc_p, ks_p, kw_p = prompt(kc, *kv_tail), prompt(ks, *kv_tail), prompt(kw, *kv_tail)
        kc_s, ks_s, kw_s = sample(kc, *kv_tail), sample(ks, *kv_tail), sample(kw, *kv_tail)
        attn_p = nsa_attention(prompt(q, N_HEADS, HEAD_DIM), kc_p, ks_p, kw_p, prompt(ng, GATE_COLS),
                               rel_bias, pe_cmp[l], w_cmp1[l], w_cmp2[l])
        full_c = jnp.concatenate([gather_pages(cache_cmp_kv[l], page_table), kc_s], axis=1)
        full_s = jnp.concatenate([gather_pages(cache_slc_kv[l], page_table), ks_s], axis=1)
        band_s = jnp.concatenate([state_win_kv[l], kw_s], axis=1)
        attn_s = nsa_attention(sample(q, N_HEADS, HEAD_DIM), full_c, full_s, band_s, sample(ng, GATE_COLS),
                               rel_bias, pe_cmp[l], w_cmp1[l], w_cmp2[l])
        attn = jnp.concatenate([attn_p.reshape(n_p, ATTN_WIDTH), attn_s.reshape(Bs * Ts, ATTN_WIDTH)], axis=0)

        u, v = jnp.split(jax.nn.gelu(uv, approximate=False), 2, axis=-1)
        gm_p, _ = spatial_gating(prompt(u, GM_WIDTH), prompt(v, GM_WIDTH), gm_ln_g[l], gm_ln_b[l], gm_w_s[l], gm_b_s[l])
        gm_s, vn_s = spatial_gating(sample(u, GM_WIDTH), sample(v, GM_WIDTH), gm_ln_g[l], gm_ln_b[l], gm_w_s[l], gm_b_s[l])
        gm = jnp.concatenate([gm_p.reshape(n_p, GM_WIDTH), gm_s.reshape(Bs * Ts, GM_WIDTH)], axis=0)

        g_a, g_b = jnp.split(jax.nn.sigmoid(mg), 2, axis=-1)
        merged = g_a * (attn @ w_br_attn[l]) + g_b * (gm @ w_br_gmlp[l])
        h = h + merged @ w_out[l]

        h = h + moe_ffn(rmsnorm(h, norm_ffn[l]), w_router[l], b_router[l], w_gate[l], b_gate[l],
                        w_up[l], b_up[l], w_down[l], b_down[l])

        cmp_p.append(kc_p)
        cmp_s.append(kc_s)
        slc_p.append(ks_p)
        slc_s.append(ks_s)
        win_p.append(kw_p[:, T - min(WINDOW, T):])
        win_s.append(band_s[:, band_s.shape[1] - win_buf:])
        gmv_s.append(vn_s)

    y = rmsnorm(h, norm_final)
    y_prompt = y[:n_p].reshape(B, T, D_MODEL)
    y_sample = y[n_p:].reshape(Bs, Ts, D_MODEL)
    return (y_prompt, y_sample, jnp.stack(cmp_p), jnp.stack(cmp_s), jnp.stack(slc_p), jnp.stack(slc_s),
            jnp.stack(win_p), jnp.stack(win_s), jnp.stack(gmv_s))
```

```python
import functools
import math

import jax
import jax.numpy as jnp
import numpy as np
from jax import lax
from jax.experimental import pallas as pl
from jax.experimental.pallas import tpu as pltpu

D_MODEL = 2048
N_HEADS = 16
HEAD_DIM = 64
N_KV_HEADS = 2
HEADS_PER_KV = N_HEADS // N_KV_HEADS
ATTN_WIDTH = N_HEADS * HEAD_DIM
CMP_BLOCK = 32
CMP_STRIDE = 16
CMP_HID = 2 * HEAD_DIM
SEL_BLOCK = 64
SEL_TOPK = 16
WINDOW = 512
GM_WIDTH = D_MODEL // 2
GM_GROUPS = 8
GM_GROUP_DIM = GM_WIDTH // GM_GROUPS
CHUNK = 128
N_BUCKETS = 32
REL_MAX_DIST = 128
N_EXPERTS = 32
TOP_K = 4
D_FF = D_MODEL
SWIGLU_LIMIT = 7.0
SWIGLU_ALPHA = 1.702
RMS_EPS = 1e-5
LN_EPS = 1e-5
NEG_INF = -1e30
KV_COLS = 2 * N_KV_HEADS * HEAD_DIM
GATE_COLS = 3 * N_HEADS
SLC_PER_PAGE = 2

LANE = 128
SUBLANE = 8
VMEM_LIMIT = 56 * 1024 * 1024

QB = 128
KT = 128
CMP_PAD = QB // CMP_STRIDE
CMP_NEAR = 2 * CMP_PAD
MOE_TM = 256
MOE_TN = 512
FEAT = 2 * HEAD_DIM

BF16 = jnp.bfloat16
F32 = jnp.float32


def _bucket_upper_bounds():
    d = np.arange(0, 4 * REL_MAX_DIST)
    exact = N_BUCKETS // 2
    out = []
    for dt in (np.float32, np.float64):
        far = exact + (np.log(np.maximum(d, exact).astype(dt) / dt(exact)) / dt(math.log(REL_MAX_DIST / exact))
                       * dt(N_BUCKETS - exact)).astype(np.int32)
        out.append(np.where(d < exact, d, np.minimum(far, N_BUCKETS - 1)))
    assert (out[0] == out[1]).all() and (np.diff(out[0]) >= 0).all()
    b = out[0]
    return [int(d[b <= k].max()) for k in range(N_BUCKETS - 1)]


BUCKET_HI = _bucket_upper_bounds()
FAR_DIST = BUCKET_HI[-1] + 1
assert FAR_DIST <= QB


def _rel_bias(d, rb_ref, h):
    acc = jnp.full(d.shape, rb_ref[N_BUCKETS - 1, h], F32)
    for k in reversed(range(N_BUCKETS - 1)):
        acc = jnp.where(d <= BUCKET_HI[k], rb_ref[k, h], acc)
    return acc


def _dot(a, b, **kw):
    return jnp.dot(a, b, preferred_element_type=F32, **kw)


def _dot_nt(a, b):
    return lax.dot_general(a, b, (((1,), (1,)), ((), ())), preferred_element_type=F32)


def _gelu(x):
    return 0.5 * x * (1.0 + lax.erf(x * np.float32(math.sqrt(0.5))))


def _params(*sem):
    return pltpu.CompilerParams(dimension_semantics=sem, vmem_limit_bytes=VMEM_LIMIT)


def _rmsnorm_kernel(x_ref, g_ref, o_ref):
    x = x_ref[...]
    ms = jnp.mean(x * x, axis=-1, keepdims=True)
    o_ref[...] = (x * lax.rsqrt(ms + RMS_EPS) * g_ref[...]).astype(o_ref.dtype)


def rmsnorm_rows(x, g, tm):
    m, d = x.shape
    return pl.pallas_call(
        _rmsnorm_kernel, grid=(m // tm,), name="rmsnorm",
        in_specs=[pl.BlockSpec((tm, d), lambda i: (i, 0)), pl.BlockSpec((1, d), lambda i: (0, 0))],
        out_specs=pl.BlockSpec((tm, d), lambda i: (i, 0)),
        out_shape=jax.ShapeDtypeStruct((m, d), BF16), compiler_params=_params("parallel"))(x, g.reshape(1, d))


def _mm_kernel(a_ref, w_ref, o_ref, *, epilogue):
    o_ref[...] = epilogue(_dot(a_ref[...], w_ref[...])).astype(o_ref.dtype)


def matmul(a, w, epilogue, out_dtype, tm, tn, name):
    m, k = a.shape
    n = w.shape[1]
    tm, tn = min(tm, m), min(tn, n)
    return pl.pallas_call(
        functools.partial(_mm_kernel, epilogue=epilogue), grid=(m // tm, n // tn), name=name,
        in_specs=[pl.BlockSpec((tm, k), lambda i, j: (i, 0)), pl.BlockSpec((k, tn), lambda i, j: (0, j))],
        out_specs=pl.BlockSpec((tm, tn), lambda i, j: (i, j)),
        out_shape=jax.ShapeDtypeStruct((m, n), out_dtype), compiler_params=_params("parallel", "parallel"))(a, w)


def _compress_kernel(c_ref, w1_ref, w2_ref, pe_ref, o_ref):
    n = c_ref.shape[0]
    half = CMP_STRIDE * HEAD_DIM
    c = c_ref[...]
    first = _dot(c, w1_ref[0:half, :])
    second = _dot(c, w1_ref[half:2 * half, :])
    pe = _dot(pe_ref[...].astype(BF16), w1_ref[...])[0:1, :]
    hid = _gelu(first + pltpu.roll(second, n - 1, 0) + pe)
    o_ref[...] = _dot(hid.astype(BF16), w2_ref[...])


def compress_chunks(chunks, w1, w2, pe):
    b, _, g, n, width = chunks.shape
    pe_rows = jnp.broadcast_to(pe.reshape(2, 1, CMP_BLOCK * HEAD_DIM), (2, SUBLANE, CMP_BLOCK * HEAD_DIM))
    return pl.pallas_call(
        _compress_kernel, grid=(b, 2, g), name="compress",
        in_specs=[pl.BlockSpec((None, None, None, n, width), lambda i, s, j: (i, s, j, 0, 0)),
                  pl.BlockSpec((None, CMP_BLOCK * HEAD_DIM, CMP_HID), lambda i, s, j: (s, 0, 0)),
                  pl.BlockSpec((None, CMP_HID, HEAD_DIM), lambda i, s, j: (s, 0, 0)),
                  pl.BlockSpec((None, SUBLANE, CMP_BLOCK * HEAD_DIM), lambda i, s, j: (s, 0, 0))],
        out_specs=pl.BlockSpec((None, None, None, n, HEAD_DIM), lambda i, s, j: (i, s, j, 0, 0)),
        out_shape=jax.ShapeDtypeStruct((b, 2, g, n, HEAD_DIM), F32),
        compiler_params=_params("parallel", "parallel", "parallel"))(chunks, w1.astype(BF16), w2.astype(BF16), pe_rows)


def _bias_tables_kernel(rb_ref, tz_ref, tc_ref):
    i = lax.broadcasted_iota(jnp.int32, (QB, KT), 0)
    j = lax.broadcasted_iota(jnp.int32, (QB, KT), 1)
    ic = lax.broadcasted_iota(jnp.int32, (QB, CMP_NEAR), 0)
    cc = lax.broadcasted_iota(jnp.int32, (QB, CMP_NEAR), 1)
    d0 = i - j
    d1 = d0 + KT
    dc = ic + (QB - CMP_BLOCK + 1) - CMP_STRIDE * cc
    for h in range(N_HEADS):
        last = rb_ref[N_BUCKETS - 1, h]
        rows = slice(h * QB, (h + 1) * QB)
        tz_ref[0, rows, :] = jnp.where(d0 < 0, NEG_INF, _rel_bias(d0, rb_ref, h) - last)
        tz_ref[1, rows, :] = _rel_bias(d1, rb_ref, h) - last
        tc_ref[rows, :] = jnp.where(dc < 0, NEG_INF, _rel_bias(dc, rb_ref, h) - last)


def bias_tables(rel_bias):
    return pl.pallas_call(
        _bias_tables_kernel, name="bias_tables",
        in_specs=[pl.BlockSpec(memory_space=pltpu.SMEM)],
        out_specs=[pl.BlockSpec(memory_space=pltpu.VMEM), pl.BlockSpec(memory_space=pltpu.VMEM)],
        out_shape=[jax.ShapeDtypeStruct((2, N_HEADS * QB, KT), F32),
                   jax.ShapeDtypeStruct((N_HEADS * QB, CMP_NEAR), F32)])(rel_bias)


def _prompt_attn_kernel(q_ref, gate_ref, kc_ref, vc_ref, ov_ref, tc_ref, tz_ref, ks_ref, vs_ref, kw_ref, vw_ref,
                        o_ref, qa_ref, m_ref, acc_ref, *, n_chunk):
    qi = pl.program_id(2)
    rows_all = HEADS_PER_KV * QB

    for h in range(HEADS_PER_KV):
        qa_ref[h * QB:(h + 1) * QB, 0:HEAD_DIM] = q_ref[:, h * HEAD_DIM:(h + 1) * HEAD_DIM]
        qa_ref[h * QB:(h + 1) * QB, HEAD_DIM:FEAT] = jnp.zeros((QB, FEAT - HEAD_DIM), BF16)
    qa = qa_ref[...]
    i_row = lax.broadcasted_iota(jnp.int32, (rows_all, 1), 0) % QB

    near0 = pl.multiple_of(qi * CMP_PAD, SUBLANE)
    r = lax.broadcasted_iota(jnp.int32, (1, n_chunk), 1)
    lf = _dot_nt(qa, kc_ref[0:n_chunk, :].astype(BF16))
    lf = jnp.where((r >= CMP_PAD) & (r < qi * CMP_PAD), lf, NEG_INF)
    c = lax.broadcasted_iota(jnp.int32, (1, CMP_NEAR), 1)
    ln = _dot_nt(qa, kc_ref[pl.ds(near0, CMP_NEAR), :].astype(BF16)) + tc_ref[...]
    ln = jnp.where(qi * CMP_PAD + c >= CMP_PAD, ln, NEG_INF)
    m = jnp.maximum(jnp.max(lf, axis=-1, keepdims=True), jnp.max(ln, axis=-1, keepdims=True))
    pf = jnp.exp(lf - m)
    pn = jnp.exp(ln - m)
    denom = jnp.sum(pf, axis=-1, keepdims=True) + jnp.sum(pn, axis=-1, keepdims=True)
    inv = jnp.where(qi * QB + i_row >= CMP_BLOCK - 1, 1.0 / denom, 0.0)
    pf = pf * inv
    pn = pn * inv
    o_cmp = (_dot(pf.astype(BF16), vc_ref[0:n_chunk, :].astype(BF16))
             + _dot(pn.astype(BF16), vc_ref[pl.ds(near0, CMP_NEAR), :].astype(BF16)))
    psf = jnp.sum(pf.reshape(HEADS_PER_KV, QB, n_chunk), axis=0)
    psn = jnp.sum(pn.reshape(HEADS_PER_KV, QB, CMP_NEAR), axis=0)
    p_slc = (_dot(psf, ov_ref[0:n_chunk, :], precision=lax.Precision.HIGHEST)
             + _dot(psn, ov_ref[pl.ds(near0, CMP_NEAR), :], precision=lax.Precision.HIGHEST))

    nf = FEAT - HEAD_DIM
    blk = lax.broadcasted_iota(jnp.int32, (QB, nf), 1)
    jt = (qi * QB + lax.broadcasted_iota(jnp.int32, (QB, nf), 0)) // SEL_BLOCK
    forced = (blk == 0) | (blk == jt) | (blk == jt - 1)
    score = jnp.where(forced, jnp.inf, jnp.where(blk <= jt, p_slc, -jnp.inf))
    rank = jnp.zeros((QB, nf), jnp.int32)
    for col in range(nf):
        sc = score[:, col:col + 1]
        rank = rank + jnp.where(blk > col, (sc >= score).astype(jnp.int32), (sc > score).astype(jnp.int32))
    feat = jnp.where((rank < SEL_TOPK) & (score > NEG_INF), 0.0, NEG_INF).astype(BF16)
    for h in range(HEADS_PER_KV):
        qa_ref[h * QB:(h + 1) * QB, HEAD_DIM:FEAT] = feat
    qa = qa_ref[...]

    def reset():
        m_ref[...] = jnp.full(m_ref.shape, -3e38, F32)
        acc_ref[...] = jnp.zeros(acc_ref.shape, F32)

    def step(k_ref, v_ref, j, bias):
        off = pl.multiple_of(j * KT, KT)
        s = _dot_nt(qa, k_ref[pl.ds(off, KT), :])
        if bias is not None:
            s = s + bias
        m_old = m_ref[...]
        m_new = jnp.maximum(m_old, jnp.max(s, axis=-1, keepdims=True))
        p = jnp.exp(s - m_new)
        acc_ref[...] = jnp.exp(m_old - m_new) * acc_ref[...] + _dot(p.astype(BF16), v_ref[pl.ds(off, KT), :])
        m_ref[...] = m_new

    def result():
        acc = acc_ref[...]
        return acc[:, 0:HEAD_DIM] * (1.0 / acc[:, HEAD_DIM:HEAD_DIM + 1])

    reset()

    def far_step(j, carry):
        step(ks_ref, vs_ref, j, None)
        return carry

    lax.fori_loop(0, jnp.maximum(qi - 1, 0), far_step, 0)

    @pl.when(qi >= 1)
    def _():
        step(ks_ref, vs_ref, qi - 1, tz_ref[1])

    step(ks_ref, vs_ref, qi, tz_ref[0])
    o_slc = result()

    reset()
    n_back = WINDOW // KT
    edge = jnp.where(lax.broadcasted_iota(jnp.int32, (rows_all, KT), 1) >= i_row, 0.0, NEG_INF)
    for back in range(n_back, -1, -1):
        bias = edge if back == n_back else tz_ref[1] if back == 1 else tz_ref[0] if back == 0 else None

        @pl.when(qi >= back)
        def _(back=back, bias=bias):
            step(kw_ref, vw_ref, qi - back, bias)

    o_win = result()

    gate = gate_ref[...]
    for h in range(HEADS_PER_KV):
        rows = slice(h * QB, (h + 1) * QB)
        o = (gate[:, h:h + 1] * o_cmp[rows] + gate[:, HEADS_PER_KV + h:HEADS_PER_KV + h + 1] * o_slc[rows]
             + gate[:, 2 * HEADS_PER_KV + h:2 * HEADS_PER_KV + h + 1] * o_win[rows])
        o_ref[:, h * HEAD_DIM:(h + 1) * HEAD_DIM] = o.astype(o_ref.dtype)


def _sel_overlap_rows(n_rows, n_cmp):
    nf = FEAT - HEAD_DIM
    cs = (np.arange(n_rows) - CMP_PAD)[:, None] * CMP_STRIDE
    ss = np.arange(nf)[None, :] * SEL_BLOCK
    ov = np.minimum(cs + CMP_BLOCK, ss + SEL_BLOCK) - np.maximum(cs, ss)
    ov = np.maximum(ov, 0) / CMP_STRIDE
    valid = (np.arange(n_rows) >= CMP_PAD) & (np.arange(n_rows) < CMP_PAD + n_cmp)
    return (ov * valid[:, None]).astype(np.float32)


def prompt_attention(q, gate, kc, vc, tc, tz, ks, vs, kw, vw, b, t):
    n_chunk = t // CMP_STRIDE
    rc = n_chunk + 2 * CMP_PAD
    assert t // SEL_BLOCK <= FEAT - HEAD_DIM and t % QB == 0 and n_chunk % LANE == 0
    ov = jnp.asarray(_sel_overlap_rows(rc, n_chunk - 1))
    nq = t // QB
    gw = HEADS_PER_KV * HEAD_DIM
    kv_spec = pl.BlockSpec((None, None, t, FEAT), lambda i, g, qi: (i, g, 0, 0))
    return pl.pallas_call(
        functools.partial(_prompt_attn_kernel, n_chunk=n_chunk), grid=(b, N_KV_HEADS, nq), name="prompt_attention",
        in_specs=[pl.BlockSpec((QB, gw), lambda i, g, qi: (i * nq + qi, g)),
                  pl.BlockSpec((QB, LANE), lambda i, g, qi: (i * nq + qi, g)),
                  pl.BlockSpec((None, None, rc, FEAT), lambda i, g, qi: (i, g, 0, 0)),
                  pl.BlockSpec((None, None, rc, HEAD_DIM), lambda i, g, qi: (i, g, 0, 0)),
                  pl.BlockSpec((rc, FEAT - HEAD_DIM), lambda i, g, qi: (0, 0)),
                  pl.BlockSpec((HEADS_PER_KV * QB, CMP_NEAR), lambda i, g, qi: (g, 0)),
                  pl.BlockSpec((2, HEADS_PER_KV * QB, KT), lambda i, g, qi: (0, g, 0)),
                  kv_spec, kv_spec, kv_spec, kv_spec],
        out_specs=pl.BlockSpec((QB, gw), lambda i, g, qi: (i * nq + qi, g)),
        out_shape=jax.ShapeDtypeStruct((b * t, ATTN_WIDTH), BF16),
        scratch_shapes=[pltpu.VMEM((HEADS_PER_KV * QB, FEAT), BF16),
                        pltpu.VMEM((HEADS_PER_KV * QB, LANE), F32),
                        pltpu.VMEM((HEADS_PER_KV * QB, LANE), F32)],
        compiler_params=_params("parallel", "parallel", "arbitrary"))(q, gate, kc, vc, ov, tc, tz, ks, vs, kw, vw)


def _gmlp_kernel(u_ref, v_ref, lng_ref, lnb_ref, ws_ref, bs_ref, o_ref):
    v = v_ref[...].astype(F32)
    mu = jnp.mean(v, axis=-1, keepdims=True)
    var = jnp.mean(jnp.square(v - mu), axis=-1, keepdims=True)
    vn = ((v - mu) * lax.rsqrt(var + LN_EPS) * lng_ref[...] + lnb_ref[...]).astype(BF16)
    tri = (lax.broadcasted_iota(jnp.int32, (CHUNK, CHUNK), 0) >= lax.broadcasted_iota(jnp.int32, (CHUNK, CHUNK), 1))
    for g in range(GM_GROUPS):
        w = jnp.where(tri, ws_ref[g], 0.0).astype(BF16)
        cols = slice(g * GM_GROUP_DIM, (g + 1) * GM_GROUP_DIM)
        for c in range(u_ref.shape[0] // CHUNK):
            rows = slice(c * CHUNK, (c + 1) * CHUNK)
            s = _dot(w, vn[rows, cols]) + bs_ref[:, g:g + 1]
            o_ref[rows, cols] = (u_ref[rows, cols].astype(F32) * s).astype(o_ref.dtype)


def gmlp_chunks(uv, ln_g, ln_b, w_s, b_s, rows):
    m = uv.shape[0]
    return pl.pallas_call(
        _gmlp_kernel, grid=(m // rows,), name="gmlp",
        in_specs=[pl.BlockSpec((rows, GM_WIDTH), lambda i: (i, 0)), pl.BlockSpec((rows, GM_WIDTH), lambda i: (i, 1)),
                  pl.BlockSpec((1, GM_WIDTH), lambda i: (0, 0)), pl.BlockSpec((1, GM_WIDTH), lambda i: (0, 0)),
                  pl.BlockSpec((GM_GROUPS, CHUNK, CHUNK), lambda i: (0, 0, 0)),
                  pl.BlockSpec((CHUNK, GM_GROUPS), lambda i: (0, 0))],
        out_specs=pl.BlockSpec((rows, GM_WIDTH), lambda i: (i, 0)),
        out_shape=jax.ShapeDtypeStruct((m, GM_WIDTH), BF16),
        compiler_params=_params("parallel"))(uv, uv, ln_g.reshape(1, -1), ln_b.reshape(1, -1), w_s, b_s.T)


def _merge_kernel(attn_ref, gm_ref, mg_ref, x_ref, wa_ref, wb_ref, wo_ref, gn_ref, wr_ref, br_ref,
                  h_ref, hn_ref, idx_ref, wt_ref):
    ta = _dot(attn_ref[...], wa_ref[...])
    tb = _dot(gm_ref[...], wb_ref[...])
    merged = mg_ref[:, 0:D_MODEL].astype(F32) * ta + mg_ref[:, D_MODEL:2 * D_MODEL].astype(F32) * tb
    h = x_ref[...] + _dot(merged.astype(BF16), wo_ref[...])
    h_ref[...] = h
    hn = h * lax.rsqrt(jnp.mean(h * h, axis=-1, keepdims=True) + RMS_EPS) * gn_ref[...]
    hn_ref[...] = hn.astype(hn_ref.dtype)
    logits = _dot(hn, wr_ref[...], precision=lax.Precision.HIGHEST) + br_ref[...]
    lane = lax.broadcasted_iota(jnp.int32, logits.shape, 1)
    vals, idxs = [], []
    for _ in range(TOP_K):
        best = jnp.max(logits, axis=-1, keepdims=True)
        ix = jnp.min(jnp.where(logits == best, lane, LANE), axis=-1, keepdims=True)
        vals.append(best)
        idxs.append(ix)
        logits = jnp.where(lane == ix, -jnp.inf, logits)
    ex = [jnp.exp(v - vals[0]) for v in vals]
    inv = 1.0 / functools.reduce(lambda a, b: a + b, ex)
    idx_out = jnp.zeros(lane.shape, jnp.int32)
    wt_out = jnp.zeros(lane.shape, F32)
    for k in range(TOP_K):
        idx_out = jnp.where(lane == k, idxs[k], idx_out)
        wt_out = jnp.where(lane == k, ex[k] * inv, wt_out)
    idx_ref[...] = idx_out
    wt_ref[...] = wt_out


def merge_and_route(attn, gm, mg, x, wa, wb, wo, g_ffn, w_router, b_router, tm):
    m = x.shape[0]
    tm = min(tm, m)
    wr = jnp.zeros((D_MODEL, LANE), F32).at[:, :N_EXPERTS].set(w_router)
    br = jnp.full((1, LANE), -jnp.inf, F32).at[0, :N_EXPERTS].set(b_router)
    row = lambda w: pl.BlockSpec((tm, w), lambda i: (i, 0))
    full = lambda a: pl.BlockSpec(a.shape, lambda i: (0, 0))
    return pl.pallas_call(
        _merge_kernel, grid=(m // tm,), name="merge_route",
        in_specs=[row(ATTN_WIDTH), row(GM_WIDTH), row(2 * D_MODEL), row(D_MODEL), full(wa), full(wb), full(wo),
                  pl.BlockSpec((1, D_MODEL), lambda i: (0, 0)), full(wr), full(br)],
        out_specs=[row(D_MODEL), row(D_MODEL), row(LANE), row(LANE)],
        out_shape=[jax.ShapeDtypeStruct((m, D_MODEL), F32), jax.ShapeDtypeStruct((m, D_MODEL), BF16),
                   jax.ShapeDtypeStruct((m, LANE), jnp.int32), jax.ShapeDtypeStruct((m, LANE), F32)],
        compiler_params=_params("parallel"))(attn, gm, mg, x, wa, wb, wo, g_ffn.reshape(1, -1), wr, br)


def _cast_weight(src_ref, dst_ref):
    step = 256
    for r0 in range(0, src_ref.shape[0], step):
        dst_ref[r0:r0 + step, :] = src_ref[r0:r0 + step, :].astype(dst_ref.dtype)


def _moe_up_kernel(te_ref, first_ref, nact_ref, x_ref, wg_ref, wu_ref, bg_ref, bu_ref, o_ref, wgb_ref, wub_ref):
    r = pl.program_id(1)

    @pl.when(first_ref[r] == 1)
    def _():
        _cast_weight(wg_ref, wgb_ref)
        _cast_weight(wu_ref, wub_ref)

    @pl.when(r < nact_ref[0])
    def _():
        x = x_ref[...]
        g = jnp.minimum(_dot(x, wgb_ref[...]) + bg_ref[...], SWIGLU_LIMIT)
        u = jnp.clip(_dot(x, wub_ref[...]) + bu_ref[...], -SWIGLU_LIMIT, SWIGLU_LIMIT)
        o_ref[...] = ((u + 1.0) * (g * jax.nn.sigmoid(SWIGLU_ALPHA * g))).astype(o_ref.dtype)


def _moe_down_kernel(te_ref, first_ref, nact_ref, a_ref, wd_ref, bd_ref, o_ref, wdb_ref):
    r = pl.program_id(1)

    @pl.when(first_ref[r] == 1)
    def _():
        _cast_weight(wd_ref, wdb_ref)

    @pl.when(r < nact_ref[0])
    def _():
        o_ref[...] = (_dot(a_ref[...], wdb_ref[...]) + bd_ref[...]).astype(o_ref.dtype)


def moe_experts(x_rows, tile_expert, tile_first, n_active, w_gate, b_gate, w_up, b_up, w_down, b_down):
    rows = x_rows.shape[0]
    n_tiles = rows // MOE_TM
    row_map = lambda j, r, te, first, nact: (jnp.minimum(r, nact[0] - 1), 0)
    out_map = lambda j, r, te, first, nact: (jnp.minimum(r, nact[0] - 1), j)
    w_map = lambda j, r, te, first, nact: (te[r], 0, j)
    w_spec = pl.BlockSpec((None, D_MODEL, MOE_TN), w_map)
    b_spec = pl.BlockSpec((None, 1, MOE_TN), w_map)
    act = pl.pallas_call(
        _moe_up_kernel, name="moe_up",
        grid_spec=pltpu.PrefetchScalarGridSpec(
            num_scalar_prefetch=3, grid=(D_FF // MOE_TN, n_tiles),
            in_specs=[pl.BlockSpec((MOE_TM, D_MODEL), row_map), w_spec, w_spec, b_spec, b_spec],
            out_specs=pl.BlockSpec((MOE_TM, MOE_TN), out_map),
            scratch_shapes=[pltpu.VMEM((D_MODEL, MOE_TN), BF16), pltpu.VMEM((D_MODEL, MOE_TN), BF16)]),
        out_shape=jax.ShapeDtypeStruct((rows, D_FF), BF16),
        compiler_params=_params("arbitrary", "arbitrary"))(
            tile_expert, tile_first, n_active, x_rows, w_gate, w_up,
            b_gate.reshape(N_EXPERTS, 1, D_FF), b_up.reshape(N_EXPERTS, 1, D_FF))
    return pl.pallas_call(
        _moe_down_kernel, name="moe_down",
        grid_spec=pltpu.PrefetchScalarGridSpec(
            num_scalar_prefetch=3, grid=(D_MODEL // MOE_TN, n_tiles),
            in_specs=[pl.BlockSpec((MOE_TM, D_FF), row_map), w_spec, b_spec],
            out_specs=pl.BlockSpec((MOE_TM, MOE_TN), out_map),
            scratch_shapes=[pltpu.VMEM((D_FF, MOE_TN), BF16)]),
        out_shape=jax.ShapeDtypeStruct((rows, D_MODEL), BF16),
        compiler_params=_params("arbitrary", "arbitrary"))(
            tile_expert, tile_first, n_active, act, w_down, b_down.reshape(N_EXPERTS, 1, D_MODEL))


def route_rows(top_i, n_rows):
    n = top_i.shape[0]
    onehot = (top_i[:, :, None] == jnp.arange(N_EXPERTS, dtype=jnp.int32)[None, None, :]).astype(jnp.int32).sum(axis=1)
    before = jnp.cumsum(onehot, axis=0) - onehot
    counts = onehot.sum(axis=0)
    padded = (counts + MOE_TM - 1) // MOE_TM * MOE_TM
    ends = jnp.cumsum(padded)
    starts = ends - padded
    pos = starts[top_i] + jnp.take_along_axis(before, top_i, axis=1)
    row_token = jnp.zeros((n_rows,), jnp.int32).at[pos.reshape(-1)].set(
        jnp.repeat(jnp.arange(n, dtype=jnp.int32), TOP_K))
    n_tiles = n_rows // MOE_TM
    n_active = (ends[-1] // MOE_TM).astype(jnp.int32)
    tile_row = jnp.minimum(jnp.arange(n_tiles, dtype=jnp.int32), n_active - 1) * MOE_TM
    tile_expert = jnp.minimum(jnp.searchsorted(ends, tile_row, side="right"), N_EXPERTS - 1).astype(jnp.int32)
    tile_first = jnp.concatenate([jnp.ones((1,), jnp.int32), (tile_expert[1:] != tile_expert[:-1]).astype(jnp.int32)])
    return pos, row_token, tile_expert, tile_first, n_active.reshape(1)


def _combine_kernel(h_ref, y_ref, w_ref, g_ref, o_ref):
    h = h_ref[...]
    w = w_ref[...]
    for k in range(TOP_K):
        h = h + w[:, k:k + 1] * y_ref[k].astype(F32)
    o_ref[...] = h * lax.rsqrt(jnp.mean(h * h, axis=-1, keepdims=True) + RMS_EPS) * g_ref[...]


def combine_and_norm(h, y4, wts, g, tm):
    m = h.shape[0]
    tm = min(tm, m)
    return pl.pallas_call(
        _combine_kernel, grid=(m // tm,), name="combine_norm",
        in_specs=[pl.BlockSpec((tm, D_MODEL), lambda i: (i, 0)), pl.BlockSpec((TOP_K, tm, D_MODEL), lambda i: (0, i, 0)),
                  pl.BlockSpec((tm, LANE), lambda i: (i, 0)), pl.BlockSpec((1, D_MODEL), lambda i: (0, 0))],
        out_specs=pl.BlockSpec((tm, D_MODEL), lambda i: (i, 0)),
        out_shape=jax.ShapeDtypeStruct((m, D_MODEL), F32), compiler_params=_params("parallel"))(
            h, y4, wts, g.reshape(1, -1))


def _in_weights(w_in):
    c0 = ATTN_WIDTH
    c1 = c0 + 3 * KV_COLS
    c2 = c1 + GATE_COLS
    c3 = c2 + 2 * GM_WIDTH
    wg = w_in[:, c1:c2].reshape(D_MODEL, N_KV_HEADS, HEADS_PER_KV, 3).transpose(0, 1, 3, 2)
    wg = jnp.pad(wg.reshape(D_MODEL, N_KV_HEADS, 3 * HEADS_PER_KV), ((0, 0), (0, 0), (0, LANE - 3 * HEADS_PER_KV)))
    cast = lambda w: w.astype(BF16)
    return (cast(w_in[:, :c0]), cast(w_in[:, c0:c1]), cast(wg.reshape(D_MODEL, N_KV_HEADS * LANE)),
            cast(w_in[:, c2:c3]), cast(w_in[:, c3:]))


def _project(a, weights, wide_dtype):
    w_q, w_kv, w_ng, w_uv, w_mg = weights
    q = matmul(a, w_q, lambda z: z * np.float32(HEAD_DIM ** -0.5), wide_dtype, 1024, 1024, "proj_q")
    kv = matmul(a, w_kv, lambda z: z, F32, 1024, 3 * KV_COLS, "proj_kv")
    ng = matmul(a, w_ng, jax.nn.sigmoid, F32, 1024, N_KV_HEADS * LANE, "proj_gate")
    uv = matmul(a, w_uv, _gelu, wide_dtype, 1024, 1024, "proj_uv")
    mg = matmul(a, w_mg, jax.nn.sigmoid, BF16, 1024, 1024, "proj_merge")
    return q, kv, ng, uv, mg


def _head_rows(fn):
    out = fn(0)
    out = jnp.broadcast_to(out, (HEADS_PER_KV, out.shape[1]))
    row = lax.broadcasted_iota(jnp.int32, out.shape, 0)
    for h in range(1, HEADS_PER_KV):
        out = jnp.where(row == h, fn(h), out)
    return out


def _sample_cmp_win_kernel(rb_ref, q_ref, kvc_ref, ov_ref, sw_ref, new_ref, gate_ref, o_ref, idx_ref, *, past, n_cmp):
    n_rows = kvc_ref.shape[2]
    n_blk = ov_ref.shape[1]
    win = sw_ref.shape[0]
    win_col = 2 * KV_COLS
    lane_blk = lax.broadcasted_iota(jnp.int32, (1, n_blk), 1)
    lane_out = lax.broadcasted_iota(jnp.int32, (1, LANE), 1)
    for g in range(N_KV_HEADS):
        heads = slice(g * HEADS_PER_KV, (g + 1) * HEADS_PER_KV)
        q = q_ref[heads, :]
        qb = q.astype(BF16)

        n = lax.broadcasted_iota(jnp.int32, (1, n_rows), 1)
        d = past - (n * CMP_STRIDE + CMP_BLOCK - 1)
        logit = _dot_nt(qb, kvc_ref[0, g].astype(BF16)) + _head_rows(
            lambda h: _rel_bias(d, rb_ref, g * HEADS_PER_KV + h))
        logit = jnp.where(n < n_cmp, logit, NEG_INF)
        p = jnp.exp(logit - jnp.max(logit, axis=-1, keepdims=True))
        p = p * (1.0 / jnp.sum(p, axis=-1, keepdims=True))
        o_cmp = _dot(p.astype(BF16), kvc_ref[1, g].astype(BF16))
        p_heads = jnp.broadcast_to(jnp.sum(p, axis=0, keepdims=True), (SUBLANE, n_rows))
        p_slc = _dot(p_heads, ov_ref[...], precision=lax.Precision.HIGHEST)[0:1, :]

        score = jnp.where((lane_blk == 0) | (lane_blk == n_blk - 1), jnp.inf, p_slc)
        picked = jnp.zeros((1, LANE), jnp.int32)
        for k in range(SEL_TOPK - 1):
            best = jnp.max(score, axis=-1, keepdims=True)
            ix = jnp.min(jnp.where(score == best, lane_blk, n_blk), axis=-1, keepdims=True)
            picked = jnp.where(lane_out == k, ix, picked)
            score = jnp.where(lane_blk == ix, -jnp.inf, score)
        idx_ref[g:g + 1, :] = picked

        kw = sw_ref[:, g * HEAD_DIM:(g + 1) * HEAD_DIM].astype(BF16)
        vw = sw_ref[:, KV_COLS // 2 + g * HEAD_DIM:KV_COLS // 2 + (g + 1) * HEAD_DIM].astype(BF16)
        dw = win - lax.broadcasted_iota(jnp.int32, (1, win), 1)
        lw = _dot_nt(qb, kw) + _head_rows(lambda h: _rel_bias(dw, rb_ref, g * HEADS_PER_KV + h))
        k_new = new_ref[:, win_col + g * HEAD_DIM:win_col + (g + 1) * HEAD_DIM]
        v_new = new_ref[:, win_col + KV_COLS // 2 + g * HEAD_DIM:win_col + KV_COLS // 2 + (g + 1) * HEAD_DIM]
        l_new = jnp.sum(q * k_new, axis=-1, keepdims=True) + _head_rows(
            lambda h: jnp.full((1, 1), rb_ref[0, g * HEADS_PER_KV + h], F32))
        m = jnp.maximum(jnp.max(lw, axis=-1, keepdims=True), l_new)
        pw = jnp.exp(lw - m)
        p_new = jnp.exp(l_new - m)
        o_win = (_dot(pw.astype(BF16), vw) + p_new * v_new) * (1.0 / (jnp.sum(pw, axis=-1, keepdims=True) + p_new))
        o_ref[heads, :] = gate_ref[0, heads, :] * o_cmp + gate_ref[2, heads, :] * o_win


def sample_cmp_win(rel_bias, q, kvc, state_win, kv_new, gate, past):
    b, _, _, n_rows, _ = kvc.shape
    n_cmp = past // CMP_STRIDE - CMP_BLOCK // CMP_STRIDE + 1
    n_blk = past // SEL_BLOCK
    assert past % SEL_BLOCK == 0 and n_blk >= SEL_TOPK and n_cmp <= n_rows
    cs = np.arange(n_rows)[:, None] * CMP_STRIDE
    ss = np.arange(n_blk)[None, :] * SEL_BLOCK
    ov = np.maximum(np.minimum(cs + CMP_BLOCK, ss + SEL_BLOCK) - np.maximum(cs, ss), 0) / CMP_STRIDE
    ov = jnp.asarray((ov * (np.arange(n_rows) < n_cmp)[:, None]).astype(np.float32))
    win = state_win.shape[1]
    return pl.pallas_call(
        functools.partial(_sample_cmp_win_kernel, past=past, n_cmp=n_cmp), grid=(b,), name="sample_cmp_win",
        in_specs=[pl.BlockSpec(memory_space=pltpu.SMEM),
                  pl.BlockSpec((None, N_HEADS, HEAD_DIM), lambda i: (i, 0, 0)),
                  pl.BlockSpec((None, 2, N_KV_HEADS, n_rows, HEAD_DIM), lambda i: (i, 0, 0, 0, 0)),
                  pl.BlockSpec(ov.shape, lambda i: (0, 0)),
                  pl.BlockSpec((None, win, KV_COLS), lambda i: (i, 0, 0)),
                  pl.BlockSpec((None, 1, 3 * KV_COLS), lambda i: (i, 0, 0)),
                  pl.BlockSpec((None, 3, N_HEADS, 1), lambda i: (i, 0, 0, 0))],
        out_specs=[pl.BlockSpec((None, N_HEADS, HEAD_DIM), lambda i: (i, 0, 0)),
                   pl.BlockSpec((None, N_KV_HEADS, LANE), lambda i: (i, 0, 0))],
        out_shape=[jax.ShapeDtypeStruct((b, N_HEADS, HEAD_DIM), F32),
                   jax.ShapeDtypeStruct((b, N_KV_HEADS, LANE), jnp.int32)],
        compiler_params=_params("parallel"))(rel_bias, q, kvc, ov, state_win, kv_new, gate)


def _sample_slc_kernel(pt_ref, idx_ref, rb_ref, q_ref, new_ref, gate_ref, part_ref, *rest, past):
    blocks, o_ref = rest[:-1], rest[-1]
    b, g = pl.program_id(0), pl.program_id(1)
    q = q_ref[...]
    qb = q.astype(BF16)
    slc_col = KV_COLS
    half = KV_COLS // 2
    pos = lax.broadcasted_iota(jnp.int32, (1, SEL_BLOCK), 1)
    logits, values = [], []
    for k, blk_ref in enumerate(blocks):
        ix = idx_ref[(b * N_KV_HEADS + g) * SEL_TOPK + k]
        kk = jnp.where(g == 0, blk_ref[:, 0:HEAD_DIM], blk_ref[:, HEAD_DIM:2 * HEAD_DIM]).astype(BF16)
        vv = jnp.where(g == 0, blk_ref[:, half:half + HEAD_DIM], blk_ref[:, half + HEAD_DIM:half + 2 * HEAD_DIM])
        d = past - (ix * SEL_BLOCK + pos)
        logits.append(_dot_nt(qb, kk) + _head_rows(lambda h: _rel_bias(d, rb_ref, g * HEADS_PER_KV + h)))
        values.append(vv.astype(BF16))
    new = new_ref[...]
    k_new = jnp.where(g == 0, new[:, slc_col:slc_col + HEAD_DIM], new[:, slc_col + HEAD_DIM:slc_col + 2 * HEAD_DIM])
    v_new = jnp.where(g == 0, new[:, slc_col + half:slc_col + half + HEAD_DIM],
                      new[:, slc_col + half + HEAD_DIM:slc_col + half + 2 * HEAD_DIM])
    l_new = jnp.sum(q * k_new, axis=-1, keepdims=True) + _head_rows(
        lambda h: jnp.full((1, 1), rb_ref[0, g * HEADS_PER_KV + h], F32))
    m = l_new
    for lg in logits:
        m = jnp.maximum(m, jnp.max(lg, axis=-1, keepdims=True))
    p_new = jnp.exp(l_new - m)
    denom = p_new
    out = p_new * v_new
    for lg, vv in zip(logits, values):
        p = jnp.exp(lg - m)
        denom = denom + jnp.sum(p, axis=-1, keepdims=True)
        out = out + _dot(p.astype(BF16), vv)
    o_ref[...] = part_ref[...] + gate_ref[1] * (out * (1.0 / denom))


def sample_slc(page_table, picked, rel_bias, q, kv_new, gate, part, cache_slc, past):
    b = q.shape[0]
    n_pick = SEL_TOPK - 1

    def blk_map(k):
        def index(i, g, pt, idx, k=k):
            ix = idx[(i * N_KV_HEADS + g) * SEL_TOPK + k]
            return (pt[i, ix // SLC_PER_PAGE] * SLC_PER_PAGE + ix % SLC_PER_PAGE, 0, 0)
        return index

    head_spec = pl.BlockSpec((None, HEADS_PER_KV, HEAD_DIM), lambda i, g, pt, idx: (i, g, 0))
    return pl.pallas_call(
        functools.partial(_sample_slc_kernel, past=past), name="sample_slc",
        grid_spec=pltpu.PrefetchScalarGridSpec(
            num_scalar_prefetch=2, grid=(b, N_KV_HEADS),
            in_specs=[pl.BlockSpec(memory_space=pltpu.SMEM), head_spec,
                      pl.BlockSpec((None, 1, 3 * KV_COLS), lambda i, g, pt, idx: (i, 0, 0)),
                      pl.BlockSpec((None, 3, HEADS_PER_KV, 1), lambda i, g, pt, idx: (i, 0, g, 0)), head_spec]
            + [pl.BlockSpec((None, SEL_BLOCK, KV_COLS), blk_map(k)) for k in range(n_pick)],
            out_specs=head_spec),
        out_shape=jax.ShapeDtypeStruct((b, N_HEADS, HEAD_DIM), F32),
        compiler_params=_params("parallel", "parallel"))(
            page_table, picked, rel_bias, q, kv_new, gate, part, *([cache_slc] * n_pick))


def _gmlp_row_kernel(uv_ref, lng_ref, lnb_ref, w0_ref, b0_ref, vn_ref, o_ref):
    v = uv_ref[:, GM_WIDTH:2 * GM_WIDTH]
    mu = jnp.mean(v, axis=-1, keepdims=True)
    var = jnp.mean(jnp.square(v - mu), axis=-1, keepdims=True)
    vn = (v - mu) * lax.rsqrt(var + LN_EPS) * lng_ref[...] + lnb_ref[...]
    vn_ref[...] = vn
    o_ref[...] = (uv_ref[:, 0:GM_WIDTH] * (w0_ref[...] * vn + b0_ref[...])).astype(o_ref.dtype)


def gmlp_first_row(uv, ln_g, ln_b, w_s, b_s):
    m = uv.shape[0]
    w0 = jnp.repeat(w_s[:, 0, 0], GM_GROUP_DIM).reshape(1, GM_WIDTH)
    b0 = jnp.repeat(b_s[:, 0], GM_GROUP_DIM).reshape(1, GM_WIDTH)
    return pl.pallas_call(
        _gmlp_row_kernel, name="gmlp_row",
        out_shape=[jax.ShapeDtypeStruct((m, GM_WIDTH), F32), jax.ShapeDtypeStruct((m, GM_WIDTH), BF16)])(
            uv, ln_g.reshape(1, -1), ln_b.reshape(1, -1), w0, b0)


def _group_major(x, feat):
    b, t = x.shape[0], x.shape[1]
    f = jnp.broadcast_to(feat.astype(BF16)[None, None], (b, N_KV_HEADS, t, FEAT - HEAD_DIM))
    return jnp.concatenate([x.transpose(0, 2, 1, 3).astype(BF16), f], axis=-1)


def _prompt_mixer_a(q, kv, ng, rel_bias, pe, w1, w2, b, t):
    kvr = kv.reshape(b, t, 3, 2, N_KV_HEADS, HEAD_DIM)
    n_chunk = t // CMP_STRIDE
    chunks = kvr[:, :, 0].reshape(b, n_chunk, CMP_STRIDE, 2, N_KV_HEADS, HEAD_DIM).transpose(0, 3, 4, 1, 2, 5)
    kvc = compress_chunks(chunks.reshape(b, 2, N_KV_HEADS, n_chunk, CMP_STRIDE * HEAD_DIM).astype(BF16), w1, w2, pe)
    kvc = jnp.pad(kvc[:, :, :, :n_chunk - 1], ((0, 0), (0, 0), (0, 0), (CMP_PAD, CMP_PAD + 1), (0, 0)))
    kc = jnp.pad(kvc[:, 0], ((0, 0), (0, 0), (0, 0), (0, FEAT - HEAD_DIM)))
    nf = FEAT - HEAD_DIM
    zeros = jnp.zeros((t, nf), BF16)
    ones = zeros.at[:, 0].set(1)
    blocks = (jnp.arange(t)[:, None] // SEL_BLOCK == jnp.arange(nf)[None, :])
    tz, tc = bias_tables(rel_bias)
    return prompt_attention(
        q, ng, kc, kvc[:, 1], tc, tz,
        _group_major(kvr[:, :, 1, 0], blocks), _group_major(kvr[:, :, 1, 1], ones),
        _group_major(kvr[:, :, 2, 0], zeros), _group_major(kvr[:, :, 2, 1], ones), b, t)


def _sample_mixer_a(q, kv, ng, rel_bias, pe, w1, w2, cache_cmp, cache_slc, state_win, page_table):
    b, n_pages = page_table.shape
    page = cache_cmp.shape[1]
    past = n_pages * page
    n_chunk = past // CMP_STRIDE
    rows = cache_cmp[page_table].reshape(b, n_chunk, CMP_STRIDE, 2, N_KV_HEADS, HEAD_DIM).transpose(0, 3, 4, 1, 2, 5)
    kvc = compress_chunks(rows.reshape(b, 2, N_KV_HEADS, n_chunk, CMP_STRIDE * HEAD_DIM).astype(BF16), w1, w2, pe)
    gate = ng.reshape(b, N_KV_HEADS, LANE)[:, :, :3 * HEADS_PER_KV].reshape(b, N_KV_HEADS, 3, HEADS_PER_KV)
    gate = gate.transpose(0, 2, 1, 3).reshape(b, 3, N_HEADS, 1)
    q3 = q.reshape(b, N_HEADS, HEAD_DIM)
    kv_new = kv.reshape(b, 1, 3 * KV_COLS)
    part, picked = sample_cmp_win(rel_bias, q3, kvc, state_win.reshape(b, -1, KV_COLS), kv_new, gate, past)
    picked = picked[:, :, :SEL_TOPK].reshape(-1)
    pool = cache_slc.reshape(cache_slc.shape[0] * (page // SEL_BLOCK), SEL_BLOCK, KV_COLS)
    out = sample_slc(page_table, picked, rel_bias, q3, kv_new, gate, part, pool, past)
    return out.reshape(b, ATTN_WIDTH).astype(BF16)


def kernel(x_prompt, x_sample, cache_cmp_kv, cache_slc_kv, state_win_kv, page_table, rel_bias, norm_mix, w_in, pe_cmp, w_cmp1, w_cmp2, gm_ln_g, gm_ln_b, gm_w_s, gm_b_s, w_br_attn, w_br_gmlp, w_out, norm_ffn, w_router, b_router, w_gate, b_gate, w_up, b_up, w_down, b_down, norm_final):
    b, t, _ = x_prompt.shape
    bs, ts, _ = x_sample.shape
    depth = norm_mix.shape[0]
    assert depth == 1 and ts == 1 and cache_slc_kv.shape[2] == SLC_PER_PAGE * SEL_BLOCK
    n_p, n_s = b * t, bs * ts
    kv_tail = (2, N_KV_HEADS, HEAD_DIM)
    lyr = 0

    xp = x_prompt.reshape(n_p, D_MODEL)
    xs = x_sample.reshape(n_s, D_MODEL)
    weights = _in_weights(w_in[lyr])
    qp, kvp, ngp, uvp, mgp = _project(rmsnorm_rows(xp, norm_mix[lyr], 512), weights, BF16)
    qs, kvs, ngs, uvs, mgs = _project(rmsnorm_rows(xs, norm_mix[lyr], n_s), weights, F32)

    attn_p = _prompt_mixer_a(qp, kvp, ngp, rel_bias, pe_cmp[lyr], w_cmp1[lyr], w_cmp2[lyr], b, t)
    attn_s = _sample_mixer_a(qs, kvs, ngs, rel_bias, pe_cmp[lyr], w_cmp1[lyr], w_cmp2[lyr], cache_cmp_kv[lyr],
                             cache_slc_kv[lyr], state_win_kv[lyr], page_table)

    gm_p = gmlp_chunks(uvp, gm_ln_g[lyr], gm_ln_b[lyr], gm_w_s[lyr], gm_b_s[lyr], 4 * CHUNK)
    vn_s, gm_s = gmlp_first_row(uvs, gm_ln_g[lyr], gm_ln_b[lyr], gm_w_s[lyr], gm_b_s[lyr])

    wa, wb, wo = w_br_attn[lyr].astype(BF16), w_br_gmlp[lyr].astype(BF16), w_out[lyr].astype(BF16)
    hp, hnp, idxp, wtp = merge_and_route(attn_p, gm_p, mgp, xp, wa, wb, wo, norm_ffn[lyr], w_router[lyr], b_router[lyr], 256)
    hs, hns, idxs, wts = merge_and_route(attn_s, gm_s, mgs, xs, wa, wb, wo, norm_ffn[lyr], w_router[lyr], b_router[lyr], 256)

    n_all = n_p + n_s
    n_rows = (n_all * TOP_K + N_EXPERTS * (MOE_TM - 1) + MOE_TM - 1) // MOE_TM * MOE_TM
    top_i = jnp.concatenate([idxp[:, :TOP_K], idxs[:, :TOP_K]], axis=0)
    pos, row_token, tile_expert, tile_first, n_active = route_rows(top_i, n_rows)
    x_rows = jnp.concatenate([hnp, hns], axis=0)[row_token]
    y_rows = moe_experts(x_rows, tile_expert, tile_first, n_active, w_gate[lyr], b_gate[lyr], w_up[lyr], b_up[lyr],
                         w_down[lyr], b_down[lyr])
    y_p = combine_and_norm(hp, y_rows[pos[:n_p].T], wtp, norm_final, 256)
    y_s = combine_and_norm(hs, y_rows[pos[n_p:].T], wts, norm_final, 256)

    kvp5 = kvp.reshape(b, t, 3, *kv_tail)
    kvs5 = kvs.reshape(bs, ts, 3, *kv_tail)
    win_buf = state_win_kv.shape[2]
    win_s = jnp.concatenate([state_win_kv[lyr], kvs5[:, :, 2]], axis=1)[:, ts:]
    return (y_p.reshape(b, t, D_MODEL), y_s.reshape(bs, ts, D_MODEL),
            kvp5[:, :, 0][None], kvs5[:, :, 0][None], kvp5[:, :, 1][None], kvs5[:, :, 1][None],
            kvp5[:, t - min(WINDOW, t):, 2][None], win_s[:, -win_buf:][None],
            vn_s.reshape(1, bs, ts, GM_WIDTH))
```

```python
import functools
import math

import jax
import jax.numpy as jnp
import numpy as np
from jax import lax
from jax.experimental import pallas as pl
from jax.experimental.pallas import tpu as pltpu

D_MODEL = 2048
N_HEADS = 16
HEAD_DIM = 64
N_KV_HEADS = 2
HEADS_PER_KV = N_HEADS // N_KV_HEADS
ATTN_WIDTH = N_HEADS * HEAD_DIM
CMP_BLOCK = 32
CMP_STRIDE = 16
CMP_HID = 2 * HEAD_DIM
SEL_BLOCK = 64
SEL_TOPK = 16
WINDOW = 512
GM_WIDTH = D_MODEL // 2
GM_GROUPS = 8
GM_GROUP_DIM = GM_WIDTH // GM_GROUPS
CHUNK = 128
N_BUCKETS = 32
REL_MAX_DIST = 128
N_EXPERTS = 32
TOP_K = 4
D_FF = D_MODEL
SWIGLU_LIMIT = 7.0
SWIGLU_ALPHA = 1.702
RMS_EPS = 1e-5
LN_EPS = 1e-5
NEG_INF = -1e30
KV_COLS = 2 * N_KV_HEADS * HEAD_DIM
GATE_COLS = 3 * N_HEADS
SLC_PER_PAGE = 2

LANE = 128
SUBLANE = 8
VMEM_LIMIT = 56 * 1024 * 1024

QB = 128
KT = 128
CMP_PAD = QB // CMP_STRIDE
CMP_NEAR = 2 * CMP_PAD
MOE_TM = 256
MOE_TN = 512
FEAT = 2 * HEAD_DIM

BF16 = jnp.bfloat16
F32 = jnp.float32


def _bucket_upper_bounds():
    d = np.arange(0, 4 * REL_MAX_DIST)
    exact = N_BUCKETS // 2
    out = []
    for dt in (np.float32, np.float64):
        far = exact + (np.log(np.maximum(d, exact).astype(dt) / dt(exact)) / dt(math.log(REL_MAX_DIST / exact))
                       * dt(N_BUCKETS - exact)).astype(np.int32)
        out.append(np.where(d < exact, d, np.minimum(far, N_BUCKETS - 1)))
    assert (out[0] == out[1]).all() and (np.diff(out[0]) >= 0).all()
    b = out[0]
    return [int(d[b <= k].max()) for k in range(N_BUCKETS - 1)]


BUCKET_HI = _bucket_upper_bounds()
FAR_DIST = BUCKET_HI[-1] + 1
assert FAR_DIST <= QB


def _rel_bias(d, rb_ref, h):
    acc = jnp.full(d.shape, rb_ref[N_BUCKETS - 1, h], F32)
    for k in reversed(range(N_BUCKETS - 1)):
        acc = jnp.where(d <= BUCKET_HI[k], rb_ref[k, h], acc)
    return acc


def _dot(a, b, **kw):
    return jnp.dot(a, b, preferred_element_type=F32, **kw)


def _dot_nt(a, b):
    return lax.dot_general(a, b, (((1,), (1,)), ((), ())), preferred_element_type=F32)


def _gelu(x):
    return 0.5 * x * (1.0 + lax.erf(x * np.float32(math.sqrt(0.5))))


def _params(*sem):
    return pltpu.CompilerParams(dimension_semantics=sem, vmem_limit_bytes=VMEM_LIMIT)


def _rmsnorm_kernel(x_ref, g_ref, o_ref):
    x = x_ref[...]
    ms = jnp.mean(x * x, axis=-1, keepdims=True)
    o_ref[...] = (x * lax.rsqrt(ms + RMS_EPS) * g_ref[...]).astype(o_ref.dtype)


def rmsnorm_rows(x, g, tm):
    m, d = x.shape
    return pl.pallas_call(
        _rmsnorm_kernel, grid=(m // tm,), name="rmsnorm",
        in_specs=[pl.BlockSpec((tm, d), lambda i: (i, 0)), pl.BlockSpec((1, d), lambda i: (0, 0))],
        out_specs=pl.BlockSpec((tm, d), lambda i: (i, 0)),
        out_shape=jax.ShapeDtypeStruct((m, d), BF16), compiler_params=_params("parallel"))(x, g.reshape(1, d))


def _mm_kernel(a_ref, w_ref, o_ref, *, epilogue):
    o_ref[...] = epilogue(_dot(a_ref[...], w_ref[...])).astype(o_ref.dtype)


def matmul(a, w, epilogue, out_dtype, tm, tn, name):
    m, k = a.shape
    n = w.shape[1]
    tm, tn = min(tm, m), min(tn, n)
    return pl.pallas_call(
        functools.partial(_mm_kernel, epilogue=epilogue), grid=(m // tm, n // tn), name=name,
        in_specs=[pl.BlockSpec((tm, k), lambda i, j: (i, 0)), pl.BlockSpec((k, tn), lambda i, j: (0, j))],
        out_specs=pl.BlockSpec((tm, tn), lambda i, j: (i, j)),
        out_shape=jax.ShapeDtypeStruct((m, n), out_dtype), compiler_params=_params("parallel", "parallel"))(a, w)


def _compress_kernel(c_ref, w1_ref, w2_ref, pe_ref, o_ref):
    n = c_ref.shape[0]
    half = CMP_STRIDE * HEAD_DIM
    c = c_ref[...]
    first = _dot(c, w1_ref[0:half, :])
    second = _dot(c, w1_ref[half:2 * half, :])
    pe = _dot(pe_ref[...].astype(BF16), w1_ref[...])[0:1, :]
    hid = _gelu(first + pltpu.roll(second, n - 1, 0) + pe)
    o_ref[...] = _dot(hid.astype(BF16), w2_ref[...])


def compress_chunks(chunks, w1, w2, pe):
    b, _, g, n, width = chunks.shape
    pe_rows = jnp.broadcast_to(pe.reshape(2, 1, CMP_BLOCK * HEAD_DIM), (2, SUBLANE, CMP_BLOCK * HEAD_DIM))
    return pl.pallas_call(
        _compress_kernel, grid=(b, 2, g), name="compress",
        in_specs=[pl.BlockSpec((None, None, None, n, width), lambda i, s, j: (i, s, j, 0, 0)),
                  pl.BlockSpec((None, CMP_BLOCK * HEAD_DIM, CMP_HID), lambda i, s, j: (s, 0, 0)),
                  pl.BlockSpec((None, CMP_HID, HEAD_DIM), lambda i, s, j: (s, 0, 0)),
                  pl.BlockSpec((None, SUBLANE, CMP_BLOCK * HEAD_DIM), lambda i, s, j: (s, 0, 0))],
        out_specs=pl.BlockSpec((None, None, None, n, HEAD_DIM), lambda i, s, j: (i, s, j, 0, 0)),
        out_shape=jax.ShapeDtypeStruct((b, 2, g, n, HEAD_DIM), F32),
        compiler_params=_params("parallel", "parallel", "parallel"))(chunks, w1.astype(BF16), w2.astype(BF16), pe_rows)


def _bias_tables_kernel(rb_ref, tz_ref, tc_ref):
    i = lax.broadcasted_iota(jnp.int32, (QB, KT), 0)
    j = lax.broadcasted_iota(jnp.int32, (QB, KT), 1)
    ic = lax.broadcasted_iota(jnp.int32, (QB, CMP_NEAR), 0)
    cc = lax.broadcasted_iota(jnp.int32, (QB, CMP_NEAR), 1)
    d0 = i - j
    d1 = d0 + KT
    dc = ic + (QB - CMP_BLOCK + 1) - CMP_STRIDE * cc
    for h in range(N_HEADS):
        last = rb_ref[N_BUCKETS - 1, h]
        rows = slice(h * QB, (h + 1) * QB)
        tz_ref[0, rows, :] = jnp.where(d0 < 0, NEG_INF, _rel_bias(d0, rb_ref, h) - last)
        tz_ref[1, rows, :] = _rel_bias(d1, rb_ref, h) - last
        tc_ref[rows, :] = jnp.where(dc < 0, NEG_INF, _rel_bias(dc, rb_ref, h) - last)


def bias_tables(rel_bias):
    return pl.pallas_call(
        _bias_tables_kernel, name="bias_tables",
        in_specs=[pl.BlockSpec(memory_space=pltpu.SMEM)],
        out_specs=[pl.BlockSpec(memory_space=pltpu.VMEM), pl.BlockSpec(memory_space=pltpu.VMEM)],
        out_shape=[jax.ShapeDtypeStruct((2, N_HEADS * QB, KT), F32),
                   jax.ShapeDtypeStruct((N_HEADS * QB, CMP_NEAR), F32)])(rel_bias)


def _prompt_attn_kernel(q_ref, gate_ref, kc_ref, vc_ref, ov_ref, tc_ref, tz_ref, ks_ref, vs_ref, kw_ref, vw_ref,
                        o_ref, qa_ref, m_ref, acc_ref, *, n_chunk):
    qi = pl.program_id(2)
    rows_all = HEADS_PER_KV * QB

    for h in range(HEADS_PER_KV):
        qa_ref[h * QB:(h + 1) * QB, 0:HEAD_DIM] = q_ref[:, h * HEAD_DIM:(h + 1) * HEAD_DIM]
        qa_ref[h * QB:(h + 1) * QB, HEAD_DIM:FEAT] = jnp.zeros((QB, FEAT - HEAD_DIM), BF16)
    qa = qa_ref[...]
    i_row = lax.broadcasted_iota(jnp.int32, (rows_all, 1), 0) % QB

    near0 = pl.multiple_of(qi * CMP_PAD, SUBLANE)
    r = lax.broadcasted_iota(jnp.int32, (1, n_chunk), 1)
    lf = _dot_nt(qa, kc_ref[0:n_chunk, :].astype(BF16))
    lf = jnp.where((r >= CMP_PAD) & (r < qi * CMP_PAD), lf, NEG_INF)
    c = lax.broadcasted_iota(jnp.int32, (1, CMP_NEAR), 1)
    ln = _dot_nt(qa, kc_ref[pl.ds(near0, CMP_NEAR), :].astype(BF16)) + tc_ref[...]
    ln = jnp.where(qi * CMP_PAD + c >= CMP_PAD, ln, NEG_INF)
    m = jnp.maximum(jnp.max(lf, axis=-1, keepdims=True), jnp.max(ln, axis=-1, keepdims=True))
    pf = jnp.exp(lf - m)
    pn = jnp.exp(ln - m)
    denom = jnp.sum(pf, axis=-1, keepdims=True) + jnp.sum(pn, axis=-1, keepdims=True)
    inv = jnp.where(qi * QB + i_row >= CMP_BLOCK - 1, 1.0 / denom, 0.0)
    pf = pf * inv
    pn = pn * inv
    o_cmp = (_dot(pf.astype(BF16), vc_ref[0:n_chunk, :].astype(BF16))
             + _dot(pn.astype(BF16), vc_ref[pl.ds(near0, CMP_NEAR), :].astype(BF16)))
    psf = jnp.sum(pf.reshape(HEADS_PER_KV, QB, n_chunk), axis=0)
    psn = jnp.sum(pn.reshape(HEADS_PER_KV, QB, CMP_NEAR), axis=0)
    p_slc = (_dot(psf, ov_ref[0:n_chunk, :], precision=lax.Precision.HIGHEST)
             + _dot(psn, ov_ref[pl.ds(near0, CMP_NEAR), :], precision=lax.Precision.HIGHEST))

    nf = FEAT - HEAD_DIM
    blk = lax.broadcasted_iota(jnp.int32, (QB, nf), 1)
    jt = (qi * QB + lax.broadcasted_iota(jnp.int32, (QB, nf), 0)) // SEL_BLOCK
    forced = (blk == 0) | (blk == jt) | (blk == jt - 1)
    score = jnp.where(forced, jnp.inf, jnp.where(blk <= jt, p_slc, -jnp.inf))
    rank = jnp.zeros((QB, nf), jnp.int32)
    for col in range(nf):
        sc = score[:, col:col + 1]
        rank = rank + jnp.where(blk > col, (sc >= score).astype(jnp.int32), (sc > score).astype(jnp.int32))
    feat = jnp.where((rank < SEL_TOPK) & (score > NEG_INF), 0.0, NEG_INF).astype(BF16)
    for h in range(HEADS_PER_KV):
        qa_ref[h * QB:(h + 1) * QB, HEAD_DIM:FEAT] = feat
    qa = qa_ref[...]

    def reset():
        m_ref[...] = jnp.full(m_ref.shape, -3e38, F32)
        acc_ref[...] = jnp.zeros(acc_ref.shape, F32)

    def step(k_ref, v_ref, j, bias):
        off = pl.multiple_of(j * KT, KT)
        s = _dot_nt(qa, k_ref[pl.ds(off, KT), :])
        if bias is not None:
            s = s + bias
        m_old = m_ref[...]
        m_new = jnp.maximum(m_old, jnp.max(s, axis=-1, keepdims=True))
        p = jnp.exp(s - m_new)
        acc_ref[...] = jnp.exp(m_old - m_new) * acc_ref[...] + _dot(p.astype(BF16), v_ref[pl.ds(off, KT), :])
        m_ref[...] = m_new

    def result():
        acc = acc_ref[...]
        return acc[:, 0:HEAD_DIM] * (1.0 / acc[:, HEAD_DIM:HEAD_DIM + 1])

    reset()

    def far_step(j, carry):
        step(ks_ref, vs_ref, j, None)
        return carry

    lax.fori_loop(0, jnp.maximum(qi - 1, 0), far_step, 0)

    @pl.when(qi >= 1)
    def _():
        step(ks_ref, vs_ref, qi - 1, tz_ref[1])

    step(ks_ref, vs_ref, qi, tz_ref[0])
    o_slc = result()

    reset()
    n_back = WINDOW // KT
    edge = jnp.where(lax.broadcasted_iota(jnp.int32, (rows_all, KT), 1) >= i_row, 0.0, NEG_INF)
    for back in range(n_back, -1, -1):
        bias = edge if back == n_back else tz_ref[1] if back == 1 else tz_ref[0] if back == 0 else None

        @pl.when(qi >= back)
        def _(back=back, bias=bias):
            step(kw_ref, vw_ref, qi - back, bias)

    o_win = result()

    gate = gate_ref[...]
    for h in range(HEADS_PER_KV):
        rows = slice(h * QB, (h + 1) * QB)
        o = (gate[:, h:h + 1] * o_cmp[rows] + gate[:, HEADS_PER_KV + h:HEADS_PER_KV + h + 1] * o_slc[rows]
             + gate[:, 2 * HEADS_PER_KV + h:2 * HEADS_PER_KV + h + 1] * o_win[rows])
        o_ref[:, h * HEAD_DIM:(h + 1) * HEAD_DIM] = o.astype(o_ref.dtype)


def _sel_overlap_rows(n_rows, n_cmp):
    nf = FEAT - HEAD_DIM
    cs = (np.arange(n_rows) - CMP_PAD)[:, None] * CMP_STRIDE
    ss = np.arange(nf)[None, :] * SEL_BLOCK
    ov = np.minimum(cs + CMP_BLOCK, ss + SEL_BLOCK) - np.maximum(cs, ss)
    ov = np.maximum(ov, 0) / CMP_STRIDE
    valid = (np.arange(n_rows) >= CMP_PAD) & (np.arange(n_rows) < CMP_PAD + n_cmp)
    return (ov * valid[:, None]).astype(np.float32)


def prompt_attention(q, gate, kc, vc, tc, tz, ks, vs, kw, vw, b, t):
    n_chunk = t // CMP_STRIDE
    rc = n_chunk + 2 * CMP_PAD
    assert t // SEL_BLOCK <= FEAT - HEAD_DIM and t % QB == 0 and n_chunk % LANE == 0
    ov = jnp.asarray(_sel_overlap_rows(rc, n_chunk - 1))
    nq = t // QB
    gw = HEADS_PER_KV * HEAD_DIM
    kv_spec = pl.BlockSpec((None, None, t, FEAT), lambda i, g, qi: (i, g, 0, 0))
    return pl.pallas_call(
        functools.partial(_prompt_attn_kernel, n_chunk=n_chunk), grid=(b, N_KV_HEADS, nq), name="prompt_attention",
        in_specs=[pl.BlockSpec((QB, gw), lambda i, g, qi: (i * nq + qi, g)),
                  pl.BlockSpec((QB, LANE), lambda i, g, qi: (i * nq + qi, g)),
                  pl.BlockSpec((None, None, rc, FEAT), lambda i, g, qi: (i, g, 0, 0)),
                  pl.BlockSpec((None, None, rc, HEAD_DIM), lambda i, g, qi: (i, g, 0, 0)),
                  pl.BlockSpec((rc, FEAT - HEAD_DIM), lambda i, g, qi: (0, 0)),
                  pl.BlockSpec((HEADS_PER_KV * QB, CMP_NEAR), lambda i, g, qi: (g, 0)),
                  pl.BlockSpec((2, HEADS_PER_KV * QB, KT), lambda i, g, qi: (0, g, 0)),
                  kv_spec, kv_spec, kv_spec, kv_spec],
        out_specs=pl.BlockSpec((QB, gw), lambda i, g, qi: (i * nq + qi, g)),
        out_shape=jax.ShapeDtypeStruct((b * t, ATTN_WIDTH), BF16),
        scratch_shapes=[pltpu.VMEM((HEADS_PER_KV * QB, FEAT), BF16),
                        pltpu.VMEM((HEADS_PER_KV * QB, LANE), F32),
                        pltpu.VMEM((HEADS_PER_KV * QB, LANE), F32)],
        compiler_params=_params("parallel", "parallel", "arbitrary"))(q, gate, kc, vc, ov, tc, tz, ks, vs, kw, vw)


def _gmlp_kernel(u_ref, v_ref, lng_ref, lnb_ref, ws_ref, bs_ref, o_ref):
    v = v_ref[...].astype(F32)
    mu = jnp.mean(v, axis=-1, keepdims=True)
    var = jnp.mean(jnp.square(v - mu), axis=-1, keepdims=True)
    vn = ((v - mu) * lax.rsqrt(var + LN_EPS) * lng_ref[...] + lnb_ref[...]).astype(BF16)
    tri = (lax.broadcasted_iota(jnp.int32, (CHUNK, CHUNK), 0) >= lax.broadcasted_iota(jnp.int32, (CHUNK, CHUNK), 1))
    for g in range(GM_GROUPS):
        w = jnp.where(tri, ws_ref[g], 0.0).astype(BF16)
        cols = slice(g * GM_GROUP_DIM, (g + 1) * GM_GROUP_DIM)
        for c in range(u_ref.shape[0] // CHUNK):
            rows = slice(c * CHUNK, (c + 1) * CHUNK)
            s = _dot(w, vn[rows, cols]) + bs_ref[:, g:g + 1]
            o_ref[rows, cols] = (u_ref[rows, cols].astype(F32) * s).astype(o_ref.dtype)


def gmlp_chunks(uv, ln_g, ln_b, w_s, b_s, rows):
    m = uv.shape[0]
    return pl.pallas_call(
        _gmlp_kernel, grid=(m // rows,), name="gmlp",
        in_specs=[pl.BlockSpec((rows, GM_WIDTH), lambda i: (i, 0)), pl.BlockSpec((rows, GM_WIDTH), lambda i: (i, 1)),
                  pl.BlockSpec((1, GM_WIDTH), lambda i: (0, 0)), pl.BlockSpec((1, GM_WIDTH), lambda i: (0, 0)),
                  pl.BlockSpec((GM_GROUPS, CHUNK, CHUNK), lambda i: (0, 0, 0)),
                  pl.BlockSpec((CHUNK, GM_GROUPS), lambda i: (0, 0))],
        out_specs=pl.BlockSpec((rows, GM_WIDTH), lambda i: (i, 0)),
        out_shape=jax.ShapeDtypeStruct((m, GM_WIDTH), BF16),
        compiler_params=_params("parallel"))(uv, uv, ln_g.reshape(1, -1), ln_b.reshape(1, -1), w_s, b_s.T)


def _merge_kernel(attn_ref, gm_ref, mg_ref, x_ref, wa_ref, wb_ref, wo_ref, gn_ref, wr_ref, br_ref,
                  h_ref, hn_ref, idx_ref, wt_ref):
    ta = _dot(attn_ref[...], wa_ref[...])
    tb = _dot(gm_ref[...], wb_ref[...])
    merged = mg_ref[:, 0:D_MODEL].astype(F32) * ta + mg_ref[:, D_MODEL:2 * D_MODEL].astype(F32) * tb
    h = x_ref[...] + _dot(merged.astype(BF16), wo_ref[...])
    h_ref[...] = h
    hn = h * lax.rsqrt(jnp.mean(h * h, axis=-1, keepdims=True) + RMS_EPS) * gn_ref[...]
    hn_ref[...] = hn.astype(hn_ref.dtype)
    logits = _dot(hn, wr_ref[...], precision=lax.Precision.HIGHEST) + br_ref[...]
    lane = lax.broadcasted_iota(jnp.int32, logits.shape, 1)
    vals, idxs = [], []
    for _ in range(TOP_K):
        best = jnp.max(logits, axis=-1, keepdims=True)
        ix = jnp.min(jnp.where(logits == best, lane, LANE), axis=-1, keepdims=True)
        vals.append(best)
        idxs.append(ix)
        logits = jnp.where(lane == ix, -jnp.inf, logits)
    ex = [jnp.exp(v - vals[0]) for v in vals]
    inv = 1.0 / functools.reduce(lambda a, b: a + b, ex)
    idx_out = jnp.zeros(lane.shape, jnp.int32)
    wt_out = jnp.zeros(lane.shape, F32)
    for k in range(TOP_K):
        idx_out = jnp.where(lane == k, idxs[k], idx_out)
        wt_out = jnp.where(lane == k, ex[k] * inv, wt_out)
    idx_ref[...] = idx_out
    wt_ref[...] = wt_out


def merge_and_route(attn, gm, mg, x, wa, wb, wo, g_ffn, w_router, b_router, tm):
    m = x.shape[0]
    tm = min(tm, m)
    wr = jnp.zeros((D_MODEL, LANE), F32).at[:, :N_EXPERTS].set(w_router)
    br = jnp.full((1, LANE), -jnp.inf, F32).at[0, :N_EXPERTS].set(b_router)
    row = lambda w: pl.BlockSpec((tm, w), lambda i: (i, 0))
    full = lambda a: pl.BlockSpec(a.shape, lambda i: (0, 0))
    return pl.pallas_call(
        _merge_kernel, grid=(m // tm,), name="merge_route",
        in_specs=[row(ATTN_WIDTH), row(GM_WIDTH), row(2 * D_MODEL), row(D_MODEL), full(wa), full(wb), full(wo),
                  pl.BlockSpec((1, D_MODEL), lambda i: (0, 0)), full(wr), full(br)],
        out_specs=[row(D_MODEL), row(D_MODEL), row(LANE), row(LANE)],
        out_shape=[jax.ShapeDtypeStruct((m, D_MODEL), F32), jax.ShapeDtypeStruct((m, D_MODEL), BF16),
                   jax.ShapeDtypeStruct((m, LANE), jnp.int32), jax.ShapeDtypeStruct((m, LANE), F32)],
        compiler_params=_params("parallel"))(attn, gm, mg, x, wa, wb, wo, g_ffn.reshape(1, -1), wr, br)


def _cast_weight(src_ref, dst_ref):
    step = 256
    for r0 in range(0, src_ref.shape[0], step):
        dst_ref[r0:r0 + step, :] = src_ref[r0:r0 + step, :].astype(dst_ref.dtype)


def _moe_up_kernel(te_ref, first_ref, nact_ref, x_ref, wg_ref, wu_ref, bg_ref, bu_ref, o_ref, wgb_ref, wub_ref):
    r = pl.program_id(1)

    @pl.when(first_ref[r] == 1)
    def _():
        _cast_weight(wg_ref, wgb_ref)
        _cast_weight(wu_ref, wub_ref)

    @pl.when(r < nact_ref[0])
    def _():
        x = x_ref[...]
        g = jnp.minimum(_dot(x, wgb_ref[...]) + bg_ref[...], SWIGLU_LIMIT)
        u = jnp.clip(_dot(x, wub_ref[...]) + bu_ref[...], -SWIGLU_LIMIT, SWIGLU_LIMIT)
        o_ref[...] = ((u + 1.0) * (g * jax.nn.sigmoid(SWIGLU_ALPHA * g))).astype(o_ref.dtype)


def _moe_down_kernel(te_ref, first_ref, nact_ref, a_ref, wd_ref, bd_ref, o_ref, wdb_ref):
    r = pl.program_id(1)

    @pl.when(first_ref[r] == 1)
    def _():
        _cast_weight(wd_ref, wdb_ref)

    @pl.when(r < nact_ref[0])
    def _():
        o_ref[...] = (_dot(a_ref[...], wdb_ref[...]) + bd_ref[...]).astype(o_ref.dtype)


def moe_experts(x_rows, tile_expert, tile_first, n_active, w_gate, b_gate, w_up, b_up, w_down, b_down):
    rows = x_rows.shape[0]
    n_tiles = rows // MOE_TM
    row_map = lambda j, r, te, first, nact: (jnp.minimum(r, nact[0] - 1), 0)
    out_map = lambda j, r, te, first, nact: (jnp.minimum(r, nact[0] - 1), j)
    w_map = lambda j, r, te, first, nact: (te[r], 0, j)
    w_spec = pl.BlockSpec((None, D_MODEL, MOE_TN), w_map)
    b_spec = pl.BlockSpec((None, 1, MOE_TN), w_map)
    act = pl.pallas_call(
        _moe_up_kernel, name="moe_up",
        grid_spec=pltpu.PrefetchScalarGridSpec(
            num_scalar_prefetch=3, grid=(D_FF // MOE_TN, n_tiles),
            in_specs=[pl.BlockSpec((MOE_TM, D_MODEL), row_map), w_spec, w_spec, b_spec, b_spec],
            out_specs=pl.BlockSpec((MOE_TM, MOE_TN), out_map),
            scratch_shapes=[pltpu.VMEM((D_MODEL, MOE_TN), BF16), pltpu.VMEM((D_MODEL, MOE_TN), BF16)]),
        out_shape=jax.ShapeDtypeStruct((rows, D_FF), BF16),
        compiler_params=_params("arbitrary", "arbitrary"))(
            tile_expert, tile_first, n_active, x_rows, w_gate, w_up,
            b_gate.reshape(N_EXPERTS, 1, D_FF), b_up.reshape(N_EXPERTS, 1, D_FF))
    return pl.pallas_call(
        _moe_down_kernel, name="moe_down",
        grid_spec=pltpu.PrefetchScalarGridSpec(
            num_scalar_prefetch=3, grid=(D_MODEL // MOE_TN, n_tiles),
            in_specs=[pl.BlockSpec((MOE_TM, D_FF), row_map), w_spec, b_spec],
            out_specs=pl.BlockSpec((MOE_TM, MOE_TN), out_map),
            scratch_shapes=[pltpu.VMEM((D_FF, MOE_TN), BF16)]),
        out_shape=jax.ShapeDtypeStruct((rows, D_MODEL), BF16),
        compiler_params=_params("arbitrary", "arbitrary"))(
            tile_expert, tile_first, n_active, act, w_down, b_down.reshape(N_EXPERTS, 1, D_MODEL))


def route_rows(top_i, n_rows):
    n = top_i.shape[0]
    hit = top_i[:, :, None] == jnp.arange(N_EXPERTS, dtype=jnp.int32)[None, None, :]
    onehot = hit.astype(jnp.int32).sum(axis=1)
    before = jnp.cumsum(onehot, axis=0) - onehot
    counts = onehot.sum(axis=0)
    padded = (counts + MOE_TM - 1) // MOE_TM * MOE_TM
    ends = jnp.cumsum(padded)
    starts = ends - padded
    pos = jnp.sum(jnp.where(hit, (starts[None, :] + before)[:, None, :], 0), axis=-1)
    row_pair = jnp.zeros((n_rows,), jnp.int32).at[pos.reshape(-1)].set(
        jnp.arange(n * TOP_K, dtype=jnp.int32), unique_indices=True)
    n_tiles = n_rows // MOE_TM
    n_active = (ends[-1] // MOE_TM).astype(jnp.int32)
    tile_row = jnp.minimum(jnp.arange(n_tiles, dtype=jnp.int32), n_active - 1) * MOE_TM
    tile_expert = jnp.minimum((tile_row[:, None] >= ends[None, :]).astype(jnp.int32).sum(axis=-1), N_EXPERTS - 1)
    tile_first = jnp.concatenate([jnp.ones((1,), jnp.int32), (tile_expert[1:] != tile_expert[:-1]).astype(jnp.int32)])
    return pos, row_pair, tile_expert, tile_first, n_active.reshape(1)


def _combine_kernel(h_ref, y_ref, w_ref, g_ref, o_ref):
    h = h_ref[...]
    w = w_ref[...]
    for k in range(TOP_K):
        h = h + w[:, k:k + 1] * y_ref[k].astype(F32)
    o_ref[...] = h * lax.rsqrt(jnp.mean(h * h, axis=-1, keepdims=True) + RMS_EPS) * g_ref[...]


def combine_and_norm(h, y4, wts, g, tm):
    m = h.shape[0]
    tm = min(tm, m)
    return pl.pallas_call(
        _combine_kernel, grid=(m // tm,), name="combine_norm",
        in_specs=[pl.BlockSpec((tm, D_MODEL), lambda i: (i, 0)), pl.BlockSpec((TOP_K, tm, D_MODEL), lambda i: (0, i, 0)),
                  pl.BlockSpec((tm, LANE), lambda i: (i, 0)), pl.BlockSpec((1, D_MODEL), lambda i: (0, 0))],
        out_specs=pl.BlockSpec((tm, D_MODEL), lambda i: (i, 0)),
        out_shape=jax.ShapeDtypeStruct((m, D_MODEL), F32), compiler_params=_params("parallel"))(
            h, y4, wts, g.reshape(1, -1))


def _in_weights(w_in):
    c0 = ATTN_WIDTH
    c1 = c0 + 3 * KV_COLS
    c2 = c1 + GATE_COLS
    c3 = c2 + 2 * GM_WIDTH
    wg = w_in[:, c1:c2].reshape(D_MODEL, N_KV_HEADS, HEADS_PER_KV, 3).transpose(0, 1, 3, 2)
    wg = jnp.pad(wg.reshape(D_MODEL, N_KV_HEADS, 3 * HEADS_PER_KV), ((0, 0), (0, 0), (0, LANE - 3 * HEADS_PER_KV)))
    cast = lambda w: w.astype(BF16)
    return (cast(w_in[:, :c0]), cast(w_in[:, c0:c1]), cast(wg.reshape(D_MODEL, N_KV_HEADS * LANE)),
            cast(w_in[:, c2:c3]), cast(w_in[:, c3:]))


def _project(a, weights, wide_dtype):
    w_q, w_kv, w_ng, w_uv, w_mg = weights
    q = matmul(a, w_q, lambda z: z * np.float32(HEAD_DIM ** -0.5), wide_dtype, 1024, 1024, "proj_q")
    kv = matmul(a, w_kv, lambda z: z, F32, 1024, 3 * KV_COLS, "proj_kv")
    ng = matmul(a, w_ng, jax.nn.sigmoid, F32, 1024, N_KV_HEADS * LANE, "proj_gate")
    uv = matmul(a, w_uv, _gelu, wide_dtype, 1024, 1024, "proj_uv")
    mg = matmul(a, w_mg, jax.nn.sigmoid, BF16, 1024, 1024, "proj_merge")
    return q, kv, ng, uv, mg


def _head_rows(fn):
    out = fn(0)
    out = jnp.broadcast_to(out, (HEADS_PER_KV, out.shape[1]))
    row = lax.broadcasted_iota(jnp.int32, out.shape, 0)
    for h in range(1, HEADS_PER_KV):
        out = jnp.where(row == h, fn(h), out)
    return out


def _sample_cmp_win_kernel(rb_ref, q_ref, kvc_ref, ov_ref, sw_ref, new_ref, gate_ref, o_ref, idx_ref, *, past, n_cmp):
    n_rows = kvc_ref.shape[2]
    n_blk = ov_ref.shape[1]
    win = sw_ref.shape[-1]
    win_col = 2 * KV_COLS
    lane_blk = lax.broadcasted_iota(jnp.int32, (1, n_blk), 1)
    lane_out = lax.broadcasted_iota(jnp.int32, (1, LANE), 1)
    for g in range(N_KV_HEADS):
        heads = slice(g * HEADS_PER_KV, (g + 1) * HEADS_PER_KV)
        q = q_ref[heads, :]
        qb = q.astype(BF16)

        n = lax.broadcasted_iota(jnp.int32, (1, n_rows), 1)
        d = past - (n * CMP_STRIDE + CMP_BLOCK - 1)
        logit = _dot_nt(qb, kvc_ref[0, g].astype(BF16)) + _head_rows(
            lambda h: _rel_bias(d, rb_ref, g * HEADS_PER_KV + h))
        logit = jnp.where(n < n_cmp, logit, NEG_INF)
        p = jnp.exp(logit - jnp.max(logit, axis=-1, keepdims=True))
        p = p * (1.0 / jnp.sum(p, axis=-1, keepdims=True))
        o_cmp = _dot(p.astype(BF16), kvc_ref[1, g].astype(BF16))
        p_heads = jnp.broadcast_to(jnp.sum(p, axis=0, keepdims=True), (SUBLANE, n_rows))
        p_slc = _dot(p_heads, ov_ref[...], precision=lax.Precision.HIGHEST)[0:1, :]

        score = jnp.where((lane_blk == 0) | (lane_blk == n_blk - 1), jnp.inf, p_slc)
        picked = jnp.zeros((1, LANE), jnp.int32)
        for k in range(SEL_TOPK - 1):
            best = jnp.max(score, axis=-1, keepdims=True)
            ix = jnp.min(jnp.where(score == best, lane_blk, n_blk), axis=-1, keepdims=True)
            picked = jnp.where(lane_out == k, ix, picked)
            score = jnp.where(lane_blk == ix, -jnp.inf, score)
        idx_ref[g:g + 1, :] = picked

        dw = win - lax.broadcasted_iota(jnp.int32, (1, win), 1)
        lw = _dot(qb, sw_ref[0, g].astype(BF16)) + _head_rows(
            lambda h: _rel_bias(dw, rb_ref, g * HEADS_PER_KV + h))
        k_new = new_ref[:, win_col + g * HEAD_DIM:win_col + (g + 1) * HEAD_DIM]
        v_new = new_ref[:, win_col + KV_COLS // 2 + g * HEAD_DIM:win_col + KV_COLS // 2 + (g + 1) * HEAD_DIM]
        l_new = jnp.sum(q * k_new, axis=-1, keepdims=True) + _head_rows(
            lambda h: jnp.full((1, 1), rb_ref[0, g * HEADS_PER_KV + h], F32))
        m = jnp.maximum(jnp.max(lw, axis=-1, keepdims=True), l_new)
        pw = jnp.exp(lw - m)
        p_new = jnp.exp(l_new - m)
        o_win = ((_dot_nt(pw.astype(BF16), sw_ref[1, g].astype(BF16)) + p_new * v_new)
                 * (1.0 / (jnp.sum(pw, axis=-1, keepdims=True) + p_new)))
        o_ref[heads, :] = gate_ref[0, heads, :] * o_cmp + gate_ref[2, heads, :] * o_win


def sample_cmp_win(rel_bias, q, kvc, state_win, kv_new, gate, past):
    b, _, _, n_rows, _ = kvc.shape
    n_cmp = past // CMP_STRIDE - CMP_BLOCK // CMP_STRIDE + 1
    n_blk = past // SEL_BLOCK
    assert past % SEL_BLOCK == 0 and n_blk >= SEL_TOPK and n_cmp <= n_rows
    cs = np.arange(n_rows)[:, None] * CMP_STRIDE
    ss = np.arange(n_blk)[None, :] * SEL_BLOCK
    ov = np.maximum(np.minimum(cs + CMP_BLOCK, ss + SEL_BLOCK) - np.maximum(cs, ss), 0) / CMP_STRIDE
    ov = jnp.asarray((ov * (np.arange(n_rows) < n_cmp)[:, None]).astype(np.float32))
    win = state_win.shape[-1]
    return pl.pallas_call(
        functools.partial(_sample_cmp_win_kernel, past=past, n_cmp=n_cmp), grid=(b,), name="sample_cmp_win",
        in_specs=[pl.BlockSpec(memory_space=pltpu.SMEM),
                  pl.BlockSpec((None, N_HEADS, HEAD_DIM), lambda i: (i, 0, 0)),
                  pl.BlockSpec((None, 2, N_KV_HEADS, n_rows, HEAD_DIM), lambda i: (i, 0, 0, 0, 0)),
                  pl.BlockSpec(ov.shape, lambda i: (0, 0)),
                  pl.BlockSpec((None, 2, N_KV_HEADS, HEAD_DIM, win), lambda i: (i, 0, 0, 0, 0)),
                  pl.BlockSpec((None, 1, 3 * KV_COLS), lambda i: (i, 0, 0)),
                  pl.BlockSpec((None, 3, N_HEADS, 1), lambda i: (i, 0, 0, 0))],
        out_specs=[pl.BlockSpec((None, N_HEADS, HEAD_DIM), lambda i: (i, 0, 0)),
                   pl.BlockSpec((None, N_KV_HEADS, LANE), lambda i: (i, 0, 0))],
        out_shape=[jax.ShapeDtypeStruct((b, N_HEADS, HEAD_DIM), F32),
                   jax.ShapeDtypeStruct((b, N_KV_HEADS, LANE), jnp.int32)],
        compiler_params=_params("parallel"))(rel_bias, q, kvc, ov, state_win, kv_new, gate)


def _sample_slc_kernel(pt_ref, idx_ref, rb_ref, q_ref, new_ref, gate_ref, part_ref, *rest, past):
    blocks, o_ref = rest[:-1], rest[-1]
    b, g = pl.program_id(0), pl.program_id(1)
    q = q_ref[...]
    qb = q.astype(BF16)
    slc_col = KV_COLS
    half = KV_COLS // 2
    page = blocks[0].shape[-1]
    pos = lax.broadcasted_iota(jnp.int32, (1, page), 1)
    logits, values = [], []
    for k, blk_ref in enumerate(blocks):
        ix = idx_ref[(b * N_KV_HEADS + g) * SEL_TOPK + k]
        d = past - ((ix // SLC_PER_PAGE) * page + pos)
        lg = _dot(qb, blk_ref[0].astype(BF16)) + _head_rows(lambda h: _rel_bias(d, rb_ref, g * HEADS_PER_KV + h))
        logits.append(jnp.where(pos // SEL_BLOCK == ix % SLC_PER_PAGE, lg, NEG_INF))
        values.append(blk_ref[1].astype(BF16))
    new = new_ref[...]
    k_new = jnp.where(g == 0, new[:, slc_col:slc_col + HEAD_DIM], new[:, slc_col + HEAD_DIM:slc_col + 2 * HEAD_DIM])
    v_new = jnp.where(g == 0, new[:, slc_col + half:slc_col + half + HEAD_DIM],
                      new[:, slc_col + half + HEAD_DIM:slc_col + half + 2 * HEAD_DIM])
    l_new = jnp.sum(q * k_new, axis=-1, keepdims=True) + _head_rows(
        lambda h: jnp.full((1, 1), rb_ref[0, g * HEADS_PER_KV + h], F32))
    m = l_new
    for lg in logits:
        m = jnp.maximum(m, jnp.max(lg, axis=-1, keepdims=True))
    p_new = jnp.exp(l_new - m)
    denom = p_new
    out = p_new * v_new
    for lg, vv in zip(logits, values):
        p = jnp.exp(lg - m)
        denom = denom + jnp.sum(p, axis=-1, keepdims=True)
        out = out + _dot_nt(p.astype(BF16), vv)
    o_ref[...] = part_ref[...] + gate_ref[1] * (out * (1.0 / denom))


def sample_slc(page_table, picked, rel_bias, q, kv_new, gate, part, cache_slc, past):
    b = q.shape[0]
    page = cache_slc.shape[-1]
    n_pick = SEL_TOPK - 1

    def blk_map(k):
        def index(i, g, pt, idx, k=k):
            ix = idx[(i * N_KV_HEADS + g) * SEL_TOPK + k]
            return (pt[i, ix // SLC_PER_PAGE], 0, g, 0, 0)
        return index

    head_spec = pl.BlockSpec((None, HEADS_PER_KV, HEAD_DIM), lambda i, g, pt, idx: (i, g, 0))
    return pl.pallas_call(
        functools.partial(_sample_slc_kernel, past=past), name="sample_slc",
        grid_spec=pltpu.PrefetchScalarGridSpec(
            num_scalar_prefetch=2, grid=(b, N_KV_HEADS),
            in_specs=[pl.BlockSpec(memory_space=pltpu.SMEM), head_spec,
                      pl.BlockSpec((None, 1, 3 * KV_COLS), lambda i, g, pt, idx: (i, 0, 0)),
                      pl.BlockSpec((None, 3, HEADS_PER_KV, 1), lambda i, g, pt, idx: (i, 0, g, 0)), head_spec]
            + [pl.BlockSpec((None, 2, None, HEAD_DIM, page), blk_map(k)) for k in range(n_pick)],
            out_specs=head_spec),
        out_shape=jax.ShapeDtypeStruct((b, N_HEADS, HEAD_DIM), F32),
        compiler_params=_params("parallel", "parallel"))(
            page_table, picked, rel_bias, q, kv_new, gate, part, *([cache_slc] * n_pick))


def _gmlp_row_kernel(uv_ref, lng_ref, lnb_ref, w0_ref, b0_ref, vn_ref, o_ref):
    v = uv_ref[:, GM_WIDTH:2 * GM_WIDTH]
    mu = jnp.mean(v, axis=-1, keepdims=True)
    var = jnp.mean(jnp.square(v - mu), axis=-1, keepdims=True)
    vn = (v - mu) * lax.rsqrt(var + LN_EPS) * lng_ref[...] + lnb_ref[...]
    vn_ref[...] = vn
    o_ref[...] = (uv_ref[:, 0:GM_WIDTH] * (w0_ref[...] * vn + b0_ref[...])).astype(o_ref.dtype)


def gmlp_first_row(uv, ln_g, ln_b, w_s, b_s):
    m = uv.shape[0]
    w0 = jnp.repeat(w_s[:, 0, 0], GM_GROUP_DIM).reshape(1, GM_WIDTH)
    b0 = jnp.repeat(b_s[:, 0], GM_GROUP_DIM).reshape(1, GM_WIDTH)
    return pl.pallas_call(
        _gmlp_row_kernel, name="gmlp_row",
        out_shape=[jax.ShapeDtypeStruct((m, GM_WIDTH), F32), jax.ShapeDtypeStruct((m, GM_WIDTH), BF16)])(
            uv, ln_g.reshape(1, -1), ln_b.reshape(1, -1), w0, b0)


def _group_major(x, feat):
    b, t = x.shape[0], x.shape[1]
    f = jnp.broadcast_to(feat.astype(BF16)[None, None], (b, N_KV_HEADS, t, FEAT - HEAD_DIM))
    return jnp.concatenate([x.transpose(0, 2, 1, 3).astype(BF16), f], axis=-1)


def _prompt_mixer_a(q, kv, ng, rel_bias, pe, w1, w2, b, t):
    kvr = kv.reshape(b, t, 3, 2, N_KV_HEADS, HEAD_DIM)
    n_chunk = t // CMP_STRIDE
    chunks = kvr[:, :, 0].reshape(b, n_chunk, CMP_STRIDE, 2, N_KV_HEADS, HEAD_DIM).transpose(0, 3, 4, 1, 2, 5)
    kvc = compress_chunks(chunks.reshape(b, 2, N_KV_HEADS, n_chunk, CMP_STRIDE * HEAD_DIM).astype(BF16), w1, w2, pe)
    kvc = jnp.pad(kvc[:, :, :, :n_chunk - 1], ((0, 0), (0, 0), (0, 0), (CMP_PAD, CMP_PAD + 1), (0, 0)))
    kc = jnp.pad(kvc[:, 0], ((0, 0), (0, 0), (0, 0), (0, FEAT - HEAD_DIM)))
    nf = FEAT - HEAD_DIM
    zeros = jnp.zeros((t, nf), BF16)
    ones = zeros.at[:, 0].set(1)
    blocks = (jnp.arange(t)[:, None] // SEL_BLOCK == jnp.arange(nf)[None, :])
    tz, tc = bias_tables(rel_bias)
    return prompt_attention(
        q, ng, kc, kvc[:, 1], tc, tz,
        _group_major(kvr[:, :, 1, 0], blocks), _group_major(kvr[:, :, 1, 1], ones),
        _group_major(kvr[:, :, 2, 0], zeros), _group_major(kvr[:, :, 2, 1], ones), b, t)


def _sample_mixer_a(q, kv, ng, rel_bias, pe, w1, w2, cache_cmp, cache_slc, state_win, page_table):
    b, n_pages = page_table.shape
    page = cache_cmp.shape[1]
    past = n_pages * page
    n_chunk = past // CMP_STRIDE
    rows = cache_cmp[page_table].reshape(b, n_chunk, CMP_STRIDE, 2, N_KV_HEADS, HEAD_DIM).transpose(0, 3, 4, 1, 2, 5)
    kvc = compress_chunks(rows.reshape(b, 2, N_KV_HEADS, n_chunk, CMP_STRIDE * HEAD_DIM).astype(BF16), w1, w2, pe)
    gate = ng.reshape(b, N_KV_HEADS, LANE)[:, :, :3 * HEADS_PER_KV].reshape(b, N_KV_HEADS, 3, HEADS_PER_KV)
    gate = gate.transpose(0, 2, 1, 3).reshape(b, 3, N_HEADS, 1)
    q3 = q.reshape(b, N_HEADS, HEAD_DIM)
    kv_new = kv.reshape(b, 1, 3 * KV_COLS)
    part, picked = sample_cmp_win(rel_bias, q3, kvc, state_win.transpose(0, 2, 3, 4, 1), kv_new, gate, past)
    picked = picked[:, :, :SEL_TOPK].reshape(-1)
    out = sample_slc(page_table, picked, rel_bias, q3, kv_new, gate, part, cache_slc.transpose(0, 2, 3, 4, 1), past)
    return out.reshape(b, ATTN_WIDTH).astype(BF16)


def kernel(x_prompt, x_sample, cache_cmp_kv, cache_slc_kv, state_win_kv, page_table, rel_bias, norm_mix, w_in, pe_cmp, w_cmp1, w_cmp2, gm_ln_g, gm_ln_b, gm_w_s, gm_b_s, w_br_attn, w_br_gmlp, w_out, norm_ffn, w_router, b_router, w_gate, b_gate, w_up, b_up, w_down, b_down, norm_final):
    b, t, _ = x_prompt.shape
    bs, ts, _ = x_sample.shape
    depth = norm_mix.shape[0]
    assert depth == 1 and ts == 1 and cache_slc_kv.shape[2] == SLC_PER_PAGE * SEL_BLOCK
    n_p, n_s = b * t, bs * ts
    kv_tail = (2, N_KV_HEADS, HEAD_DIM)
    lyr = 0

    xp = x_prompt.reshape(n_p, D_MODEL)
    xs = x_sample.reshape(n_s, D_MODEL)
    weights = _in_weights(w_in[lyr])
    qp, kvp, ngp, uvp, mgp = _project(rmsnorm_rows(xp, norm_mix[lyr], 512), weights, BF16)
    qs, kvs, ngs, uvs, mgs = _project(rmsnorm_rows(xs, norm_mix[lyr], n_s), weights, F32)

    attn_p = _prompt_mixer_a(qp, kvp, ngp, rel_bias, pe_cmp[lyr], w_cmp1[lyr], w_cmp2[lyr], b, t)
    attn_s = _sample_mixer_a(qs, kvs, ngs, rel_bias, pe_cmp[lyr], w_cmp1[lyr], w_cmp2[lyr], cache_cmp_kv[lyr],
                             cache_slc_kv[lyr], state_win_kv[lyr], page_table)

    gm_p = gmlp_chunks(uvp, gm_ln_g[lyr], gm_ln_b[lyr], gm_w_s[lyr], gm_b_s[lyr], 4 * CHUNK)
    vn_s, gm_s = gmlp_first_row(uvs, gm_ln_g[lyr], gm_ln_b[lyr], gm_w_s[lyr], gm_b_s[lyr])

    wa, wb, wo = w_br_attn[lyr].astype(BF16), w_br_gmlp[lyr].astype(BF16), w_out[lyr].astype(BF16)
    hp, hnp, idxp, wtp = merge_and_route(attn_p, gm_p, mgp, xp, wa, wb, wo, norm_ffn[lyr], w_router[lyr], b_router[lyr], 256)
    hs, hns, idxs, wts = merge_and_route(attn_s, gm_s, mgs, xs, wa, wb, wo, norm_ffn[lyr], w_router[lyr], b_router[lyr], 256)

    n_all = n_p + n_s
    n_rows = (n_all * TOP_K + N_EXPERTS * (MOE_TM - 1) + MOE_TM - 1) // MOE_TM * MOE_TM
    top_i = jnp.concatenate([idxp[:, :TOP_K], idxs[:, :TOP_K]], axis=0)
    pos, row_pair, tile_expert, tile_first, n_active = route_rows(top_i, n_rows)
    pairs = jnp.repeat(jnp.concatenate([hnp, hns], axis=0), TOP_K, axis=0)
    x_rows = jnp.pad(pairs, ((0, n_rows - n_all * TOP_K), (0, 0)))[row_pair]
    y_rows = moe_experts(x_rows, tile_expert, tile_first, n_active, w_gate[lyr], b_gate[lyr], w_up[lyr], b_up[lyr],
                         w_down[lyr], b_down[lyr])
    y_p = combine_and_norm(hp, y_rows[pos[:n_p].T], wtp, norm_final, 256)
    y_s = combine_and_norm(hs, y_rows[pos[n_p:].T], wts, norm_final, 256)

    kvp5 = kvp.reshape(b, t, 3, *kv_tail)
    kvs5 = kvs.reshape(bs, ts, 3, *kv_tail)
    win_buf = state_win_kv.shape[2]
    win_s = jnp.concatenate([state_win_kv[lyr], kvs5[:, :, 2]], axis=1)[:, ts:]
    return (y_p.reshape(b, t, D_MODEL), y_s.reshape(bs, ts, D_MODEL),
            kvp5[:, :, 0][None], kvs5[:, :, 0][None], kvp5[:, :, 1][None], kvs5[:, :, 1][None],
            kvp5[:, t - min(WINDOW, t):, 2][None], win_s[:, -win_buf:][None],
            vn_s.reshape(1, bs, ts, GM_WIDTH))
```

```python
import functools
import math

import jax
import jax.numpy as jnp
import numpy as np
from jax import lax
from jax.experimental import pallas as pl
from jax.experimental.pallas import tpu as pltpu

D_MODEL = 2048
N_HEADS = 16
HEAD_DIM = 64
N_KV_HEADS = 2
HEADS_PER_KV = N_HEADS // N_KV_HEADS
ATTN_WIDTH = N_HEADS * HEAD_DIM
CMP_BLOCK = 32
CMP_STRIDE = 16
CMP_HID = 2 * HEAD_DIM
SEL_BLOCK = 64
SEL_TOPK = 16
WINDOW = 512
GM_WIDTH = D_MODEL // 2
GM_GROUPS = 8
GM_GROUP_DIM = GM_WIDTH // GM_GROUPS
CHUNK = 128
N_BUCKETS = 32
REL_MAX_DIST = 128
N_EXPERTS = 32
TOP_K = 4
D_FF = D_MODEL
SWIGLU_LIMIT = 7.0
SWIGLU_ALPHA = 1.702
RMS_EPS = 1e-5
LN_EPS = 1e-5
NEG_INF = -1e30
KV_COLS = 2 * N_KV_HEADS * HEAD_DIM
GATE_COLS = 3 * N_HEADS
SLC_PER_PAGE = 2

LANE = 128
SUBLANE = 8
VMEM_LIMIT = 56 * 1024 * 1024

QB = 128
KT = 128
CMP_PAD = QB // CMP_STRIDE
CMP_NEAR = 2 * CMP_PAD
MOE_TM = 256
MOE_TN = 512
FEAT = 2 * HEAD_DIM

BF16 = jnp.bfloat16
F32 = jnp.float32


def _bucket_upper_bounds():
    d = np.arange(0, 4 * REL_MAX_DIST)
    exact = N_BUCKETS // 2
    out = []
    for dt in (np.float32, np.float64):
        far = exact + (np.log(np.maximum(d, exact).astype(dt) / dt(exact)) / dt(math.log(REL_MAX_DIST / exact))
                       * dt(N_BUCKETS - exact)).astype(np.int32)
        out.append(np.where(d < exact, d, np.minimum(far, N_BUCKETS - 1)))
    assert (out[0] == out[1]).all() and (np.diff(out[0]) >= 0).all()
    b = out[0]
    return [int(d[b <= k].max()) for k in range(N_BUCKETS - 1)]


BUCKET_HI = _bucket_upper_bounds()
FAR_DIST = BUCKET_HI[-1] + 1
assert FAR_DIST <= QB


def _rel_bias(d, rb_ref, h):
    acc = jnp.full(d.shape, rb_ref[N_BUCKETS - 1, h], F32)
    for k in reversed(range(N_BUCKETS - 1)):
        acc = jnp.where(d <= BUCKET_HI[k], rb_ref[k, h], acc)
    return acc


def _dot(a, b, **kw):
    return jnp.dot(a, b, preferred_element_type=F32, **kw)


def _dot_nt(a, b):
    return lax.dot_general(a, b, (((1,), (1,)), ((), ())), preferred_element_type=F32)


def _gelu(x):
    return 0.5 * x * (1.0 + lax.erf(x * np.float32(math.sqrt(0.5))))


def _params(*sem):
    return pltpu.CompilerParams(dimension_semantics=sem, vmem_limit_bytes=VMEM_LIMIT)


def _rmsnorm_kernel(x_ref, g_ref, o_ref):
    x = x_ref[...]
    ms = jnp.mean(x * x, axis=-1, keepdims=True)
    o_ref[...] = (x * lax.rsqrt(ms + RMS_EPS) * g_ref[...]).astype(o_ref.dtype)


def rmsnorm_rows(x, g, tm):
    m, d = x.shape
    return pl.pallas_call(
        _rmsnorm_kernel, grid=(m // tm,), name="rmsnorm",
        in_specs=[pl.BlockSpec((tm, d), lambda i: (i, 0)), pl.BlockSpec((1, d), lambda i: (0, 0))],
        out_specs=pl.BlockSpec((tm, d), lambda i: (i, 0)),
        out_shape=jax.ShapeDtypeStruct((m, d), BF16), compiler_params=_params("parallel"))(x, g.reshape(1, d))


def _mm_kernel(a_ref, w_ref, o_ref, *, epilogue):
    o_ref[...] = epilogue(_dot(a_ref[...], w_ref[...])).astype(o_ref.dtype)


def matmul(a, w, epilogue, out_dtype, tm, tn, name):
    m, k = a.shape
    n = w.shape[1]
    tm, tn = min(tm, m), min(tn, n)
    return pl.pallas_call(
        functools.partial(_mm_kernel, epilogue=epilogue), grid=(m // tm, n // tn), name=name,
        in_specs=[pl.BlockSpec((tm, k), lambda i, j: (i, 0)), pl.BlockSpec((k, tn), lambda i, j: (0, j))],
        out_specs=pl.BlockSpec((tm, tn), lambda i, j: (i, j)),
        out_shape=jax.ShapeDtypeStruct((m, n), out_dtype), compiler_params=_params("parallel", "parallel"))(a, w)


def _compress_kernel(c_ref, w1_ref, w2_ref, pe_ref, o_ref):
    n = c_ref.shape[0]
    half = CMP_STRIDE * HEAD_DIM
    c = c_ref[...]
    first = _dot(c, w1_ref[0:half, :])
    second = _dot(c, w1_ref[half:2 * half, :])
    pe = _dot(pe_ref[...].astype(BF16), w1_ref[...])[0:1, :]
    hid = _gelu(first + pltpu.roll(second, n - 1, 0) + pe)
    o_ref[...] = _dot(hid.astype(BF16), w2_ref[...])


def compress_chunks(chunks, w1, w2, pe):
    b, _, g, n, width = chunks.shape
    pe_rows = jnp.broadcast_to(pe.reshape(2, 1, CMP_BLOCK * HEAD_DIM), (2, SUBLANE, CMP_BLOCK * HEAD_DIM))
    return pl.pallas_call(
        _compress_kernel, grid=(b, 2, g), name="compress",
        in_specs=[pl.BlockSpec((None, None, None, n, width), lambda i, s, j: (i, s, j, 0, 0)),
                  pl.BlockSpec((None, CMP_BLOCK * HEAD_DIM, CMP_HID), lambda i, s, j: (s, 0, 0)),
                  pl.BlockSpec((None, CMP_HID, HEAD_DIM), lambda i, s, j: (s, 0, 0)),
                  pl.BlockSpec((None, SUBLANE, CMP_BLOCK * HEAD_DIM), lambda i, s, j: (s, 0, 0))],
        out_specs=pl.BlockSpec((None, None, None, n, HEAD_DIM), lambda i, s, j: (i, s, j, 0, 0)),
        out_shape=jax.ShapeDtypeStruct((b, 2, g, n, HEAD_DIM), F32),
        compiler_params=_params("parallel", "parallel", "parallel"))(chunks, w1.astype(BF16), w2.astype(BF16), pe_rows)


def _bias_tables_kernel(rb_ref, tz_ref, tc_ref):
    i = lax.broadcasted_iota(jnp.int32, (QB, KT), 0)
    j = lax.broadcasted_iota(jnp.int32, (QB, KT), 1)
    ic = lax.broadcasted_iota(jnp.int32, (QB, CMP_NEAR), 0)
    cc = lax.broadcasted_iota(jnp.int32, (QB, CMP_NEAR), 1)
    d0 = i - j
    d1 = d0 + KT
    dc = ic + (QB - CMP_BLOCK + 1) - CMP_STRIDE * cc
    for h in range(N_HEADS):
        last = rb_ref[N_BUCKETS - 1, h]
        rows = slice(h * QB, (h + 1) * QB)
        tz_ref[0, rows, :] = jnp.where(d0 < 0, NEG_INF, _rel_bias(d0, rb_ref, h) - last)
        tz_ref[1, rows, :] = _rel_bias(d1, rb_ref, h) - last
        tc_ref[rows, :] = jnp.where(dc < 0, NEG_INF, _rel_bias(dc, rb_ref, h) - last)


def bias_tables(rel_bias):
    return pl.pallas_call(
        _bias_tables_kernel, name="bias_tables",
        in_specs=[pl.BlockSpec(memory_space=pltpu.SMEM)],
        out_specs=[pl.BlockSpec(memory_space=pltpu.VMEM), pl.BlockSpec(memory_space=pltpu.VMEM)],
        out_shape=[jax.ShapeDtypeStruct((2, N_HEADS * QB, KT), F32),
                   jax.ShapeDtypeStruct((N_HEADS * QB, CMP_NEAR), F32)])(rel_bias)


def _prompt_attn_kernel(q_ref, gate_ref, kc_ref, vc_ref, ov_ref, tc_ref, tz_ref, ks_ref, vs_ref, kw_ref, vw_ref,
                        o_ref, qa_ref, m_ref, acc_ref, *, n_chunk):
    qi = pl.program_id(2)
    rows_all = HEADS_PER_KV * QB

    for h in range(HEADS_PER_KV):
        qa_ref[h * QB:(h + 1) * QB, 0:HEAD_DIM] = q_ref[:, h * HEAD_DIM:(h + 1) * HEAD_DIM]
        qa_ref[h * QB:(h + 1) * QB, HEAD_DIM:FEAT] = jnp.zeros((QB, FEAT - HEAD_DIM), BF16)
    qa = qa_ref[...]
    i_row = lax.broadcasted_iota(jnp.int32, (rows_all, 1), 0) % QB

    near0 = pl.multiple_of(qi * CMP_PAD, SUBLANE)
    r = lax.broadcasted_iota(jnp.int32, (1, n_chunk), 1)
    lf = _dot_nt(qa, kc_ref[0:n_chunk, :].astype(BF16))
    lf = jnp.where((r >= CMP_PAD) & (r < qi * CMP_PAD), lf, NEG_INF)
    c = lax.broadcasted_iota(jnp.int32, (1, CMP_NEAR), 1)
    ln = _dot_nt(qa, kc_ref[pl.ds(near0, CMP_NEAR), :].astype(BF16)) + tc_ref[...]
    ln = jnp.where(qi * CMP_PAD + c >= CMP_PAD, ln, NEG_INF)
    m = jnp.maximum(jnp.max(lf, axis=-1, keepdims=True), jnp.max(ln, axis=-1, keepdims=True))
    pf = jnp.exp(lf - m)
    pn = jnp.exp(ln - m)
    denom = jnp.sum(pf, axis=-1, keepdims=True) + jnp.sum(pn, axis=-1, keepdims=True)
    inv = jnp.where(qi * QB + i_row >= CMP_BLOCK - 1, 1.0 / denom, 0.0)
    pf = pf * inv
    pn = pn * inv
    o_cmp = (_dot(pf.astype(BF16), vc_ref[0:n_chunk, :].astype(BF16))
             + _dot(pn.astype(BF16), vc_ref[pl.ds(near0, CMP_NEAR), :].astype(BF16)))
    psf = jnp.sum(pf.reshape(HEADS_PER_KV, QB, n_chunk), axis=0)
    psn = jnp.sum(pn.reshape(HEADS_PER_KV, QB, CMP_NEAR), axis=0)
    p_slc = (_dot(psf, ov_ref[0:n_chunk, :], precision=lax.Precision.HIGHEST)
             + _dot(psn, ov_ref[pl.ds(near0, CMP_NEAR), :], precision=lax.Precision.HIGHEST))

    nf = FEAT - HEAD_DIM
    blk = lax.broadcasted_iota(jnp.int32, (QB, nf), 1)
    jt = (qi * QB + lax.broadcasted_iota(jnp.int32, (QB, nf), 0)) // SEL_BLOCK
    forced = (blk == 0) | (blk == jt) | (blk == jt - 1)
    score = jnp.where(forced, jnp.inf, jnp.where(blk <= jt, p_slc, -jnp.inf))
    rank = jnp.zeros((QB, nf), jnp.int32)
    for col in range(nf):
        sc = score[:, col:col + 1]
        rank = rank + jnp.where(blk > col, (sc >= score).astype(jnp.int32), (sc > score).astype(jnp.int32))
    feat = jnp.where((rank < SEL_TOPK) & (score > NEG_INF), 0.0, NEG_INF).astype(BF16)
    for h in range(HEADS_PER_KV):
        qa_ref[h * QB:(h + 1) * QB, HEAD_DIM:FEAT] = feat
    qa = qa_ref[...]

    def reset():
        m_ref[...] = jnp.full(m_ref.shape, -3e38, F32)
        acc_ref[...] = jnp.zeros(acc_ref.shape, F32)

    def step(k_ref, v_ref, j, bias):
        off = pl.multiple_of(j * KT, KT)
        s = _dot_nt(qa, k_ref[pl.ds(off, KT), :])
        if bias is not None:
            s = s + bias
        m_old = m_ref[...]
        m_new = jnp.maximum(m_old, jnp.max(s, axis=-1, keepdims=True))
        p = jnp.exp(s - m_new)
        acc_ref[...] = jnp.exp(m_old - m_new) * acc_ref[...] + _dot(p.astype(BF16), v_ref[pl.ds(off, KT), :])
        m_ref[...] = m_new

    def result():
        acc = acc_ref[...]
        return acc[:, 0:HEAD_DIM] * (1.0 / acc[:, HEAD_DIM:HEAD_DIM + 1])

    reset()

    def far_step(j, carry):
        step(ks_ref, vs_ref, j, None)
        return carry

    lax.fori_loop(0, jnp.maximum(qi - 1, 0), far_step, 0)

    @pl.when(qi >= 1)
    def _():
        step(ks_ref, vs_ref, qi - 1, tz_ref[1])

    step(ks_ref, vs_ref, qi, tz_ref[0])
    o_slc = result()

    reset()
    n_back = WINDOW // KT
    edge = jnp.where(lax.broadcasted_iota(jnp.int32, (rows_all, KT), 1) >= i_row, 0.0, NEG_INF)
    for back in range(n_back, -1, -1):
        bias = edge if back == n_back else tz_ref[1] if back == 1 else tz_ref[0] if back == 0 else None

        @pl.when(qi >= back)
        def _(back=back, bias=bias):
            step(kw_ref, vw_ref, qi - back, bias)

    o_win = result()

    gate = gate_ref[...]
    for h in range(HEADS_PER_KV):
        rows = slice(h * QB, (h + 1) * QB)
        o = (gate[:, h:h + 1] * o_cmp[rows] + gate[:, HEADS_PER_KV + h:HEADS_PER_KV + h + 1] * o_slc[rows]
             + gate[:, 2 * HEADS_PER_KV + h:2 * HEADS_PER_KV + h + 1] * o_win[rows])
        o_ref[:, h * HEAD_DIM:(h + 1) * HEAD_DIM] = o.astype(o_ref.dtype)


def _sel_overlap_rows(n_rows, n_cmp, pad):
    nf = FEAT - HEAD_DIM
    cs = (np.arange(n_rows) - pad)[:, None] * CMP_STRIDE
    ss = np.arange(nf)[None, :] * SEL_BLOCK
    ov = np.minimum(cs + CMP_BLOCK, ss + SEL_BLOCK) - np.maximum(cs, ss)
    ov = np.maximum(ov, 0) / CMP_STRIDE
    valid = (np.arange(n_rows) >= pad) & (np.arange(n_rows) < pad + n_cmp)
    return (ov * valid[:, None]).astype(np.float32)


def prompt_attention(q, gate, kc, vc, tc, tz, ks, vs, kw, vw, b, t):
    n_chunk = t // CMP_STRIDE
    rc = n_chunk + 2 * CMP_PAD
    assert t // SEL_BLOCK <= FEAT - HEAD_DIM and t % QB == 0 and n_chunk % LANE == 0
    ov = jnp.asarray(_sel_overlap_rows(rc, n_chunk - 1, CMP_PAD))
    nq = t // QB
    gw = HEADS_PER_KV * HEAD_DIM
    kv_spec = pl.BlockSpec((None, None, t, FEAT), lambda i, g, qi: (i, g, 0, 0))
    return pl.pallas_call(
        functools.partial(_prompt_attn_kernel, n_chunk=n_chunk), grid=(b, N_KV_HEADS, nq), name="prompt_attention",
        in_specs=[pl.BlockSpec((QB, gw), lambda i, g, qi: (i * nq + qi, g)),
                  pl.BlockSpec((QB, LANE), lambda i, g, qi: (i * nq + qi, g)),
                  pl.BlockSpec((None, None, rc, FEAT), lambda i, g, qi: (i, g, 0, 0)),
                  pl.BlockSpec((None, None, rc, HEAD_DIM), lambda i, g, qi: (i, g, 0, 0)),
                  pl.BlockSpec((rc, FEAT - HEAD_DIM), lambda i, g, qi: (0, 0)),
                  pl.BlockSpec((HEADS_PER_KV * QB, CMP_NEAR), lambda i, g, qi: (g, 0)),
                  pl.BlockSpec((2, HEADS_PER_KV * QB, KT), lambda i, g, qi: (0, g, 0)),
                  kv_spec, kv_spec, kv_spec, kv_spec],
        out_specs=pl.BlockSpec((QB, gw), lambda i, g, qi: (i * nq + qi, g)),
        out_shape=jax.ShapeDtypeStruct((b * t, ATTN_WIDTH), BF16),
        scratch_shapes=[pltpu.VMEM((HEADS_PER_KV * QB, FEAT), BF16),
                        pltpu.VMEM((HEADS_PER_KV * QB, LANE), F32),
                        pltpu.VMEM((HEADS_PER_KV * QB, LANE), F32)],
        compiler_params=_params("parallel", "parallel", "arbitrary"))(q, gate, kc, vc, ov, tc, tz, ks, vs, kw, vw)


TQ = 256
T_PAD = TQ // CMP_STRIDE
T_NEAR = 2 * T_PAD
V_ROWS = HEAD_DIM + 16
assert FAR_DIST <= TQ and WINDOW % TQ == 0


def _bias_tables_t_kernel(rb_ref, tz_ref, tc_ref):
    j = lax.broadcasted_iota(jnp.int32, (TQ, TQ), 0)
    i = lax.broadcasted_iota(jnp.int32, (TQ, TQ), 1)
    cc = lax.broadcasted_iota(jnp.int32, (T_NEAR, TQ), 0)
    ic = lax.broadcasted_iota(jnp.int32, (T_NEAR, TQ), 1)
    d0 = i - j
    d1 = d0 + TQ
    dc = ic + (TQ - CMP_BLOCK + 1) - CMP_STRIDE * cc
    for h in range(N_HEADS):
        last = rb_ref[N_BUCKETS - 1, h]
        tz_ref[0, h] = jnp.where(d0 < 0, NEG_INF, _rel_bias(d0, rb_ref, h) - last)
        tz_ref[1, h] = _rel_bias(d1, rb_ref, h) - last
        tc_ref[h] = jnp.where(dc < 0, NEG_INF, _rel_bias(dc, rb_ref, h) - last)


def bias_tables_t(rel_bias):
    return pl.pallas_call(
        _bias_tables_t_kernel, name="bias_tables",
        in_specs=[pl.BlockSpec(memory_space=pltpu.SMEM)],
        out_specs=[pl.BlockSpec(memory_space=pltpu.VMEM), pl.BlockSpec(memory_space=pltpu.VMEM)],
        out_shape=[jax.ShapeDtypeStruct((2, N_HEADS, TQ, TQ), F32),
                   jax.ShapeDtypeStruct((N_HEADS, T_NEAR, TQ), F32)],
        compiler_params=pltpu.CompilerParams(vmem_limit_bytes=VMEM_LIMIT))(rel_bias)


def _prompt_attn_t_kernel(qt_ref, gate_ref, kc_ref, vct_ref, ovt_ref, tc_ref, tz_ref, ks_ref, vst_ref, kw_ref, vwt_ref,
                          o_ref, qa_ref, lc_ref, m_ref, acc_ref, ot_ref):
    qi = pl.program_id(2)
    nf = FEAT - HEAD_DIM
    hp = HEADS_PER_KV
    for h in range(hp):
        qa_ref[h, 0:HEAD_DIM, :] = qt_ref[h * HEAD_DIM:(h + 1) * HEAD_DIM, :]
        qa_ref[h, HEAD_DIM:FEAT, :] = jnp.zeros((nf, TQ), BF16)
    t_lane = qi * TQ + lax.broadcasted_iota(jnp.int32, (1, TQ), 1)

    near0 = pl.multiple_of(qi * T_PAD, T_PAD)
    rc = kc_ref.shape[0]
    row = lax.broadcasted_iota(jnp.int32, (rc, 1), 0)
    row_ok = (row >= T_PAD) & (row < near0 + T_NEAR)
    has_block = t_lane >= CMP_BLOCK - 1
    kc = kc_ref[...].astype(BF16)
    vct = vct_ref[...].astype(BF16)
    for h in range(hp):
        lc_ref[h] = _dot(kc, qa_ref[h])
    for h in range(hp):
        lc_ref[h, pl.ds(near0, T_NEAR), :] = lc_ref[h, pl.ds(near0, T_NEAR), :] + tc_ref[h]
    p_heads = jnp.zeros((rc, TQ), F32)
    for h in range(hp):
        lc = jnp.where(row_ok, lc_ref[h], NEG_INF)
        p = jnp.exp(lc - jnp.max(lc, axis=0, keepdims=True))
        p = p * jnp.where(has_block, 1.0 / jnp.sum(p, axis=0, keepdims=True), 0.0)
        p_heads = p_heads + p
        ot_ref[h * HEAD_DIM:(h + 1) * HEAD_DIM, :] = gate_ref[h:h + 1, :] * _dot(vct, p.astype(BF16))
    p_slc = _dot(ovt_ref[...], p_heads, precision=lax.Precision.HIGHEST)

    blk = lax.broadcasted_iota(jnp.int32, (nf, TQ), 0)
    jt = t_lane // SEL_BLOCK
    forced = (blk == 0) | (blk == jt) | (blk == jt - 1)
    score = jnp.where(forced, jnp.inf, jnp.where(blk <= jt, p_slc, -jnp.inf))
    n_grp = nf // SUBLANE
    groups = [score[SUBLANE * g:SUBLANE * (g + 1), :] for g in range(n_grp)]
    ranks = [jnp.zeros((SUBLANE, TQ), jnp.int32) for _ in range(n_grp)]
    sub = lax.broadcasted_iota(jnp.int32, (SUBLANE, TQ), 0)
    for c in range(nf):
        cg, cs = divmod(c, SUBLANE)
        other = jnp.broadcast_to(groups[cg][cs:cs + 1, :], (SUBLANE, TQ))
        for g in range(n_grp):
            if g > cg:
                beats = (other >= groups[g]).astype(jnp.int32)
            elif g < cg:
                beats = (other > groups[g]).astype(jnp.int32)
            else:
                beats = jnp.where(sub > cs, (other >= groups[g]).astype(jnp.int32), (other > groups[g]).astype(jnp.int32))
            ranks[g] = ranks[g] + beats
    feat = jnp.concatenate(
        [jnp.where((ranks[g] < SEL_TOPK) & (groups[g] > NEG_INF), 0.0, NEG_INF) for g in range(n_grp)], axis=0)
    feat = feat.astype(BF16)
    for h in range(hp):
        qa_ref[h, HEAD_DIM:FEAT, :] = feat

    def reset():
        m_ref[...] = jnp.full(m_ref.shape, -3e38, F32)
        acc_ref[...] = jnp.zeros(acc_ref.shape, F32)

    def step(k_ref, vt_ref, j, bias):
        k = k_ref[pl.ds(pl.multiple_of(j * TQ, TQ), TQ), :]
        vt = vt_ref[j]
        m_old = [m_ref[h] for h in range(hp)]
        acc_old = [acc_ref[h] for h in range(hp)]
        s = [_dot(k, qa_ref[h]) for h in range(hp)]
        if bias is not None:
            s = [s[h] + bias(h) for h in range(hp)]
        m_new = [jnp.maximum(m_old[h], jnp.max(s[h], axis=0, keepdims=True)) for h in range(hp)]
        pv = [_dot(vt, jnp.exp(s[h] - m_new[h]).astype(BF16)) for h in range(hp)]
        for h in range(hp):
            acc_ref[h] = jnp.exp(m_old[h] - m_new[h]) * acc_old[h] + pv[h]
            m_ref[h] = m_new[h]

    def add_result(branch):
        for h in range(hp):
            acc = acc_ref[h]
            o = acc[0:HEAD_DIM, :] * (1.0 / acc[HEAD_DIM:HEAD_DIM + 1, :])
            rows = slice(h * HEAD_DIM, (h + 1) * HEAD_DIM)
            ot_ref[rows, :] = ot_ref[rows, :] + gate_ref[branch * hp + h:branch * hp + h + 1, :] * o

    diag = lambda h: tz_ref[0, h]
    prev = lambda h: tz_ref[1, h]

    reset()

    def far_step(j, carry):
        step(ks_ref, vst_ref, j, None)
        return carry

    lax.fori_loop(0, jnp.maximum(qi - 1, 0), far_step, 0)

    @pl.when(qi >= 1)
    def _():
        step(ks_ref, vst_ref, qi - 1, prev)

    step(ks_ref, vst_ref, qi, diag)
    add_result(1)

    reset()
    n_back = WINDOW // TQ
    edge = jnp.where(lax.broadcasted_iota(jnp.int32, (TQ, TQ), 0) >= lax.broadcasted_iota(jnp.int32, (TQ, TQ), 1),
                     0.0, NEG_INF)
    for back in range(n_back, -1, -1):
        bias = (lambda h: edge) if back == n_back else prev if back == 1 else diag if back == 0 else None

        @pl.when(qi >= back)
        def _(back=back, bias=bias):
            step(kw_ref, vwt_ref, qi - back, bias)

    add_result(2)
    o_ref[...] = ot_ref[...].T.astype(o_ref.dtype)


def prompt_attention_t(qt, gate_t, kc, vct, tc, tz, ks, vst, kw, vwt, b, t):
    n_chunk = t // CMP_STRIDE
    rc = kc.shape[2]
    nf = FEAT - HEAD_DIM
    nq = t // TQ
    assert t // SEL_BLOCK <= nf and t % TQ == 0 and rc >= T_PAD * (nq + 1) and rc >= T_PAD + n_chunk - 1
    ovt = jnp.asarray(_sel_overlap_rows(rc, n_chunk - 1, T_PAD).T)
    hp = HEADS_PER_KV
    gw = hp * HEAD_DIM
    k_spec = pl.BlockSpec((None, None, t, FEAT), lambda i, g, qi: (i, g, 0, 0))
    v_spec = pl.BlockSpec((None, None, nq, V_ROWS, TQ), lambda i, g, qi: (i, g, 0, 0, 0))
    return pl.pallas_call(
        _prompt_attn_t_kernel, grid=(b, N_KV_HEADS, nq), name="prompt_attention",
        in_specs=[pl.BlockSpec((gw, TQ), lambda i, g, qi: (g, i * nq + qi)),
                  pl.BlockSpec((4 * hp, TQ), lambda i, g, qi: (g, i * nq + qi)),
                  pl.BlockSpec((None, None, rc, FEAT), lambda i, g, qi: (i, g, 0, 0)),
                  pl.BlockSpec((None, None, HEAD_DIM, rc), lambda i, g, qi: (i, g, 0, 0)),
                  pl.BlockSpec((nf, rc), lambda i, g, qi: (0, 0)),
                  pl.BlockSpec((hp, T_NEAR, TQ), lambda i, g, qi: (g, 0, 0)),
                  pl.BlockSpec((2, hp, TQ, TQ), lambda i, g, qi: (0, g, 0, 0)),
                  k_spec, v_spec, k_spec, v_spec],
        out_specs=pl.BlockSpec((TQ, gw), lambda i, g, qi: (i * nq + qi, g)),
        out_shape=jax.ShapeDtypeStruct((b * t, ATTN_WIDTH), BF16),
        scratch_shapes=[pltpu.VMEM((hp, FEAT, TQ), BF16), pltpu.VMEM((hp, rc, TQ), F32),
                        pltpu.VMEM((hp, 1, TQ), F32), pltpu.VMEM((hp, V_ROWS, TQ), F32), pltpu.VMEM((gw, TQ), F32)],
        compiler_params=_params("parallel", "parallel", "arbitrary"))(
            qt, gate_t, kc, vct, ovt, tc, tz, ks, vst, kw, vwt)


def _gmlp_kernel(u_ref, v_ref, lng_ref, lnb_ref, ws_ref, bs_ref, o_ref):
    v = v_ref[...].astype(F32)
    mu = jnp.mean(v, axis=-1, keepdims=True)
    var = jnp.mean(jnp.square(v - mu), axis=-1, keepdims=True)
    vn = ((v - mu) * lax.rsqrt(var + LN_EPS) * lng_ref[...] + lnb_ref[...]).astype(BF16)
    tri = (lax.broadcasted_iota(jnp.int32, (CHUNK, CHUNK), 0) >= lax.broadcasted_iota(jnp.int32, (CHUNK, CHUNK), 1))
    for g in range(GM_GROUPS):
        w = jnp.where(tri, ws_ref[g], 0.0).astype(BF16)
        cols = slice(g * GM_GROUP_DIM, (g + 1) * GM_GROUP_DIM)
        for c in range(u_ref.shape[0] // CHUNK):
            rows = slice(c * CHUNK, (c + 1) * CHUNK)
            s = _dot(w, vn[rows, cols]) + bs_ref[:, g:g + 1]
            o_ref[rows, cols] = (u_ref[rows, cols].astype(F32) * s).astype(o_ref.dtype)


def gmlp_chunks(uv, ln_g, ln_b, w_s, b_s, rows):
    m = uv.shape[0]
    return pl.pallas_call(
        _gmlp_kernel, grid=(m // rows,), name="gmlp",
        in_specs=[pl.BlockSpec((rows, GM_WIDTH), lambda i: (i, 0)), pl.BlockSpec((rows, GM_WIDTH), lambda i: (i, 1)),
                  pl.BlockSpec((1, GM_WIDTH), lambda i: (0, 0)), pl.BlockSpec((1, GM_WIDTH), lambda i: (0, 0)),
                  pl.BlockSpec((GM_GROUPS, CHUNK, CHUNK), lambda i: (0, 0, 0)),
                  pl.BlockSpec((CHUNK, GM_GROUPS), lambda i: (0, 0))],
        out_specs=pl.BlockSpec((rows, GM_WIDTH), lambda i: (i, 0)),
        out_shape=jax.ShapeDtypeStruct((m, GM_WIDTH), BF16),
        compiler_params=_params("parallel"))(uv, uv, ln_g.reshape(1, -1), ln_b.reshape(1, -1), w_s, b_s.T)


def _merge_kernel(attn_ref, gm_ref, mg_ref, x_ref, wa_ref, wb_ref, wo_ref, gn_ref, wr_ref, br_ref,
                  h_ref, hn_ref, idx_ref, wt_ref):
    ta = _dot(attn_ref[...], wa_ref[...])
    tb = _dot(gm_ref[...], wb_ref[...])
    merged = mg_ref[:, 0:D_MODEL].astype(F32) * ta + mg_ref[:, D_MODEL:2 * D_MODEL].astype(F32) * tb
    h = x_ref[...] + _dot(merged.astype(BF16), wo_ref[...])
    h_ref[...] = h
    hn = h * lax.rsqrt(jnp.mean(h * h, axis=-1, keepdims=True) + RMS_EPS) * gn_ref[...]
    hn_ref[...] = hn.astype(hn_ref.dtype)
    logits = _dot(hn, wr_ref[...], precision=lax.Precision.HIGHEST) + br_ref[...]
    lane = lax.broadcasted_iota(jnp.int32, logits.shape, 1)
    vals, idxs = [], []
    for _ in range(TOP_K):
        best = jnp.max(logits, axis=-1, keepdims=True)
        ix = jnp.min(jnp.where(logits == best, lane, LANE), axis=-1, keepdims=True)
        vals.append(best)
        idxs.append(ix)
        logits = jnp.where(lane == ix, -jnp.inf, logits)
    ex = [jnp.exp(v - vals[0]) for v in vals]
    inv = 1.0 / functools.reduce(lambda a, b: a + b, ex)
    idx_out = jnp.zeros(lane.shape, jnp.int32)
    wt_out = jnp.zeros(lane.shape, F32)
    for k in range(TOP_K):
        idx_out = jnp.where(lane == k, idxs[k], idx_out)
        wt_out = jnp.where(lane == k, ex[k] * inv, wt_out)
    idx_ref[...] = idx_out
    wt_ref[...] = wt_out


def merge_and_route(attn, gm, mg, x, wa, wb, wo, g_ffn, w_router, b_router, tm):
    m = x.shape[0]
    tm = min(tm, m)
    wr = jnp.zeros((D_MODEL, LANE), F32).at[:, :N_EXPERTS].set(w_router)
    br = jnp.full((1, LANE), -jnp.inf, F32).at[0, :N_EXPERTS].set(b_router)
    row = lambda w: pl.BlockSpec((tm, w), lambda i: (i, 0))
    full = lambda a: pl.BlockSpec(a.shape, lambda i: (0, 0))
    return pl.pallas_call(
        _merge_kernel, grid=(m // tm,), name="merge_route",
        in_specs=[row(ATTN_WIDTH), row(GM_WIDTH), row(2 * D_MODEL), row(D_MODEL), full(wa), full(wb), full(wo),
                  pl.BlockSpec((1, D_MODEL), lambda i: (0, 0)), full(wr), full(br)],
        out_specs=[row(D_MODEL), row(D_MODEL), row(LANE), row(LANE)],
        out_shape=[jax.ShapeDtypeStruct((m, D_MODEL), F32), jax.ShapeDtypeStruct((m, D_MODEL), BF16),
                   jax.ShapeDtypeStruct((m, LANE), jnp.int32), jax.ShapeDtypeStruct((m, LANE), F32)],
        compiler_params=_params("parallel"))(attn, gm, mg, x, wa, wb, wo, g_ffn.reshape(1, -1), wr, br)


def _cast_weight(src_ref, dst_ref):
    step = 256
    for r0 in range(0, src_ref.shape[0], step):
        dst_ref[r0:r0 + step, :] = src_ref[r0:r0 + step, :].astype(dst_ref.dtype)


def _moe_up_kernel(te_ref, first_ref, nact_ref, x_ref, wg_ref, wu_ref, bg_ref, bu_ref, o_ref, wgb_ref, wub_ref):
    r = pl.program_id(1)

    @pl.when(first_ref[r] == 1)
    def _():
        _cast_weight(wg_ref, wgb_ref)
        _cast_weight(wu_ref, wub_ref)

    @pl.when(r < nact_ref[0])
    def _():
        x = x_ref[...]
        g = jnp.minimum(_dot(x, wgb_ref[...]) + bg_ref[...], SWIGLU_LIMIT)
        u = jnp.clip(_dot(x, wub_ref[...]) + bu_ref[...], -SWIGLU_LIMIT, SWIGLU_LIMIT)
        o_ref[...] = ((u + 1.0) * (g * jax.nn.sigmoid(SWIGLU_ALPHA * g))).astype(o_ref.dtype)

    @pl.when(r >= nact_ref[0])
    def _():
        o_ref[...] = jnp.zeros(o_ref.shape, o_ref.dtype)


def _moe_down_kernel(te_ref, first_ref, nact_ref, a_ref, wd_ref, bd_ref, o_ref, wdb_ref):
    r = pl.program_id(1)

    @pl.when(first_ref[r] == 1)
    def _():
        _cast_weight(wd_ref, wdb_ref)

    @pl.when(r < nact_ref[0])
    def _():
        o_ref[...] = (_dot(a_ref[...], wdb_ref[...]) + bd_ref[...]).astype(o_ref.dtype)

    @pl.when(r >= nact_ref[0])
    def _():
        o_ref[...] = jnp.zeros(o_ref.shape, o_ref.dtype)


def moe_experts(x_rows, tile_expert, tile_first, n_active, w_gate, b_gate, w_up, b_up, w_down, b_down):
    rows = x_rows.shape[0]
    n_tiles = rows // MOE_TM
    row_map = lambda j, r, te, first, nact: (jnp.minimum(r, nact[0] - 1), 0)
    out_map = lambda j, r, te, first, nact: (r, j)
    w_map = lambda j, r, te, first, nact: (te[r], 0, j)
    w_spec = pl.BlockSpec((None, D_MODEL, MOE_TN), w_map)
    b_spec = pl.BlockSpec((None, 1, MOE_TN), w_map)
    act = pl.pallas_call(
        _moe_up_kernel, name="moe_up",
        grid_spec=pltpu.PrefetchScalarGridSpec(
            num_scalar_prefetch=3, grid=(D_FF // MOE_TN, n_tiles),
            in_specs=[pl.BlockSpec((MOE_TM, D_MODEL), row_map), w_spec, w_spec, b_spec, b_spec],
            out_specs=pl.BlockSpec((MOE_TM, MOE_TN), out_map),
            scratch_shapes=[pltpu.VMEM((D_MODEL, MOE_TN), BF16), pltpu.VMEM((D_MODEL, MOE_TN), BF16)]),
        out_shape=jax.ShapeDtypeStruct((rows, D_FF), BF16),
        compiler_params=_params("arbitrary", "arbitrary"))(
            tile_expert, tile_first, n_active, x_rows, w_gate, w_up,
            b_gate.reshape(N_EXPERTS, 1, D_FF), b_up.reshape(N_EXPERTS, 1, D_FF))
    return pl.pallas_call(
        _moe_down_kernel, name="moe_down",
        grid_spec=pltpu.PrefetchScalarGridSpec(
            num_scalar_prefetch=3, grid=(D_MODEL // MOE_TN, n_tiles),
            in_specs=[pl.BlockSpec((MOE_TM, D_FF), row_map), w_spec, b_spec],
            out_specs=pl.BlockSpec((MOE_TM, MOE_TN), out_map),
            scratch_shapes=[pltpu.VMEM((D_FF, MOE_TN), BF16)]),
        out_shape=jax.ShapeDtypeStruct((rows, D_MODEL), BF16),
        compiler_params=_params("arbitrary", "arbitrary"))(
            tile_expert, tile_first, n_active, act, w_down, b_down.reshape(N_EXPERTS, 1, D_MODEL))


def route_rows(top_i, n_rows):
    n = top_i.shape[0]
    hit = top_i[:, :, None] == jnp.arange(N_EXPERTS, dtype=jnp.int32)[None, None, :]
    onehot = hit.astype(jnp.int32).sum(axis=1)
    before = jnp.cumsum(onehot, axis=0) - onehot
    counts = onehot.sum(axis=0)
    padded = (counts + MOE_TM - 1) // MOE_TM * MOE_TM
    ends = jnp.cumsum(padded)
    starts = ends - padded
    pos = jnp.sum(jnp.where(hit, (starts[None, :] + before)[:, None, :], 0), axis=-1)
    pair_id = jnp.arange(TOP_K, dtype=jnp.int32)[None, :] * n + jnp.arange(n, dtype=jnp.int32)[:, None]
    row_pair = jnp.arange(n_rows, dtype=jnp.int32).at[pos.reshape(-1)].set(pair_id.reshape(-1), unique_indices=True)
    n_tiles = n_rows // MOE_TM
    n_active = (ends[-1] // MOE_TM).astype(jnp.int32)
    tile_row = jnp.minimum(jnp.arange(n_tiles, dtype=jnp.int32), n_active - 1) * MOE_TM
    tile_expert = jnp.minimum((tile_row[:, None] >= ends[None, :]).astype(jnp.int32).sum(axis=-1), N_EXPERTS - 1)
    tile_first = jnp.concatenate([jnp.ones((1,), jnp.int32), (tile_expert[1:] != tile_expert[:-1]).astype(jnp.int32)])
    return pos, row_pair, tile_expert, tile_first, n_active.reshape(1)


def _combine_kernel(h_ref, y_ref, w_ref, g_ref, o_ref):
    h = h_ref[...]
    w = w_ref[...]
    for k in range(TOP_K):
        h = h + w[:, k:k + 1] * y_ref[k].astype(F32)
    o_ref[...] = h * lax.rsqrt(jnp.mean(h * h, axis=-1, keepdims=True) + RMS_EPS) * g_ref[...]


def combine_and_norm(h, y4, wts, g, tm):
    m = h.shape[0]
    tm = min(tm, m)
    return pl.pallas_call(
        _combine_kernel, grid=(m // tm,), name="combine_norm",
        in_specs=[pl.BlockSpec((tm, D_MODEL), lambda i: (i, 0)), pl.BlockSpec((TOP_K, tm, D_MODEL), lambda i: (0, i, 0)),
                  pl.BlockSpec((tm, LANE), lambda i: (i, 0)), pl.BlockSpec((1, D_MODEL), lambda i: (0, 0))],
        out_specs=pl.BlockSpec((tm, D_MODEL), lambda i: (i, 0)),
        out_shape=jax.ShapeDtypeStruct((m, D_MODEL), F32), compiler_params=_params("parallel"))(
            h, y4, wts, g.reshape(1, -1))


def _in_weights(w_in):
    c0 = ATTN_WIDTH
    c1 = c0 + 3 * KV_COLS
    c2 = c1 + GATE_COLS
    c3 = c2 + 2 * GM_WIDTH
    wg = w_in[:, c1:c2].reshape(D_MODEL, N_KV_HEADS, HEADS_PER_KV, 3).transpose(0, 1, 3, 2)
    wg = jnp.pad(wg.reshape(D_MODEL, N_KV_HEADS, 3 * HEADS_PER_KV), ((0, 0), (0, 0), (0, LANE - 3 * HEADS_PER_KV)))
    cast = lambda w: w.astype(BF16)
    return (cast(w_in[:, :c0]), cast(w_in[:, c0:c1]), cast(wg.reshape(D_MODEL, N_KV_HEADS * LANE)),
            cast(w_in[:, c2:c3]), cast(w_in[:, c3:]))


def _project(a, weights, wide_dtype):
    w_q, w_kv, w_ng, w_uv, w_mg = weights
    q = matmul(a, w_q, lambda z: z * np.float32(HEAD_DIM ** -0.5), wide_dtype, 1024, 1024, "proj_q")
    kv = matmul(a, w_kv, lambda z: z, F32, 1024, 3 * KV_COLS, "proj_kv")
    ng = matmul(a, w_ng, jax.nn.sigmoid, F32, 1024, N_KV_HEADS * LANE, "proj_gate")
    uv = matmul(a, w_uv, _gelu, wide_dtype, 1024, 1024, "proj_uv")
    mg = matmul(a, w_mg, jax.nn.sigmoid, BF16, 1024, 1024, "proj_merge")
    return q, kv, ng, uv, mg


def _head_rows(fn):
    out = fn(0)
    out = jnp.broadcast_to(out, (HEADS_PER_KV, out.shape[1]))
    row = lax.broadcasted_iota(jnp.int32, out.shape, 0)
    for h in range(1, HEADS_PER_KV):
        out = jnp.where(row == h, fn(h), out)
    return out


def _sample_cmp_win_kernel(rb_ref, q_ref, kvc_ref, ov_ref, sw_ref, new_ref, gate_ref, o_ref, idx_ref, *, past, n_cmp):
    n_rows = kvc_ref.shape[2]
    n_blk = ov_ref.shape[1]
    win = sw_ref.shape[-1]
    win_col = 2 * KV_COLS
    lane_blk = lax.broadcasted_iota(jnp.int32, (1, n_blk), 1)
    lane_out = lax.broadcasted_iota(jnp.int32, (1, LANE), 1)
    for g in range(N_KV_HEADS):
        heads = slice(g * HEADS_PER_KV, (g + 1) * HEADS_PER_KV)
        q = q_ref[heads, :]
        qb = q.astype(BF16)

        n = lax.broadcasted_iota(jnp.int32, (1, n_rows), 1)
        d = past - (n * CMP_STRIDE + CMP_BLOCK - 1)
        logit = _dot_nt(qb, kvc_ref[0, g].astype(BF16)) + _head_rows(
            lambda h: _rel_bias(d, rb_ref, g * HEADS_PER_KV + h))
        logit = jnp.where(n < n_cmp, logit, NEG_INF)
        p = jnp.exp(logit - jnp.max(logit, axis=-1, keepdims=True))
        p = p * (1.0 / jnp.sum(p, axis=-1, keepdims=True))
        o_cmp = _dot(p.astype(BF16), kvc_ref[1, g].astype(BF16))
        p_heads = jnp.broadcast_to(jnp.sum(p, axis=0, keepdims=True), (SUBLANE, n_rows))
        p_slc = _dot(p_heads, ov_ref[...], precision=lax.Precision.HIGHEST)[0:1, :]

        score = jnp.where((lane_blk == 0) | (lane_blk == n_blk - 1), jnp.inf, p_slc)
        picked = jnp.zeros((1, LANE), jnp.int32)
        for k in range(SEL_TOPK - 1):
            best = jnp.max(score, axis=-1, keepdims=True)
            ix = jnp.min(jnp.where(score == best, lane_blk, n_blk), axis=-1, keepdims=True)
            picked = jnp.where(lane_out == k, ix, picked)
            score = jnp.where(lane_blk == ix, -jnp.inf, score)
        idx_ref[g:g + 1, :] = picked

        dw = win - lax.broadcasted_iota(jnp.int32, (1, win), 1)
        lw = _dot(qb, sw_ref[0, g].astype(BF16)) + _head_rows(
            lambda h: _rel_bias(dw, rb_ref, g * HEADS_PER_KV + h))
        k_new = new_ref[:, win_col + g * HEAD_DIM:win_col + (g + 1) * HEAD_DIM]
        v_new = new_ref[:, win_col + KV_COLS // 2 + g * HEAD_DIM:win_col + KV_COLS // 2 + (g + 1) * HEAD_DIM]
        l_new = jnp.sum(q * k_new, axis=-1, keepdims=True) + _head_rows(
            lambda h: jnp.full((1, 1), rb_ref[0, g * HEADS_PER_KV + h], F32))
        m = jnp.maximum(jnp.max(lw, axis=-1, keepdims=True), l_new)
        pw = jnp.exp(lw - m)
        p_new = jnp.exp(l_new - m)
        o_win = ((_dot_nt(pw.astype(BF16), sw_ref[1, g].astype(BF16)) + p_new * v_new)
                 * (1.0 / (jnp.sum(pw, axis=-1, keepdims=True) + p_new)))
        o_ref[heads, :] = gate_ref[0, heads, :] * o_cmp + gate_ref[2, heads, :] * o_win


def sample_cmp_win(rel_bias, q, kvc, state_win, kv_new, gate, past):
    b, _, _, n_rows, _ = kvc.shape
    n_cmp = past // CMP_STRIDE - CMP_BLOCK // CMP_STRIDE + 1
    n_blk = past // SEL_BLOCK
    assert past % SEL_BLOCK == 0 and n_blk >= SEL_TOPK and n_cmp <= n_rows
    cs = np.arange(n_rows)[:, None] * CMP_STRIDE
    ss = np.arange(n_blk)[None, :] * SEL_BLOCK
    ov = np.maximum(np.minimum(cs + CMP_BLOCK, ss + SEL_BLOCK) - np.maximum(cs, ss), 0) / CMP_STRIDE
    ov = jnp.asarray((ov * (np.arange(n_rows) < n_cmp)[:, None]).astype(np.float32))
    win = state_win.shape[-1]
    return pl.pallas_call(
        functools.partial(_sample_cmp_win_kernel, past=past, n_cmp=n_cmp), grid=(b,), name="sample_cmp_win",
        in_specs=[pl.BlockSpec(memory_space=pltpu.SMEM),
                  pl.BlockSpec((None, N_HEADS, HEAD_DIM), lambda i: (i, 0, 0)),
                  pl.BlockSpec((None, 2, N_KV_HEADS, n_rows, HEAD_DIM), lambda i: (i, 0, 0, 0, 0)),
                  pl.BlockSpec(ov.shape, lambda i: (0, 0)),
                  pl.BlockSpec((None, 2, N_KV_HEADS, HEAD_DIM, win), lambda i: (i, 0, 0, 0, 0)),
                  pl.BlockSpec((None, 1, 3 * KV_COLS), lambda i: (i, 0, 0)),
                  pl.BlockSpec((None, 3, N_HEADS, 1), lambda i: (i, 0, 0, 0))],
        out_specs=[pl.BlockSpec((None, N_HEADS, HEAD_DIM), lambda i: (i, 0, 0)),
                   pl.BlockSpec((None, N_KV_HEADS, LANE), lambda i: (i, 0, 0))],
        out_shape=[jax.ShapeDtypeStruct((b, N_HEADS, HEAD_DIM), F32),
                   jax.ShapeDtypeStruct((b, N_KV_HEADS, LANE), jnp.int32)],
        compiler_params=_params("parallel"))(rel_bias, q, kvc, ov, state_win, kv_new, gate)


def _sample_slc_kernel(pt_ref, idx_ref, rb_ref, q_ref, new_ref, gate_ref, part_ref, *rest, past):
    blocks, o_ref = rest[:-1], rest[-1]
    b, g = pl.program_id(0), pl.program_id(1)
    q = q_ref[...]
    qb = q.astype(BF16)
    slc_col = KV_COLS
    half = KV_COLS // 2
    page = blocks[0].shape[-1]
    pos = lax.broadcasted_iota(jnp.int32, (1, page), 1)
    logits, values = [], []
    for k, blk_ref in enumerate(blocks):
        ix = idx_ref[(b * N_KV_HEADS + g) * SEL_TOPK + k]
        d = past - ((ix // SLC_PER_PAGE) * page + pos)
        lg = _dot(qb, blk_ref[0].astype(BF16)) + _head_rows(lambda h: _rel_bias(d, rb_ref, g * HEADS_PER_KV + h))
        logits.append(jnp.where(pos // SEL_BLOCK == ix % SLC_PER_PAGE, lg, NEG_INF))
        values.append(blk_ref[1].astype(BF16))
    new = new_ref[...]
    k_new = jnp.where(g == 0, new[:, slc_col:slc_col + HEAD_DIM], new[:, slc_col + HEAD_DIM:slc_col + 2 * HEAD_DIM])
    v_new = jnp.where(g == 0, new[:, slc_col + half:slc_col + half + HEAD_DIM],
                      new[:, slc_col + half + HEAD_DIM:slc_col + half + 2 * HEAD_DIM])
    l_new = jnp.sum(q * k_new, axis=-1, keepdims=True) + _head_rows(
        lambda h: jnp.full((1, 1), rb_ref[0, g * HEADS_PER_KV + h], F32))
    m = l_new
    for lg in logits:
        m = jnp.maximum(m, jnp.max(lg, axis=-1, keepdims=True))
    p_new = jnp.exp(l_new - m)
    denom = p_new
    out = p_new * v_new
    for lg, vv in zip(logits, values):
        p = jnp.exp(lg - m)
        denom = denom + jnp.sum(p, axis=-1, keepdims=True)
        out = out + _dot_nt(p.astype(BF16), vv)
    o_ref[...] = part_ref[...] + gate_ref[1] * (out * (1.0 / denom))


def sample_slc(page_table, picked, rel_bias, q, kv_new, gate, part, cache_slc, past):
    b = q.shape[0]
    page = cache_slc.shape[-1]
    n_pick = SEL_TOPK - 1

    def blk_map(k):
        def index(i, g, pt, idx, k=k):
            ix = idx[(i * N_KV_HEADS + g) * SEL_TOPK + k]
            return (pt[i, ix // SLC_PER_PAGE], 0, g, 0, 0)
        return index

    head_spec = pl.BlockSpec((None, HEADS_PER_KV, HEAD_DIM), lambda i, g, pt, idx: (i, g, 0))
    return pl.pallas_call(
        functools.partial(_sample_slc_kernel, past=past), name="sample_slc",
        grid_spec=pltpu.PrefetchScalarGridSpec(
            num_scalar_prefetch=2, grid=(b, N_KV_HEADS),
            in_specs=[pl.BlockSpec(memory_space=pltpu.SMEM), head_spec,
                      pl.BlockSpec((None, 1, 3 * KV_COLS), lambda i, g, pt, idx: (i, 0, 0)),
                      pl.BlockSpec((None, 3, HEADS_PER_KV, 1), lambda i, g, pt, idx: (i, 0, g, 0)), head_spec]
            + [pl.BlockSpec((None, 2, None, HEAD_DIM, page), blk_map(k)) for k in range(n_pick)],
            out_specs=head_spec),
        out_shape=jax.ShapeDtypeStruct((b, N_HEADS, HEAD_DIM), F32),
        compiler_params=_params("parallel", "parallel"))(
            page_table, picked, rel_bias, q, kv_new, gate, part, *([cache_slc] * n_pick))


def _gmlp_row_kernel(uv_ref, lng_ref, lnb_ref, w0_ref, b0_ref, vn_ref, o_ref):
    v = uv_ref[:, GM_WIDTH:2 * GM_WIDTH]
    mu = jnp.mean(v, axis=-1, keepdims=True)
    var = jnp.mean(jnp.square(v - mu), axis=-1, keepdims=True)
    vn = (v - mu) * lax.rsqrt(var + LN_EPS) * lng_ref[...] + lnb_ref[...]
    vn_ref[...] = vn
    o_ref[...] = (uv_ref[:, 0:GM_WIDTH] * (w0_ref[...] * vn + b0_ref[...])).astype(o_ref.dtype)


def gmlp_first_row(uv, ln_g, ln_b, w_s, b_s):
    m = uv.shape[0]
    w0 = jnp.repeat(w_s[:, 0, 0], GM_GROUP_DIM).reshape(1, GM_WIDTH)
    b0 = jnp.repeat(b_s[:, 0], GM_GROUP_DIM).reshape(1, GM_WIDTH)
    return pl.pallas_call(
        _gmlp_row_kernel, name="gmlp_row",
        out_shape=[jax.ShapeDtypeStruct((m, GM_WIDTH), F32), jax.ShapeDtypeStruct((m, GM_WIDTH), BF16)])(
            uv, ln_g.reshape(1, -1), ln_b.reshape(1, -1), w0, b0)


def _group_major(x, feat):
    b, t = x.shape[0], x.shape[1]
    f = jnp.broadcast_to(feat.astype(BF16)[None, None], (b, N_KV_HEADS, t, FEAT - HEAD_DIM))
    return jnp.concatenate([x.transpose(0, 2, 1, 3).astype(BF16), f], axis=-1)


def _value_tiles(x):
    b, t = x.shape[0], x.shape[1]
    xt = x.transpose(0, 2, 3, 1).astype(BF16)
    extra = jnp.zeros((b, N_KV_HEADS, V_ROWS - HEAD_DIM, t), BF16).at[:, :, 0].set(1)
    tiles = jnp.concatenate([xt, extra], axis=2).reshape(b, N_KV_HEADS, V_ROWS, t // TQ, TQ)
    return tiles.transpose(0, 1, 3, 2, 4)


def _prompt_mixer_a(q, kv, ng, rel_bias, pe, w1, w2, b, t):
    kvr = kv.reshape(b, t, 3, 2, N_KV_HEADS, HEAD_DIM)
    n_chunk = t // CMP_STRIDE
    chunks = kvr[:, :, 0].reshape(b, n_chunk, CMP_STRIDE, 2, N_KV_HEADS, HEAD_DIM).transpose(0, 3, 4, 1, 2, 5)
    kvc = compress_chunks(chunks.reshape(b, 2, N_KV_HEADS, n_chunk, CMP_STRIDE * HEAD_DIM).astype(BF16), w1, w2, pe)
    rc = -(-(T_PAD + max(n_chunk - 1, T_PAD * t // TQ)) // LANE) * LANE
    kvc = jnp.pad(kvc[:, :, :, :n_chunk - 1], ((0, 0), (0, 0), (0, 0), (T_PAD, rc - T_PAD - n_chunk + 1), (0, 0)))
    kc = jnp.pad(kvc[:, 0], ((0, 0), (0, 0), (0, 0), (0, FEAT - HEAD_DIM)))
    nf = FEAT - HEAD_DIM
    blocks = (jnp.arange(t)[:, None] // SEL_BLOCK == jnp.arange(nf)[None, :])
    tz, tc = bias_tables_t(rel_bias)
    gate_t = ng.reshape(b * t, N_KV_HEADS, LANE)[:, :, :4 * HEADS_PER_KV].reshape(b * t, -1).T
    return prompt_attention_t(
        q.T, gate_t, kc, kvc[:, 1].transpose(0, 1, 3, 2), tc, tz,
        _group_major(kvr[:, :, 1, 0], blocks), _value_tiles(kvr[:, :, 1, 1]),
        _group_major(kvr[:, :, 2, 0], jnp.zeros((t, nf), BF16)), _value_tiles(kvr[:, :, 2, 1]), b, t)


def _sample_mixer_a(q, kv, ng, rel_bias, pe, w1, w2, cache_cmp, cache_slc, state_win, page_table):
    b, n_pages = page_table.shape
    page = cache_cmp.shape[1]
    past = n_pages * page
    n_chunk = past // CMP_STRIDE
    rows = cache_cmp[page_table].reshape(b, n_chunk, CMP_STRIDE, 2, N_KV_HEADS, HEAD_DIM).transpose(0, 3, 4, 1, 2, 5)
    kvc = compress_chunks(rows.reshape(b, 2, N_KV_HEADS, n_chunk, CMP_STRIDE * HEAD_DIM).astype(BF16), w1, w2, pe)
    gate = ng.reshape(b, N_KV_HEADS, LANE)[:, :, :3 * HEADS_PER_KV].reshape(b, N_KV_HEADS, 3, HEADS_PER_KV)
    gate = gate.transpose(0, 2, 1, 3).reshape(b, 3, N_HEADS, 1)
    q3 = q.reshape(b, N_HEADS, HEAD_DIM)
    kv_new = kv.reshape(b, 1, 3 * KV_COLS)
    part, picked = sample_cmp_win(rel_bias, q3, kvc, state_win.transpose(0, 2, 3, 4, 1), kv_new, gate, past)
    picked = picked[:, :, :SEL_TOPK].reshape(-1)
    out = sample_slc(page_table, picked, rel_bias, q3, kv_new, gate, part, cache_slc.transpose(0, 2, 3, 4, 1), past)
    return out.reshape(b, ATTN_WIDTH).astype(BF16)


def kernel(x_prompt, x_sample, cache_cmp_kv, cache_slc_kv, state_win_kv, page_table, rel_bias, norm_mix, w_in, pe_cmp, w_cmp1, w_cmp2, gm_ln_g, gm_ln_b, gm_w_s, gm_b_s, w_br_attn, w_br_gmlp, w_out, norm_ffn, w_router, b_router, w_gate, b_gate, w_up, b_up, w_down, b_down, norm_final):
    b, t, _ = x_prompt.shape
    bs, ts, _ = x_sample.shape
    depth = norm_mix.shape[0]
    assert depth == 1 and ts == 1 and cache_slc_kv.shape[2] == SLC_PER_PAGE * SEL_BLOCK
    n_p, n_s = b * t, bs * ts
    kv_tail = (2, N_KV_HEADS, HEAD_DIM)
    lyr = 0

    xp = x_prompt.reshape(n_p, D_MODEL)
    xs = x_sample.reshape(n_s, D_MODEL)
    weights = _in_weights(w_in[lyr])
    qp, kvp, ngp, uvp, mgp = _project(rmsnorm_rows(xp, norm_mix[lyr], 512), weights, BF16)
    qs, kvs, ngs, uvs, mgs = _project(rmsnorm_rows(xs, norm_mix[lyr], n_s), weights, F32)

    attn_p = _prompt_mixer_a(qp, kvp, ngp, rel_bias, pe_cmp[lyr], w_cmp1[lyr], w_cmp2[lyr], b, t)
    attn_s = _sample_mixer_a(qs, kvs, ngs, rel_bias, pe_cmp[lyr], w_cmp1[lyr], w_cmp2[lyr], cache_cmp_kv[lyr],
                             cache_slc_kv[lyr], state_win_kv[lyr], page_table)

    gm_p = gmlp_chunks(uvp, gm_ln_g[lyr], gm_ln_b[lyr], gm_w_s[lyr], gm_b_s[lyr], 4 * CHUNK)
    vn_s, gm_s = gmlp_first_row(uvs, gm_ln_g[lyr], gm_ln_b[lyr], gm_w_s[lyr], gm_b_s[lyr])

    wa, wb, wo = w_br_attn[lyr].astype(BF16), w_br_gmlp[lyr].astype(BF16), w_out[lyr].astype(BF16)
    hp, hnp, idxp, wtp = merge_and_route(attn_p, gm_p, mgp, xp, wa, wb, wo, norm_ffn[lyr], w_router[lyr], b_router[lyr], 256)
    hs, hns, idxs, wts = merge_and_route(attn_s, gm_s, mgs, xs, wa, wb, wo, norm_ffn[lyr], w_router[lyr], b_router[lyr], 256)

    n_all = n_p + n_s
    n_rows = (n_all * TOP_K + N_EXPERTS * (MOE_TM - 1) + MOE_TM - 1) // MOE_TM * MOE_TM
    top_i = jnp.concatenate([idxp[:, :TOP_K], idxs[:, :TOP_K]], axis=0)
    pos, row_pair, tile_expert, tile_first, n_active = route_rows(top_i, n_rows)
    x_rows = jnp.concatenate([hnp, hns] * -(-n_rows // n_all), axis=0)[row_pair]
    y_rows = moe_experts(x_rows, tile_expert, tile_first, n_active, w_gate[lyr], b_gate[lyr], w_up[lyr], b_up[lyr],
                         w_down[lyr], b_down[lyr])
    y_p = combine_and_norm(hp, y_rows[pos[:n_p].T], wtp, norm_final, 256)
    y_s = combine_and_norm(hs, y_rows[pos[n_p:].T], wts, norm_final, 256)

    kvp5 = kvp.reshape(b, t, 3, *kv_tail)
    kvs5 = kvs.reshape(bs, ts, 3, *kv_tail)
    win_buf = state_win_kv.shape[2]
    win_s = jnp.concatenate([state_win_kv[lyr], kvs5[:, :, 2]], axis=1)[:, ts:]
    return (y_p.reshape(b, t, D_MODEL), y_s.reshape(bs, ts, D_MODEL),
            kvp5[:, :, 0][None], kvs5[:, :, 0][None], kvp5[:, :, 1][None], kvs5[:, :, 1][None],
            kvp5[:, t - min(WINDOW, t):, 2][None], win_s[:, -win_buf:][None],
            vn_s.reshape(1, bs, ts, GM_WIDTH))
```

```python
import functools
import math

import jax
import jax.numpy as jnp
import numpy as np
from jax import lax
from jax.experimental import pallas as pl
from jax.experimental.pallas import tpu as pltpu

D_MODEL = 2048
N_HEADS = 16
HEAD_DIM = 64
N_KV_HEADS = 2
HEADS_PER_KV = N_HEADS // N_KV_HEADS
ATTN_WIDTH = N_HEADS * HEAD_DIM
CMP_BLOCK = 32
CMP_STRIDE = 16
CMP_HID = 2 * HEAD_DIM
SEL_BLOCK = 64
SEL_TOPK = 16
WINDOW = 512
GM_WIDTH = D_MODEL // 2
GM_GROUPS = 8
GM_GROUP_DIM = GM_WIDTH // GM_GROUPS
CHUNK = 128
N_BUCKETS = 32
REL_MAX_DIST = 128
N_EXPERTS = 32
TOP_K = 4
D_FF = D_MODEL
SWIGLU_LIMIT = 7.0
SWIGLU_ALPHA = 1.702
RMS_EPS = 1e-5
LN_EPS = 1e-5
NEG_INF = -1e30
KV_COLS = 2 * N_KV_HEADS * HEAD_DIM
GATE_COLS = 3 * N_HEADS
SLC_PER_PAGE = 2

LANE = 128
SUBLANE = 8
VMEM_LIMIT = 56 * 1024 * 1024

QB = 128
KT = 128
CMP_PAD = QB // CMP_STRIDE
CMP_NEAR = 2 * CMP_PAD
MOE_TM = 256
MOE_TN = 1024
FEAT = 2 * HEAD_DIM

BF16 = jnp.bfloat16
F32 = jnp.float32


def _bucket_upper_bounds():
    d = np.arange(0, 4 * REL_MAX_DIST)
    exact = N_BUCKETS // 2
    out = []
    for dt in (np.float32, np.float64):
        far = exact + (np.log(np.maximum(d, exact).astype(dt) / dt(exact)) / dt(math.log(REL_MAX_DIST / exact))
                       * dt(N_BUCKETS - exact)).astype(np.int32)
        out.append(np.where(d < exact, d, np.minimum(far, N_BUCKETS - 1)))
    assert (out[0] == out[1]).all() and (np.diff(out[0]) >= 0).all()
    b = out[0]
    return [int(d[b <= k].max()) for k in range(N_BUCKETS - 1)]


BUCKET_HI = _bucket_upper_bounds()
FAR_DIST = BUCKET_HI[-1] + 1
assert FAR_DIST <= QB


def _rel_bias(d, rb_ref, h):
    acc = jnp.full(d.shape, rb_ref[N_BUCKETS - 1, h], F32)
    for k in reversed(range(N_BUCKETS - 1)):
        acc = jnp.where(d <= BUCKET_HI[k], rb_ref[k, h], acc)
    return acc


def _dot(a, b, **kw):
    return jnp.dot(a, b, preferred_element_type=F32, **kw)


def _dot_nt(a, b):
    return lax.dot_general(a, b, (((1,), (1,)), ((), ())), preferred_element_type=F32)


def _gelu(x):
    return 0.5 * x * (1.0 + lax.erf(x * np.float32(math.sqrt(0.5))))


def _params(*sem):
    return pltpu.CompilerParams(dimension_semantics=sem, vmem_limit_bytes=VMEM_LIMIT)


def _rmsnorm_kernel(x_ref, g_ref, o_ref):
    x = x_ref[...]
    ms = jnp.mean(x * x, axis=-1, keepdims=True)
    o_ref[...] = (x * lax.rsqrt(ms + RMS_EPS) * g_ref[...]).astype(o_ref.dtype)


def rmsnorm_rows(x, g, tm):
    m, d = x.shape
    return pl.pallas_call(
        _rmsnorm_kernel, grid=(m // tm,), name="rmsnorm",
        in_specs=[pl.BlockSpec((tm, d), lambda i: (i, 0)), pl.BlockSpec((1, d), lambda i: (0, 0))],
        out_specs=pl.BlockSpec((tm, d), lambda i: (i, 0)),
        out_shape=jax.ShapeDtypeStruct((m, d), BF16), compiler_params=_params("parallel"))(x, g.reshape(1, d))


def _mm_kernel(a_ref, w_ref, o_ref, *, epilogue):
    o_ref[...] = epilogue(_dot(a_ref[...], w_ref[...])).astype(o_ref.dtype)


def matmul(a, w, epilogue, out_dtype, tm, tn, name):
    m, k = a.shape
    n = w.shape[1]
    tm, tn = min(tm, m), min(tn, n)
    return pl.pallas_call(
        functools.partial(_mm_kernel, epilogue=epilogue), grid=(m // tm, n // tn), name=name,
        in_specs=[pl.BlockSpec((tm, k), lambda i, j: (i, 0)), pl.BlockSpec((k, tn), lambda i, j: (0, j))],
        out_specs=pl.BlockSpec((tm, tn), lambda i, j: (i, j)),
        out_shape=jax.ShapeDtypeStruct((m, n), out_dtype), compiler_params=_params("parallel", "parallel"))(a, w)


def _compress_kernel(c_ref, w1_ref, w2_ref, pe_ref, o_ref):
    n = c_ref.shape[0]
    half = CMP_STRIDE * HEAD_DIM
    c = c_ref[...]
    first = _dot(c, w1_ref[0:half, :])
    second = _dot(c, w1_ref[half:2 * half, :])
    pe = _dot(pe_ref[...].astype(BF16), w1_ref[...])[0:1, :]
    hid = _gelu(first + pltpu.roll(second, n - 1, 0) + pe)
    o_ref[...] = _dot(hid.astype(BF16), w2_ref[...])


def compress_chunks(chunks, w1, w2, pe):
    b, _, g, n, width = chunks.shape
    pe_rows = jnp.broadcast_to(pe.reshape(2, 1, CMP_BLOCK * HEAD_DIM), (2, SUBLANE, CMP_BLOCK * HEAD_DIM))
    return pl.pallas_call(
        _compress_kernel, grid=(b, 2, g), name="compress",
        in_specs=[pl.BlockSpec((None, None, None, n, width), lambda i, s, j: (i, s, j, 0, 0)),
                  pl.BlockSpec((None, CMP_BLOCK * HEAD_DIM, CMP_HID), lambda i, s, j: (s, 0, 0)),
                  pl.BlockSpec((None, CMP_HID, HEAD_DIM), lambda i, s, j: (s, 0, 0)),
                  pl.BlockSpec((None, SUBLANE, CMP_BLOCK * HEAD_DIM), lambda i, s, j: (s, 0, 0))],
        out_specs=pl.BlockSpec((None, None, None, n, HEAD_DIM), lambda i, s, j: (i, s, j, 0, 0)),
        out_shape=jax.ShapeDtypeStruct((b, 2, g, n, HEAD_DIM), F32),
        compiler_params=_params("parallel", "parallel", "parallel"))(chunks, w1.astype(BF16), w2.astype(BF16), pe_rows)


def _bias_tables_kernel(rb_ref, tz_ref, tc_ref):
    i = lax.broadcasted_iota(jnp.int32, (QB, KT), 0)
    j = lax.broadcasted_iota(jnp.int32, (QB, KT), 1)
    ic = lax.broadcasted_iota(jnp.int32, (QB, CMP_NEAR), 0)
    cc = lax.broadcasted_iota(jnp.int32, (QB, CMP_NEAR), 1)
    d0 = i - j
    d1 = d0 + KT
    dc = ic + (QB - CMP_BLOCK + 1) - CMP_STRIDE * cc
    for h in range(N_HEADS):
        last = rb_ref[N_BUCKETS - 1, h]
        rows = slice(h * QB, (h + 1) * QB)
        tz_ref[0, rows, :] = jnp.where(d0 < 0, NEG_INF, _rel_bias(d0, rb_ref, h) - last)
        tz_ref[1, rows, :] = _rel_bias(d1, rb_ref, h) - last
        tc_ref[rows, :] = jnp.where(dc < 0, NEG_INF, _rel_bias(dc, rb_ref, h) - last)


def bias_tables(rel_bias):
    return pl.pallas_call(
        _bias_tables_kernel, name="bias_tables",
        in_specs=[pl.BlockSpec(memory_space=pltpu.SMEM)],
        out_specs=[pl.BlockSpec(memory_space=pltpu.VMEM), pl.BlockSpec(memory_space=pltpu.VMEM)],
        out_shape=[jax.ShapeDtypeStruct((2, N_HEADS * QB, KT), F32),
                   jax.ShapeDtypeStruct((N_HEADS * QB, CMP_NEAR), F32)])(rel_bias)


def _prompt_attn_kernel(q_ref, gate_ref, kc_ref, vc_ref, ov_ref, tc_ref, tz_ref, ks_ref, vs_ref, kw_ref, vw_ref,
                        o_ref, qa_ref, m_ref, acc_ref, *, n_chunk):
    qi = pl.program_id(2)
    rows_all = HEADS_PER_KV * QB

    for h in range(HEADS_PER_KV):
        qa_ref[h * QB:(h + 1) * QB, 0:HEAD_DIM] = q_ref[:, h * HEAD_DIM:(h + 1) * HEAD_DIM]
        qa_ref[h * QB:(h + 1) * QB, HEAD_DIM:FEAT] = jnp.zeros((QB, FEAT - HEAD_DIM), BF16)
    qa = qa_ref[...]
    i_row = lax.broadcasted_iota(jnp.int32, (rows_all, 1), 0) % QB

    near0 = pl.multiple_of(qi * CMP_PAD, SUBLANE)
    r = lax.broadcasted_iota(jnp.int32, (1, n_chunk), 1)
    lf = _dot_nt(qa, kc_ref[0:n_chunk, :].astype(BF16))
    lf = jnp.where((r >= CMP_PAD) & (r < qi * CMP_PAD), lf, NEG_INF)
    c = lax.broadcasted_iota(jnp.int32, (1, CMP_NEAR), 1)
    ln = _dot_nt(qa, kc_ref[pl.ds(near0, CMP_NEAR), :].astype(BF16)) + tc_ref[...]
    ln = jnp.where(qi * CMP_PAD + c >= CMP_PAD, ln, NEG_INF)
    m = jnp.maximum(jnp.max(lf, axis=-1, keepdims=True), jnp.max(ln, axis=-1, keepdims=True))
    pf = jnp.exp(lf - m)
    pn = jnp.exp(ln - m)
    denom = jnp.sum(pf, axis=-1, keepdims=True) + jnp.sum(pn, axis=-1, keepdims=True)
    inv = jnp.where(qi * QB + i_row >= CMP_BLOCK - 1, 1.0 / denom, 0.0)
    pf = pf * inv
    pn = pn * inv
    o_cmp = (_dot(pf.astype(BF16), vc_ref[0:n_chunk, :].astype(BF16))
             + _dot(pn.astype(BF16), vc_ref[pl.ds(near0, CMP_NEAR), :].astype(BF16)))
    psf = jnp.sum(pf.reshape(HEADS_PER_KV, QB, n_chunk), axis=0)
    psn = jnp.sum(pn.reshape(HEADS_PER_KV, QB, CMP_NEAR), axis=0)
    p_slc = (_dot(psf, ov_ref[0:n_chunk, :], precision=lax.Precision.HIGHEST)
             + _dot(psn, ov_ref[pl.ds(near0, CMP_NEAR), :], precision=lax.Precision.HIGHEST))

    nf = FEAT - HEAD_DIM
    blk = lax.broadcasted_iota(jnp.int32, (QB, nf), 1)
    jt = (qi * QB + lax.broadcasted_iota(jnp.int32, (QB, nf), 0)) // SEL_BLOCK
    forced = (blk == 0) | (blk == jt) | (blk == jt - 1)
    score = jnp.where(forced, jnp.inf, jnp.where(blk <= jt, p_slc, -jnp.inf))
    rank = jnp.zeros((QB, nf), jnp.int32)
    for col in range(nf):
        sc = score[:, col:col + 1]
        rank = rank + jnp.where(blk > col, (sc >= score).astype(jnp.int32), (sc > score).astype(jnp.int32))
    feat = jnp.where((rank < SEL_TOPK) & (score > NEG_INF), 0.0, NEG_INF).astype(BF16)
    for h in range(HEADS_PER_KV):
        qa_ref[h * QB:(h + 1) * QB, HEAD_DIM:FEAT] = feat
    qa = qa_ref[...]

    def reset():
        m_ref[...] = jnp.full(m_ref.shape, -3e38, F32)
        acc_ref[...] = jnp.zeros(acc_ref.shape, F32)

    def step(k_ref, v_ref, j, bias):
        off = pl.multiple_of(j * KT, KT)
        s = _dot_nt(qa, k_ref[pl.ds(off, KT), :])
        if bias is not None:
            s = s + bias
        m_old = m_ref[...]
        m_new = jnp.maximum(m_old, jnp.max(s, axis=-1, keepdims=True))
        p = jnp.exp(s - m_new)
        acc_ref[...] = jnp.exp(m_old - m_new) * acc_ref[...] + _dot(p.astype(BF16), v_ref[pl.ds(off, KT), :])
        m_ref[...] = m_new

    def result():
        acc = acc_ref[...]
        return acc[:, 0:HEAD_DIM] * (1.0 / acc[:, HEAD_DIM:HEAD_DIM + 1])

    reset()

    def far_step(j, carry):
        step(ks_ref, vs_ref, j, None)
        return carry

    lax.fori_loop(0, jnp.maximum(qi - 1, 0), far_step, 0)

    @pl.when(qi >= 1)
    def _():
        step(ks_ref, vs_ref, qi - 1, tz_ref[1])

    step(ks_ref, vs_ref, qi, tz_ref[0])
    o_slc = result()

    reset()
    n_back = WINDOW // KT
    edge = jnp.where(lax.broadcasted_iota(jnp.int32, (rows_all, KT), 1) >= i_row, 0.0, NEG_INF)
    for back in range(n_back, -1, -1):
        bias = edge if back == n_back else tz_ref[1] if back == 1 else tz_ref[0] if back == 0 else None

        @pl.when(qi >= back)
        def _(back=back, bias=bias):
            step(kw_ref, vw_ref, qi - back, bias)

    o_win = result()

    gate = gate_ref[...]
    for h in range(HEADS_PER_KV):
        rows = slice(h * QB, (h + 1) * QB)
        o = (gate[:, h:h + 1] * o_cmp[rows] + gate[:, HEADS_PER_KV + h:HEADS_PER_KV + h + 1] * o_slc[rows]
             + gate[:, 2 * HEADS_PER_KV + h:2 * HEADS_PER_KV + h + 1] * o_win[rows])
        o_ref[:, h * HEAD_DIM:(h + 1) * HEAD_DIM] = o.astype(o_ref.dtype)


def _sel_overlap_rows(n_rows, n_cmp, pad):
    nf = FEAT - HEAD_DIM
    cs = (np.arange(n_rows) - pad)[:, None] * CMP_STRIDE
    ss = np.arange(nf)[None, :] * SEL_BLOCK
    ov = np.minimum(cs + CMP_BLOCK, ss + SEL_BLOCK) - np.maximum(cs, ss)
    ov = np.maximum(ov, 0) / CMP_STRIDE
    valid = (np.arange(n_rows) >= pad) & (np.arange(n_rows) < pad + n_cmp)
    return (ov * valid[:, None]).astype(np.float32)


def prompt_attention(q, gate, kc, vc, tc, tz, ks, vs, kw, vw, b, t):
    n_chunk = t // CMP_STRIDE
    rc = n_chunk + 2 * CMP_PAD
    assert t // SEL_BLOCK <= FEAT - HEAD_DIM and t % QB == 0 and n_chunk % LANE == 0
    ov = jnp.asarray(_sel_overlap_rows(rc, n_chunk - 1, CMP_PAD))
    nq = t // QB
    gw = HEADS_PER_KV * HEAD_DIM
    kv_spec = pl.BlockSpec((None, None, t, FEAT), lambda i, g, qi: (i, g, 0, 0))
    return pl.pallas_call(
        functools.partial(_prompt_attn_kernel, n_chunk=n_chunk), grid=(b, N_KV_HEADS, nq), name="prompt_attention",
        in_specs=[pl.BlockSpec((QB, gw), lambda i, g, qi: (i * nq + qi, g)),
                  pl.BlockSpec((QB, LANE), lambda i, g, qi: (i * nq + qi, g)),
                  pl.BlockSpec((None, None, rc, FEAT), lambda i, g, qi: (i, g, 0, 0)),
                  pl.BlockSpec((None, None, rc, HEAD_DIM), lambda i, g, qi: (i, g, 0, 0)),
                  pl.BlockSpec((rc, FEAT - HEAD_DIM), lambda i, g, qi: (0, 0)),
                  pl.BlockSpec((HEADS_PER_KV * QB, CMP_NEAR), lambda i, g, qi: (g, 0)),
                  pl.BlockSpec((2, HEADS_PER_KV * QB, KT), lambda i, g, qi: (0, g, 0)),
                  kv_spec, kv_spec, kv_spec, kv_spec],
        out_specs=pl.BlockSpec((QB, gw), lambda i, g, qi: (i * nq + qi, g)),
        out_shape=jax.ShapeDtypeStruct((b * t, ATTN_WIDTH), BF16),
        scratch_shapes=[pltpu.VMEM((HEADS_PER_KV * QB, FEAT), BF16),
                        pltpu.VMEM((HEADS_PER_KV * QB, LANE), F32),
                        pltpu.VMEM((HEADS_PER_KV * QB, LANE), F32)],
        compiler_params=_params("parallel", "parallel", "arbitrary"))(q, gate, kc, vc, ov, tc, tz, ks, vs, kw, vw)


TQ = 256
T_PAD = TQ // CMP_STRIDE
T_NEAR = 2 * T_PAD
V_ROWS = HEAD_DIM + 16
assert FAR_DIST <= TQ and WINDOW % TQ == 0


def _bias_tables_t_kernel(rb_ref, tz_ref, tc_ref):
    j = lax.broadcasted_iota(jnp.int32, (TQ, TQ), 0)
    i = lax.broadcasted_iota(jnp.int32, (TQ, TQ), 1)
    cc = lax.broadcasted_iota(jnp.int32, (T_NEAR, TQ), 0)
    ic = lax.broadcasted_iota(jnp.int32, (T_NEAR, TQ), 1)
    d0 = i - j
    d1 = d0 + TQ
    dc = ic + (TQ - CMP_BLOCK + 1) - CMP_STRIDE * cc
    for h in range(N_HEADS):
        last = rb_ref[N_BUCKETS - 1, h]
        tz_ref[0, h] = jnp.where(d0 < 0, NEG_INF, _rel_bias(d0, rb_ref, h) - last)
        tz_ref[1, h] = _rel_bias(d1, rb_ref, h) - last
        tc_ref[h] = jnp.where(dc < 0, NEG_INF, _rel_bias(dc, rb_ref, h) - last)


def bias_tables_t(rel_bias):
    return pl.pallas_call(
        _bias_tables_t_kernel, name="bias_tables",
        in_specs=[pl.BlockSpec(memory_space=pltpu.SMEM)],
        out_specs=[pl.BlockSpec(memory_space=pltpu.VMEM), pl.BlockSpec(memory_space=pltpu.VMEM)],
        out_shape=[jax.ShapeDtypeStruct((2, N_HEADS, TQ, TQ), F32),
                   jax.ShapeDtypeStruct((N_HEADS, T_NEAR, TQ), F32)],
        compiler_params=pltpu.CompilerParams(vmem_limit_bytes=VMEM_LIMIT))(rel_bias)


def _prompt_attn_t_kernel(qt_ref, gate_ref, kc_ref, vct_ref, ovt_ref, tc_ref, tz_ref, ks_ref, vst_ref, kw_ref, vwt_ref,
                          o_ref, qa_ref, lc_ref, m_ref, acc_ref, ot_ref):
    qi = pl.program_id(2)
    nf = FEAT - HEAD_DIM
    hp = HEADS_PER_KV
    for h in range(hp):
        qa_ref[h, 0:HEAD_DIM, :] = qt_ref[h * HEAD_DIM:(h + 1) * HEAD_DIM, :]
        qa_ref[h, HEAD_DIM:FEAT, :] = jnp.zeros((nf, TQ), BF16)
    t_lane = qi * TQ + lax.broadcasted_iota(jnp.int32, (1, TQ), 1)

    near0 = pl.multiple_of(qi * T_PAD, T_PAD)
    rc = kc_ref.shape[0]
    row = lax.broadcasted_iota(jnp.int32, (rc, 1), 0)
    row_ok = (row >= T_PAD) & (row < near0 + T_NEAR)
    has_block = t_lane >= CMP_BLOCK - 1
    kc = kc_ref[...].astype(BF16)
    vct = vct_ref[...].astype(BF16)
    for h in range(hp):
        lc_ref[h] = _dot(kc, qa_ref[h])
    for h in range(hp):
        lc_ref[h, pl.ds(near0, T_NEAR), :] = lc_ref[h, pl.ds(near0, T_NEAR), :] + tc_ref[h]
    p_heads = jnp.zeros((rc, TQ), F32)
    for h in range(hp):
        lc = jnp.where(row_ok, lc_ref[h], NEG_INF)
        p = jnp.exp(lc - jnp.max(lc, axis=0, keepdims=True))
        p = p * jnp.where(has_block, 1.0 / jnp.sum(p, axis=0, keepdims=True), 0.0)
        p_heads = p_heads + p
        ot_ref[h * HEAD_DIM:(h + 1) * HEAD_DIM, :] = gate_ref[h:h + 1, :] * _dot(vct, p.astype(BF16))
    p_slc = _dot(ovt_ref[...], p_heads, precision=lax.Precision.HIGHEST)

    blk = lax.broadcasted_iota(jnp.int32, (nf, TQ), 0)
    jt = t_lane // SEL_BLOCK
    forced = (blk == 0) | (blk == jt) | (blk == jt - 1)
    score = jnp.where(forced, jnp.inf, jnp.where(blk <= jt, p_slc, -jnp.inf))
    n_grp = nf // SUBLANE
    groups = [score[SUBLANE * g:SUBLANE * (g + 1), :] for g in range(n_grp)]
    ranks = [jnp.zeros((SUBLANE, TQ), jnp.int32) for _ in range(n_grp)]
    sub = lax.broadcasted_iota(jnp.int32, (SUBLANE, TQ), 0)
    for c in range(nf):
        cg, cs = divmod(c, SUBLANE)
        other = jnp.broadcast_to(groups[cg][cs:cs + 1, :], (SUBLANE, TQ))
        for g in range(n_grp):
            if g > cg:
                beats = (other >= groups[g]).astype(jnp.int32)
            elif g < cg:
                beats = (other > groups[g]).astype(jnp.int32)
            else:
                beats = jnp.where(sub > cs, (other >= groups[g]).astype(jnp.int32), (other > groups[g]).astype(jnp.int32))
            ranks[g] = ranks[g] + beats
    feat = jnp.concatenate(
        [jnp.where((ranks[g] < SEL_TOPK) & (groups[g] > NEG_INF), 0.0, NEG_INF) for g in range(n_grp)], axis=0)
    feat = feat.astype(BF16)
    for h in range(hp):
        qa_ref[h, HEAD_DIM:FEAT, :] = feat

    def reset():
        m_ref[...] = jnp.full(m_ref.shape, -3e38, F32)
        acc_ref[...] = jnp.zeros(acc_ref.shape, F32)

    def step(k_ref, vt_ref, j, bias):
        k = k_ref[pl.ds(pl.multiple_of(j * TQ, TQ), TQ), :]
        vt = vt_ref[j]
        m_old = [m_ref[h] for h in range(hp)]
        acc_old = [acc_ref[h] for h in range(hp)]
        s = [_dot(k, qa_ref[h]) for h in range(hp)]
        if bias is not None:
            s = [s[h] + bias(h) for h in range(hp)]
        m_new = [jnp.maximum(m_old[h], jnp.max(s[h], axis=0, keepdims=True)) for h in range(hp)]
        pv = [_dot(vt, jnp.exp(s[h] - m_new[h]).astype(BF16)) for h in range(hp)]
        for h in range(hp):
            acc_ref[h] = jnp.exp(m_old[h] - m_new[h]) * acc_old[h] + pv[h]
            m_ref[h] = m_new[h]

    def add_result(branch):
        for h in range(hp):
            acc = acc_ref[h]
            o = acc[0:HEAD_DIM, :] * (1.0 / acc[HEAD_DIM:HEAD_DIM + 1, :])
            rows = slice(h * HEAD_DIM, (h + 1) * HEAD_DIM)
            ot_ref[rows, :] = ot_ref[rows, :] + gate_ref[branch * hp + h:branch * hp + h + 1, :] * o

    diag = lambda h: tz_ref[0, h]
    prev = lambda h: tz_ref[1, h]

    reset()

    def far_step(j, carry):
        step(ks_ref, vst_ref, j, None)
        return carry

    lax.fori_loop(0, jnp.maximum(qi - 1, 0), far_step, 0)

    @pl.when(qi >= 1)
    def _():
        step(ks_ref, vst_ref, qi - 1, prev)

    step(ks_ref, vst_ref, qi, diag)
    add_result(1)

    reset()
    n_back = WINDOW // TQ
    edge = jnp.where(lax.broadcasted_iota(jnp.int32, (TQ, TQ), 0) >= lax.broadcasted_iota(jnp.int32, (TQ, TQ), 1),
                     0.0, NEG_INF)
    for back in range(n_back, -1, -1):
        bias = (lambda h: edge) if back == n_back else prev if back == 1 else diag if back == 0 else None

        @pl.when(qi >= back)
        def _(back=back, bias=bias):
            step(kw_ref, vwt_ref, qi - back, bias)

    add_result(2)
    o_ref[...] = ot_ref[...].T.astype(o_ref.dtype)


def prompt_attention_t(qt, gate_t, kc, vct, tc, tz, ks, vst, kw, vwt, b, t):
    n_chunk = t // CMP_STRIDE
    rc = kc.shape[2]
    nf = FEAT - HEAD_DIM
    nq = t // TQ
    assert t // SEL_BLOCK <= nf and t % TQ == 0 and rc >= T_PAD * (nq + 1) and rc >= T_PAD + n_chunk - 1
    ovt = jnp.asarray(_sel_overlap_rows(rc, n_chunk - 1, T_PAD).T)
    hp = HEADS_PER_KV
    gw = hp * HEAD_DIM
    k_spec = pl.BlockSpec((None, None, t, FEAT), lambda i, g, qi: (i, g, 0, 0))
    v_spec = pl.BlockSpec((None, None, nq, V_ROWS, TQ), lambda i, g, qi: (i, g, 0, 0, 0))
    return pl.pallas_call(
        _prompt_attn_t_kernel, grid=(b, N_KV_HEADS, nq), name="prompt_attention",
        in_specs=[pl.BlockSpec((gw, TQ), lambda i, g, qi: (g, i * nq + qi)),
                  pl.BlockSpec((4 * hp, TQ), lambda i, g, qi: (g, i * nq + qi)),
                  pl.BlockSpec((None, None, rc, FEAT), lambda i, g, qi: (i, g, 0, 0)),
                  pl.BlockSpec((None, None, HEAD_DIM, rc), lambda i, g, qi: (i, g, 0, 0)),
                  pl.BlockSpec((nf, rc), lambda i, g, qi: (0, 0)),
                  pl.BlockSpec((hp, T_NEAR, TQ), lambda i, g, qi: (g, 0, 0)),
                  pl.BlockSpec((2, hp, TQ, TQ), lambda i, g, qi: (0, g, 0, 0)),
                  k_spec, v_spec, k_spec, v_spec],
        out_specs=pl.BlockSpec((TQ, gw), lambda i, g, qi: (i * nq + qi, g)),
        out_shape=jax.ShapeDtypeStruct((b * t, ATTN_WIDTH), BF16),
        scratch_shapes=[pltpu.VMEM((hp, FEAT, TQ), BF16), pltpu.VMEM((hp, rc, TQ), F32),
                        pltpu.VMEM((hp, 1, TQ), F32), pltpu.VMEM((hp, V_ROWS, TQ), F32), pltpu.VMEM((gw, TQ), F32)],
        compiler_params=_params("parallel", "parallel", "arbitrary"))(
            qt, gate_t, kc, vct, ovt, tc, tz, ks, vst, kw, vwt)


def _gmlp_kernel(u_ref, v_ref, lng_ref, lnb_ref, ws_ref, bs_ref, o_ref):
    v = v_ref[...].astype(F32)
    mu = jnp.mean(v, axis=-1, keepdims=True)
    var = jnp.mean(jnp.square(v - mu), axis=-1, keepdims=True)
    vn = ((v - mu) * lax.rsqrt(var + LN_EPS) * lng_ref[...] + lnb_ref[...]).astype(BF16)
    tri = (lax.broadcasted_iota(jnp.int32, (CHUNK, CHUNK), 0) >= lax.broadcasted_iota(jnp.int32, (CHUNK, CHUNK), 1))
    for g in range(GM_GROUPS):
        w = jnp.where(tri, ws_ref[g], 0.0).astype(BF16)
        cols = slice(g * GM_GROUP_DIM, (g + 1) * GM_GROUP_DIM)
        for c in range(u_ref.shape[0] // CHUNK):
            rows = slice(c * CHUNK, (c + 1) * CHUNK)
            s = _dot(w, vn[rows, cols]) + bs_ref[:, g:g + 1]
            o_ref[rows, cols] = (u_ref[rows, cols].astype(F32) * s).astype(o_ref.dtype)


def gmlp_chunks(uv, ln_g, ln_b, w_s, b_s, rows):
    m = uv.shape[0]
    return pl.pallas_call(
        _gmlp_kernel, grid=(m // rows,), name="gmlp",
        in_specs=[pl.BlockSpec((rows, GM_WIDTH), lambda i: (i, 0)), pl.BlockSpec((rows, GM_WIDTH), lambda i: (i, 1)),
                  pl.BlockSpec((1, GM_WIDTH), lambda i: (0, 0)), pl.BlockSpec((1, GM_WIDTH), lambda i: (0, 0)),
                  pl.BlockSpec((GM_GROUPS, CHUNK, CHUNK), lambda i: (0, 0, 0)),
                  pl.BlockSpec((CHUNK, GM_GROUPS), lambda i: (0, 0))],
        out_specs=pl.BlockSpec((rows, GM_WIDTH), lambda i: (i, 0)),
        out_shape=jax.ShapeDtypeStruct((m, GM_WIDTH), BF16),
        compiler_params=_params("parallel"))(uv, uv, ln_g.reshape(1, -1), ln_b.reshape(1, -1), w_s, b_s.T)


def _merge_kernel(attn_ref, gm_ref, mg_ref, x_ref, wa_ref, wb_ref, wo_ref, gn_ref, wr_ref, br_ref,
                  h_ref, hn_ref, idx_ref, wt_ref):
    ta = _dot(attn_ref[...], wa_ref[...])
    tb = _dot(gm_ref[...], wb_ref[...])
    merged = mg_ref[:, 0:D_MODEL].astype(F32) * ta + mg_ref[:, D_MODEL:2 * D_MODEL].astype(F32) * tb
    h = x_ref[...] + _dot(merged.astype(BF16), wo_ref[...])
    h_ref[...] = h
    hn = h * lax.rsqrt(jnp.mean(h * h, axis=-1, keepdims=True) + RMS_EPS) * gn_ref[...]
    hn_ref[...] = hn.astype(hn_ref.dtype)
    logits = _dot(hn, wr_ref[...], precision=lax.Precision.HIGHEST) + br_ref[...]
    lane = lax.broadcasted_iota(jnp.int32, logits.shape, 1)
    vals, idxs = [], []
    for _ in range(TOP_K):
        best = jnp.max(logits, axis=-1, keepdims=True)
        ix = jnp.min(jnp.where(logits == best, lane, LANE), axis=-1, keepdims=True)
        vals.append(best)
        idxs.append(ix)
        logits = jnp.where(lane == ix, -jnp.inf, logits)
    ex = [jnp.exp(v - vals[0]) for v in vals]
    inv = 1.0 / functools.reduce(lambda a, b: a + b, ex)
    idx_out = jnp.zeros(lane.shape, jnp.int32)
    wt_out = jnp.zeros(lane.shape, F32)
    for k in range(TOP_K):
        idx_out = jnp.where(lane == k, idxs[k], idx_out)
        wt_out = jnp.where(lane == k, ex[k] * inv, wt_out)
    idx_ref[...] = idx_out
    wt_ref[...] = wt_out


def merge_and_route(attn, gm, mg, x, wa, wb, wo, g_ffn, w_router, b_router, tm):
    m = x.shape[0]
    tm = min(tm, m)
    wr = jnp.zeros((D_MODEL, LANE), F32).at[:, :N_EXPERTS].set(w_router)
    br = jnp.full((1, LANE), -jnp.inf, F32).at[0, :N_EXPERTS].set(b_router)
    row = lambda w: pl.BlockSpec((tm, w), lambda i: (i, 0))
    full = lambda a: pl.BlockSpec(a.shape, lambda i: (0, 0))
    return pl.pallas_call(
        _merge_kernel, grid=(m // tm,), name="merge_route",
        in_specs=[row(ATTN_WIDTH), row(GM_WIDTH), row(2 * D_MODEL), row(D_MODEL), full(wa), full(wb), full(wo),
                  pl.BlockSpec((1, D_MODEL), lambda i: (0, 0)), full(wr), full(br)],
        out_specs=[row(D_MODEL), row(D_MODEL), row(LANE), row(LANE)],
        out_shape=[jax.ShapeDtypeStruct((m, D_MODEL), F32), jax.ShapeDtypeStruct((m, D_MODEL), BF16),
                   jax.ShapeDtypeStruct((m, LANE), jnp.int32), jax.ShapeDtypeStruct((m, LANE), F32)],
        compiler_params=_params("parallel"))(attn, gm, mg, x, wa, wb, wo, g_ffn.reshape(1, -1), wr, br)


def _cast_weight(src_ref, dst_ref):
    step = 256
    for r0 in range(0, src_ref.shape[0], step):
        dst_ref[r0:r0 + step, :] = src_ref[r0:r0 + step, :].astype(dst_ref.dtype)


def _expert_weights(te_ref, first_ref, nxt_ref, w_hbm, stage_ref, cache_ref, sem):
    j, r = pl.program_id(0), pl.program_id(1)
    tn = stage_ref.shape[-1]

    def copies(e, jj):
        cols = pl.ds(pl.multiple_of(jj * tn, tn), tn)
        return [pltpu.make_async_copy(w.at[e, :, cols], stage_ref.at[i], sem.at[i]) for i, w in enumerate(w_hbm)]

    @pl.when((j == 0) & (r == 0))
    def _():
        for c in copies(te_ref[0], 0):
            c.start()

    @pl.when(first_ref[r] == 1)
    def _():
        for c in copies(te_ref[r], j):
            c.wait()
        for i in range(len(w_hbm)):
            _cast_weight(stage_ref.at[i], cache_ref.at[i])
        wraps = nxt_ref[r] < 0
        e_next = jnp.where(wraps, te_ref[0], nxt_ref[r])
        j_next = jnp.where(wraps, j + 1, j)

        @pl.when(j_next < pl.num_programs(0))
        def _():
            for c in copies(e_next, j_next):
                c.start()


def _moe_up_kernel(te_ref, first_ref, nact_ref, nxt_ref, x_ref, wg_hbm, wu_hbm, bg_ref, bu_ref, o_ref,
                   stage_ref, cache_ref, sem):
    r = pl.program_id(1)
    _expert_weights(te_ref, first_ref, nxt_ref, (wg_hbm, wu_hbm), stage_ref, cache_ref, sem)
    wgb_ref, wub_ref = cache_ref.at[0], cache_ref.at[1]

    @pl.when(r < nact_ref[0])
    def _():
        x = x_ref[...]
        g = jnp.minimum(_dot(x, wgb_ref[...]) + bg_ref[...], SWIGLU_LIMIT)
        u = jnp.clip(_dot(x, wub_ref[...]) + bu_ref[...], -SWIGLU_LIMIT, SWIGLU_LIMIT)
        o_ref[...] = ((u + 1.0) * (g * jax.nn.sigmoid(SWIGLU_ALPHA * g))).astype(o_ref.dtype)

    @pl.when(r >= nact_ref[0])
    def _():
        o_ref[...] = jnp.zeros(o_ref.shape, o_ref.dtype)


def _moe_down_kernel(te_ref, first_ref, nact_ref, nxt_ref, a_ref, wd_hbm, bd_ref, o_ref, stage_ref, cache_ref, sem):
    r = pl.program_id(1)
    _expert_weights(te_ref, first_ref, nxt_ref, (wd_hbm,), stage_ref, cache_ref, sem)
    wdb_ref = cache_ref.at[0]

    @pl.when(r < nact_ref[0])
    def _():
        o_ref[...] = (_dot(a_ref[...], wdb_ref[...]) + bd_ref[...]).astype(o_ref.dtype)

    @pl.when(r >= nact_ref[0])
    def _():
        o_ref[...] = jnp.zeros(o_ref.shape, o_ref.dtype)


def moe_experts(x_rows, tile_expert, tile_first, n_active, tile_next, w_gate, b_gate, w_up, b_up, w_down, b_down):
    rows = x_rows.shape[0]
    n_tiles = rows // MOE_TM
    row_map = lambda j, r, te, first, nact, nxt: (jnp.minimum(r, nact[0] - 1), 0)
    out_map = lambda j, r, te, first, nact, nxt: (r, j)
    b_spec = pl.BlockSpec((None, 1, MOE_TN), lambda j, r, te, first, nact, nxt: (te[r], 0, j))
    hbm = pl.BlockSpec(memory_space=pl.ANY)

    def scratch(n_w, k):
        return [pltpu.VMEM((n_w, k, MOE_TN), F32), pltpu.VMEM((n_w, k, MOE_TN), BF16), pltpu.SemaphoreType.DMA((n_w,))]

    act = pl.pallas_call(
        _moe_up_kernel, name="moe_up",
        grid_spec=pltpu.PrefetchScalarGridSpec(
            num_scalar_prefetch=4, grid=(D_FF // MOE_TN, n_tiles),
            in_specs=[pl.BlockSpec((MOE_TM, D_MODEL), row_map), hbm, hbm, b_spec, b_spec],
            out_specs=pl.BlockSpec((MOE_TM, MOE_TN), out_map),
            scratch_shapes=scratch(2, D_MODEL)),
        out_shape=jax.ShapeDtypeStruct((rows, D_FF), BF16),
        compiler_params=_params("arbitrary", "arbitrary"))(
            tile_expert, tile_first, n_active, tile_next, x_rows, w_gate, w_up,
            b_gate.reshape(N_EXPERTS, 1, D_FF), b_up.reshape(N_EXPERTS, 1, D_FF))
    return pl.pallas_call(
        _moe_down_kernel, name="moe_down",
        grid_spec=pltpu.PrefetchScalarGridSpec(
            num_scalar_prefetch=4, grid=(D_MODEL // MOE_TN, n_tiles),
            in_specs=[pl.BlockSpec((MOE_TM, D_FF), row_map), hbm, b_spec],
            out_specs=pl.BlockSpec((MOE_TM, MOE_TN), out_map),
            scratch_shapes=scratch(1, D_FF)),
        out_shape=jax.ShapeDtypeStruct((rows, D_MODEL), BF16),
        compiler_params=_params("arbitrary", "arbitrary"))(
            tile_expert, tile_first, n_active, tile_next, act, w_down, b_down.reshape(N_EXPERTS, 1, D_MODEL))


def route_rows(top_i, n_rows):
    n = top_i.shape[0]
    hit = top_i[:, :, None] == jnp.arange(N_EXPERTS, dtype=jnp.int32)[None, None, :]
    onehot = hit.astype(jnp.int32).sum(axis=1)
    before = jnp.cumsum(onehot, axis=0) - onehot
    counts = onehot.sum(axis=0)
    padded = (counts + MOE_TM - 1) // MOE_TM * MOE_TM
    ends = jnp.cumsum(padded)
    starts = ends - padded
    pos = jnp.sum(jnp.where(hit, (starts[None, :] + before)[:, None, :], 0), axis=-1)
    pair_id = jnp.arange(TOP_K, dtype=jnp.int32)[None, :] * n + jnp.arange(n, dtype=jnp.int32)[:, None]
    row_pair = jnp.arange(n_rows, dtype=jnp.int32).at[pos.reshape(-1)].set(pair_id.reshape(-1), unique_indices=True)
    n_tiles = n_rows // MOE_TM
    n_active = (ends[-1] // MOE_TM).astype(jnp.int32)
    tile_row = jnp.minimum(jnp.arange(n_tiles, dtype=jnp.int32), n_active - 1) * MOE_TM
    tile_expert = jnp.minimum((tile_row[:, None] >= ends[None, :]).astype(jnp.int32).sum(axis=-1), N_EXPERTS - 1)
    tile_first = jnp.concatenate([jnp.ones((1,), jnp.int32), (tile_expert[1:] != tile_expert[:-1]).astype(jnp.int32)])
    tile_id = jnp.arange(n_tiles, dtype=jnp.int32)
    start_id = jnp.where(tile_first == 1, tile_id, n_tiles)
    next_start = jnp.concatenate([lax.cummin(start_id[::-1])[::-1][1:], jnp.full((1,), n_tiles, jnp.int32)])
    tile_next = jnp.where(next_start < n_tiles, tile_expert[jnp.minimum(next_start, n_tiles - 1)], -1)
    return pos, row_pair, tile_expert, tile_first, n_active.reshape(1), tile_next.astype(jnp.int32)


def _combine_kernel(h_ref, y_ref, w_ref, g_ref, o_ref):
    h = h_ref[...]
    w = w_ref[...]
    for k in range(TOP_K):
        h = h + w[:, k:k + 1] * y_ref[k].astype(F32)
    o_ref[...] = h * lax.rsqrt(jnp.mean(h * h, axis=-1, keepdims=True) + RMS_EPS) * g_ref[...]


def combine_and_norm(h, y4, wts, g, tm):
    m = h.shape[0]
    tm = min(tm, m)
    return pl.pallas_call(
        _combine_kernel, grid=(m // tm,), name="combine_norm",
        in_specs=[pl.BlockSpec((tm, D_MODEL), lambda i: (i, 0)), pl.BlockSpec((TOP_K, tm, D_MODEL), lambda i: (0, i, 0)),
                  pl.BlockSpec((tm, LANE), lambda i: (i, 0)), pl.BlockSpec((1, D_MODEL), lambda i: (0, 0))],
        out_specs=pl.BlockSpec((tm, D_MODEL), lambda i: (i, 0)),
        out_shape=jax.ShapeDtypeStruct((m, D_MODEL), F32), compiler_params=_params("parallel"))(
            h, y4, wts, g.reshape(1, -1))


def _in_weights(w_in):
    c0 = ATTN_WIDTH
    c1 = c0 + 3 * KV_COLS
    c2 = c1 + GATE_COLS
    c3 = c2 + 2 * GM_WIDTH
    wg = w_in[:, c1:c2].reshape(D_MODEL, N_KV_HEADS, HEADS_PER_KV, 3).transpose(0, 1, 3, 2)
    wg = jnp.pad(wg.reshape(D_MODEL, N_KV_HEADS, 3 * HEADS_PER_KV), ((0, 0), (0, 0), (0, LANE - 3 * HEADS_PER_KV)))
    cast = lambda w: w.astype(BF16)
    return (cast(w_in[:, :c0]), cast(w_in[:, c0:c1]), cast(wg.reshape(D_MODEL, N_KV_HEADS * LANE)),
            cast(w_in[:, c2:c3]), cast(w_in[:, c3:]))


def _project(a, weights, wide_dtype):
    w_q, w_kv, w_ng, w_uv, w_mg = weights
    q = matmul(a, w_q, lambda z: z * np.float32(HEAD_DIM ** -0.5), wide_dtype, 1024, 1024, "proj_q")
    kv = matmul(a, w_kv, lambda z: z, F32, 1024, 3 * KV_COLS, "proj_kv")
    ng = matmul(a, w_ng, jax.nn.sigmoid, F32, 1024, N_KV_HEADS * LANE, "proj_gate")
    uv = matmul(a, w_uv, _gelu, wide_dtype, 1024, 1024, "proj_uv")
    mg = matmul(a, w_mg, jax.nn.sigmoid, BF16, 1024, 1024, "proj_merge")
    return q, kv, ng, uv, mg


def _head_rows(fn):
    out = fn(0)
    out = jnp.broadcast_to(out, (HEADS_PER_KV, out.shape[1]))
    row = lax.broadcasted_iota(jnp.int32, out.shape, 0)
    for h in range(1, HEADS_PER_KV):
        out = jnp.where(row == h, fn(h), out)
    return out


def _sample_cmp_win_kernel(rb_ref, q_ref, kvc_ref, ov_ref, sw_ref, new_ref, gate_ref, o_ref, idx_ref, *, past, n_cmp):
    n_rows = kvc_ref.shape[2]
    n_blk = ov_ref.shape[1]
    win = sw_ref.shape[-1]
    win_col = 2 * KV_COLS
    lane_blk = lax.broadcasted_iota(jnp.int32, (1, n_blk), 1)
    lane_out = lax.broadcasted_iota(jnp.int32, (1, LANE), 1)
    for g in range(N_KV_HEADS):
        heads = slice(g * HEADS_PER_KV, (g + 1) * HEADS_PER_KV)
        q = q_ref[heads, :]
        qb = q.astype(BF16)

        n = lax.broadcasted_iota(jnp.int32, (1, n_rows), 1)
        d = past - (n * CMP_STRIDE + CMP_BLOCK - 1)
        logit = _dot_nt(qb, kvc_ref[0, g].astype(BF16)) + _head_rows(
            lambda h: _rel_bias(d, rb_ref, g * HEADS_PER_KV + h))
        logit = jnp.where(n < n_cmp, logit, NEG_INF)
        p = jnp.exp(logit - jnp.max(logit, axis=-1, keepdims=True))
        p = p * (1.0 / jnp.sum(p, axis=-1, keepdims=True))
        o_cmp = _dot(p.astype(BF16), kvc_ref[1, g].astype(BF16))
        p_heads = jnp.broadcast_to(jnp.sum(p, axis=0, keepdims=True), (SUBLANE, n_rows))
        p_slc = _dot(p_heads, ov_ref[...], precision=lax.Precision.HIGHEST)[0:1, :]

        score = jnp.where((lane_blk == 0) | (lane_blk == n_blk - 1), jnp.inf, p_slc)
        picked = jnp.zeros((1, LANE), jnp.int32)
        for k in range(SEL_TOPK - 1):
            best = jnp.max(score, axis=-1, keepdims=True)
            ix = jnp.min(jnp.where(score == best, lane_blk, n_blk), axis=-1, keepdims=True)
            picked = jnp.where(lane_out == k, ix, picked)
            score = jnp.where(lane_blk == ix, -jnp.inf, score)
        idx_ref[g:g + 1, :] = picked

        dw = win - lax.broadcasted_iota(jnp.int32, (1, win), 1)
        lw = _dot(qb, sw_ref[0, g].astype(BF16)) + _head_rows(
            lambda h: _rel_bias(dw, rb_ref, g * HEADS_PER_KV + h))
        k_new = new_ref[:, win_col + g * HEAD_DIM:win_col + (g + 1) * HEAD_DIM]
        v_new = new_ref[:, win_col + KV_COLS // 2 + g * HEAD_DIM:win_col + KV_COLS // 2 + (g + 1) * HEAD_DIM]
        l_new = jnp.sum(q * k_new, axis=-1, keepdims=True) + _head_rows(
            lambda h: jnp.full((1, 1), rb_ref[0, g * HEADS_PER_KV + h], F32))
        m = jnp.maximum(jnp.max(lw, axis=-1, keepdims=True), l_new)
        pw = jnp.exp(lw - m)
        p_new = jnp.exp(l_new - m)
        o_win = ((_dot_nt(pw.astype(BF16), sw_ref[1, g].astype(BF16)) + p_new * v_new)
                 * (1.0 / (jnp.sum(pw, axis=-1, keepdims=True) + p_new)))
        o_ref[heads, :] = gate_ref[0, heads, :] * o_cmp + gate_ref[2, heads, :] * o_win


def sample_cmp_win(rel_bias, q, kvc, state_win, kv_new, gate, past):
    b, _, _, n_rows, _ = kvc.shape
    n_cmp = past // CMP_STRIDE - CMP_BLOCK // CMP_STRIDE + 1
    n_blk = past // SEL_BLOCK
    assert past % SEL_BLOCK == 0 and n_blk >= SEL_TOPK and n_cmp <= n_rows
    cs = np.arange(n_rows)[:, None] * CMP_STRIDE
    ss = np.arange(n_blk)[None, :] * SEL_BLOCK
    ov = np.maximum(np.minimum(cs + CMP_BLOCK, ss + SEL_BLOCK) - np.maximum(cs, ss), 0) / CMP_STRIDE
    ov = jnp.asarray((ov * (np.arange(n_rows) < n_cmp)[:, None]).astype(np.float32))
    win = state_win.shape[-1]
    return pl.pallas_call(
        functools.partial(_sample_cmp_win_kernel, past=past, n_cmp=n_cmp), grid=(b,), name="sample_cmp_win",
        in_specs=[pl.BlockSpec(memory_space=pltpu.SMEM),
                  pl.BlockSpec((None, N_HEADS, HEAD_DIM), lambda i: (i, 0, 0)),
                  pl.BlockSpec((None, 2, N_KV_HEADS, n_rows, HEAD_DIM), lambda i: (i, 0, 0, 0, 0)),
                  pl.BlockSpec(ov.shape, lambda i: (0, 0)),
                  pl.BlockSpec((None, 2, N_KV_HEADS, HEAD_DIM, win), lambda i: (i, 0, 0, 0, 0)),
                  pl.BlockSpec((None, 1, 3 * KV_COLS), lambda i: (i, 0, 0)),
                  pl.BlockSpec((None, 3, N_HEADS, 1), lambda i: (i, 0, 0, 0))],
        out_specs=[pl.BlockSpec((None, N_HEADS, HEAD_DIM), lambda i: (i, 0, 0)),
                   pl.BlockSpec((None, N_KV_HEADS, LANE), lambda i: (i, 0, 0))],
        out_shape=[jax.ShapeDtypeStruct((b, N_HEADS, HEAD_DIM), F32),
                   jax.ShapeDtypeStruct((b, N_KV_HEADS, LANE), jnp.int32)],
        compiler_params=_params("parallel"))(rel_bias, q, kvc, ov, state_win, kv_new, gate)


def _sample_slc_kernel(pt_ref, idx_ref, rb_ref, q_ref, new_ref, gate_ref, part_ref, *rest, past):
    blocks, o_ref = rest[:-1], rest[-1]
    b, g = pl.program_id(0), pl.program_id(1)
    q = q_ref[...]
    qb = q.astype(BF16)
    slc_col = KV_COLS
    half = KV_COLS // 2
    page = blocks[0].shape[-1]
    pos = lax.broadcasted_iota(jnp.int32, (1, page), 1)
    logits, values = [], []
    for k, blk_ref in enumerate(blocks):
        ix = idx_ref[(b * N_KV_HEADS + g) * SEL_TOPK + k]
        d = past - ((ix // SLC_PER_PAGE) * page + pos)
        lg = _dot(qb, blk_ref[0].astype(BF16)) + _head_rows(lambda h: _rel_bias(d, rb_ref, g * HEADS_PER_KV + h))
        logits.append(jnp.where(pos // SEL_BLOCK == ix % SLC_PER_PAGE, lg, NEG_INF))
        values.append(blk_ref[1].astype(BF16))
    new = new_ref[...]
    k_new = jnp.where(g == 0, new[:, slc_col:slc_col + HEAD_DIM], new[:, slc_col + HEAD_DIM:slc_col + 2 * HEAD_DIM])
    v_new = jnp.where(g == 0, new[:, slc_col + half:slc_col + half + HEAD_DIM],
                      new[:, slc_col + half + HEAD_DIM:slc_col + half + 2 * HEAD_DIM])
    l_new = jnp.sum(q * k_new, axis=-1, keepdims=True) + _head_rows(
        lambda h: jnp.full((1, 1), rb_ref[0, g * HEADS_PER_KV + h], F32))
    m = l_new
    for lg in logits:
        m = jnp.maximum(m, jnp.max(lg, axis=-1, keepdims=True))
    p_new = jnp.exp(l_new - m)
    denom = p_new
    out = p_new * v_new
    for lg, vv in zip(logits, values):
        p = jnp.exp(lg - m)
        denom = denom + jnp.sum(p, axis=-1, keepdims=True)
        out = out + _dot_nt(p.astype(BF16), vv)
    o_ref[...] = part_ref[...] + gate_ref[1] * (out * (1.0 / denom))


def sample_slc(page_table, picked, rel_bias, q, kv_new, gate, part, cache_slc, past):
    b = q.shape[0]
    page = cache_slc.shape[-1]
    n_pick = SEL_TOPK - 1

    def blk_map(k):
        def index(i, g, pt, idx, k=k):
            ix = idx[(i * N_KV_HEADS + g) * SEL_TOPK + k]
            return (pt[i, ix // SLC_PER_PAGE], 0, g, 0, 0)
        return index

    head_spec = pl.BlockSpec((None, HEADS_PER_KV, HEAD_DIM), lambda i, g, pt, idx: (i, g, 0))
    return pl.pallas_call(
        functools.partial(_sample_slc_kernel, past=past), name="sample_slc",
        grid_spec=pltpu.PrefetchScalarGridSpec(
            num_scalar_prefetch=2, grid=(b, N_KV_HEADS),
            in_specs=[pl.BlockSpec(memory_space=pltpu.SMEM), head_spec,
                      pl.BlockSpec((None, 1, 3 * KV_COLS), lambda i, g, pt, idx: (i, 0, 0)),
                      pl.BlockSpec((None, 3, HEADS_PER_KV, 1), lambda i, g, pt, idx: (i, 0, g, 0)), head_spec]
            + [pl.BlockSpec((None, 2, None, HEAD_DIM, page), blk_map(k)) for k in range(n_pick)],
            out_specs=head_spec),
        out_shape=jax.ShapeDtypeStruct((b, N_HEADS, HEAD_DIM), F32),
        compiler_params=_params("parallel", "parallel"))(
            page_table, picked, rel_bias, q, kv_new, gate, part, *([cache_slc] * n_pick))


def _gmlp_row_kernel(uv_ref, lng_ref, lnb_ref, w0_ref, b0_ref, vn_ref, o_ref):
    v = uv_ref[:, GM_WIDTH:2 * GM_WIDTH]
    mu = jnp.mean(v, axis=-1, keepdims=True)
    var = jnp.mean(jnp.square(v - mu), axis=-1, keepdims=True)
    vn = (v - mu) * lax.rsqrt(var + LN_EPS) * lng_ref[...] + lnb_ref[...]
    vn_ref[...] = vn
    o_ref[...] = (uv_ref[:, 0:GM_WIDTH] * (w0_ref[...] * vn + b0_ref[...])).astype(o_ref.dtype)


def gmlp_first_row(uv, ln_g, ln_b, w_s, b_s):
    m = uv.shape[0]
    w0 = jnp.repeat(w_s[:, 0, 0], GM_GROUP_DIM).reshape(1, GM_WIDTH)
    b0 = jnp.repeat(b_s[:, 0], GM_GROUP_DIM).reshape(1, GM_WIDTH)
    return pl.pallas_call(
        _gmlp_row_kernel, name="gmlp_row",
        out_shape=[jax.ShapeDtypeStruct((m, GM_WIDTH), F32), jax.ShapeDtypeStruct((m, GM_WIDTH), BF16)])(
            uv, ln_g.reshape(1, -1), ln_b.reshape(1, -1), w0, b0)


def _group_major(x, feat):
    b, t = x.shape[0], x.shape[1]
    f = jnp.broadcast_to(feat.astype(BF16)[None, None], (b, N_KV_HEADS, t, FEAT - HEAD_DIM))
    return jnp.concatenate([x.transpose(0, 2, 1, 3).astype(BF16), f], axis=-1)


def _value_tiles(x):
    b, t = x.shape[0], x.shape[1]
    xt = x.transpose(0, 2, 3, 1).astype(BF16)
    extra = jnp.zeros((b, N_KV_HEADS, V_ROWS - HEAD_DIM, t), BF16).at[:, :, 0].set(1)
    tiles = jnp.concatenate([xt, extra], axis=2).reshape(b, N_KV_HEADS, V_ROWS, t // TQ, TQ)
    return tiles.transpose(0, 1, 3, 2, 4)


def _prompt_mixer_a(q, kv, ng, rel_bias, pe, w1, w2, b, t):
    kvr = kv.reshape(b, t, 3, 2, N_KV_HEADS, HEAD_DIM)
    n_chunk = t // CMP_STRIDE
    chunks = kvr[:, :, 0].reshape(b, n_chunk, CMP_STRIDE, 2, N_KV_HEADS, HEAD_DIM).transpose(0, 3, 4, 1, 2, 5)
    kvc = compress_chunks(chunks.reshape(b, 2, N_KV_HEADS, n_chunk, CMP_STRIDE * HEAD_DIM).astype(BF16), w1, w2, pe)
    rc = -(-(T_PAD + max(n_chunk - 1, T_PAD * t // TQ)) // LANE) * LANE
    kvc = jnp.pad(kvc[:, :, :, :n_chunk - 1], ((0, 0), (0, 0), (0, 0), (T_PAD, rc - T_PAD - n_chunk + 1), (0, 0)))
    kc = jnp.pad(kvc[:, 0], ((0, 0), (0, 0), (0, 0), (0, FEAT - HEAD_DIM)))
    nf = FEAT - HEAD_DIM
    blocks = (jnp.arange(t)[:, None] // SEL_BLOCK == jnp.arange(nf)[None, :])
    tz, tc = bias_tables_t(rel_bias)
    gate_t = ng.reshape(b * t, N_KV_HEADS, LANE)[:, :, :4 * HEADS_PER_KV].reshape(b * t, -1).T
    return prompt_attention_t(
        q.T, gate_t, kc, kvc[:, 1].transpose(0, 1, 3, 2), tc, tz,
        _group_major(kvr[:, :, 1, 0], blocks), _value_tiles(kvr[:, :, 1, 1]),
        _group_major(kvr[:, :, 2, 0], jnp.zeros((t, nf), BF16)), _value_tiles(kvr[:, :, 2, 1]), b, t)


def _sample_mixer_a(q, kv, ng, rel_bias, pe, w1, w2, cache_cmp, cache_slc, state_win, page_table):
    b, n_pages = page_table.shape
    page = cache_cmp.shape[1]
    past = n_pages * page
    n_chunk = past // CMP_STRIDE
    rows = cache_cmp[page_table].reshape(b, n_chunk, CMP_STRIDE, 2, N_KV_HEADS, HEAD_DIM).transpose(0, 3, 4, 1, 2, 5)
    kvc = compress_chunks(rows.reshape(b, 2, N_KV_HEADS, n_chunk, CMP_STRIDE * HEAD_DIM).astype(BF16), w1, w2, pe)
    gate = ng.reshape(b, N_KV_HEADS, LANE)[:, :, :3 * HEADS_PER_KV].reshape(b, N_KV_HEADS, 3, HEADS_PER_KV)
    gate = gate.transpose(0, 2, 1, 3).reshape(b, 3, N_HEADS, 1)
    q3 = q.reshape(b, N_HEADS, HEAD_DIM)
    kv_new = kv.reshape(b, 1, 3 * KV_COLS)
    part, picked = sample_cmp_win(rel_bias, q3, kvc, state_win.transpose(0, 2, 3, 4, 1), kv_new, gate, past)
    picked = picked[:, :, :SEL_TOPK].reshape(-1)
    out = sample_slc(page_table, picked, rel_bias, q3, kv_new, gate, part, cache_slc.transpose(0, 2, 3, 4, 1), past)
    return out.reshape(b, ATTN_WIDTH).astype(BF16)


def kernel(x_prompt, x_sample, cache_cmp_kv, cache_slc_kv, state_win_kv, page_table, rel_bias, norm_mix, w_in, pe_cmp, w_cmp1, w_cmp2, gm_ln_g, gm_ln_b, gm_w_s, gm_b_s, w_br_attn, w_br_gmlp, w_out, norm_ffn, w_router, b_router, w_gate, b_gate, w_up, b_up, w_down, b_down, norm_final):
    b, t, _ = x_prompt.shape
    bs, ts, _ = x_sample.shape
    depth = norm_mix.shape[0]
    assert depth == 1 and ts == 1 and cache_slc_kv.shape[2] == SLC_PER_PAGE * SEL_BLOCK
    n_p, n_s = b * t, bs * ts
    kv_tail = (2, N_KV_HEADS, HEAD_DIM)
    lyr = 0

    xp = x_prompt.reshape(n_p, D_MODEL)
    xs = x_sample.reshape(n_s, D_MODEL)
    weights = _in_weights(w_in[lyr])
    qp, kvp, ngp, uvp, mgp = _project(rmsnorm_rows(xp, norm_mix[lyr], 512), weights, BF16)
    qs, kvs, ngs, uvs, mgs = _project(rmsnorm_rows(xs, norm_mix[lyr], n_s), weights, F32)

    attn_p = _prompt_mixer_a(qp, kvp, ngp, rel_bias, pe_cmp[lyr], w_cmp1[lyr], w_cmp2[lyr], b, t)
    attn_s = _sample_mixer_a(qs, kvs, ngs, rel_bias, pe_cmp[lyr], w_cmp1[lyr], w_cmp2[lyr], cache_cmp_kv[lyr],
                             cache_slc_kv[lyr], state_win_kv[lyr], page_table)

    gm_p = gmlp_chunks(uvp, gm_ln_g[lyr], gm_ln_b[lyr], gm_w_s[lyr], gm_b_s[lyr], 4 * CHUNK)
    vn_s, gm_s = gmlp_first_row(uvs, gm_ln_g[lyr], gm_ln_b[lyr], gm_w_s[lyr], gm_b_s[lyr])

    wa, wb, wo = w_br_attn[lyr].astype(BF16), w_br_gmlp[lyr].astype(BF16), w_out[lyr].astype(BF16)
    hp, hnp, idxp, wtp = merge_and_route(attn_p, gm_p, mgp, xp, wa, wb, wo, norm_ffn[lyr], w_router[lyr], b_router[lyr], 256)
    hs, hns, idxs, wts = merge_and_route(attn_s, gm_s, mgs, xs, wa, wb, wo, norm_ffn[lyr], w_router[lyr], b_router[lyr], 256)

    n_all = n_p + n_s
    n_rows = (n_all * TOP_K + N_EXPERTS * (MOE_TM - 1) + MOE_TM - 1) // MOE_TM * MOE_TM
    top_i = jnp.concatenate([idxp[:, :TOP_K], idxs[:, :TOP_K]], axis=0)
    pos, row_pair, tile_expert, tile_first, n_active, tile_next = route_rows(top_i, n_rows)
    x_rows = jnp.concatenate([hnp, hns] * -(-n_rows // n_all), axis=0)[row_pair]
    y_rows = moe_experts(x_rows, tile_expert, tile_first, n_active, tile_next, w_gate[lyr], b_gate[lyr], w_up[lyr],
                         b_up[lyr], w_down[lyr], b_down[lyr])
    y_p = combine_and_norm(hp, y_rows[pos[:n_p].T], wtp, norm_final, 256)
    y_s = combine_and_norm(hs, y_rows[pos[n_p:].T], wts, norm_final, 256)

    kvp5 = kvp.reshape(b, t, 3, *kv_tail)
    kvs5 = kvs.reshape(bs, ts, 3, *kv_tail)
    win_buf = state_win_kv.shape[2]
    win_s = jnp.concatenate([state_win_kv[lyr], kvs5[:, :, 2]], axis=1)[:, ts:]
    return (y_p.reshape(b, t, D_MODEL), y_s.reshape(bs, ts, D_MODEL),
            kvp5[:, :, 0][None], kvs5[:, :, 0][None], kvp5[:, :, 1][None], kvs5[:, :, 1][None],
            kvp5[:, t - min(WINDOW, t):, 2][None], win_s[:, -win_buf:][None],
            vn_s.reshape(1, bs, ts, GM_WIDTH))
```

```python
import functools
import math

import jax
import jax.numpy as jnp
import numpy as np
from jax import lax
from jax.experimental import pallas as pl
from jax.experimental.pallas import tpu as pltpu

D_MODEL = 2048
N_HEADS = 16
HEAD_DIM = 64
N_KV_HEADS = 2
HEADS_PER_KV = N_HEADS // N_KV_HEADS
ATTN_WIDTH = N_HEADS * HEAD_DIM
CMP_BLOCK = 32
CMP_STRIDE = 16
CMP_HID = 2 * HEAD_DIM
SEL_BLOCK = 64
SEL_TOPK = 16
WINDOW = 512
GM_WIDTH = D_MODEL // 2
GM_GROUPS = 8
GM_GROUP_DIM = GM_WIDTH // GM_GROUPS
CHUNK = 128
N_BUCKETS = 32
REL_MAX_DIST = 128
N_EXPERTS = 32
TOP_K = 4
D_FF = D_MODEL
SWIGLU_LIMIT = 7.0
SWIGLU_ALPHA = 1.702
RMS_EPS = 1e-5
LN_EPS = 1e-5
NEG_INF = -1e30
KV_COLS = 2 * N_KV_HEADS * HEAD_DIM
GATE_COLS = 3 * N_HEADS
SLC_PER_PAGE = 2

LANE = 128
SUBLANE = 8
VMEM_LIMIT = 56 * 1024 * 1024

QB = 128
KT = 128
CMP_PAD = QB // CMP_STRIDE
CMP_NEAR = 2 * CMP_PAD
MOE_TM = 256
MOE_TN = 1024
FEAT = 2 * HEAD_DIM

BF16 = jnp.bfloat16
F32 = jnp.float32


def _bucket_upper_bounds():
    d = np.arange(0, 4 * REL_MAX_DIST)
    exact = N_BUCKETS // 2
    out = []
    for dt in (np.float32, np.float64):
        far = exact + (np.log(np.maximum(d, exact).astype(dt) / dt(exact)) / dt(math.log(REL_MAX_DIST / exact))
                       * dt(N_BUCKETS - exact)).astype(np.int32)
        out.append(np.where(d < exact, d, np.minimum(far, N_BUCKETS - 1)))
    assert (out[0] == out[1]).all() and (np.diff(out[0]) >= 0).all()
    b = out[0]
    return [int(d[b <= k].max()) for k in range(N_BUCKETS - 1)]


BUCKET_HI = _bucket_upper_bounds()
FAR_DIST = BUCKET_HI[-1] + 1
assert FAR_DIST <= QB


def _rel_bias(d, rb_ref, h):
    acc = jnp.full(d.shape, rb_ref[N_BUCKETS - 1, h], F32)
    for k in reversed(range(N_BUCKETS - 1)):
        acc = jnp.where(d <= BUCKET_HI[k], rb_ref[k, h], acc)
    return acc


def _dot(a, b, **kw):
    return jnp.dot(a, b, preferred_element_type=F32, **kw)


def _dot_nt(a, b):
    return lax.dot_general(a, b, (((1,), (1,)), ((), ())), preferred_element_type=F32)


def _gelu(x):
    return 0.5 * x * (1.0 + lax.erf(x * np.float32(math.sqrt(0.5))))


def _params(*sem):
    return pltpu.CompilerParams(dimension_semantics=sem, vmem_limit_bytes=VMEM_LIMIT)


def _rmsnorm_kernel(x_ref, g_ref, o_ref):
    x = x_ref[...]
    ms = jnp.mean(x * x, axis=-1, keepdims=True)
    o_ref[...] = (x * lax.rsqrt(ms + RMS_EPS) * g_ref[...]).astype(o_ref.dtype)


def rmsnorm_rows(x, g, tm):
    m, d = x.shape
    return pl.pallas_call(
        _rmsnorm_kernel, grid=(m // tm,), name="rmsnorm",
        in_specs=[pl.BlockSpec((tm, d), lambda i: (i, 0)), pl.BlockSpec((1, d), lambda i: (0, 0))],
        out_specs=pl.BlockSpec((tm, d), lambda i: (i, 0)),
        out_shape=jax.ShapeDtypeStruct((m, d), BF16), compiler_params=_params("parallel"))(x, g.reshape(1, d))


def _mm_kernel(a_ref, w_ref, o_ref, *, epilogue):
    o_ref[...] = epilogue(_dot(a_ref[...], w_ref[...])).astype(o_ref.dtype)


def matmul(a, w, epilogue, out_dtype, tm, tn, name):
    m, k = a.shape
    n = w.shape[1]
    tm, tn = min(tm, m), min(tn, n)
    return pl.pallas_call(
        functools.partial(_mm_kernel, epilogue=epilogue), grid=(m // tm, n // tn), name=name,
        in_specs=[pl.BlockSpec((tm, k), lambda i, j: (i, 0)), pl.BlockSpec((k, tn), lambda i, j: (0, j))],
        out_specs=pl.BlockSpec((tm, tn), lambda i, j: (i, j)),
        out_shape=jax.ShapeDtypeStruct((m, n), out_dtype), compiler_params=_params("parallel", "parallel"))(a, w)


def _compress_kernel(c_ref, w1_ref, w2_ref, pe_ref, o_ref):
    n = c_ref.shape[0]
    half = CMP_STRIDE * HEAD_DIM
    c = c_ref[...]
    first = _dot(c, w1_ref[0:half, :])
    second = _dot(c, w1_ref[half:2 * half, :])
    pe = _dot(pe_ref[...].astype(BF16), w1_ref[...])[0:1, :]
    hid = _gelu(first + pltpu.roll(second, n - 1, 0) + pe)
    o_ref[...] = _dot(hid.astype(BF16), w2_ref[...])


def compress_chunks(chunks, w1, w2, pe):
    b, _, g, n, width = chunks.shape
    pe_rows = jnp.broadcast_to(pe.reshape(2, 1, CMP_BLOCK * HEAD_DIM), (2, SUBLANE, CMP_BLOCK * HEAD_DIM))
    return pl.pallas_call(
        _compress_kernel, grid=(b, 2, g), name="compress",
        in_specs=[pl.BlockSpec((None, None, None, n, width), lambda i, s, j: (i, s, j, 0, 0)),
                  pl.BlockSpec((None, CMP_BLOCK * HEAD_DIM, CMP_HID), lambda i, s, j: (s, 0, 0)),
                  pl.BlockSpec((None, CMP_HID, HEAD_DIM), lambda i, s, j: (s, 0, 0)),
                  pl.BlockSpec((None, SUBLANE, CMP_BLOCK * HEAD_DIM), lambda i, s, j: (s, 0, 0))],
        out_specs=pl.BlockSpec((None, None, None, n, HEAD_DIM), lambda i, s, j: (i, s, j, 0, 0)),
        out_shape=jax.ShapeDtypeStruct((b, 2, g, n, HEAD_DIM), F32),
        compiler_params=_params("parallel", "parallel", "parallel"))(chunks, w1.astype(BF16), w2.astype(BF16), pe_rows)


def _bias_tables_kernel(rb_ref, tz_ref, tc_ref):
    i = lax.broadcasted_iota(jnp.int32, (QB, KT), 0)
    j = lax.broadcasted_iota(jnp.int32, (QB, KT), 1)
    ic = lax.broadcasted_iota(jnp.int32, (QB, CMP_NEAR), 0)
    cc = lax.broadcasted_iota(jnp.int32, (QB, CMP_NEAR), 1)
    d0 = i - j
    d1 = d0 + KT
    dc = ic + (QB - CMP_BLOCK + 1) - CMP_STRIDE * cc
    for h in range(N_HEADS):
        last = rb_ref[N_BUCKETS - 1, h]
        rows = slice(h * QB, (h + 1) * QB)
        tz_ref[0, rows, :] = jnp.where(d0 < 0, NEG_INF, _rel_bias(d0, rb_ref, h) - last)
        tz_ref[1, rows, :] = _rel_bias(d1, rb_ref, h) - last
        tc_ref[rows, :] = jnp.where(dc < 0, NEG_INF, _rel_bias(dc, rb_ref, h) - last)


def bias_tables(rel_bias):
    return pl.pallas_call(
        _bias_tables_kernel, name="bias_tables",
        in_specs=[pl.BlockSpec(memory_space=pltpu.SMEM)],
        out_specs=[pl.BlockSpec(memory_space=pltpu.VMEM), pl.BlockSpec(memory_space=pltpu.VMEM)],
        out_shape=[jax.ShapeDtypeStruct((2, N_HEADS * QB, KT), F32),
                   jax.ShapeDtypeStruct((N_HEADS * QB, CMP_NEAR), F32)])(rel_bias)


def _prompt_attn_kernel(q_ref, gate_ref, kc_ref, vc_ref, ov_ref, tc_ref, tz_ref, ks_ref, vs_ref, kw_ref, vw_ref,
                        o_ref, qa_ref, m_ref, acc_ref, *, n_chunk):
    qi = pl.program_id(2)
    rows_all = HEADS_PER_KV * QB

    for h in range(HEADS_PER_KV):
        qa_ref[h * QB:(h + 1) * QB, 0:HEAD_DIM] = q_ref[:, h * HEAD_DIM:(h + 1) * HEAD_DIM]
        qa_ref[h * QB:(h + 1) * QB, HEAD_DIM:FEAT] = jnp.zeros((QB, FEAT - HEAD_DIM), BF16)
    qa = qa_ref[...]
    i_row = lax.broadcasted_iota(jnp.int32, (rows_all, 1), 0) % QB

    near0 = pl.multiple_of(qi * CMP_PAD, SUBLANE)
    r = lax.broadcasted_iota(jnp.int32, (1, n_chunk), 1)
    lf = _dot_nt(qa, kc_ref[0:n_chunk, :].astype(BF16))
    lf = jnp.where((r >= CMP_PAD) & (r < qi * CMP_PAD), lf, NEG_INF)
    c = lax.broadcasted_iota(jnp.int32, (1, CMP_NEAR), 1)
    ln = _dot_nt(qa, kc_ref[pl.ds(near0, CMP_NEAR), :].astype(BF16)) + tc_ref[...]
    ln = jnp.where(qi * CMP_PAD + c >= CMP_PAD, ln, NEG_INF)
    m = jnp.maximum(jnp.max(lf, axis=-1, keepdims=True), jnp.max(ln, axis=-1, keepdims=True))
    pf = jnp.exp(lf - m)
    pn = jnp.exp(ln - m)
    denom = jnp.sum(pf, axis=-1, keepdims=True) + jnp.sum(pn, axis=-1, keepdims=True)
    inv = jnp.where(qi * QB + i_row >= CMP_BLOCK - 1, 1.0 / denom, 0.0)
    pf = pf * inv
    pn = pn * inv
    o_cmp = (_dot(pf.astype(BF16), vc_ref[0:n_chunk, :].astype(BF16))
             + _dot(pn.astype(BF16), vc_ref[pl.ds(near0, CMP_NEAR), :].astype(BF16)))
    psf = jnp.sum(pf.reshape(HEADS_PER_KV, QB, n_chunk), axis=0)
    psn = jnp.sum(pn.reshape(HEADS_PER_KV, QB, CMP_NEAR), axis=0)
    p_slc = (_dot(psf, ov_ref[0:n_chunk, :], precision=lax.Precision.HIGHEST)
             + _dot(psn, ov_ref[pl.ds(near0, CMP_NEAR), :], precision=lax.Precision.HIGHEST))

    nf = FEAT - HEAD_DIM
    blk = lax.broadcasted_iota(jnp.int32, (QB, nf), 1)
    jt = (qi * QB + lax.broadcasted_iota(jnp.int32, (QB, nf), 0)) // SEL_BLOCK
    forced = (blk == 0) | (blk == jt) | (blk == jt - 1)
    score = jnp.where(forced, jnp.inf, jnp.where(blk <= jt, p_slc, -jnp.inf))
    rank = jnp.zeros((QB, nf), jnp.int32)
    for col in range(nf):
        sc = score[:, col:col + 1]
        rank = rank + jnp.where(blk > col, (sc >= score).astype(jnp.int32), (sc > score).astype(jnp.int32))
    feat = jnp.where((rank < SEL_TOPK) & (score > NEG_INF), 0.0, NEG_INF).astype(BF16)
    for h in range(HEADS_PER_KV):
        qa_ref[h * QB:(h + 1) * QB, HEAD_DIM:FEAT] = feat
    qa = qa_ref[...]

    def reset():
        m_ref[...] = jnp.full(m_ref.shape, -3e38, F32)
        acc_ref[...] = jnp.zeros(acc_ref.shape, F32)

    def step(k_ref, v_ref, j, bias):
        off = pl.multiple_of(j * KT, KT)
        s = _dot_nt(qa, k_ref[pl.ds(off, KT), :])
        if bias is not None:
            s = s + bias
        m_old = m_ref[...]
        m_new = jnp.maximum(m_old, jnp.max(s, axis=-1, keepdims=True))
        p = jnp.exp(s - m_new)
        acc_ref[...] = jnp.exp(m_old - m_new) * acc_ref[...] + _dot(p.astype(BF16), v_ref[pl.ds(off, KT), :])
        m_ref[...] = m_new

    def result():
        acc = acc_ref[...]
        return acc[:, 0:HEAD_DIM] * (1.0 / acc[:, HEAD_DIM:HEAD_DIM + 1])

    reset()

    def far_step(j, carry):
        step(ks_ref, vs_ref, j, None)
        return carry

    lax.fori_loop(0, jnp.maximum(qi - 1, 0), far_step, 0)

    @pl.when(qi >= 1)
    def _():
        step(ks_ref, vs_ref, qi - 1, tz_ref[1])

    step(ks_ref, vs_ref, qi, tz_ref[0])
    o_slc = result()

    reset()
    n_back = WINDOW // KT
    edge = jnp.where(lax.broadcasted_iota(jnp.int32, (rows_all, KT), 1) >= i_row, 0.0, NEG_INF)
    for back in range(n_back, -1, -1):
        bias = edge if back == n_back else tz_ref[1] if back == 1 else tz_ref[0] if back == 0 else None

        @pl.when(qi >= back)
        def _(back=back, bias=bias):
            step(kw_ref, vw_ref, qi - back, bias)

    o_win = result()

    gate = gate_ref[...]
    for h in range(HEADS_PER_KV):
        rows = slice(h * QB, (h + 1) * QB)
        o = (gate[:, h:h + 1] * o_cmp[rows] + gate[:, HEADS_PER_KV + h:HEADS_PER_KV + h + 1] * o_slc[rows]
             + gate[:, 2 * HEADS_PER_KV + h:2 * HEADS_PER_KV + h + 1] * o_win[rows])
        o_ref[:, h * HEAD_DIM:(h + 1) * HEAD_DIM] = o.astype(o_ref.dtype)


def _sel_overlap_rows(n_rows, n_cmp, pad):
    nf = FEAT - HEAD_DIM
    cs = (np.arange(n_rows) - pad)[:, None] * CMP_STRIDE
    ss = np.arange(nf)[None, :] * SEL_BLOCK
    ov = np.minimum(cs + CMP_BLOCK, ss + SEL_BLOCK) - np.maximum(cs, ss)
    ov = np.maximum(ov, 0) / CMP_STRIDE
    valid = (np.arange(n_rows) >= pad) & (np.arange(n_rows) < pad + n_cmp)
    return (ov * valid[:, None]).astype(np.float32)


def prompt_attention(q, gate, kc, vc, tc, tz, ks, vs, kw, vw, b, t):
    n_chunk = t // CMP_STRIDE
    rc = n_chunk + 2 * CMP_PAD
    assert t // SEL_BLOCK <= FEAT - HEAD_DIM and t % QB == 0 and n_chunk % LANE == 0
    ov = jnp.asarray(_sel_overlap_rows(rc, n_chunk - 1, CMP_PAD))
    nq = t // QB
    gw = HEADS_PER_KV * HEAD_DIM
    kv_spec = pl.BlockSpec((None, None, t, FEAT), lambda i, g, qi: (i, g, 0, 0))
    return pl.pallas_call(
        functools.partial(_prompt_attn_kernel, n_chunk=n_chunk), grid=(b, N_KV_HEADS, nq), name="prompt_attention",
        in_specs=[pl.BlockSpec((QB, gw), lambda i, g, qi: (i * nq + qi, g)),
                  pl.BlockSpec((QB, LANE), lambda i, g, qi: (i * nq + qi, g)),
                  pl.BlockSpec((None, None, rc, FEAT), lambda i, g, qi: (i, g, 0, 0)),
                  pl.BlockSpec((None, None, rc, HEAD_DIM), lambda i, g, qi: (i, g, 0, 0)),
                  pl.BlockSpec((rc, FEAT - HEAD_DIM), lambda i, g, qi: (0, 0)),
                  pl.BlockSpec((HEADS_PER_KV * QB, CMP_NEAR), lambda i, g, qi: (g, 0)),
                  pl.BlockSpec((2, HEADS_PER_KV * QB, KT), lambda i, g, qi: (0, g, 0)),
                  kv_spec, kv_spec, kv_spec, kv_spec],
        out_specs=pl.BlockSpec((QB, gw), lambda i, g, qi: (i * nq + qi, g)),
        out_shape=jax.ShapeDtypeStruct((b * t, ATTN_WIDTH), BF16),
        scratch_shapes=[pltpu.VMEM((HEADS_PER_KV * QB, FEAT), BF16),
                        pltpu.VMEM((HEADS_PER_KV * QB, LANE), F32),
                        pltpu.VMEM((HEADS_PER_KV * QB, LANE), F32)],
        compiler_params=_params("parallel", "parallel", "arbitrary"))(q, gate, kc, vc, ov, tc, tz, ks, vs, kw, vw)


TQ = 256
T_PAD = TQ // CMP_STRIDE
T_NEAR = 2 * T_PAD
V_ROWS = HEAD_DIM + 16
assert FAR_DIST <= TQ and WINDOW % TQ == 0


def _bias_tables_t_kernel(rb_ref, tz_ref, tc_ref):
    j = lax.broadcasted_iota(jnp.int32, (TQ, TQ), 0)
    i = lax.broadcasted_iota(jnp.int32, (TQ, TQ), 1)
    cc = lax.broadcasted_iota(jnp.int32, (T_NEAR, TQ), 0)
    ic = lax.broadcasted_iota(jnp.int32, (T_NEAR, TQ), 1)
    d0 = i - j
    d1 = d0 + TQ
    dc = ic + (TQ - CMP_BLOCK + 1) - CMP_STRIDE * cc
    for h in range(N_HEADS):
        last = rb_ref[N_BUCKETS - 1, h]
        tz_ref[0, h] = jnp.where(d0 < 0, NEG_INF, _rel_bias(d0, rb_ref, h) - last)
        tz_ref[1, h] = _rel_bias(d1, rb_ref, h) - last
        tc_ref[h] = jnp.where(dc < 0, NEG_INF, _rel_bias(dc, rb_ref, h) - last)


def bias_tables_t(rel_bias):
    return pl.pallas_call(
        _bias_tables_t_kernel, name="bias_tables",
        in_specs=[pl.BlockSpec(memory_space=pltpu.SMEM)],
        out_specs=[pl.BlockSpec(memory_space=pltpu.VMEM), pl.BlockSpec(memory_space=pltpu.VMEM)],
        out_shape=[jax.ShapeDtypeStruct((2, N_HEADS, TQ, TQ), F32),
                   jax.ShapeDtypeStruct((N_HEADS, T_NEAR, TQ), F32)],
        compiler_params=pltpu.CompilerParams(vmem_limit_bytes=VMEM_LIMIT))(rel_bias)


def _prompt_attn_t_kernel(qt_ref, gate_ref, kc_ref, vct_ref, ovt_ref, tc_ref, tz_ref, ks_ref, vst_ref, kw_ref, vwt_ref,
                          o_ref, qa_ref, lc_ref, m_ref, acc_ref, ot_ref):
    qi = pl.program_id(2)
    nf = FEAT - HEAD_DIM
    hp = HEADS_PER_KV
    for h in range(hp):
        qa_ref[h, 0:HEAD_DIM, :] = qt_ref[h * HEAD_DIM:(h + 1) * HEAD_DIM, :]
        qa_ref[h, HEAD_DIM:FEAT, :] = jnp.zeros((nf, TQ), BF16)
    t_lane = qi * TQ + lax.broadcasted_iota(jnp.int32, (1, TQ), 1)

    near0 = pl.multiple_of(qi * T_PAD, T_PAD)
    rc = kc_ref.shape[0]
    row = lax.broadcasted_iota(jnp.int32, (rc, 1), 0)
    row_ok = (row >= T_PAD) & (row < near0 + T_NEAR)
    has_block = t_lane >= CMP_BLOCK - 1
    kc = kc_ref[...].astype(BF16)
    vct = vct_ref[...].astype(BF16)
    for h in range(hp):
        lc_ref[h] = _dot(kc, qa_ref[h])
    for h in range(hp):
        lc_ref[h, pl.ds(near0, T_NEAR), :] = lc_ref[h, pl.ds(near0, T_NEAR), :] + tc_ref[h]
    p_heads = jnp.zeros((rc, TQ), F32)
    for h in range(hp):
        lc = jnp.where(row_ok, lc_ref[h], NEG_INF)
        p = jnp.exp(lc - jnp.max(lc, axis=0, keepdims=True))
        p = p * jnp.where(has_block, 1.0 / jnp.sum(p, axis=0, keepdims=True), 0.0)
        p_heads = p_heads + p
        ot_ref[h * HEAD_DIM:(h + 1) * HEAD_DIM, :] = gate_ref[h:h + 1, :] * _dot(vct, p.astype(BF16))
    p_slc = _dot(ovt_ref[...], p_heads, precision=lax.Precision.HIGHEST)

    blk = lax.broadcasted_iota(jnp.int32, (nf, TQ), 0)
    jt = t_lane // SEL_BLOCK
    forced = (blk == 0) | (blk == jt) | (blk == jt - 1)
    score = jnp.where(forced, jnp.inf, jnp.where(blk <= jt, p_slc, -jnp.inf))
    n_grp = nf // SUBLANE
    groups = [score[SUBLANE * g:SUBLANE * (g + 1), :] for g in range(n_grp)]
    ranks = [jnp.zeros((SUBLANE, TQ), jnp.int32) for _ in range(n_grp)]
    sub = lax.broadcasted_iota(jnp.int32, (SUBLANE, TQ), 0)
    for c in range(nf):
        cg, cs = divmod(c, SUBLANE)
        other = jnp.broadcast_to(groups[cg][cs:cs + 1, :], (SUBLANE, TQ))
        for g in range(n_grp):
            if g > cg:
                beats = (other >= groups[g]).astype(jnp.int32)
            elif g < cg:
                beats = (other > groups[g]).astype(jnp.int32)
            else:
                beats = jnp.where(sub > cs, (other >= groups[g]).astype(jnp.int32), (other > groups[g]).astype(jnp.int32))
            ranks[g] = ranks[g] + beats
    feat = jnp.concatenate(
        [jnp.where((ranks[g] < SEL_TOPK) & (groups[g] > NEG_INF), 0.0, NEG_INF) for g in range(n_grp)], axis=0)
    feat = feat.astype(BF16)
    for h in range(hp):
        qa_ref[h, HEAD_DIM:FEAT, :] = feat

    def reset():
        m_ref[...] = jnp.full(m_ref.shape, -3e38, F32)
        acc_ref[...] = jnp.zeros(acc_ref.shape, F32)

    def step(k_ref, vt_ref, j, bias):
        k = k_ref[pl.ds(pl.multiple_of(j * TQ, TQ), TQ), :]
        vt = vt_ref[j]
        m_old = [m_ref[h] for h in range(hp)]
        acc_old = [acc_ref[h] for h in range(hp)]
        s = [_dot(k, qa_ref[h]) for h in range(hp)]
        if bias is not None:
            s = [s[h] + bias(h) for h in range(hp)]
        m_new = [jnp.maximum(m_old[h], jnp.max(s[h], axis=0, keepdims=True)) for h in range(hp)]
        pv = [_dot(vt, jnp.exp(s[h] - m_new[h]).astype(BF16)) for h in range(hp)]
        for h in range(hp):
            acc_ref[h] = jnp.exp(m_old[h] - m_new[h]) * acc_old[h] + pv[h]
            m_ref[h] = m_new[h]

    def add_result(branch):
        for h in range(hp):
            acc = acc_ref[h]
            o = acc[0:HEAD_DIM, :] * (1.0 / acc[HEAD_DIM:HEAD_DIM + 1, :])
            rows = slice(h * HEAD_DIM, (h + 1) * HEAD_DIM)
            ot_ref[rows, :] = ot_ref[rows, :] + gate_ref[branch * hp + h:branch * hp + h + 1, :] * o

    diag = lambda h: tz_ref[0, h]
    prev = lambda h: tz_ref[1, h]

    reset()

    def far_step(j, carry):
        step(ks_ref, vst_ref, j, None)
        return carry

    lax.fori_loop(0, jnp.maximum(qi - 1, 0), far_step, 0)

    @pl.when(qi >= 1)
    def _():
        step(ks_ref, vst_ref, qi - 1, prev)

    step(ks_ref, vst_ref, qi, diag)
    add_result(1)

    reset()
    n_back = WINDOW // TQ
    edge = jnp.where(lax.broadcasted_iota(jnp.int32, (TQ, TQ), 0) >= lax.broadcasted_iota(jnp.int32, (TQ, TQ), 1),
                     0.0, NEG_INF)
    for back in range(n_back, -1, -1):
        bias = (lambda h: edge) if back == n_back else prev if back == 1 else diag if back == 0 else None

        @pl.when(qi >= back)
        def _(back=back, bias=bias):
            step(kw_ref, vwt_ref, qi - back, bias)

    add_result(2)
    o_ref[...] = ot_ref[...].T.astype(o_ref.dtype)


def prompt_attention_t(qt, gate_t, kc, vct, tc, tz, ks, vst, kw, vwt, b, t):
    n_chunk = t // CMP_STRIDE
    rc = kc.shape[2]
    nf = FEAT - HEAD_DIM
    nq = t // TQ
    assert t // SEL_BLOCK <= nf and t % TQ == 0 and rc >= T_PAD * (nq + 1) and rc >= T_PAD + n_chunk - 1
    ovt = jnp.asarray(_sel_overlap_rows(rc, n_chunk - 1, T_PAD).T)
    hp = HEADS_PER_KV
    gw = hp * HEAD_DIM
    k_spec = pl.BlockSpec((None, None, t, FEAT), lambda i, g, qi: (i, g, 0, 0))
    v_spec = pl.BlockSpec((None, None, nq, V_ROWS, TQ), lambda i, g, qi: (i, g, 0, 0, 0))
    return pl.pallas_call(
        _prompt_attn_t_kernel, grid=(b, N_KV_HEADS, nq), name="prompt_attention",
        in_specs=[pl.BlockSpec((gw, TQ), lambda i, g, qi: (g, i * nq + qi)),
                  pl.BlockSpec((4 * hp, TQ), lambda i, g, qi: (g, i * nq + qi)),
                  pl.BlockSpec((None, None, rc, FEAT), lambda i, g, qi: (i, g, 0, 0)),
                  pl.BlockSpec((None, None, HEAD_DIM, rc), lambda i, g, qi: (i, g, 0, 0)),
                  pl.BlockSpec((nf, rc), lambda i, g, qi: (0, 0)),
                  pl.BlockSpec((hp, T_NEAR, TQ), lambda i, g, qi: (g, 0, 0)),
                  pl.BlockSpec((2, hp, TQ, TQ), lambda i, g, qi: (0, g, 0, 0)),
                  k_spec, v_spec, k_spec, v_spec],
        out_specs=pl.BlockSpec((TQ, gw), lambda i, g, qi: (i * nq + qi, g)),
        out_shape=jax.ShapeDtypeStruct((b * t, ATTN_WIDTH), BF16),
        scratch_shapes=[pltpu.VMEM((hp, FEAT, TQ), BF16), pltpu.VMEM((hp, rc, TQ), F32),
                        pltpu.VMEM((hp, 1, TQ), F32), pltpu.VMEM((hp, V_ROWS, TQ), F32), pltpu.VMEM((gw, TQ), F32)],
        compiler_params=_params("parallel", "parallel", "arbitrary"))(
            qt, gate_t, kc, vct, ovt, tc, tz, ks, vst, kw, vwt)


def _gmlp_kernel(u_ref, v_ref, lng_ref, lnb_ref, ws_ref, bs_ref, o_ref):
    v = v_ref[...].astype(F32)
    mu = jnp.mean(v, axis=-1, keepdims=True)
    var = jnp.mean(jnp.square(v - mu), axis=-1, keepdims=True)
    vn = ((v - mu) * lax.rsqrt(var + LN_EPS) * lng_ref[...] + lnb_ref[...]).astype(BF16)
    tri = (lax.broadcasted_iota(jnp.int32, (CHUNK, CHUNK), 0) >= lax.broadcasted_iota(jnp.int32, (CHUNK, CHUNK), 1))
    for g in range(GM_GROUPS):
        w = jnp.where(tri, ws_ref[g], 0.0).astype(BF16)
        cols = slice(g * GM_GROUP_DIM, (g + 1) * GM_GROUP_DIM)
        for c in range(u_ref.shape[0] // CHUNK):
            rows = slice(c * CHUNK, (c + 1) * CHUNK)
            s = _dot(w, vn[rows, cols]) + bs_ref[:, g:g + 1]
            o_ref[rows, cols] = (u_ref[rows, cols].astype(F32) * s).astype(o_ref.dtype)


def gmlp_chunks(uv, ln_g, ln_b, w_s, b_s, rows):
    m = uv.shape[0]
    return pl.pallas_call(
        _gmlp_kernel, grid=(m // rows,), name="gmlp",
        in_specs=[pl.BlockSpec((rows, GM_WIDTH), lambda i: (i, 0)), pl.BlockSpec((rows, GM_WIDTH), lambda i: (i, 1)),
                  pl.BlockSpec((1, GM_WIDTH), lambda i: (0, 0)), pl.BlockSpec((1, GM_WIDTH), lambda i: (0, 0)),
                  pl.BlockSpec((GM_GROUPS, CHUNK, CHUNK), lambda i: (0, 0, 0)),
                  pl.BlockSpec((CHUNK, GM_GROUPS), lambda i: (0, 0))],
        out_specs=pl.BlockSpec((rows, GM_WIDTH), lambda i: (i, 0)),
        out_shape=jax.ShapeDtypeStruct((m, GM_WIDTH), BF16),
        compiler_params=_params("parallel"))(uv, uv, ln_g.reshape(1, -1), ln_b.reshape(1, -1), w_s, b_s.T)


def _merge_kernel(attn_ref, gm_ref, mg_ref, x_ref, wa_ref, wb_ref, wo_ref, gn_ref, wr_ref, br_ref,
                  h_ref, hn_ref, idx_ref, wt_ref):
    tm = x_ref.shape[0]
    sub = min(MERGE_SUB, tm)
    parts = [slice(r0, r0 + sub) for r0 in range(0, tm, sub)]
    ta = [_dot(attn_ref[p, :], wa_ref[...]) for p in parts]
    tb = [_dot(gm_ref[p, :], wb_ref[...]) for p in parts]
    merged = [(mg_ref[p, 0:D_MODEL].astype(F32) * ta[i] + mg_ref[p, D_MODEL:2 * D_MODEL].astype(F32) * tb[i])
              .astype(BF16) for i, p in enumerate(parts)]
    h = [x_ref[p, :] + _dot(merged[i], wo_ref[...]) for i, p in enumerate(parts)]
    hn = [hh * lax.rsqrt(jnp.mean(hh * hh, axis=-1, keepdims=True) + RMS_EPS) * gn_ref[...] for hh in h]
    hi = [v.astype(BF16) for v in hn]
    lo = [(v - hi[i].astype(F32)).astype(BF16) for i, v in enumerate(hn)]
    logits = [_dot(hi[i], wr_ref[0]) + _dot(lo[i], wr_ref[0]) + _dot(hi[i], wr_ref[1]) + br_ref[...]
              for i in range(len(parts))]
    routed = [_route(lg) for lg in logits]
    for i, p in enumerate(parts):
        h_ref[p, :] = h[i]
        hn_ref[p, :] = hi[i]
        idx_ref[p, :] = routed[i][0]
        wt_ref[p, :] = routed[i][1]


def _route(logits):
    lane = lax.broadcasted_iota(jnp.int32, logits.shape, 1)
    vals, idxs = [], []
    for _ in range(TOP_K):
        best = jnp.max(logits, axis=-1, keepdims=True)
        ix = jnp.min(jnp.where(logits == best, lane, LANE), axis=-1, keepdims=True)
        vals.append(best)
        idxs.append(ix)
        logits = jnp.where(lane == ix, -jnp.inf, logits)
    ex = [jnp.exp(v - vals[0]) for v in vals]
    inv = 1.0 / functools.reduce(lambda a, b: a + b, ex)
    idx_out = jnp.zeros(lane.shape, jnp.int32)
    wt_out = jnp.zeros(lane.shape, F32)
    for k in range(TOP_K):
        idx_out = jnp.where(lane == k, idxs[k], idx_out)
        wt_out = jnp.where(lane == k, ex[k] * inv, wt_out)
    return idx_out, wt_out


MERGE_SUB = 128


def merge_and_route(attn, gm, mg, x, wa, wb, wo, g_ffn, w_router, b_router, tm):
    m = x.shape[0]
    tm = min(tm, m)
    wr = jnp.zeros((D_MODEL, LANE), F32).at[:, :N_EXPERTS].set(w_router)
    wr_hi = wr.astype(BF16)
    wr = jnp.stack([wr_hi, (wr - wr_hi.astype(F32)).astype(BF16)])
    br = jnp.full((1, LANE), -jnp.inf, F32).at[0, :N_EXPERTS].set(b_router)
    row = lambda w: pl.BlockSpec((tm, w), lambda i: (i, 0))
    full = lambda a: pl.BlockSpec(a.shape, lambda i: (0,) * a.ndim, pipeline_mode=pl.Buffered(1))
    return pl.pallas_call(
        _merge_kernel, grid=(m // tm,), name="merge_route",
        in_specs=[row(ATTN_WIDTH), row(GM_WIDTH), row(2 * D_MODEL), row(D_MODEL), full(wa), full(wb), full(wo),
                  pl.BlockSpec((1, D_MODEL), lambda i: (0, 0)), full(wr), full(br)],
        out_specs=[row(D_MODEL), row(D_MODEL), row(LANE), row(LANE)],
        out_shape=[jax.ShapeDtypeStruct((m, D_MODEL), F32), jax.ShapeDtypeStruct((m, D_MODEL), BF16),
                   jax.ShapeDtypeStruct((m, LANE), jnp.int32), jax.ShapeDtypeStruct((m, LANE), F32)],
        compiler_params=_params("parallel"))(attn, gm, mg, x, wa, wb, wo, g_ffn.reshape(1, -1), wr, br)


MOE_CAST_COLS = 256


def _expert_tile(te_ref, first_ref, nact_ref, nxt_ref, w_hbm, stage_ref, cache_ref, sem, o_ref, compute):
    j, r = pl.program_id(0), pl.program_id(1)
    tn = stage_ref.shape[-1]
    rows = stage_ref.shape[1]

    def copies(e, jj):
        cols = pl.ds(pl.multiple_of(jj * tn, tn), tn)
        return [pltpu.make_async_copy(w.at[e, :, cols], stage_ref.at[i], sem.at[i]) for i, w in enumerate(w_hbm)]

    @pl.when((j == 0) & (r == 0))
    def _():
        for c in copies(te_ref[0], 0):
            c.start()

    @pl.when(first_ref[r] == 1)
    def _():
        for c in copies(te_ref[r], j):
            c.wait()
        for c0 in range(0, tn, MOE_CAST_COLS):
            c1 = c0 + MOE_CAST_COLS
            for i in range(len(w_hbm)):
                for r0 in range(0, rows, 256):
                    cache_ref[i, r0:r0 + 256, c0:c1] = stage_ref[i, r0:r0 + 256, c0:c1].astype(cache_ref.dtype)
            if c1 == tn:
                wraps = nxt_ref[r] < 0
                e_next = jnp.where(wraps, te_ref[0], nxt_ref[r])
                j_next = jnp.where(wraps, j + 1, j)

                @pl.when(j_next < pl.num_programs(0))
                def _():
                    for c in copies(e_next, j_next):
                        c.start()
            compute(c0, c1)

    @pl.when((first_ref[r] != 1) & (r < nact_ref[0]))
    def _():
        compute(0, tn)

    @pl.when(r >= nact_ref[0])
    def _():
        o_ref[...] = jnp.zeros(o_ref.shape, o_ref.dtype)


def _moe_up_kernel(te_ref, first_ref, nact_ref, nxt_ref, x_ref, wg_hbm, wu_hbm, bg_ref, bu_ref, o_ref,
                   stage_ref, cache_ref, sem):
    def compute(c0, c1):
        x = x_ref[...]
        g = jnp.minimum(_dot(x, cache_ref[0, :, c0:c1]) + bg_ref[:, c0:c1], SWIGLU_LIMIT)
        u = jnp.clip(_dot(x, cache_ref[1, :, c0:c1]) + bu_ref[:, c0:c1], -SWIGLU_LIMIT, SWIGLU_LIMIT)
        o_ref[:, c0:c1] = ((u + 1.0) * (g * jax.nn.sigmoid(SWIGLU_ALPHA * g))).astype(o_ref.dtype)

    _expert_tile(te_ref, first_ref, nact_ref, nxt_ref, (wg_hbm, wu_hbm), stage_ref, cache_ref, sem, o_ref, compute)


def _moe_down_kernel(te_ref, first_ref, nact_ref, nxt_ref, a_ref, wd_hbm, bd_ref, o_ref, stage_ref, cache_ref, sem):
    def compute(c0, c1):
        o_ref[:, c0:c1] = (_dot(a_ref[...], cache_ref[0, :, c0:c1]) + bd_ref[:, c0:c1]).astype(o_ref.dtype)

    _expert_tile(te_ref, first_ref, nact_ref, nxt_ref, (wd_hbm,), stage_ref, cache_ref, sem, o_ref, compute)


def moe_experts(x_rows, tile_expert, tile_first, n_active, tile_next, w_gate, b_gate, w_up, b_up, w_down, b_down):
    rows = x_rows.shape[0]
    n_tiles = rows // MOE_TM
    row_map = lambda j, r, te, first, nact, nxt: (jnp.minimum(r, nact[0] - 1), 0)
    out_map = lambda j, r, te, first, nact, nxt: (r, j)
    b_spec = pl.BlockSpec((None, 1, MOE_TN), lambda j, r, te, first, nact, nxt: (te[r], 0, j))
    hbm = pl.BlockSpec(memory_space=pl.ANY)

    def scratch(n_w, k):
        return [pltpu.VMEM((n_w, k, MOE_TN), F32), pltpu.VMEM((n_w, k, MOE_TN), BF16), pltpu.SemaphoreType.DMA((n_w,))]

    act = pl.pallas_call(
        _moe_up_kernel, name="moe_up",
        grid_spec=pltpu.PrefetchScalarGridSpec(
            num_scalar_prefetch=4, grid=(D_FF // MOE_TN, n_tiles),
            in_specs=[pl.BlockSpec((MOE_TM, D_MODEL), row_map), hbm, hbm, b_spec, b_spec],
            out_specs=pl.BlockSpec((MOE_TM, MOE_TN), out_map),
            scratch_shapes=scratch(2, D_MODEL)),
        out_shape=jax.ShapeDtypeStruct((rows, D_FF), BF16),
        compiler_params=_params("arbitrary", "arbitrary"))(
            tile_expert, tile_first, n_active, tile_next, x_rows, w_gate, w_up,
            b_gate.reshape(N_EXPERTS, 1, D_FF), b_up.reshape(N_EXPERTS, 1, D_FF))
    return pl.pallas_call(
        _moe_down_kernel, name="moe_down",
        grid_spec=pltpu.PrefetchScalarGridSpec(
            num_scalar_prefetch=4, grid=(D_MODEL // MOE_TN, n_tiles),
            in_specs=[pl.BlockSpec((MOE_TM, D_FF), row_map), hbm, b_spec],
            out_specs=pl.BlockSpec((MOE_TM, MOE_TN), out_map),
            scratch_shapes=scratch(1, D_FF)),
        out_shape=jax.ShapeDtypeStruct((rows, D_MODEL), BF16),
        compiler_params=_params("arbitrary", "arbitrary"))(
            tile_expert, tile_first, n_active, tile_next, act, w_down, b_down.reshape(N_EXPERTS, 1, D_MODEL))


def route_rows(top_i, n_rows):
    n = top_i.shape[0]
    hit = top_i[:, :, None] == jnp.arange(N_EXPERTS, dtype=jnp.int32)[None, None, :]
    onehot = hit.astype(jnp.int32).sum(axis=1)
    before = jnp.cumsum(onehot, axis=0) - onehot
    counts = onehot.sum(axis=0)
    padded = (counts + MOE_TM - 1) // MOE_TM * MOE_TM
    ends = jnp.cumsum(padded)
    starts = ends - padded
    pos = jnp.sum(jnp.where(hit, (starts[None, :] + before)[:, None, :], 0), axis=-1)
    pair_id = jnp.arange(TOP_K, dtype=jnp.int32)[None, :] * n + jnp.arange(n, dtype=jnp.int32)[:, None]
    row_pair = jnp.arange(n_rows, dtype=jnp.int32).at[pos.reshape(-1)].set(pair_id.reshape(-1), unique_indices=True)
    n_tiles = n_rows // MOE_TM
    n_active = (ends[-1] // MOE_TM).astype(jnp.int32)
    tile_row = jnp.minimum(jnp.arange(n_tiles, dtype=jnp.int32), n_active - 1) * MOE_TM
    tile_expert = jnp.minimum((tile_row[:, None] >= ends[None, :]).astype(jnp.int32).sum(axis=-1), N_EXPERTS - 1)
    tile_first = jnp.concatenate([jnp.ones((1,), jnp.int32), (tile_expert[1:] != tile_expert[:-1]).astype(jnp.int32)])
    tile_id = jnp.arange(n_tiles, dtype=jnp.int32)
    start_id = jnp.where(tile_first == 1, tile_id, n_tiles)
    next_start = jnp.concatenate([lax.cummin(start_id[::-1])[::-1][1:], jnp.full((1,), n_tiles, jnp.int32)])
    tile_next = jnp.where(next_start < n_tiles, tile_expert[jnp.minimum(next_start, n_tiles - 1)], -1)
    return pos, row_pair, tile_expert, tile_first, n_active.reshape(1), tile_next.astype(jnp.int32)


def _combine_kernel(h_ref, y_ref, w_ref, g_ref, o_ref):
    h = h_ref[...]
    w = w_ref[...]
    for k in range(TOP_K):
        h = h + w[:, k:k + 1] * y_ref[k].astype(F32)
    o_ref[...] = h * lax.rsqrt(jnp.mean(h * h, axis=-1, keepdims=True) + RMS_EPS) * g_ref[...]


def combine_and_norm(h, y4, wts, g, tm):
    m = h.shape[0]
    tm = min(tm, m)
    return pl.pallas_call(
        _combine_kernel, grid=(m // tm,), name="combine_norm",
        in_specs=[pl.BlockSpec((tm, D_MODEL), lambda i: (i, 0)), pl.BlockSpec((TOP_K, tm, D_MODEL), lambda i: (0, i, 0)),
                  pl.BlockSpec((tm, LANE), lambda i: (i, 0)), pl.BlockSpec((1, D_MODEL), lambda i: (0, 0))],
        out_specs=pl.BlockSpec((tm, D_MODEL), lambda i: (i, 0)),
        out_shape=jax.ShapeDtypeStruct((m, D_MODEL), F32), compiler_params=_params("parallel"))(
            h, y4, wts, g.reshape(1, -1))


def _in_weights(w_in):
    c0 = ATTN_WIDTH
    c1 = c0 + 3 * KV_COLS
    c2 = c1 + GATE_COLS
    c3 = c2 + 2 * GM_WIDTH
    wg = w_in[:, c1:c2].reshape(D_MODEL, N_KV_HEADS, HEADS_PER_KV, 3).transpose(0, 1, 3, 2)
    wg = jnp.pad(wg.reshape(D_MODEL, N_KV_HEADS, 3 * HEADS_PER_KV), ((0, 0), (0, 0), (0, LANE - 3 * HEADS_PER_KV)))
    cast = lambda w: w.astype(BF16)
    return (cast(w_in[:, :c0]), cast(w_in[:, c0:c1]), cast(wg.reshape(D_MODEL, N_KV_HEADS * LANE)),
            cast(w_in[:, c2:c3]), cast(w_in[:, c3:]))


def _project(a, weights, wide_dtype):
    w_q, w_kv, w_ng, w_uv, w_mg = weights
    q = matmul(a, w_q, lambda z: z * np.float32(HEAD_DIM ** -0.5), wide_dtype, 1024, 1024, "proj_q")
    kv = matmul(a, w_kv, lambda z: z, F32, 1024, 3 * KV_COLS, "proj_kv")
    ng = matmul(a, w_ng, jax.nn.sigmoid, F32, 1024, N_KV_HEADS * LANE, "proj_gate")
    uv = matmul(a, w_uv, _gelu, wide_dtype, 1024, 1024, "proj_uv")
    mg = matmul(a, w_mg, jax.nn.sigmoid, BF16, 1024, 1024, "proj_merge")
    return q, kv, ng, uv, mg


def _head_rows(fn):
    out = fn(0)
    out = jnp.broadcast_to(out, (HEADS_PER_KV, out.shape[1]))
    row = lax.broadcasted_iota(jnp.int32, out.shape, 0)
    for h in range(1, HEADS_PER_KV):
        out = jnp.where(row == h, fn(h), out)
    return out


def _sample_cmp_win_kernel(rb_ref, q_ref, kvc_ref, ov_ref, sw_ref, new_ref, gate_ref, o_ref, idx_ref, *, past, n_cmp):
    n_rows = kvc_ref.shape[2]
    n_blk = ov_ref.shape[1]
    win = sw_ref.shape[-1]
    win_col = 2 * KV_COLS
    lane_blk = lax.broadcasted_iota(jnp.int32, (1, n_blk), 1)
    lane_out = lax.broadcasted_iota(jnp.int32, (1, LANE), 1)
    for g in range(N_KV_HEADS):
        heads = slice(g * HEADS_PER_KV, (g + 1) * HEADS_PER_KV)
        q = q_ref[heads, :]
        qb = q.astype(BF16)

        n = lax.broadcasted_iota(jnp.int32, (1, n_rows), 1)
        d = past - (n * CMP_STRIDE + CMP_BLOCK - 1)
        logit = _dot_nt(qb, kvc_ref[0, g].astype(BF16)) + _head_rows(
            lambda h: _rel_bias(d, rb_ref, g * HEADS_PER_KV + h))
        logit = jnp.where(n < n_cmp, logit, NEG_INF)
        p = jnp.exp(logit - jnp.max(logit, axis=-1, keepdims=True))
        p = p * (1.0 / jnp.sum(p, axis=-1, keepdims=True))
        o_cmp = _dot(p.astype(BF16), kvc_ref[1, g].astype(BF16))
        p_heads = jnp.broadcast_to(jnp.sum(p, axis=0, keepdims=True), (SUBLANE, n_rows))
        p_slc = _dot(p_heads, ov_ref[...], precision=lax.Precision.HIGHEST)[0:1, :]

        score = jnp.where((lane_blk == 0) | (lane_blk == n_blk - 1), jnp.inf, p_slc)
        picked = jnp.zeros((1, LANE), jnp.int32)
        for k in range(SEL_TOPK - 1):
            best = jnp.max(score, axis=-1, keepdims=True)
            ix = jnp.min(jnp.where(score == best, lane_blk, n_blk), axis=-1, keepdims=True)
            picked = jnp.where(lane_out == k, ix, picked)
            score = jnp.where(lane_blk == ix, -jnp.inf, score)
        idx_ref[g:g + 1, :] = picked

        dw = win - lax.broadcasted_iota(jnp.int32, (1, win), 1)
        lw = _dot(qb, sw_ref[0, g].astype(BF16)) + _head_rows(
            lambda h: _rel_bias(dw, rb_ref, g * HEADS_PER_KV + h))
        k_new = new_ref[:, win_col + g * HEAD_DIM:win_col + (g + 1) * HEAD_DIM]
        v_new = new_ref[:, win_col + KV_COLS // 2 + g * HEAD_DIM:win_col + KV_COLS // 2 + (g + 1) * HEAD_DIM]
        l_new = jnp.sum(q * k_new, axis=-1, keepdims=True) + _head_rows(
            lambda h: jnp.full((1, 1), rb_ref[0, g * HEADS_PER_KV + h], F32))
        m = jnp.maximum(jnp.max(lw, axis=-1, keepdims=True), l_new)
        pw = jnp.exp(lw - m)
        p_new = jnp.exp(l_new - m)
        o_win = ((_dot_nt(pw.astype(BF16), sw_ref[1, g].astype(BF16)) + p_new * v_new)
                 * (1.0 / (jnp.sum(pw, axis=-1, keepdims=True) + p_new)))
        o_ref[heads, :] = gate_ref[0, heads, :] * o_cmp + gate_ref[2, heads, :] * o_win


def sample_cmp_win(rel_bias, q, kvc, state_win, kv_new, gate, past):
    b, _, _, n_rows, _ = kvc.shape
    n_cmp = past // CMP_STRIDE - CMP_BLOCK // CMP_STRIDE + 1
    n_blk = past // SEL_BLOCK
    assert past % SEL_BLOCK == 0 and n_blk >= SEL_TOPK and n_cmp <= n_rows
    cs = np.arange(n_rows)[:, None] * CMP_STRIDE
    ss = np.arange(n_blk)[None, :] * SEL_BLOCK
    ov = np.maximum(np.minimum(cs + CMP_BLOCK, ss + SEL_BLOCK) - np.maximum(cs, ss), 0) / CMP_STRIDE
    ov = jnp.asarray((ov * (np.arange(n_rows) < n_cmp)[:, None]).astype(np.float32))
    win = state_win.shape[-1]
    return pl.pallas_call(
        functools.partial(_sample_cmp_win_kernel, past=past, n_cmp=n_cmp), grid=(b,), name="sample_cmp_win",
        in_specs=[pl.BlockSpec(memory_space=pltpu.SMEM),
                  pl.BlockSpec((None, N_HEADS, HEAD_DIM), lambda i: (i, 0, 0)),
                  pl.BlockSpec((None, 2, N_KV_HEADS, n_rows, HEAD_DIM), lambda i: (i, 0, 0, 0, 0)),
                  pl.BlockSpec(ov.shape, lambda i: (0, 0)),
                  pl.BlockSpec((None, 2, N_KV_HEADS, HEAD_DIM, win), lambda i: (i, 0, 0, 0, 0)),
                  pl.BlockSpec((None, 1, 3 * KV_COLS), lambda i: (i, 0, 0)),
                  pl.BlockSpec((None, 3, N_HEADS, 1), lambda i: (i, 0, 0, 0))],
        out_specs=[pl.BlockSpec((None, N_HEADS, HEAD_DIM), lambda i: (i, 0, 0)),
                   pl.BlockSpec((None, N_KV_HEADS, LANE), lambda i: (i, 0, 0))],
        out_shape=[jax.ShapeDtypeStruct((b, N_HEADS, HEAD_DIM), F32),
                   jax.ShapeDtypeStruct((b, N_KV_HEADS, LANE), jnp.int32)],
        compiler_params=_params("parallel"))(rel_bias, q, kvc, ov, state_win, kv_new, gate)


def _sample_slc_kernel(pt_ref, idx_ref, rb_ref, q_ref, new_ref, gate_ref, part_ref, *rest, past):
    blocks, o_ref = rest[:-1], rest[-1]
    b, g = pl.program_id(0), pl.program_id(1)
    q = q_ref[...]
    qb = q.astype(BF16)
    slc_col = KV_COLS
    half = KV_COLS // 2
    page = blocks[0].shape[-1]
    pos = lax.broadcasted_iota(jnp.int32, (1, page), 1)
    logits, values = [], []
    for k, blk_ref in enumerate(blocks):
        ix = idx_ref[(b * N_KV_HEADS + g) * SEL_TOPK + k]
        d = past - ((ix // SLC_PER_PAGE) * page + pos)
        lg = _dot(qb, blk_ref[0].astype(BF16)) + _head_rows(lambda h: _rel_bias(d, rb_ref, g * HEADS_PER_KV + h))
        logits.append(jnp.where(pos // SEL_BLOCK == ix % SLC_PER_PAGE, lg, NEG_INF))
        values.append(blk_ref[1].astype(BF16))
    new = new_ref[...]
    k_new = jnp.where(g == 0, new[:, slc_col:slc_col + HEAD_DIM], new[:, slc_col + HEAD_DIM:slc_col + 2 * HEAD_DIM])
    v_new = jnp.where(g == 0, new[:, slc_col + half:slc_col + half + HEAD_DIM],
                      new[:, slc_col + half + HEAD_DIM:slc_col + half + 2 * HEAD_DIM])
    l_new = jnp.sum(q * k_new, axis=-1, keepdims=True) + _head_rows(
        lambda h: jnp.full((1, 1), rb_ref[0, g * HEADS_PER_KV + h], F32))
    m = l_new
    for lg in logits:
        m = jnp.maximum(m, jnp.max(lg, axis=-1, keepdims=True))
    p_new = jnp.exp(l_new - m)
    denom = p_new
    out = p_new * v_new
    for lg, vv in zip(logits, values):
        p = jnp.exp(lg - m)
        denom = denom + jnp.sum(p, axis=-1, keepdims=True)
        out = out + _dot_nt(p.astype(BF16), vv)
    o_ref[...] = part_ref[...] + gate_ref[1] * (out * (1.0 / denom))


def sample_slc(page_table, picked, rel_bias, q, kv_new, gate, part, cache_slc, past):
    b = q.shape[0]
    page = cache_slc.shape[-1]
    n_pick = SEL_TOPK - 1

    def blk_map(k):
        def index(i, g, pt, idx, k=k):
            ix = idx[(i * N_KV_HEADS + g) * SEL_TOPK + k]
            return (pt[i, ix // SLC_PER_PAGE], 0, g, 0, 0)
        return index

    head_spec = pl.BlockSpec((None, HEADS_PER_KV, HEAD_DIM), lambda i, g, pt, idx: (i, g, 0))
    return pl.pallas_call(
        functools.partial(_sample_slc_kernel, past=past), name="sample_slc",
        grid_spec=pltpu.PrefetchScalarGridSpec(
            num_scalar_prefetch=2, grid=(b, N_KV_HEADS),
            in_specs=[pl.BlockSpec(memory_space=pltpu.SMEM), head_spec,
                      pl.BlockSpec((None, 1, 3 * KV_COLS), lambda i, g, pt, idx: (i, 0, 0)),
                      pl.BlockSpec((None, 3, HEADS_PER_KV, 1), lambda i, g, pt, idx: (i, 0, g, 0)), head_spec]
            + [pl.BlockSpec((None, 2, None, HEAD_DIM, page), blk_map(k)) for k in range(n_pick)],
            out_specs=head_spec),
        out_shape=jax.ShapeDtypeStruct((b, N_HEADS, HEAD_DIM), F32),
        compiler_params=_params("parallel", "parallel"))(
            page_table, picked, rel_bias, q, kv_new, gate, part, *([cache_slc] * n_pick))


def _gmlp_row_kernel(uv_ref, lng_ref, lnb_ref, w0_ref, b0_ref, vn_ref, o_ref):
    v = uv_ref[:, GM_WIDTH:2 * GM_WIDTH]
    mu = jnp.mean(v, axis=-1, keepdims=True)
    var = jnp.mean(jnp.square(v - mu), axis=-1, keepdims=True)
    vn = (v - mu) * lax.rsqrt(var + LN_EPS) * lng_ref[...] + lnb_ref[...]
    vn_ref[...] = vn
    o_ref[...] = (uv_ref[:, 0:GM_WIDTH] * (w0_ref[...] * vn + b0_ref[...])).astype(o_ref.dtype)


def gmlp_first_row(uv, ln_g, ln_b, w_s, b_s):
    m = uv.shape[0]
    w0 = jnp.repeat(w_s[:, 0, 0], GM_GROUP_DIM).reshape(1, GM_WIDTH)
    b0 = jnp.repeat(b_s[:, 0], GM_GROUP_DIM).reshape(1, GM_WIDTH)
    return pl.pallas_call(
        _gmlp_row_kernel, name="gmlp_row",
        out_shape=[jax.ShapeDtypeStruct((m, GM_WIDTH), F32), jax.ShapeDtypeStruct((m, GM_WIDTH), BF16)])(
            uv, ln_g.reshape(1, -1), ln_b.reshape(1, -1), w0, b0)


def _group_major(x, feat):
    b, t = x.shape[0], x.shape[1]
    f = jnp.broadcast_to(feat.astype(BF16)[None, None], (b, N_KV_HEADS, t, FEAT - HEAD_DIM))
    return jnp.concatenate([x.transpose(0, 2, 1, 3).astype(BF16), f], axis=-1)


def _value_tiles(x):
    b, t = x.shape[0], x.shape[1]
    xt = x.transpose(0, 2, 3, 1).astype(BF16)
    extra = jnp.zeros((b, N_KV_HEADS, V_ROWS - HEAD_DIM, t), BF16).at[:, :, 0].set(1)
    tiles = jnp.concatenate([xt, extra], axis=2).reshape(b, N_KV_HEADS, V_ROWS, t // TQ, TQ)
    return tiles.transpose(0, 1, 3, 2, 4)


def _prompt_mixer_a(q, kv, ng, rel_bias, pe, w1, w2, b, t):
    kvr = kv.reshape(b, t, 3, 2, N_KV_HEADS, HEAD_DIM)
    n_chunk = t // CMP_STRIDE
    chunks = kvr[:, :, 0].reshape(b, n_chunk, CMP_STRIDE, 2, N_KV_HEADS, HEAD_DIM).transpose(0, 3, 4, 1, 2, 5)
    kvc = compress_chunks(chunks.reshape(b, 2, N_KV_HEADS, n_chunk, CMP_STRIDE * HEAD_DIM).astype(BF16), w1, w2, pe)
    rc = -(-(T_PAD + max(n_chunk - 1, T_PAD * t // TQ)) // LANE) * LANE
    kvc = jnp.pad(kvc[:, :, :, :n_chunk - 1], ((0, 0), (0, 0), (0, 0), (T_PAD, rc - T_PAD - n_chunk + 1), (0, 0)))
    kc = jnp.pad(kvc[:, 0], ((0, 0), (0, 0), (0, 0), (0, FEAT - HEAD_DIM)))
    nf = FEAT - HEAD_DIM
    blocks = (jnp.arange(t)[:, None] // SEL_BLOCK == jnp.arange(nf)[None, :])
    tz, tc = bias_tables_t(rel_bias)
    gate_t = ng.reshape(b * t, N_KV_HEADS, LANE)[:, :, :4 * HEADS_PER_KV].reshape(b * t, -1).T
    return prompt_attention_t(
        q.T, gate_t, kc, kvc[:, 1].transpose(0, 1, 3, 2), tc, tz,
        _group_major(kvr[:, :, 1, 0], blocks), _value_tiles(kvr[:, :, 1, 1]),
        _group_major(kvr[:, :, 2, 0], jnp.zeros((t, nf), BF16)), _value_tiles(kvr[:, :, 2, 1]), b, t)


def _sample_mixer_a(q, kv, ng, rel_bias, pe, w1, w2, cache_cmp, cache_slc, state_win, page_table):
    b, n_pages = page_table.shape
    page = cache_cmp.shape[1]
    past = n_pages * page
    n_chunk = past // CMP_STRIDE
    rows = cache_cmp[page_table].reshape(b, n_chunk, CMP_STRIDE, 2, N_KV_HEADS, HEAD_DIM).transpose(0, 3, 4, 1, 2, 5)
    kvc = compress_chunks(rows.reshape(b, 2, N_KV_HEADS, n_chunk, CMP_STRIDE * HEAD_DIM).astype(BF16), w1, w2, pe)
    gate = ng.reshape(b, N_KV_HEADS, LANE)[:, :, :3 * HEADS_PER_KV].reshape(b, N_KV_HEADS, 3, HEADS_PER_KV)
    gate = gate.transpose(0, 2, 1, 3).reshape(b, 3, N_HEADS, 1)
    q3 = q.reshape(b, N_HEADS, HEAD_DIM)
    kv_new = kv.reshape(b, 1, 3 * KV_COLS)
    part, picked = sample_cmp_win(rel_bias, q3, kvc, state_win.transpose(0, 2, 3, 4, 1), kv_new, gate, past)
    picked = picked[:, :, :SEL_TOPK].reshape(-1)
    out = sample_slc(page_table, picked, rel_bias, q3, kv_new, gate, part, cache_slc.transpose(0, 2, 3, 4, 1), past)
    return out.reshape(b, ATTN_WIDTH).astype(BF16)


def kernel(x_prompt, x_sample, cache_cmp_kv, cache_slc_kv, state_win_kv, page_table, rel_bias, norm_mix, w_in, pe_cmp, w_cmp1, w_cmp2, gm_ln_g, gm_ln_b, gm_w_s, gm_b_s, w_br_attn, w_br_gmlp, w_out, norm_ffn, w_router, b_router, w_gate, b_gate, w_up, b_up, w_down, b_down, norm_final):
    b, t, _ = x_prompt.shape
    bs, ts, _ = x_sample.shape
    depth = norm_mix.shape[0]
    assert depth == 1 and ts == 1 and cache_slc_kv.shape[2] == SLC_PER_PAGE * SEL_BLOCK
    n_p, n_s = b * t, bs * ts
    kv_tail = (2, N_KV_HEADS, HEAD_DIM)
    lyr = 0

    xp = x_prompt.reshape(n_p, D_MODEL)
    xs = x_sample.reshape(n_s, D_MODEL)
    weights = _in_weights(w_in[lyr])
    qp, kvp, ngp, uvp, mgp = _project(rmsnorm_rows(xp, norm_mix[lyr], 512), weights, BF16)
    qs, kvs, ngs, uvs, mgs = _project(rmsnorm_rows(xs, norm_mix[lyr], n_s), weights, F32)

    attn_p = _prompt_mixer_a(qp, kvp, ngp, rel_bias, pe_cmp[lyr], w_cmp1[lyr], w_cmp2[lyr], b, t)
    attn_s = _sample_mixer_a(qs, kvs, ngs, rel_bias, pe_cmp[lyr], w_cmp1[lyr], w_cmp2[lyr], cache_cmp_kv[lyr],
                             cache_slc_kv[lyr], state_win_kv[lyr], page_table)

    gm_p = gmlp_chunks(uvp, gm_ln_g[lyr], gm_ln_b[lyr], gm_w_s[lyr], gm_b_s[lyr], 4 * CHUNK)
    vn_s, gm_s = gmlp_first_row(uvs, gm_ln_g[lyr], gm_ln_b[lyr], gm_w_s[lyr], gm_b_s[lyr])

    wa, wb, wo = w_br_attn[lyr].astype(BF16), w_br_gmlp[lyr].astype(BF16), w_out[lyr].astype(BF16)
    hp, hnp, idxp, wtp = merge_and_route(attn_p, gm_p, mgp, xp, wa, wb, wo, norm_ffn[lyr], w_router[lyr], b_router[lyr], 256)
    hs, hns, idxs, wts = merge_and_route(attn_s, gm_s, mgs, xs, wa, wb, wo, norm_ffn[lyr], w_router[lyr], b_router[lyr], 256)

    n_all = n_p + n_s
    n_rows = (n_all * TOP_K + N_EXPERTS * (MOE_TM - 1) + MOE_TM - 1) // MOE_TM * MOE_TM
    top_i = jnp.concatenate([idxp[:, :TOP_K], idxs[:, :TOP_K]], axis=0)
    pos, row_pair, tile_expert, tile_first, n_active, tile_next = route_rows(top_i, n_rows)
    x_rows = jnp.concatenate([hnp, hns] * -(-n_rows // n_all), axis=0)[row_pair]
    y_rows = moe_experts(x_rows, tile_expert, tile_first, n_active, tile_next, w_gate[lyr], b_gate[lyr], w_up[lyr],
                         b_up[lyr], w_down[lyr], b_down[lyr])
    y_p = combine_and_norm(hp, y_rows[pos[:n_p].T], wtp, norm_final, 256)
    y_s = combine_and_norm(hs, y_rows[pos[n_p:].T], wts, norm_final, 256)

    kvp5 = kvp.reshape(b, t, 3, *kv_tail)
    kvs5 = kvs.reshape(bs, ts, 3, *kv_tail)
    win_buf = state_win_kv.shape[2]
    win_s = jnp.concatenate([state_win_kv[lyr], kvs5[:, :, 2]], axis=1)[:, ts:]
    return (y_p.reshape(b, t, D_MODEL), y_s.reshape(bs, ts, D_MODEL),
            kvp5[:, :, 0][None], kvs5[:, :, 0][None], kvp5[:, :, 1][None], kvs5[:, :, 1][None],
            kvp5[:, t - min(WINDOW, t):, 2][None], win_s[:, -win_buf:][None],
            vn_s.reshape(1, bs, ts, GM_WIDTH))
```

```python
import functools
import math

import jax
import jax.numpy as jnp
import numpy as np
from jax import lax
from jax.experimental import pallas as pl
from jax.experimental.pallas import tpu as pltpu

D_MODEL = 2048
N_HEADS = 16
HEAD_DIM = 64
N_KV_HEADS = 2
HEADS_PER_KV = N_HEADS // N_KV_HEADS
ATTN_WIDTH = N_HEADS * HEAD_DIM
CMP_BLOCK = 32
CMP_STRIDE = 16
CMP_HID = 2 * HEAD_DIM
SEL_BLOCK = 64
SEL_TOPK = 16
WINDOW = 512
GM_WIDTH = D_MODEL // 2
GM_GROUPS = 8
GM_GROUP_DIM = GM_WIDTH // GM_GROUPS
CHUNK = 128
N_BUCKETS = 32
REL_MAX_DIST = 128
N_EXPERTS = 32
TOP_K = 4
D_FF = D_MODEL
SWIGLU_LIMIT = 7.0
SWIGLU_ALPHA = 1.702
RMS_EPS = 1e-5
LN_EPS = 1e-5
NEG_INF = -1e30
KV_COLS = 2 * N_KV_HEADS * HEAD_DIM
GATE_COLS = 3 * N_HEADS
SLC_PER_PAGE = 2

LANE = 128
SUBLANE = 8
VMEM_LIMIT = 56 * 1024 * 1024

QB = 128
KT = 128
CMP_PAD = QB // CMP_STRIDE
CMP_NEAR = 2 * CMP_PAD
MOE_TM = 256
MOE_TN = 1024
FEAT = 2 * HEAD_DIM

BF16 = jnp.bfloat16
F32 = jnp.float32


def _bucket_upper_bounds():
    d = np.arange(0, 4 * REL_MAX_DIST)
    exact = N_BUCKETS // 2
    out = []
    for dt in (np.float32, np.float64):
        far = exact + (np.log(np.maximum(d, exact).astype(dt) / dt(exact)) / dt(math.log(REL_MAX_DIST / exact))
                       * dt(N_BUCKETS - exact)).astype(np.int32)
        out.append(np.where(d < exact, d, np.minimum(far, N_BUCKETS - 1)))
    assert (out[0] == out[1]).all() and (np.diff(out[0]) >= 0).all()
    b = out[0]
    return [int(d[b <= k].max()) for k in range(N_BUCKETS - 1)]


BUCKET_HI = _bucket_upper_bounds()
FAR_DIST = BUCKET_HI[-1] + 1
assert FAR_DIST <= QB


def _rel_bias(d, rb_ref, h):
    acc = jnp.full(d.shape, rb_ref[N_BUCKETS - 1, h], F32)
    for k in reversed(range(N_BUCKETS - 1)):
        acc = jnp.where(d <= BUCKET_HI[k], rb_ref[k, h], acc)
    return acc


def _dot(a, b, **kw):
    return jnp.dot(a, b, preferred_element_type=F32, **kw)


def _dot_nt(a, b):
    return lax.dot_general(a, b, (((1,), (1,)), ((), ())), preferred_element_type=F32)


def _gelu(x):
    return 0.5 * x * (1.0 + lax.erf(x * np.float32(math.sqrt(0.5))))


def _params(*sem):
    return pltpu.CompilerParams(dimension_semantics=sem, vmem_limit_bytes=VMEM_LIMIT)


def _rmsnorm_kernel(x_ref, g_ref, o_ref):
    x = x_ref[...]
    ms = jnp.mean(x * x, axis=-1, keepdims=True)
    o_ref[...] = (x * lax.rsqrt(ms + RMS_EPS) * g_ref[...]).astype(o_ref.dtype)


def rmsnorm_rows(x, g, tm):
    m, d = x.shape
    return pl.pallas_call(
        _rmsnorm_kernel, grid=(m // tm,), name="rmsnorm",
        in_specs=[pl.BlockSpec((tm, d), lambda i: (i, 0)), pl.BlockSpec((1, d), lambda i: (0, 0))],
        out_specs=pl.BlockSpec((tm, d), lambda i: (i, 0)),
        out_shape=jax.ShapeDtypeStruct((m, d), BF16), compiler_params=_params("parallel"))(x, g.reshape(1, d))


def _mm_kernel(a_ref, w_ref, o_ref, *, epilogue):
    o_ref[...] = epilogue(_dot(a_ref[...], w_ref[...])).astype(o_ref.dtype)


def matmul(a, w, epilogue, out_dtype, tm, tn, name):
    m, k = a.shape
    n = w.shape[1]
    tm, tn = min(tm, m), min(tn, n)
    return pl.pallas_call(
        functools.partial(_mm_kernel, epilogue=epilogue), grid=(m // tm, n // tn), name=name,
        in_specs=[pl.BlockSpec((tm, k), lambda i, j: (i, 0)), pl.BlockSpec((k, tn), lambda i, j: (0, j))],
        out_specs=pl.BlockSpec((tm, tn), lambda i, j: (i, j)),
        out_shape=jax.ShapeDtypeStruct((m, n), out_dtype), compiler_params=_params("parallel", "parallel"))(a, w)


def _compress_kernel(c_ref, w1_ref, w2_ref, pe_ref, o_ref):
    n = c_ref.shape[0]
    half = CMP_STRIDE * HEAD_DIM
    c = c_ref[...]
    first = _dot(c, w1_ref[0:half, :])
    second = _dot(c, w1_ref[half:2 * half, :])
    pe = _dot(pe_ref[...].astype(BF16), w1_ref[...])[0:1, :]
    hid = _gelu(first + pltpu.roll(second, n - 1, 0) + pe)
    o_ref[...] = _dot(hid.astype(BF16), w2_ref[...])


def compress_chunks(chunks, w1, w2, pe):
    b, _, g, n, width = chunks.shape
    pe_rows = jnp.broadcast_to(pe.reshape(2, 1, CMP_BLOCK * HEAD_DIM), (2, SUBLANE, CMP_BLOCK * HEAD_DIM))
    return pl.pallas_call(
        _compress_kernel, grid=(b, 2, g), name="compress",
        in_specs=[pl.BlockSpec((None, None, None, n, width), lambda i, s, j: (i, s, j, 0, 0)),
                  pl.BlockSpec((None, CMP_BLOCK * HEAD_DIM, CMP_HID), lambda i, s, j: (s, 0, 0)),
                  pl.BlockSpec((None, CMP_HID, HEAD_DIM), lambda i, s, j: (s, 0, 0)),
                  pl.BlockSpec((None, SUBLANE, CMP_BLOCK * HEAD_DIM), lambda i, s, j: (s, 0, 0))],
        out_specs=pl.BlockSpec((None, None, None, n, HEAD_DIM), lambda i, s, j: (i, s, j, 0, 0)),
        out_shape=jax.ShapeDtypeStruct((b, 2, g, n, HEAD_DIM), F32),
        compiler_params=_params("parallel", "parallel", "parallel"))(chunks, w1.astype(BF16), w2.astype(BF16), pe_rows)


def _bias_tables_kernel(rb_ref, tz_ref, tc_ref):
    i = lax.broadcasted_iota(jnp.int32, (QB, KT), 0)
    j = lax.broadcasted_iota(jnp.int32, (QB, KT), 1)
    ic = lax.broadcasted_iota(jnp.int32, (QB, CMP_NEAR), 0)
    cc = lax.broadcasted_iota(jnp.int32, (QB, CMP_NEAR), 1)
    d0 = i - j
    d1 = d0 + KT
    dc = ic + (QB - CMP_BLOCK + 1) - CMP_STRIDE * cc
    for h in range(N_HEADS):
        last = rb_ref[N_BUCKETS - 1, h]
        rows = slice(h * QB, (h + 1) * QB)
        tz_ref[0, rows, :] = jnp.where(d0 < 0, NEG_INF, _rel_bias(d0, rb_ref, h) - last)
        tz_ref[1, rows, :] = _rel_bias(d1, rb_ref, h) - last
        tc_ref[rows, :] = jnp.where(dc < 0, NEG_INF, _rel_bias(dc, rb_ref, h) - last)


def bias_tables(rel_bias):
    return pl.pallas_call(
        _bias_tables_kernel, name="bias_tables",
        in_specs=[pl.BlockSpec(memory_space=pltpu.SMEM)],
        out_specs=[pl.BlockSpec(memory_space=pltpu.VMEM), pl.BlockSpec(memory_space=pltpu.VMEM)],
        out_shape=[jax.ShapeDtypeStruct((2, N_HEADS * QB, KT), F32),
                   jax.ShapeDtypeStruct((N_HEADS * QB, CMP_NEAR), F32)])(rel_bias)


def _prompt_attn_kernel(q_ref, gate_ref, kc_ref, vc_ref, ov_ref, tc_ref, tz_ref, ks_ref, vs_ref, kw_ref, vw_ref,
                        o_ref, qa_ref, m_ref, acc_ref, *, n_chunk):
    qi = pl.program_id(2)
    rows_all = HEADS_PER_KV * QB

    for h in range(HEADS_PER_KV):
        qa_ref[h * QB:(h + 1) * QB, 0:HEAD_DIM] = q_ref[:, h * HEAD_DIM:(h + 1) * HEAD_DIM]
        qa_ref[h * QB:(h + 1) * QB, HEAD_DIM:FEAT] = jnp.zeros((QB, FEAT - HEAD_DIM), BF16)
    qa = qa_ref[...]
    i_row = lax.broadcasted_iota(jnp.int32, (rows_all, 1), 0) % QB

    near0 = pl.multiple_of(qi * CMP_PAD, SUBLANE)
    r = lax.broadcasted_iota(jnp.int32, (1, n_chunk), 1)
    lf = _dot_nt(qa, kc_ref[0:n_chunk, :].astype(BF16))
    lf = jnp.where((r >= CMP_PAD) & (r < qi * CMP_PAD), lf, NEG_INF)
    c = lax.broadcasted_iota(jnp.int32, (1, CMP_NEAR), 1)
    ln = _dot_nt(qa, kc_ref[pl.ds(near0, CMP_NEAR), :].astype(BF16)) + tc_ref[...]
    ln = jnp.where(qi * CMP_PAD + c >= CMP_PAD, ln, NEG_INF)
    m = jnp.maximum(jnp.max(lf, axis=-1, keepdims=True), jnp.max(ln, axis=-1, keepdims=True))
    pf = jnp.exp(lf - m)
    pn = jnp.exp(ln - m)
    denom = jnp.sum(pf, axis=-1, keepdims=True) + jnp.sum(pn, axis=-1, keepdims=True)
    inv = jnp.where(qi * QB + i_row >= CMP_BLOCK - 1, 1.0 / denom, 0.0)
    pf = pf * inv
    pn = pn * inv
    o_cmp = (_dot(pf.astype(BF16), vc_ref[0:n_chunk, :].astype(BF16))
             + _dot(pn.astype(BF16), vc_ref[pl.ds(near0, CMP_NEAR), :].astype(BF16)))
    psf = jnp.sum(pf.reshape(HEADS_PER_KV, QB, n_chunk), axis=0)
    psn = jnp.sum(pn.reshape(HEADS_PER_KV, QB, CMP_NEAR), axis=0)
    p_slc = (_dot(psf, ov_ref[0:n_chunk, :], precision=lax.Precision.HIGHEST)
             + _dot(psn, ov_ref[pl.ds(near0, CMP_NEAR), :], precision=lax.Precision.HIGHEST))

    nf = FEAT - HEAD_DIM
    blk = lax.broadcasted_iota(jnp.int32, (QB, nf), 1)
    jt = (qi * QB + lax.broadcasted_iota(jnp.int32, (QB, nf), 0)) // SEL_BLOCK
    forced = (blk == 0) | (blk == jt) | (blk == jt - 1)
    score = jnp.where(forced, jnp.inf, jnp.where(blk <= jt, p_slc, -jnp.inf))
    rank = jnp.zeros((QB, nf), jnp.int32)
    for col in range(nf):
        sc = score[:, col:col + 1]
        rank = rank + jnp.where(blk > col, (sc >= score).astype(jnp.int32), (sc > score).astype(jnp.int32))
    feat = jnp.where((rank < SEL_TOPK) & (score > NEG_INF), 0.0, NEG_INF).astype(BF16)
    for h in range(HEADS_PER_KV):
        qa_ref[h * QB:(h + 1) * QB, HEAD_DIM:FEAT] = feat
    qa = qa_ref[...]

    def reset():
        m_ref[...] = jnp.full(m_ref.shape, -3e38, F32)
        acc_ref[...] = jnp.zeros(acc_ref.shape, F32)

    def step(k_ref, v_ref, j, bias):
        off = pl.multiple_of(j * KT, KT)
        s = _dot_nt(qa, k_ref[pl.ds(off, KT), :])
        if bias is not None:
            s = s + bias
        m_old = m_ref[...]
        m_new = jnp.maximum(m_old, jnp.max(s, axis=-1, keepdims=True))
        p = jnp.exp(s - m_new)
        acc_ref[...] = jnp.exp(m_old - m_new) * acc_ref[...] + _dot(p.astype(BF16), v_ref[pl.ds(off, KT), :])
        m_ref[...] = m_new

    def result():
        acc = acc_ref[...]
        return acc[:, 0:HEAD_DIM] * (1.0 / acc[:, HEAD_DIM:HEAD_DIM + 1])

    reset()

    def far_step(j, carry):
        step(ks_ref, vs_ref, j, None)
        return carry

    lax.fori_loop(0, jnp.maximum(qi - 1, 0), far_step, 0)

    @pl.when(qi >= 1)
    def _():
        step(ks_ref, vs_ref, qi - 1, tz_ref[1])

    step(ks_ref, vs_ref, qi, tz_ref[0])
    o_slc = result()

    reset()
    n_back = WINDOW // KT
    edge = jnp.where(lax.broadcasted_iota(jnp.int32, (rows_all, KT), 1) >= i_row, 0.0, NEG_INF)
    for back in range(n_back, -1, -1):
        bias = edge if back == n_back else tz_ref[1] if back == 1 else tz_ref[0] if back == 0 else None

        @pl.when(qi >= back)
        def _(back=back, bias=bias):
            step(kw_ref, vw_ref, qi - back, bias)

    o_win = result()

    gate = gate_ref[...]
    for h in range(HEADS_PER_KV):
        rows = slice(h * QB, (h + 1) * QB)
        o = (gate[:, h:h + 1] * o_cmp[rows] + gate[:, HEADS_PER_KV + h:HEADS_PER_KV + h + 1] * o_slc[rows]
             + gate[:, 2 * HEADS_PER_KV + h:2 * HEADS_PER_KV + h + 1] * o_win[rows])
        o_ref[:, h * HEAD_DIM:(h + 1) * HEAD_DIM] = o.astype(o_ref.dtype)


def _sel_overlap_rows(n_rows, n_cmp, pad):
    nf = FEAT - HEAD_DIM
    cs = (np.arange(n_rows) - pad)[:, None] * CMP_STRIDE
    ss = np.arange(nf)[None, :] * SEL_BLOCK
    ov = np.minimum(cs + CMP_BLOCK, ss + SEL_BLOCK) - np.maximum(cs, ss)
    ov = np.maximum(ov, 0) / CMP_STRIDE
    valid = (np.arange(n_rows) >= pad) & (np.arange(n_rows) < pad + n_cmp)
    return (ov * valid[:, None]).astype(np.float32)


def prompt_attention(q, gate, kc, vc, tc, tz, ks, vs, kw, vw, b, t):
    n_chunk = t // CMP_STRIDE
    rc = n_chunk + 2 * CMP_PAD
    assert t // SEL_BLOCK <= FEAT - HEAD_DIM and t % QB == 0 and n_chunk % LANE == 0
    ov = jnp.asarray(_sel_overlap_rows(rc, n_chunk - 1, CMP_PAD))
    nq = t // QB
    gw = HEADS_PER_KV * HEAD_DIM
    kv_spec = pl.BlockSpec((None, None, t, FEAT), lambda i, g, qi: (i, g, 0, 0))
    return pl.pallas_call(
        functools.partial(_prompt_attn_kernel, n_chunk=n_chunk), grid=(b, N_KV_HEADS, nq), name="prompt_attention",
        in_specs=[pl.BlockSpec((QB, gw), lambda i, g, qi: (i * nq + qi, g)),
                  pl.BlockSpec((QB, LANE), lambda i, g, qi: (i * nq + qi, g)),
                  pl.BlockSpec((None, None, rc, FEAT), lambda i, g, qi: (i, g, 0, 0)),
                  pl.BlockSpec((None, None, rc, HEAD_DIM), lambda i, g, qi: (i, g, 0, 0)),
                  pl.BlockSpec((rc, FEAT - HEAD_DIM), lambda i, g, qi: (0, 0)),
                  pl.BlockSpec((HEADS_PER_KV * QB, CMP_NEAR), lambda i, g, qi: (g, 0)),
                  pl.BlockSpec((2, HEADS_PER_KV * QB, KT), lambda i, g, qi: (0, g, 0)),
                  kv_spec, kv_spec, kv_spec, kv_spec],
        out_specs=pl.BlockSpec((QB, gw), lambda i, g, qi: (i * nq + qi, g)),
        out_shape=jax.ShapeDtypeStruct((b * t, ATTN_WIDTH), BF16),
        scratch_shapes=[pltpu.VMEM((HEADS_PER_KV * QB, FEAT), BF16),
                        pltpu.VMEM((HEADS_PER_KV * QB, LANE), F32),
                        pltpu.VMEM((HEADS_PER_KV * QB, LANE), F32)],
        compiler_params=_params("parallel", "parallel", "arbitrary"))(q, gate, kc, vc, ov, tc, tz, ks, vs, kw, vw)


TQ = 256
T_PAD = TQ // CMP_STRIDE
T_NEAR = 2 * T_PAD
V_ROWS = HEAD_DIM + 16
assert FAR_DIST <= TQ and WINDOW % TQ == 0


def _bias_tables_t_kernel(rb_ref, tz_ref, tc_ref):
    j = lax.broadcasted_iota(jnp.int32, (TQ, TQ), 0)
    i = lax.broadcasted_iota(jnp.int32, (TQ, TQ), 1)
    cc = lax.broadcasted_iota(jnp.int32, (T_NEAR, TQ), 0)
    ic = lax.broadcasted_iota(jnp.int32, (T_NEAR, TQ), 1)
    d0 = i - j
    d1 = d0 + TQ
    dc = ic + (TQ - CMP_BLOCK + 1) - CMP_STRIDE * cc
    for h in range(N_HEADS):
        last = rb_ref[N_BUCKETS - 1, h]
        tz_ref[0, h] = jnp.where(d0 < 0, NEG_INF, _rel_bias(d0, rb_ref, h) - last)
        tz_ref[1, h] = _rel_bias(d1, rb_ref, h) - last
        tc_ref[h] = jnp.where(dc < 0, NEG_INF, _rel_bias(dc, rb_ref, h) - last)


def bias_tables_t(rel_bias):
    return pl.pallas_call(
        _bias_tables_t_kernel, name="bias_tables",
        in_specs=[pl.BlockSpec(memory_space=pltpu.SMEM)],
        out_specs=[pl.BlockSpec(memory_space=pltpu.VMEM), pl.BlockSpec(memory_space=pltpu.VMEM)],
        out_shape=[jax.ShapeDtypeStruct((2, N_HEADS, TQ, TQ), F32),
                   jax.ShapeDtypeStruct((N_HEADS, T_NEAR, TQ), F32)],
        compiler_params=pltpu.CompilerParams(vmem_limit_bytes=VMEM_LIMIT))(rel_bias)


def _prompt_attn_t_kernel(qt_ref, gate_ref, kc_ref, vct_ref, ovt_ref, tc_ref, tz_ref, ks_ref, vst_ref, kw_ref, vwt_ref,
                          o_ref, qa_ref, lc_ref, m_ref, acc_ref, ot_ref):
    qi = pl.program_id(2)
    nf = FEAT - HEAD_DIM
    hp = HEADS_PER_KV
    for h in range(hp):
        qa_ref[h, 0:HEAD_DIM, :] = qt_ref[h * HEAD_DIM:(h + 1) * HEAD_DIM, :]
        qa_ref[h, HEAD_DIM:FEAT, :] = jnp.zeros((nf, TQ), BF16)
    t_lane = qi * TQ + lax.broadcasted_iota(jnp.int32, (1, TQ), 1)

    near0 = pl.multiple_of(qi * T_PAD, T_PAD)
    rc = kc_ref.shape[0]
    row = lax.broadcasted_iota(jnp.int32, (rc, 1), 0)
    row_ok = (row >= T_PAD) & (row < near0 + T_NEAR)
    has_block = t_lane >= CMP_BLOCK - 1
    kc = kc_ref[...].astype(BF16)
    vct = vct_ref[...].astype(BF16)
    for h in range(hp):
        lc_ref[h] = _dot(kc, qa_ref[h])
    for h in range(hp):
        lc_ref[h, pl.ds(near0, T_NEAR), :] = lc_ref[h, pl.ds(near0, T_NEAR), :] + tc_ref[h]
    p_heads = jnp.zeros((rc, TQ), F32)
    for h in range(hp):
        lc = jnp.where(row_ok, lc_ref[h], NEG_INF)
        p = jnp.exp(lc - jnp.max(lc, axis=0, keepdims=True))
        p = p * jnp.where(has_block, 1.0 / jnp.sum(p, axis=0, keepdims=True), 0.0)
        p_heads = p_heads + p
        ot_ref[h * HEAD_DIM:(h + 1) * HEAD_DIM, :] = gate_ref[h:h + 1, :] * _dot(vct, p.astype(BF16))
    p_slc = _dot(ovt_ref[...], p_heads, precision=lax.Precision.HIGHEST)

    blk = lax.broadcasted_iota(jnp.int32, (nf, TQ), 0)
    jt = t_lane // SEL_BLOCK
    forced = (blk == 0) | (blk == jt) | (blk == jt - 1)
    score = jnp.where(forced, jnp.inf, jnp.where(blk <= jt, p_slc, -jnp.inf))
    n_grp = nf // SUBLANE
    groups = [score[SUBLANE * g:SUBLANE * (g + 1), :] for g in range(n_grp)]
    ranks = [jnp.zeros((SUBLANE, TQ), jnp.int32) for _ in range(n_grp)]
    sub = lax.broadcasted_iota(jnp.int32, (SUBLANE, TQ), 0)
    for c in range(nf):
        cg, cs = divmod(c, SUBLANE)
        other = jnp.broadcast_to(groups[cg][cs:cs + 1, :], (SUBLANE, TQ))
        for g in range(n_grp):
            if g > cg:
                beats = (other >= groups[g]).astype(jnp.int32)
            elif g < cg:
                beats = (other > groups[g]).astype(jnp.int32)
            else:
                beats = jnp.where(sub > cs, (other >= groups[g]).astype(jnp.int32), (other > groups[g]).astype(jnp.int32))
            ranks[g] = ranks[g] + beats
    feat = jnp.concatenate(
        [jnp.where((ranks[g] < SEL_TOPK) & (groups[g] > NEG_INF), 0.0, NEG_INF) for g in range(n_grp)], axis=0)
    feat = feat.astype(BF16)
    for h in range(hp):
        qa_ref[h, HEAD_DIM:FEAT, :] = feat

    def reset():
        m_ref[...] = jnp.full(m_ref.shape, -3e38, F32)
        acc_ref[...] = jnp.zeros(acc_ref.shape, F32)

    def step(k_ref, vt_ref, j, bias):
        k = k_ref[pl.ds(pl.multiple_of(j * TQ, TQ), TQ), :]
        vt = vt_ref[j]
        m_old = [m_ref[h] for h in range(hp)]
        acc_old = [acc_ref[h] for h in range(hp)]
        s = [_dot(k, qa_ref[h]) for h in range(hp)]
        if bias is not None:
            s = [s[h] + bias(h) for h in range(hp)]
        m_new = [jnp.maximum(m_old[h], jnp.max(s[h], axis=0, keepdims=True)) for h in range(hp)]
        pv = [_dot(vt, jnp.exp(s[h] - m_new[h]).astype(BF16)) for h in range(hp)]
        for h in range(hp):
            acc_ref[h] = jnp.exp(m_old[h] - m_new[h]) * acc_old[h] + pv[h]
            m_ref[h] = m_new[h]

    def add_result(branch):
        for h in range(hp):
            acc = acc_ref[h]
            o = acc[0:HEAD_DIM, :] * (1.0 / acc[HEAD_DIM:HEAD_DIM + 1, :])
            rows = slice(h * HEAD_DIM, (h + 1) * HEAD_DIM)
            ot_ref[rows, :] = ot_ref[rows, :] + gate_ref[branch * hp + h:branch * hp + h + 1, :] * o

    diag = lambda h: tz_ref[0, h]
    prev = lambda h: tz_ref[1, h]

    reset()

    def far_step(j, carry):
        step(ks_ref, vst_ref, j, None)
        return carry

    lax.fori_loop(0, jnp.maximum(qi - 1, 0), far_step, 0)

    @pl.when(qi >= 1)
    def _():
        step(ks_ref, vst_ref, qi - 1, prev)

    step(ks_ref, vst_ref, qi, diag)
    add_result(1)

    reset()
    n_back = WINDOW // TQ
    edge = jnp.where(lax.broadcasted_iota(jnp.int32, (TQ, TQ), 0) >= lax.broadcasted_iota(jnp.int32, (TQ, TQ), 1),
                     0.0, NEG_INF)
    for back in range(n_back, -1, -1):
        bias = (lambda h: edge) if back == n_back else prev if back == 1 else diag if back == 0 else None

        @pl.when(qi >= back)
        def _(back=back, bias=bias):
            step(kw_ref, vwt_ref, qi - back, bias)

    add_result(2)
    o_ref[...] = ot_ref[...].T.astype(o_ref.dtype)


def prompt_attention_t(qt, gate_t, kc, vct, tc, tz, ks, vst, kw, vwt, b, t):
    n_chunk = t // CMP_STRIDE
    rc = kc.shape[2]
    nf = FEAT - HEAD_DIM
    nq = t // TQ
    assert t // SEL_BLOCK <= nf and t % TQ == 0 and rc >= T_PAD * (nq + 1) and rc >= T_PAD + n_chunk - 1
    ovt = jnp.asarray(_sel_overlap_rows(rc, n_chunk - 1, T_PAD).T)
    hp = HEADS_PER_KV
    gw = hp * HEAD_DIM
    k_spec = pl.BlockSpec((None, None, t, FEAT), lambda i, g, qi: (i, g, 0, 0))
    v_spec = pl.BlockSpec((None, None, nq, V_ROWS, TQ), lambda i, g, qi: (i, g, 0, 0, 0))
    return pl.pallas_call(
        _prompt_attn_t_kernel, grid=(b, N_KV_HEADS, nq), name="prompt_attention",
        in_specs=[pl.BlockSpec((gw, TQ), lambda i, g, qi: (g, i * nq + qi)),
                  pl.BlockSpec((4 * hp, TQ), lambda i, g, qi: (g, i * nq + qi)),
                  pl.BlockSpec((None, None, rc, FEAT), lambda i, g, qi: (i, g, 0, 0)),
                  pl.BlockSpec((None, None, HEAD_DIM, rc), lambda i, g, qi: (i, g, 0, 0)),
                  pl.BlockSpec((nf, rc), lambda i, g, qi: (0, 0)),
                  pl.BlockSpec((hp, T_NEAR, TQ), lambda i, g, qi: (g, 0, 0)),
                  pl.BlockSpec((2, hp, TQ, TQ), lambda i, g, qi: (0, g, 0, 0)),
                  k_spec, v_spec, k_spec, v_spec],
        out_specs=pl.BlockSpec((TQ, gw), lambda i, g, qi: (i * nq + qi, g)),
        out_shape=jax.ShapeDtypeStruct((b * t, ATTN_WIDTH), BF16),
        scratch_shapes=[pltpu.VMEM((hp, FEAT, TQ), BF16), pltpu.VMEM((hp, rc, TQ), F32),
                        pltpu.VMEM((hp, 1, TQ), F32), pltpu.VMEM((hp, V_ROWS, TQ), F32), pltpu.VMEM((gw, TQ), F32)],
        compiler_params=_params("parallel", "parallel", "arbitrary"))(
            qt, gate_t, kc, vct, ovt, tc, tz, ks, vst, kw, vwt)


def _gmlp_kernel(u_ref, v_ref, lng_ref, lnb_ref, ws_ref, bs_ref, o_ref):
    v = v_ref[...].astype(F32)
    mu = jnp.mean(v, axis=-1, keepdims=True)
    var = jnp.mean(jnp.square(v - mu), axis=-1, keepdims=True)
    vn = ((v - mu) * lax.rsqrt(var + LN_EPS) * lng_ref[...] + lnb_ref[...]).astype(BF16)
    tri = (lax.broadcasted_iota(jnp.int32, (CHUNK, CHUNK), 0) >= lax.broadcasted_iota(jnp.int32, (CHUNK, CHUNK), 1))
    for g in range(GM_GROUPS):
        w = jnp.where(tri, ws_ref[g], 0.0).astype(BF16)
        cols = slice(g * GM_GROUP_DIM, (g + 1) * GM_GROUP_DIM)
        for c in range(u_ref.shape[0] // CHUNK):
            rows = slice(c * CHUNK, (c + 1) * CHUNK)
            s = _dot(w, vn[rows, cols]) + bs_ref[:, g:g + 1]
            o_ref[rows, cols] = (u_ref[rows, cols].astype(F32) * s).astype(o_ref.dtype)


def gmlp_chunks(uv, ln_g, ln_b, w_s, b_s, rows):
    m = uv.shape[0]
    return pl.pallas_call(
        _gmlp_kernel, grid=(m // rows,), name="gmlp",
        in_specs=[pl.BlockSpec((rows, GM_WIDTH), lambda i: (i, 0)), pl.BlockSpec((rows, GM_WIDTH), lambda i: (i, 1)),
                  pl.BlockSpec((1, GM_WIDTH), lambda i: (0, 0)), pl.BlockSpec((1, GM_WIDTH), lambda i: (0, 0)),
                  pl.BlockSpec((GM_GROUPS, CHUNK, CHUNK), lambda i: (0, 0, 0)),
                  pl.BlockSpec((CHUNK, GM_GROUPS), lambda i: (0, 0))],
        out_specs=pl.BlockSpec((rows, GM_WIDTH), lambda i: (i, 0)),
        out_shape=jax.ShapeDtypeStruct((m, GM_WIDTH), BF16),
        compiler_params=_params("parallel"))(uv, uv, ln_g.reshape(1, -1), ln_b.reshape(1, -1), w_s, b_s.T)


def _merge_kernel(attn_ref, gm_ref, mg_ref, x_ref, wa_ref, wb_ref, wo_ref, gn_ref, wr_ref, br_ref,
                  h_ref, hn_ref, idx_ref, wt_ref):
    tm = x_ref.shape[0]
    sub = min(MERGE_SUB, tm)
    parts = [slice(r0, r0 + sub) for r0 in range(0, tm, sub)]
    ta = [_dot(attn_ref[p, :], wa_ref[...]) for p in parts]
    tb = [_dot(gm_ref[p, :], wb_ref[...]) for p in parts]
    merged = [(mg_ref[p, 0:D_MODEL].astype(F32) * ta[i] + mg_ref[p, D_MODEL:2 * D_MODEL].astype(F32) * tb[i])
              .astype(BF16) for i, p in enumerate(parts)]
    h = [x_ref[p, :] + _dot(merged[i], wo_ref[...]) for i, p in enumerate(parts)]
    hn = [hh * lax.rsqrt(jnp.mean(hh * hh, axis=-1, keepdims=True) + RMS_EPS) * gn_ref[...] for hh in h]
    hi = [v.astype(BF16) for v in hn]
    lo = [(v - hi[i].astype(F32)).astype(BF16) for i, v in enumerate(hn)]
    logits = [_dot(hi[i], wr_ref[0]) + _dot(lo[i], wr_ref[0]) + _dot(hi[i], wr_ref[1]) + br_ref[...]
              for i in range(len(parts))]
    routed = [_route(lg) for lg in logits]
    for i, p in enumerate(parts):
        h_ref[p, :] = h[i]
        hn_ref[p, :] = hi[i]
        idx_ref[p, :] = routed[i][0]
        wt_ref[p, :] = routed[i][1]


def _route(logits):
    lane = lax.broadcasted_iota(jnp.int32, logits.shape, 1)
    vals, idxs = [], []
    for _ in range(TOP_K):
        best = jnp.max(logits, axis=-1, keepdims=True)
        ix = jnp.min(jnp.where(logits == best, lane, LANE), axis=-1, keepdims=True)
        vals.append(best)
        idxs.append(ix)
        logits = jnp.where(lane == ix, -jnp.inf, logits)
    ex = [jnp.exp(v - vals[0]) for v in vals]
    inv = 1.0 / functools.reduce(lambda a, b: a + b, ex)
    idx_out = jnp.zeros(lane.shape, jnp.int32)
    wt_out = jnp.zeros(lane.shape, F32)
    for k in range(TOP_K):
        idx_out = jnp.where(lane == k, idxs[k], idx_out)
        wt_out = jnp.where(lane == k, ex[k] * inv, wt_out)
    return idx_out, wt_out


MERGE_SUB = 128


def merge_and_route(attn, gm, mg, x, wa, wb, wo, g_ffn, w_router, b_router, tm):
    m = x.shape[0]
    tm = min(tm, m)
    wr = jnp.zeros((D_MODEL, LANE), F32).at[:, :N_EXPERTS].set(w_router)
    wr_hi = wr.astype(BF16)
    wr = jnp.stack([wr_hi, (wr - wr_hi.astype(F32)).astype(BF16)])
    br = jnp.full((1, LANE), -jnp.inf, F32).at[0, :N_EXPERTS].set(b_router)
    row = lambda w: pl.BlockSpec((tm, w), lambda i: (i, 0))
    full = lambda a: pl.BlockSpec(a.shape, lambda i: (0,) * a.ndim, pipeline_mode=pl.Buffered(1))
    return pl.pallas_call(
        _merge_kernel, grid=(m // tm,), name="merge_route",
        in_specs=[row(ATTN_WIDTH), row(GM_WIDTH), row(2 * D_MODEL), row(D_MODEL), full(wa), full(wb), full(wo),
                  pl.BlockSpec((1, D_MODEL), lambda i: (0, 0)), full(wr), full(br)],
        out_specs=[row(D_MODEL), row(D_MODEL), row(LANE), row(LANE)],
        out_shape=[jax.ShapeDtypeStruct((m, D_MODEL), F32), jax.ShapeDtypeStruct((m, D_MODEL), BF16),
                   jax.ShapeDtypeStruct((m, LANE), jnp.int32), jax.ShapeDtypeStruct((m, LANE), F32)],
        compiler_params=_params("parallel"))(attn, gm, mg, x, wa, wb, wo, g_ffn.reshape(1, -1), wr, br)


def _expert_tile(te_ref, first_ref, nact_ref, nxt_ref, w_hbm, stage_ref, cache_ref, sem, o_ref, compute):
    j, r = pl.program_id(0), pl.program_id(1)
    tn = stage_ref.shape[-1]
    rows = stage_ref.shape[1]

    def copies(e, jj):
        cols = pl.ds(pl.multiple_of(jj * tn, tn), tn)
        return [pltpu.make_async_copy(w.at[e, :, cols], stage_ref.at[i], sem.at[i]) for i, w in enumerate(w_hbm)]

    @pl.when((j == 0) & (r == 0))
    def _():
        for c in copies(te_ref[0], 0):
            c.start()

    @pl.when(first_ref[r] == 1)
    def _():
        for c in copies(te_ref[r], j):
            c.wait()
        for i in range(len(w_hbm)):
            for r0 in range(0, rows, 256):
                cache_ref[i, r0:r0 + 256, :] = stage_ref[i, r0:r0 + 256, :].astype(cache_ref.dtype)
        wraps = nxt_ref[r] < 0
        e_next = jnp.where(wraps, te_ref[0], nxt_ref[r])
        j_next = jnp.where(wraps, j + 1, j)

        @pl.when(j_next < pl.num_programs(0))
        def _():
            for c in copies(e_next, j_next):
                c.start()

    @pl.when(r < nact_ref[0])
    def _():
        compute(0, tn)

    @pl.when(r >= nact_ref[0])
    def _():
        o_ref[...] = jnp.zeros(o_ref.shape, o_ref.dtype)


def _moe_up_kernel(te_ref, first_ref, nact_ref, nxt_ref, x_ref, wg_hbm, wu_hbm, bg_ref, bu_ref, o_ref,
                   stage_ref, cache_ref, sem):
    def compute(c0, c1):
        x = x_ref[...]
        g = jnp.minimum(_dot(x, cache_ref[0, :, c0:c1]) + bg_ref[:, c0:c1], SWIGLU_LIMIT)
        u = jnp.clip(_dot(x, cache_ref[1, :, c0:c1]) + bu_ref[:, c0:c1], -SWIGLU_LIMIT, SWIGLU_LIMIT)
        o_ref[:, c0:c1] = ((u + 1.0) * (g * jax.nn.sigmoid(SWIGLU_ALPHA * g))).astype(o_ref.dtype)

    _expert_tile(te_ref, first_ref, nact_ref, nxt_ref, (wg_hbm, wu_hbm), stage_ref, cache_ref, sem, o_ref, compute)


def _moe_down_kernel(te_ref, first_ref, nact_ref, nxt_ref, a_ref, wd_hbm, bd_ref, o_ref, stage_ref, cache_ref, sem):
    def compute(c0, c1):
        o_ref[:, c0:c1] = (_dot(a_ref[...], cache_ref[0, :, c0:c1]) + bd_ref[:, c0:c1]).astype(o_ref.dtype)

    _expert_tile(te_ref, first_ref, nact_ref, nxt_ref, (wd_hbm,), stage_ref, cache_ref, sem, o_ref, compute)


def moe_experts(x_rows, tile_expert, tile_first, n_active, tile_next, w_gate, b_gate, w_up, b_up, w_down, b_down):
    rows = x_rows.shape[0]
    n_tiles = rows // MOE_TM
    row_map = lambda j, r, te, first, nact, nxt: (jnp.minimum(r, nact[0] - 1), 0)
    out_map = lambda j, r, te, first, nact, nxt: (r, j)
    b_spec = pl.BlockSpec((None, 1, MOE_TN), lambda j, r, te, first, nact, nxt: (te[r], 0, j))
    hbm = pl.BlockSpec(memory_space=pl.ANY)

    def scratch(n_w, k):
        return [pltpu.VMEM((n_w, k, MOE_TN), F32), pltpu.VMEM((n_w, k, MOE_TN), BF16), pltpu.SemaphoreType.DMA((n_w,))]

    act = pl.pallas_call(
        _moe_up_kernel, name="moe_up",
        grid_spec=pltpu.PrefetchScalarGridSpec(
            num_scalar_prefetch=4, grid=(D_FF // MOE_TN, n_tiles),
            in_specs=[pl.BlockSpec((MOE_TM, D_MODEL), row_map), hbm, hbm, b_spec, b_spec],
            out_specs=pl.BlockSpec((MOE_TM, MOE_TN), out_map),
            scratch_shapes=scratch(2, D_MODEL)),
        out_shape=jax.ShapeDtypeStruct((rows, D_FF), BF16),
        compiler_params=_params("arbitrary", "arbitrary"))(
            tile_expert, tile_first, n_active, tile_next, x_rows, w_gate, w_up,
            b_gate.reshape(N_EXPERTS, 1, D_FF), b_up.reshape(N_EXPERTS, 1, D_FF))
    return pl.pallas_call(
        _moe_down_kernel, name="moe_down",
        grid_spec=pltpu.PrefetchScalarGridSpec(
            num_scalar_prefetch=4, grid=(D_MODEL // MOE_TN, n_tiles),
            in_specs=[pl.BlockSpec((MOE_TM, D_FF), row_map), hbm, b_spec],
            out_specs=pl.BlockSpec((MOE_TM, MOE_TN), out_map),
            scratch_shapes=scratch(1, D_FF)),
        out_shape=jax.ShapeDtypeStruct((rows, D_MODEL), BF16),
        compiler_params=_params("arbitrary", "arbitrary"))(
            tile_expert, tile_first, n_active, tile_next, act, w_down, b_down.reshape(N_EXPERTS, 1, D_MODEL))


def route_rows(top_i, n_rows):
    n = top_i.shape[0]
    hit = top_i[:, :, None] == jnp.arange(N_EXPERTS, dtype=jnp.int32)[None, None, :]
    onehot = hit.astype(jnp.int32).sum(axis=1)
    before = jnp.cumsum(onehot, axis=0) - onehot
    counts = onehot.sum(axis=0)
    padded = (counts + MOE_TM - 1) // MOE_TM * MOE_TM
    ends = jnp.cumsum(padded)
    starts = ends - padded
    pos = jnp.sum(jnp.where(hit, (starts[None, :] + before)[:, None, :], 0), axis=-1)
    pair_id = jnp.arange(TOP_K, dtype=jnp.int32)[None, :] * n + jnp.arange(n, dtype=jnp.int32)[:, None]
    row_pair = jnp.arange(n_rows, dtype=jnp.int32).at[pos.reshape(-1)].set(pair_id.reshape(-1), unique_indices=True)
    n_tiles = n_rows // MOE_TM
    n_active = (ends[-1] // MOE_TM).astype(jnp.int32)
    tile_row = jnp.minimum(jnp.arange(n_tiles, dtype=jnp.int32), n_active - 1) * MOE_TM
    tile_expert = jnp.minimum((tile_row[:, None] >= ends[None, :]).astype(jnp.int32).sum(axis=-1), N_EXPERTS - 1)
    tile_first = jnp.concatenate([jnp.ones((1,), jnp.int32), (tile_expert[1:] != tile_expert[:-1]).astype(jnp.int32)])
    tile_id = jnp.arange(n_tiles, dtype=jnp.int32)
    start_id = jnp.where(tile_first == 1, tile_id, n_tiles)
    next_start = jnp.concatenate([lax.cummin(start_id[::-1])[::-1][1:], jnp.full((1,), n_tiles, jnp.int32)])
    tile_next = jnp.where(next_start < n_tiles, tile_expert[jnp.minimum(next_start, n_tiles - 1)], -1)
    return pos, row_pair, tile_expert, tile_first, n_active.reshape(1), tile_next.astype(jnp.int32)


def _combine_kernel(h_ref, y_ref, w_ref, g_ref, o_ref):
    h = h_ref[...]
    w = w_ref[...]
    for k in range(TOP_K):
        h = h + w[:, k:k + 1] * y_ref[k].astype(F32)
    o_ref[...] = h * lax.rsqrt(jnp.mean(h * h, axis=-1, keepdims=True) + RMS_EPS) * g_ref[...]


def combine_and_norm(h, y4, wts, g, tm):
    m = h.shape[0]
    tm = min(tm, m)
    return pl.pallas_call(
        _combine_kernel, grid=(m // tm,), name="combine_norm",
        in_specs=[pl.BlockSpec((tm, D_MODEL), lambda i: (i, 0)), pl.BlockSpec((TOP_K, tm, D_MODEL), lambda i: (0, i, 0)),
                  pl.BlockSpec((tm, LANE), lambda i: (i, 0)), pl.BlockSpec((1, D_MODEL), lambda i: (0, 0))],
        out_specs=pl.BlockSpec((tm, D_MODEL), lambda i: (i, 0)),
        out_shape=jax.ShapeDtypeStruct((m, D_MODEL), F32), compiler_params=_params("parallel"))(
            h, y4, wts, g.reshape(1, -1))


def _in_weights(w_in):
    c0 = ATTN_WIDTH
    c1 = c0 + 3 * KV_COLS
    c2 = c1 + GATE_COLS
    c3 = c2 + 2 * GM_WIDTH
    wg = w_in[:, c1:c2].reshape(D_MODEL, N_KV_HEADS, HEADS_PER_KV, 3).transpose(0, 1, 3, 2)
    wg = jnp.pad(wg.reshape(D_MODEL, N_KV_HEADS, 3 * HEADS_PER_KV), ((0, 0), (0, 0), (0, LANE - 3 * HEADS_PER_KV)))
    cast = lambda w: w.astype(BF16)
    return (cast(w_in[:, :c0]), cast(w_in[:, c0:c1]), cast(wg.reshape(D_MODEL, N_KV_HEADS * LANE)),
            cast(w_in[:, c2:c3]), cast(w_in[:, c3:]))


def _project(a, weights, wide_dtype):
    w_q, w_kv, w_ng, w_uv, w_mg = weights
    q = matmul(a, w_q, lambda z: z * np.float32(HEAD_DIM ** -0.5), wide_dtype, 1024, 1024, "proj_q")
    kv = matmul(a, w_kv, lambda z: z, F32, 1024, 3 * KV_COLS, "proj_kv")
    ng = matmul(a, w_ng, jax.nn.sigmoid, F32, 1024, N_KV_HEADS * LANE, "proj_gate")
    uv = matmul(a, w_uv, _gelu, wide_dtype, 1024, 1024, "proj_uv")
    mg = matmul(a, w_mg, jax.nn.sigmoid, BF16, 1024, 1024, "proj_merge")
    return q, kv, ng, uv, mg


def _head_rows(fn):
    out = fn(0)
    out = jnp.broadcast_to(out, (HEADS_PER_KV, out.shape[1]))
    row = lax.broadcasted_iota(jnp.int32, out.shape, 0)
    for h in range(1, HEADS_PER_KV):
        out = jnp.where(row == h, fn(h), out)
    return out


def _sample_cmp_win_kernel(rb_ref, q_ref, kvc_ref, ov_ref, sw_ref, new_ref, gate_ref, o_ref, idx_ref, *, past, n_cmp):
    n_rows = kvc_ref.shape[2]
    n_blk = ov_ref.shape[1]
    win = sw_ref.shape[-1]
    win_col = 2 * KV_COLS
    lane_blk = lax.broadcasted_iota(jnp.int32, (1, n_blk), 1)
    lane_out = lax.broadcasted_iota(jnp.int32, (1, LANE), 1)
    for g in range(N_KV_HEADS):
        heads = slice(g * HEADS_PER_KV, (g + 1) * HEADS_PER_KV)
        q = q_ref[heads, :]
        qb = q.astype(BF16)

        n = lax.broadcasted_iota(jnp.int32, (1, n_rows), 1)
        d = past - (n * CMP_STRIDE + CMP_BLOCK - 1)
        logit = _dot_nt(qb, kvc_ref[0, g].astype(BF16)) + _head_rows(
            lambda h: _rel_bias(d, rb_ref, g * HEADS_PER_KV + h))
        logit = jnp.where(n < n_cmp, logit, NEG_INF)
        p = jnp.exp(logit - jnp.max(logit, axis=-1, keepdims=True))
        p = p * (1.0 / jnp.sum(p, axis=-1, keepdims=True))
        o_cmp = _dot(p.astype(BF16), kvc_ref[1, g].astype(BF16))
        p_heads = jnp.broadcast_to(jnp.sum(p, axis=0, keepdims=True), (SUBLANE, n_rows))
        p_slc = _dot(p_heads, ov_ref[...], precision=lax.Precision.HIGHEST)[0:1, :]

        score = jnp.where((lane_blk == 0) | (lane_blk == n_blk - 1), jnp.inf, p_slc)
        picked = jnp.zeros((1, LANE), jnp.int32)
        for k in range(SEL_TOPK - 1):
            best = jnp.max(score, axis=-1, keepdims=True)
            ix = jnp.min(jnp.where(score == best, lane_blk, n_blk), axis=-1, keepdims=True)
            picked = jnp.where(lane_out == k, ix, picked)
            score = jnp.where(lane_blk == ix, -jnp.inf, score)
        idx_ref[g:g + 1, :] = picked

        dw = win - lax.broadcasted_iota(jnp.int32, (1, win), 1)
        lw = _dot(qb, sw_ref[0, g].astype(BF16)) + _head_rows(
            lambda h: _rel_bias(dw, rb_ref, g * HEADS_PER_KV + h))
        k_new = new_ref[:, win_col + g * HEAD_DIM:win_col + (g + 1) * HEAD_DIM]
        v_new = new_ref[:, win_col + KV_COLS // 2 + g * HEAD_DIM:win_col + KV_COLS // 2 + (g + 1) * HEAD_DIM]
        l_new = jnp.sum(q * k_new, axis=-1, keepdims=True) + _head_rows(
            lambda h: jnp.full((1, 1), rb_ref[0, g * HEADS_PER_KV + h], F32))
        m = jnp.maximum(jnp.max(lw, axis=-1, keepdims=True), l_new)
        pw = jnp.exp(lw - m)
        p_new = jnp.exp(l_new - m)
        o_win = ((_dot_nt(pw.astype(BF16), sw_ref[1, g].astype(BF16)) + p_new * v_new)
                 * (1.0 / (jnp.sum(pw, axis=-1, keepdims=True) + p_new)))
        o_ref[heads, :] = gate_ref[0, heads, :] * o_cmp + gate_ref[2, heads, :] * o_win


def sample_cmp_win(rel_bias, q, kvc, state_win, kv_new, gate, past):
    b, _, _, n_rows, _ = kvc.shape
    n_cmp = past // CMP_STRIDE - CMP_BLOCK // CMP_STRIDE + 1
    n_blk = past // SEL_BLOCK
    assert past % SEL_BLOCK == 0 and n_blk >= SEL_TOPK and n_cmp <= n_rows
    cs = np.arange(n_rows)[:, None] * CMP_STRIDE
    ss = np.arange(n_blk)[None, :] * SEL_BLOCK
    ov = np.maximum(np.minimum(cs + CMP_BLOCK, ss + SEL_BLOCK) - np.maximum(cs, ss), 0) / CMP_STRIDE
    ov = jnp.asarray((ov * (np.arange(n_rows) < n_cmp)[:, None]).astype(np.float32))
    win = state_win.shape[-1]
    return pl.pallas_call(
        functools.partial(_sample_cmp_win_kernel, past=past, n_cmp=n_cmp), grid=(b,), name="sample_cmp_win",
        in_specs=[pl.BlockSpec(memory_space=pltpu.SMEM),
                  pl.BlockSpec((None, N_HEADS, HEAD_DIM), lambda i: (i, 0, 0)),
                  pl.BlockSpec((None, 2, N_KV_HEADS, n_rows, HEAD_DIM), lambda i: (i, 0, 0, 0, 0)),
                  pl.BlockSpec(ov.shape, lambda i: (0, 0)),
                  pl.BlockSpec((None, 2, N_KV_HEADS, HEAD_DIM, win), lambda i: (i, 0, 0, 0, 0)),
                  pl.BlockSpec((None, 1, 3 * KV_COLS), lambda i: (i, 0, 0)),
                  pl.BlockSpec((None, 3, N_HEADS, 1), lambda i: (i, 0, 0, 0))],
        out_specs=[pl.BlockSpec((None, N_HEADS, HEAD_DIM), lambda i: (i, 0, 0)),
                   pl.BlockSpec((None, N_KV_HEADS, LANE), lambda i: (i, 0, 0))],
        out_shape=[jax.ShapeDtypeStruct((b, N_HEADS, HEAD_DIM), F32),
                   jax.ShapeDtypeStruct((b, N_KV_HEADS, LANE), jnp.int32)],
        compiler_params=_params("parallel"))(rel_bias, q, kvc, ov, state_win, kv_new, gate)


def _sample_slc_kernel(pt_ref, idx_ref, rb_ref, q_ref, new_ref, gate_ref, part_ref, *rest, past):
    blocks, o_ref = rest[:-1], rest[-1]
    b, g = pl.program_id(0), pl.program_id(1)
    q = q_ref[...]
    qb = q.astype(BF16)
    slc_col = KV_COLS
    half = KV_COLS // 2
    page = blocks[0].shape[-1]
    pos = lax.broadcasted_iota(jnp.int32, (1, page), 1)
    logits, values = [], []
    for k, blk_ref in enumerate(blocks):
        ix = idx_ref[(b * N_KV_HEADS + g) * SEL_TOPK + k]
        d = past - ((ix // SLC_PER_PAGE) * page + pos)
        lg = _dot(qb, blk_ref[0].astype(BF16)) + _head_rows(lambda h: _rel_bias(d, rb_ref, g * HEADS_PER_KV + h))
        logits.append(jnp.where(pos // SEL_BLOCK == ix % SLC_PER_PAGE, lg, NEG_INF))
        values.append(blk_ref[1].astype(BF16))
    new = new_ref[...]
    k_new = jnp.where(g == 0, new[:, slc_col:slc_col + HEAD_DIM], new[:, slc_col + HEAD_DIM:slc_col + 2 * HEAD_DIM])
    v_new = jnp.where(g == 0, new[:, slc_col + half:slc_col + half + HEAD_DIM],
                      new[:, slc_col + half + HEAD_DIM:slc_col + half + 2 * HEAD_DIM])
    l_new = jnp.sum(q * k_new, axis=-1, keepdims=True) + _head_rows(
        lambda h: jnp.full((1, 1), rb_ref[0, g * HEADS_PER_KV + h], F32))
    m = l_new
    for lg in logits:
        m = jnp.maximum(m, jnp.max(lg, axis=-1, keepdims=True))
    p_new = jnp.exp(l_new - m)
    denom = p_new
    out = p_new * v_new
    for lg, vv in zip(logits, values):
        p = jnp.exp(lg - m)
        denom = denom + jnp.sum(p, axis=-1, keepdims=True)
        out = out + _dot_nt(p.astype(BF16), vv)
    o_ref[...] = part_ref[...] + gate_ref[1] * (out * (1.0 / denom))


def sample_slc(page_table, picked, rel_bias, q, kv_new, gate, part, cache_slc, past):
    b = q.shape[0]
    page = cache_slc.shape[-1]
    n_pick = SEL_TOPK - 1

    def blk_map(k):
        def index(i, g, pt, idx, k=k):
            ix = idx[(i * N_KV_HEADS + g) * SEL_TOPK + k]
            return (pt[i, ix // SLC_PER_PAGE], 0, g, 0, 0)
        return index

    head_spec = pl.BlockSpec((None, HEADS_PER_KV, HEAD_DIM), lambda i, g, pt, idx: (i, g, 0))
    return pl.pallas_call(
        functools.partial(_sample_slc_kernel, past=past), name="sample_slc",
        grid_spec=pltpu.PrefetchScalarGridSpec(
            num_scalar_prefetch=2, grid=(b, N_KV_HEADS),
            in_specs=[pl.BlockSpec(memory_space=pltpu.SMEM), head_spec,
                      pl.BlockSpec((None, 1, 3 * KV_COLS), lambda i, g, pt, idx: (i, 0, 0)),
                      pl.BlockSpec((None, 3, HEADS_PER_KV, 1), lambda i, g, pt, idx: (i, 0, g, 0)), head_spec]
            + [pl.BlockSpec((None, 2, None, HEAD_DIM, page), blk_map(k)) for k in range(n_pick)],
            out_specs=head_spec),
        out_shape=jax.ShapeDtypeStruct((b, N_HEADS, HEAD_DIM), F32),
        compiler_params=_params("parallel", "parallel"))(
            page_table, picked, rel_bias, q, kv_new, gate, part, *([cache_slc] * n_pick))


def _gmlp_row_kernel(uv_ref, lng_ref, lnb_ref, w0_ref, b0_ref, vn_ref, o_ref):
    v = uv_ref[:, GM_WIDTH:2 * GM_WIDTH]
    mu = jnp.mean(v, axis=-1, keepdims=True)
    var = jnp.mean(jnp.square(v - mu), axis=-1, keepdims=True)
    vn = (v - mu) * lax.rsqrt(var + LN_EPS) * lng_ref[...] + lnb_ref[...]
    vn_ref[...] = vn
    o_ref[...] = (uv_ref[:, 0:GM_WIDTH] * (w0_ref[...] * vn + b0_ref[...])).astype(o_ref.dtype)


def gmlp_first_row(uv, ln_g, ln_b, w_s, b_s):
    m = uv.shape[0]
    w0 = jnp.repeat(w_s[:, 0, 0], GM_GROUP_DIM).reshape(1, GM_WIDTH)
    b0 = jnp.repeat(b_s[:, 0], GM_GROUP_DIM).reshape(1, GM_WIDTH)
    return pl.pallas_call(
        _gmlp_row_kernel, name="gmlp_row",
        out_shape=[jax.ShapeDtypeStruct((m, GM_WIDTH), F32), jax.ShapeDtypeStruct((m, GM_WIDTH), BF16)])(
            uv, ln_g.reshape(1, -1), ln_b.reshape(1, -1), w0, b0)


def _group_major(x, feat):
    b, t = x.shape[0], x.shape[1]
    f = jnp.broadcast_to(feat.astype(BF16)[None, None], (b, N_KV_HEADS, t, FEAT - HEAD_DIM))
    return jnp.concatenate([x.transpose(0, 2, 1, 3).astype(BF16), f], axis=-1)


def _value_tiles(x):
    b, t = x.shape[0], x.shape[1]
    xt = x.transpose(0, 2, 3, 1).astype(BF16)
    extra = jnp.zeros((b, N_KV_HEADS, V_ROWS - HEAD_DIM, t), BF16).at[:, :, 0].set(1)
    tiles = jnp.concatenate([xt, extra], axis=2).reshape(b, N_KV_HEADS, V_ROWS, t // TQ, TQ)
    return tiles.transpose(0, 1, 3, 2, 4)


def _prompt_mixer_a(q, kv, ng, rel_bias, pe, w1, w2, b, t):
    kvr = kv.reshape(b, t, 3, 2, N_KV_HEADS, HEAD_DIM)
    n_chunk = t // CMP_STRIDE
    chunks = kvr[:, :, 0].reshape(b, n_chunk, CMP_STRIDE, 2, N_KV_HEADS, HEAD_DIM).transpose(0, 3, 4, 1, 2, 5)
    kvc = compress_chunks(chunks.reshape(b, 2, N_KV_HEADS, n_chunk, CMP_STRIDE * HEAD_DIM).astype(BF16), w1, w2, pe)
    rc = -(-(T_PAD + max(n_chunk - 1, T_PAD * t // TQ)) // LANE) * LANE
    kvc = jnp.pad(kvc[:, :, :, :n_chunk - 1], ((0, 0), (0, 0), (0, 0), (T_PAD, rc - T_PAD - n_chunk + 1), (0, 0)))
    kc = jnp.pad(kvc[:, 0], ((0, 0), (0, 0), (0, 0), (0, FEAT - HEAD_DIM)))
    nf = FEAT - HEAD_DIM
    blocks = (jnp.arange(t)[:, None] // SEL_BLOCK == jnp.arange(nf)[None, :])
    tz, tc = bias_tables_t(rel_bias)
    gate_t = ng.reshape(b * t, N_KV_HEADS, LANE)[:, :, :4 * HEADS_PER_KV].reshape(b * t, -1).T
    return prompt_attention_t(
        q.T, gate_t, kc, kvc[:, 1].transpose(0, 1, 3, 2), tc, tz,
        _group_major(kvr[:, :, 1, 0], blocks), _value_tiles(kvr[:, :, 1, 1]),
        _group_major(kvr[:, :, 2, 0], jnp.zeros((t, nf), BF16)), _value_tiles(kvr[:, :, 2, 1]), b, t)


def _sample_mixer_a(q, kv, ng, rel_bias, pe, w1, w2, cache_cmp, cache_slc, state_win, page_table):
    b, n_pages = page_table.shape
    page = cache_cmp.shape[1]
    past = n_pages * page
    n_chunk = past // CMP_STRIDE
    rows = cache_cmp[page_table].reshape(b, n_chunk, CMP_STRIDE, 2, N_KV_HEADS, HEAD_DIM).transpose(0, 3, 4, 1, 2, 5)
    kvc = compress_chunks(rows.reshape(b, 2, N_KV_HEADS, n_chunk, CMP_STRIDE * HEAD_DIM).astype(BF16), w1, w2, pe)
    gate = ng.reshape(b, N_KV_HEADS, LANE)[:, :, :3 * HEADS_PER_KV].reshape(b, N_KV_HEADS, 3, HEADS_PER_KV)
    gate = gate.transpose(0, 2, 1, 3).reshape(b, 3, N_HEADS, 1)
    q3 = q.reshape(b, N_HEADS, HEAD_DIM)
    kv_new = kv.reshape(b, 1, 3 * KV_COLS)
    part, picked = sample_cmp_win(rel_bias, q3, kvc, state_win.transpose(0, 2, 3, 4, 1), kv_new, gate, past)
    picked = picked[:, :, :SEL_TOPK].reshape(-1)
    out = sample_slc(page_table, picked, rel_bias, q3, kv_new, gate, part, cache_slc.transpose(0, 2, 3, 4, 1), past)
    return out.reshape(b, ATTN_WIDTH).astype(BF16)


def kernel(x_prompt, x_sample, cache_cmp_kv, cache_slc_kv, state_win_kv, page_table, rel_bias, norm_mix, w_in, pe_cmp, w_cmp1, w_cmp2, gm_ln_g, gm_ln_b, gm_w_s, gm_b_s, w_br_attn, w_br_gmlp, w_out, norm_ffn, w_router, b_router, w_gate, b_gate, w_up, b_up, w_down, b_down, norm_final):
    b, t, _ = x_prompt.shape
    bs, ts, _ = x_sample.shape
    depth = norm_mix.shape[0]
    assert depth == 1 and ts == 1 and cache_slc_kv.shape[2] == SLC_PER_PAGE * SEL_BLOCK
    n_p, n_s = b * t, bs * ts
    kv_tail = (2, N_KV_HEADS, HEAD_DIM)
    lyr = 0

    xp = x_prompt.reshape(n_p, D_MODEL)
    xs = x_sample.reshape(n_s, D_MODEL)
    weights = _in_weights(w_in[lyr])
    qp, kvp, ngp, uvp, mgp = _project(rmsnorm_rows(xp, norm_mix[lyr], 512), weights, BF16)
    qs, kvs, ngs, uvs, mgs = _project(rmsnorm_rows(xs, norm_mix[lyr], n_s), weights, F32)

    attn_p = _prompt_mixer_a(qp, kvp, ngp, rel_bias, pe_cmp[lyr], w_cmp1[lyr], w_cmp2[lyr], b, t)
    attn_s = _sample_mixer_a(qs, kvs, ngs, rel_bias, pe_cmp[lyr], w_cmp1[lyr], w_cmp2[lyr], cache_cmp_kv[lyr],
                             cache_slc_kv[lyr], state_win_kv[lyr], page_table)

    gm_p = gmlp_chunks(uvp, gm_ln_g[lyr], gm_ln_b[lyr], gm_w_s[lyr], gm_b_s[lyr], 4 * CHUNK)
    vn_s, gm_s = gmlp_first_row(uvs, gm_ln_g[lyr], gm_ln_b[lyr], gm_w_s[lyr], gm_b_s[lyr])

    wa, wb, wo = w_br_attn[lyr].astype(BF16), w_br_gmlp[lyr].astype(BF16), w_out[lyr].astype(BF16)
    hp, hnp, idxp, wtp = merge_and_route(attn_p, gm_p, mgp, xp, wa, wb, wo, norm_ffn[lyr], w_router[lyr], b_router[lyr], 256)
    hs, hns, idxs, wts = merge_and_route(attn_s, gm_s, mgs, xs, wa, wb, wo, norm_ffn[lyr], w_router[lyr], b_router[lyr], 256)

    n_all = n_p + n_s
    n_rows = (n_all * TOP_K + N_EXPERTS * (MOE_TM - 1) + MOE_TM - 1) // MOE_TM * MOE_TM
    top_i = jnp.concatenate([idxp[:, :TOP_K], idxs[:, :TOP_K]], axis=0)
    pos, row_pair, tile_expert, tile_first, n_active, tile_next = route_rows(top_i, n_rows)
    x_rows = jnp.concatenate([hnp, hns] * -(-n_rows // n_all), axis=0)[row_pair]
    y_rows = moe_experts(x_rows, tile_expert, tile_first, n_active, tile_next, w_gate[lyr], b_gate[lyr], w_up[lyr],
                         b_up[lyr], w_down[lyr], b_down[lyr])
    y_p = combine_and_norm(hp, y_rows[pos[:n_p].T], wtp, norm_final, 256)
    y_s = combine_and_norm(hs, y_rows[pos[n_p:].T], wts, norm_final, 256)

    kvp5 = kvp.reshape(b, t, 3, *kv_tail)
    kvs5 = kvs.reshape(bs, ts, 3, *kv_tail)
    win_buf = state_win_kv.shape[2]
    win_s = jnp.concatenate([state_win_kv[lyr], kvs5[:, :, 2]], axis=1)[:, ts:]
    return (y_p.reshape(b, t, D_MODEL), y_s.reshape(bs, ts, D_MODEL),
            kvp5[:, :, 0][None], kvs5[:, :, 0][None], kvp5[:, :, 1][None], kvs5[:, :, 1][None],
            kvp5[:, t - min(WINDOW, t):, 2][None], win_s[:, -win_buf:][None],
            vn_s.reshape(1, bs, ts, GM_WIDTH))
```

```python
import functools
import math

import jax
import jax.numpy as jnp
import numpy as np
from jax import lax
from jax.experimental import pallas as pl
from jax.experimental.pallas import tpu as pltpu

D_MODEL = 2048
N_HEADS = 16
HEAD_DIM = 64
N_KV_HEADS = 2
HEADS_PER_KV = N_HEADS // N_KV_HEADS
ATTN_WIDTH = N_HEADS * HEAD_DIM
CMP_BLOCK = 32
CMP_STRIDE = 16
CMP_HID = 2 * HEAD_DIM
SEL_BLOCK = 64
SEL_TOPK = 16
WINDOW = 512
GM_WIDTH = D_MODEL // 2
GM_GROUPS = 8
GM_GROUP_DIM = GM_WIDTH // GM_GROUPS
CHUNK = 128
N_BUCKETS = 32
REL_MAX_DIST = 128
N_EXPERTS = 32
TOP_K = 4
D_FF = D_MODEL
SWIGLU_LIMIT = 7.0
SWIGLU_ALPHA = 1.702
RMS_EPS = 1e-5
LN_EPS = 1e-5
NEG_INF = -1e30
LOG2_E = math.log2(math.e)
KV_COLS = 2 * N_KV_HEADS * HEAD_DIM
GATE_COLS = 3 * N_HEADS
SLC_PER_PAGE = 2

LANE = 128
SUBLANE = 8
VMEM_LIMIT = 56 * 1024 * 1024

QB = 128
KT = 128
CMP_PAD = QB // CMP_STRIDE
CMP_NEAR = 2 * CMP_PAD
MOE_TM = 256
MOE_TN = 1024
MOE_DMA_SPLIT = 4
FEAT = 2 * HEAD_DIM

BF16 = jnp.bfloat16
F32 = jnp.float32


def _bucket_upper_bounds():
    d = np.arange(0, 4 * REL_MAX_DIST)
    exact = N_BUCKETS // 2
    out = []
    for dt in (np.float32, np.float64):
        far = exact + (np.log(np.maximum(d, exact).astype(dt) / dt(exact)) / dt(math.log(REL_MAX_DIST / exact))
                       * dt(N_BUCKETS - exact)).astype(np.int32)
        out.append(np.where(d < exact, d, np.minimum(far, N_BUCKETS - 1)))
    assert (out[0] == out[1]).all() and (np.diff(out[0]) >= 0).all()
    b = out[0]
    return [int(d[b <= k].max()) for k in range(N_BUCKETS - 1)]


BUCKET_HI = _bucket_upper_bounds()
FAR_DIST = BUCKET_HI[-1] + 1
assert FAR_DIST <= QB


def _rel_bias(d, rb_ref, h):
    acc = jnp.full(d.shape, rb_ref[N_BUCKETS - 1, h], F32)
    for k in reversed(range(N_BUCKETS - 1)):
        acc = jnp.where(d <= BUCKET_HI[k], rb_ref[k, h], acc)
    return acc


def _dot(a, b, **kw):
    return jnp.dot(a, b, preferred_element_type=F32, **kw)


def _dot_nt(a, b):
    return lax.dot_general(a, b, (((1,), (1,)), ((), ())), preferred_element_type=F32)


def _gelu(x):
    return 0.5 * x * (1.0 + lax.erf(x * np.float32(math.sqrt(0.5))))


def _params(*sem):
    return pltpu.CompilerParams(dimension_semantics=sem, vmem_limit_bytes=VMEM_LIMIT)


def _rmsnorm_kernel(x_ref, g_ref, o_ref):
    x = x_ref[...]
    ms = jnp.mean(x * x, axis=-1, keepdims=True)
    o_ref[...] = (x * lax.rsqrt(ms + RMS_EPS) * g_ref[...]).astype(o_ref.dtype)


def rmsnorm_rows(x, g, tm):
    m, d = x.shape
    return pl.pallas_call(
        _rmsnorm_kernel, grid=(m // tm,), name="rmsnorm",
        in_specs=[pl.BlockSpec((tm, d), lambda i: (i, 0)), pl.BlockSpec((1, d), lambda i: (0, 0))],
        out_specs=pl.BlockSpec((tm, d), lambda i: (i, 0)),
        out_shape=jax.ShapeDtypeStruct((m, d), BF16), compiler_params=_params("parallel"))(x, g.reshape(1, d))


def _mm_kernel(a_ref, w_ref, o_ref, *, epilogue):
    o_ref[...] = epilogue(_dot(a_ref[...], w_ref[...])).astype(o_ref.dtype)


def matmul(a, w, epilogue, out_dtype, tm, tn, name):
    m, k = a.shape
    n = w.shape[1]
    tm, tn = min(tm, m), min(tn, n)
    return pl.pallas_call(
        functools.partial(_mm_kernel, epilogue=epilogue), grid=(m // tm, n // tn), name=name,
        in_specs=[pl.BlockSpec((tm, k), lambda i, j: (i, 0)), pl.BlockSpec((k, tn), lambda i, j: (0, j))],
        out_specs=pl.BlockSpec((tm, tn), lambda i, j: (i, j)),
        out_shape=jax.ShapeDtypeStruct((m, n), out_dtype), compiler_params=_params("parallel", "parallel"))(a, w)


def _compress_kernel(c_ref, w1_ref, w2_ref, pe_ref, o_ref):
    n = c_ref.shape[0]
    half = CMP_STRIDE * HEAD_DIM
    c = c_ref[...]
    first = _dot(c, w1_ref[0:half, :])
    second = _dot(c, w1_ref[half:2 * half, :])
    pe = _dot(pe_ref[...].astype(BF16), w1_ref[...])[0:1, :]
    hid = _gelu(first + pltpu.roll(second, n - 1, 0) + pe)
    o_ref[...] = _dot(hid.astype(BF16), w2_ref[...])


def compress_chunks(chunks, w1, w2, pe):
    b, _, g, n, width = chunks.shape
    pe_rows = jnp.broadcast_to(pe.reshape(2, 1, CMP_BLOCK * HEAD_DIM), (2, SUBLANE, CMP_BLOCK * HEAD_DIM))
    return pl.pallas_call(
        _compress_kernel, grid=(b, 2, g), name="compress",
        in_specs=[pl.BlockSpec((None, None, None, n, width), lambda i, s, j: (i, s, j, 0, 0)),
                  pl.BlockSpec((None, CMP_BLOCK * HEAD_DIM, CMP_HID), lambda i, s, j: (s, 0, 0)),
                  pl.BlockSpec((None, CMP_HID, HEAD_DIM), lambda i, s, j: (s, 0, 0)),
                  pl.BlockSpec((None, SUBLANE, CMP_BLOCK * HEAD_DIM), lambda i, s, j: (s, 0, 0))],
        out_specs=pl.BlockSpec((None, None, None, n, HEAD_DIM), lambda i, s, j: (i, s, j, 0, 0)),
        out_shape=jax.ShapeDtypeStruct((b, 2, g, n, HEAD_DIM), F32),
        compiler_params=_params("parallel", "parallel", "parallel"))(chunks, w1.astype(BF16), w2.astype(BF16), pe_rows)


def _bias_tables_kernel(rb_ref, tz_ref, tc_ref):
    i = lax.broadcasted_iota(jnp.int32, (QB, KT), 0)
    j = lax.broadcasted_iota(jnp.int32, (QB, KT), 1)
    ic = lax.broadcasted_iota(jnp.int32, (QB, CMP_NEAR), 0)
    cc = lax.broadcasted_iota(jnp.int32, (QB, CMP_NEAR), 1)
    d0 = i - j
    d1 = d0 + KT
    dc = ic + (QB - CMP_BLOCK + 1) - CMP_STRIDE * cc
    for h in range(N_HEADS):
        last = rb_ref[N_BUCKETS - 1, h]
        rows = slice(h * QB, (h + 1) * QB)
        tz_ref[0, rows, :] = jnp.where(d0 < 0, NEG_INF, _rel_bias(d0, rb_ref, h) - last)
        tz_ref[1, rows, :] = _rel_bias(d1, rb_ref, h) - last
        tc_ref[rows, :] = jnp.where(dc < 0, NEG_INF, _rel_bias(dc, rb_ref, h) - last)


def bias_tables(rel_bias):
    return pl.pallas_call(
        _bias_tables_kernel, name="bias_tables",
        in_specs=[pl.BlockSpec(memory_space=pltpu.SMEM)],
        out_specs=[pl.BlockSpec(memory_space=pltpu.VMEM), pl.BlockSpec(memory_space=pltpu.VMEM)],
        out_shape=[jax.ShapeDtypeStruct((2, N_HEADS * QB, KT), F32),
                   jax.ShapeDtypeStruct((N_HEADS * QB, CMP_NEAR), F32)])(rel_bias)


def _prompt_attn_kernel(q_ref, gate_ref, kc_ref, vc_ref, ov_ref, tc_ref, tz_ref, ks_ref, vs_ref, kw_ref, vw_ref,
                        o_ref, qa_ref, m_ref, acc_ref, *, n_chunk):
    qi = pl.program_id(2)
    rows_all = HEADS_PER_KV * QB

    for h in range(HEADS_PER_KV):
        qa_ref[h * QB:(h + 1) * QB, 0:HEAD_DIM] = q_ref[:, h * HEAD_DIM:(h + 1) * HEAD_DIM]
        qa_ref[h * QB:(h + 1) * QB, HEAD_DIM:FEAT] = jnp.zeros((QB, FEAT - HEAD_DIM), BF16)
    qa = qa_ref[...]
    i_row = lax.broadcasted_iota(jnp.int32, (rows_all, 1), 0) % QB

    near0 = pl.multiple_of(qi * CMP_PAD, SUBLANE)
    r = lax.broadcasted_iota(jnp.int32, (1, n_chunk), 1)
    lf = _dot_nt(qa, kc_ref[0:n_chunk, :].astype(BF16))
    lf = jnp.where((r >= CMP_PAD) & (r < qi * CMP_PAD), lf, NEG_INF)
    c = lax.broadcasted_iota(jnp.int32, (1, CMP_NEAR), 1)
    ln = _dot_nt(qa, kc_ref[pl.ds(near0, CMP_NEAR), :].astype(BF16)) + tc_ref[...]
    ln = jnp.where(qi * CMP_PAD + c >= CMP_PAD, ln, NEG_INF)
    m = jnp.maximum(jnp.max(lf, axis=-1, keepdims=True), jnp.max(ln, axis=-1, keepdims=True))
    pf = jnp.exp(lf - m)
    pn = jnp.exp(ln - m)
    denom = jnp.sum(pf, axis=-1, keepdims=True) + jnp.sum(pn, axis=-1, keepdims=True)
    inv = jnp.where(qi * QB + i_row >= CMP_BLOCK - 1, 1.0 / denom, 0.0)
    pf = pf * inv
    pn = pn * inv
    o_cmp = (_dot(pf.astype(BF16), vc_ref[0:n_chunk, :].astype(BF16))
             + _dot(pn.astype(BF16), vc_ref[pl.ds(near0, CMP_NEAR), :].astype(BF16)))
    psf = jnp.sum(pf.reshape(HEADS_PER_KV, QB, n_chunk), axis=0)
    psn = jnp.sum(pn.reshape(HEADS_PER_KV, QB, CMP_NEAR), axis=0)
    p_slc = (_dot(psf, ov_ref[0:n_chunk, :], precision=lax.Precision.HIGHEST)
             + _dot(psn, ov_ref[pl.ds(near0, CMP_NEAR), :], precision=lax.Precision.HIGHEST))

    nf = FEAT - HEAD_DIM
    blk = lax.broadcasted_iota(jnp.int32, (QB, nf), 1)
    jt = (qi * QB + lax.broadcasted_iota(jnp.int32, (QB, nf), 0)) // SEL_BLOCK
    forced = (blk == 0) | (blk == jt) | (blk == jt - 1)
    score = jnp.where(forced, jnp.inf, jnp.where(blk <= jt, p_slc, -jnp.inf))
    rank = jnp.zeros((QB, nf), jnp.int32)
    for col in range(nf):
        sc = score[:, col:col + 1]
        rank = rank + jnp.where(blk > col, (sc >= score).astype(jnp.int32), (sc > score).astype(jnp.int32))
    feat = jnp.where((rank < SEL_TOPK) & (score > NEG_INF), 0.0, NEG_INF).astype(BF16)
    for h in range(HEADS_PER_KV):
        qa_ref[h * QB:(h + 1) * QB, HEAD_DIM:FEAT] = feat
    qa = qa_ref[...]

    def reset():
        m_ref[...] = jnp.full(m_ref.shape, -3e38, F32)
        acc_ref[...] = jnp.zeros(acc_ref.shape, F32)

    def step(k_ref, v_ref, j, bias):
        off = pl.multiple_of(j * KT, KT)
        s = _dot_nt(qa, k_ref[pl.ds(off, KT), :])
        if bias is not None:
            s = s + bias
        m_old = m_ref[...]
        m_new = jnp.maximum(m_old, jnp.max(s, axis=-1, keepdims=True))
        p = jnp.exp(s - m_new)
        acc_ref[...] = jnp.exp(m_old - m_new) * acc_ref[...] + _dot(p.astype(BF16), v_ref[pl.ds(off, KT), :])
        m_ref[...] = m_new

    def result():
        acc = acc_ref[...]
        return acc[:, 0:HEAD_DIM] * (1.0 / acc[:, HEAD_DIM:HEAD_DIM + 1])

    reset()

    def far_step(j, carry):
        step(ks_ref, vs_ref, j, None)
        return carry

    lax.fori_loop(0, jnp.maximum(qi - 1, 0), far_step, 0)

    @pl.when(qi >= 1)
    def _():
        step(ks_ref, vs_ref, qi - 1, tz_ref[1])

    step(ks_ref, vs_ref, qi, tz_ref[0])
    o_slc = result()

    reset()
    n_back = WINDOW // KT
    edge = jnp.where(lax.broadcasted_iota(jnp.int32, (rows_all, KT), 1) >= i_row, 0.0, NEG_INF)
    for back in range(n_back, -1, -1):
        bias = edge if back == n_back else tz_ref[1] if back == 1 else tz_ref[0] if back == 0 else None

        @pl.when(qi >= back)
        def _(back=back, bias=bias):
            step(kw_ref, vw_ref, qi - back, bias)

    o_win = result()

    gate = gate_ref[...]
    for h in range(HEADS_PER_KV):
        rows = slice(h * QB, (h + 1) * QB)
        o = (gate[:, h:h + 1] * o_cmp[rows] + gate[:, HEADS_PER_KV + h:HEADS_PER_KV + h + 1] * o_slc[rows]
             + gate[:, 2 * HEADS_PER_KV + h:2 * HEADS_PER_KV + h + 1] * o_win[rows])
        o_ref[:, h * HEAD_DIM:(h + 1) * HEAD_DIM] = o.astype(o_ref.dtype)


def _sel_overlap_rows(n_rows, n_cmp, pad):
    nf = FEAT - HEAD_DIM
    cs = (np.arange(n_rows) - pad)[:, None] * CMP_STRIDE
    ss = np.arange(nf)[None, :] * SEL_BLOCK
    ov = np.minimum(cs + CMP_BLOCK, ss + SEL_BLOCK) - np.maximum(cs, ss)
    ov = np.maximum(ov, 0) / CMP_STRIDE
    valid = (np.arange(n_rows) >= pad) & (np.arange(n_rows) < pad + n_cmp)
    return (ov * valid[:, None]).astype(np.float32)


def prompt_attention(q, gate, kc, vc, tc, tz, ks, vs, kw, vw, b, t):
    n_chunk = t // CMP_STRIDE
    rc = n_chunk + 2 * CMP_PAD
    assert t // SEL_BLOCK <= FEAT - HEAD_DIM and t % QB == 0 and n_chunk % LANE == 0
    ov = jnp.asarray(_sel_overlap_rows(rc, n_chunk - 1, CMP_PAD))
    nq = t // QB
    gw = HEADS_PER_KV * HEAD_DIM
    kv_spec = pl.BlockSpec((None, None, t, FEAT), lambda i, g, qi: (i, g, 0, 0))
    return pl.pallas_call(
        functools.partial(_prompt_attn_kernel, n_chunk=n_chunk), grid=(b, N_KV_HEADS, nq), name="prompt_attention",
        in_specs=[pl.BlockSpec((QB, gw), lambda i, g, qi: (i * nq + qi, g)),
                  pl.BlockSpec((QB, LANE), lambda i, g, qi: (i * nq + qi, g)),
                  pl.BlockSpec((None, None, rc, FEAT), lambda i, g, qi: (i, g, 0, 0)),
                  pl.BlockSpec((None, None, rc, HEAD_DIM), lambda i, g, qi: (i, g, 0, 0)),
                  pl.BlockSpec((rc, FEAT - HEAD_DIM), lambda i, g, qi: (0, 0)),
                  pl.BlockSpec((HEADS_PER_KV * QB, CMP_NEAR), lambda i, g, qi: (g, 0)),
                  pl.BlockSpec((2, HEADS_PER_KV * QB, KT), lambda i, g, qi: (0, g, 0)),
                  kv_spec, kv_spec, kv_spec, kv_spec],
        out_specs=pl.BlockSpec((QB, gw), lambda i, g, qi: (i * nq + qi, g)),
        out_shape=jax.ShapeDtypeStruct((b * t, ATTN_WIDTH), BF16),
        scratch_shapes=[pltpu.VMEM((HEADS_PER_KV * QB, FEAT), BF16),
                        pltpu.VMEM((HEADS_PER_KV * QB, LANE), F32),
                        pltpu.VMEM((HEADS_PER_KV * QB, LANE), F32)],
        compiler_params=_params("parallel", "parallel", "arbitrary"))(q, gate, kc, vc, ov, tc, tz, ks, vs, kw, vw)


TQ = 256
T_PAD = TQ // CMP_STRIDE
T_NEAR = 2 * T_PAD
V_ROWS = HEAD_DIM + 16
assert FAR_DIST <= TQ and WINDOW % TQ == 0


def _bias_tables_t_kernel(rb_ref, tz_ref, tc_ref):
    j = lax.broadcasted_iota(jnp.int32, (TQ, TQ), 0)
    i = lax.broadcasted_iota(jnp.int32, (TQ, TQ), 1)
    cc = lax.broadcasted_iota(jnp.int32, (T_NEAR, TQ), 0)
    ic = lax.broadcasted_iota(jnp.int32, (T_NEAR, TQ), 1)
    d0 = i - j
    d1 = d0 + TQ
    dc = ic + (TQ - CMP_BLOCK + 1) - CMP_STRIDE * cc
    for h in range(N_HEADS):
        last = rb_ref[N_BUCKETS - 1, h]
        tz_ref[0, h] = jnp.where(d0 < 0, NEG_INF, (_rel_bias(d0, rb_ref, h) - last) * LOG2_E)
        tz_ref[1, h] = (_rel_bias(d1, rb_ref, h) - last) * LOG2_E
        tc_ref[h] = jnp.where(dc < 0, NEG_INF, (_rel_bias(dc, rb_ref, h) - last) * LOG2_E)


def bias_tables_t(rel_bias):
    return pl.pallas_call(
        _bias_tables_t_kernel, name="bias_tables",
        in_specs=[pl.BlockSpec(memory_space=pltpu.SMEM)],
        out_specs=[pl.BlockSpec(memory_space=pltpu.VMEM), pl.BlockSpec(memory_space=pltpu.VMEM)],
        out_shape=[jax.ShapeDtypeStruct((2, N_HEADS, TQ, TQ), F32),
                   jax.ShapeDtypeStruct((N_HEADS, T_NEAR, TQ), F32)],
        compiler_params=pltpu.CompilerParams(vmem_limit_bytes=VMEM_LIMIT))(rel_bias)


def _prompt_attn_t_kernel(qt_ref, gate_ref, kc_ref, vct_ref, ovt_ref, tc_ref, tz_ref, ks_ref, vst_ref, kw_ref, vwt_ref,
                          o_ref, qa_ref, lc_ref, m_ref, acc_ref, ot_ref):
    qi = pl.program_id(2)
    nf = FEAT - HEAD_DIM
    hp = HEADS_PER_KV
    for h in range(hp):
        qa_ref[h, 0:HEAD_DIM, :] = qt_ref[h * HEAD_DIM:(h + 1) * HEAD_DIM, :]
        qa_ref[h, HEAD_DIM:FEAT, :] = jnp.zeros((nf, TQ), BF16)
    t_lane = qi * TQ + lax.broadcasted_iota(jnp.int32, (1, TQ), 1)

    near0 = pl.multiple_of(qi * T_PAD, T_PAD)
    rc = kc_ref.shape[0]
    row = lax.broadcasted_iota(jnp.int32, (rc, 1), 0)
    row_ok = (row >= T_PAD) & (row < near0 + T_NEAR)
    has_block = t_lane >= CMP_BLOCK - 1
    kc = kc_ref[...].astype(BF16)
    vct = vct_ref[...].astype(BF16)
    for h in range(hp):
        lc_ref[h] = _dot(kc, qa_ref[h])
    for h in range(hp):
        lc_ref[h, pl.ds(near0, T_NEAR), :] = lc_ref[h, pl.ds(near0, T_NEAR), :] + tc_ref[h]
    p_heads = jnp.zeros((rc, TQ), F32)
    for h in range(hp):
        lc = jnp.where(row_ok, lc_ref[h], NEG_INF)
        p = jnp.exp2(lc - jnp.max(lc, axis=0, keepdims=True))
        p = p * jnp.where(has_block, 1.0 / jnp.sum(p, axis=0, keepdims=True), 0.0)
        p_heads = p_heads + p
        ot_ref[h * HEAD_DIM:(h + 1) * HEAD_DIM, :] = gate_ref[h:h + 1, :] * _dot(vct, p.astype(BF16))
    p_slc = _dot(ovt_ref[...], p_heads, precision=lax.Precision.HIGHEST)

    blk = lax.broadcasted_iota(jnp.int32, (nf, TQ), 0)
    jt = t_lane // SEL_BLOCK
    forced = (blk == 0) | (blk == jt) | (blk == jt - 1)
    score = jnp.where(forced, jnp.inf, jnp.where(blk <= jt, p_slc, -jnp.inf))
    n_grp = nf // SUBLANE
    groups = [score[SUBLANE * g:SUBLANE * (g + 1), :] for g in range(n_grp)]
    ranks = [jnp.zeros((SUBLANE, TQ), jnp.int32) for _ in range(n_grp)]
    sub = lax.broadcasted_iota(jnp.int32, (SUBLANE, TQ), 0)
    for c in range(nf):
        cg, cs = divmod(c, SUBLANE)
        other = jnp.broadcast_to(groups[cg][cs:cs + 1, :], (SUBLANE, TQ))
        for g in range(n_grp):
            if g > cg:
                beats = (other >= groups[g]).astype(jnp.int32)
            elif g < cg:
                beats = (other > groups[g]).astype(jnp.int32)
            else:
                beats = jnp.where(sub > cs, (other >= groups[g]).astype(jnp.int32), (other > groups[g]).astype(jnp.int32))
            ranks[g] = ranks[g] + beats
    feat = jnp.concatenate(
        [jnp.where((ranks[g] < SEL_TOPK) & (groups[g] > NEG_INF), 0.0, NEG_INF) for g in range(n_grp)], axis=0)
    feat = feat.astype(BF16)
    for h in range(hp):
        qa_ref[h, HEAD_DIM:FEAT, :] = feat

    def reset():
        m_ref[...] = jnp.full(m_ref.shape, -3e38, F32)
        acc_ref[...] = jnp.zeros(acc_ref.shape, F32)

    def step(k_ref, vt_ref, j, bias):
        k = k_ref[pl.ds(pl.multiple_of(j * TQ, TQ), TQ), :]
        vt = vt_ref[j]
        m_old = [m_ref[h] for h in range(hp)]
        acc_old = [acc_ref[h] for h in range(hp)]
        s = [_dot(k, qa_ref[h]) for h in range(hp)]
        if bias is not None:
            s = [s[h] + bias(h) for h in range(hp)]
        m_new = [jnp.maximum(m_old[h], jnp.max(s[h], axis=0, keepdims=True)) for h in range(hp)]
        pv = [_dot(vt, jnp.exp2(s[h] - m_new[h]).astype(BF16)) for h in range(hp)]
        for h in range(hp):
            acc_ref[h] = jnp.exp2(m_old[h] - m_new[h]) * acc_old[h] + pv[h]
            m_ref[h] = m_new[h]

    def add_result(branch):
        for h in range(hp):
            acc = acc_ref[h]
            o = acc[0:HEAD_DIM, :] * (1.0 / acc[HEAD_DIM:HEAD_DIM + 1, :])
            rows = slice(h * HEAD_DIM, (h + 1) * HEAD_DIM)
            ot_ref[rows, :] = ot_ref[rows, :] + gate_ref[branch * hp + h:branch * hp + h + 1, :] * o

    diag = lambda h: tz_ref[0, h]
    prev = lambda h: tz_ref[1, h]

    reset()

    def far_step(j, carry):
        step(ks_ref, vst_ref, j, None)
        return carry

    lax.fori_loop(0, jnp.maximum(qi - 1, 0), far_step, 0)

    @pl.when(qi >= 1)
    def _():
        step(ks_ref, vst_ref, qi - 1, prev)

    step(ks_ref, vst_ref, qi, diag)
    add_result(1)

    reset()
    n_back = WINDOW // TQ
    edge = jnp.where(lax.broadcasted_iota(jnp.int32, (TQ, TQ), 0) >= lax.broadcasted_iota(jnp.int32, (TQ, TQ), 1),
                     0.0, NEG_INF)
    for back in range(n_back, -1, -1):
        bias = (lambda h: edge) if back == n_back else prev if back == 1 else diag if back == 0 else None

        @pl.when(qi >= back)
        def _(back=back, bias=bias):
            step(kw_ref, vwt_ref, qi - back, bias)

    add_result(2)
    o_ref[...] = ot_ref[...].T.astype(o_ref.dtype)


def prompt_attention_t(qt, gate_t, kc, vct, tc, tz, ks, vst, kw, vwt, b, t):
    n_chunk = t // CMP_STRIDE
    rc = kc.shape[2]
    nf = FEAT - HEAD_DIM
    nq = t // TQ
    assert t // SEL_BLOCK <= nf and t % TQ == 0 and rc >= T_PAD * (nq + 1) and rc >= T_PAD + n_chunk - 1
    ovt = jnp.asarray(_sel_overlap_rows(rc, n_chunk - 1, T_PAD).T)
    hp = HEADS_PER_KV
    gw = hp * HEAD_DIM
    k_spec = pl.BlockSpec((None, None, t, FEAT), lambda i, g, qi: (i, g, 0, 0))
    v_spec = pl.BlockSpec((None, None, nq, V_ROWS, TQ), lambda i, g, qi: (i, g, 0, 0, 0))
    return pl.pallas_call(
        _prompt_attn_t_kernel, grid=(b, N_KV_HEADS, nq), name="prompt_attention",
        in_specs=[pl.BlockSpec((gw, TQ), lambda i, g, qi: (g, i * nq + qi)),
                  pl.BlockSpec((4 * hp, TQ), lambda i, g, qi: (g, i * nq + qi)),
                  pl.BlockSpec((None, None, rc, FEAT), lambda i, g, qi: (i, g, 0, 0)),
                  pl.BlockSpec((None, None, HEAD_DIM, rc), lambda i, g, qi: (i, g, 0, 0)),
                  pl.BlockSpec((nf, rc), lambda i, g, qi: (0, 0)),
                  pl.BlockSpec((hp, T_NEAR, TQ), lambda i, g, qi: (g, 0, 0)),
                  pl.BlockSpec((2, hp, TQ, TQ), lambda i, g, qi: (0, g, 0, 0)),
                  k_spec, v_spec, k_spec, v_spec],
        out_specs=pl.BlockSpec((TQ, gw), lambda i, g, qi: (i * nq + qi, g)),
        out_shape=jax.ShapeDtypeStruct((b * t, ATTN_WIDTH), BF16),
        scratch_shapes=[pltpu.VMEM((hp, FEAT, TQ), BF16), pltpu.VMEM((hp, rc, TQ), F32),
                        pltpu.VMEM((hp, 1, TQ), F32), pltpu.VMEM((hp, V_ROWS, TQ), F32), pltpu.VMEM((gw, TQ), F32)],
        compiler_params=_params("parallel", "parallel", "arbitrary"))(
            qt, gate_t, kc, vct, ovt, tc, tz, ks, vst, kw, vwt)


def _gmlp_kernel(u_ref, v_ref, lng_ref, lnb_ref, ws_ref, bs_ref, o_ref):
    v = v_ref[...].astype(F32)
    mu = jnp.mean(v, axis=-1, keepdims=True)
    var = jnp.mean(jnp.square(v - mu), axis=-1, keepdims=True)
    vn = ((v - mu) * lax.rsqrt(var + LN_EPS) * lng_ref[...] + lnb_ref[...]).astype(BF16)
    tri = (lax.broadcasted_iota(jnp.int32, (CHUNK, CHUNK), 0) >= lax.broadcasted_iota(jnp.int32, (CHUNK, CHUNK), 1))
    for g in range(GM_GROUPS):
        w = jnp.where(tri, ws_ref[g], 0.0).astype(BF16)
        cols = slice(g * GM_GROUP_DIM, (g + 1) * GM_GROUP_DIM)
        for c in range(u_ref.shape[0] // CHUNK):
            rows = slice(c * CHUNK, (c + 1) * CHUNK)
            s = _dot(w, vn[rows, cols]) + bs_ref[:, g:g + 1]
            o_ref[rows, cols] = (u_ref[rows, cols].astype(F32) * s).astype(o_ref.dtype)


def gmlp_chunks(uv, ln_g, ln_b, w_s, b_s, rows):
    m = uv.shape[0]
    return pl.pallas_call(
        _gmlp_kernel, grid=(m // rows,), name="gmlp",
        in_specs=[pl.BlockSpec((rows, GM_WIDTH), lambda i: (i, 0)), pl.BlockSpec((rows, GM_WIDTH), lambda i: (i, 1)),
                  pl.BlockSpec((1, GM_WIDTH), lambda i: (0, 0)), pl.BlockSpec((1, GM_WIDTH), lambda i: (0, 0)),
                  pl.BlockSpec((GM_GROUPS, CHUNK, CHUNK), lambda i: (0, 0, 0)),
                  pl.BlockSpec((CHUNK, GM_GROUPS), lambda i: (0, 0))],
        out_specs=pl.BlockSpec((rows, GM_WIDTH), lambda i: (i, 0)),
        out_shape=jax.ShapeDtypeStruct((m, GM_WIDTH), BF16),
        compiler_params=_params("parallel"))(uv, uv, ln_g.reshape(1, -1), ln_b.reshape(1, -1), w_s, b_s.T)


def _merge_kernel(attn_ref, gm_ref, mg_ref, x_ref, wa_ref, wb_ref, wo_ref, gn_ref, wr_ref, br_ref,
                  h_ref, hn_ref, idx_ref, wt_ref):
    tm = x_ref.shape[0]
    sub = min(MERGE_SUB, tm)
    parts = [slice(r0, r0 + sub) for r0 in range(0, tm, sub)]
    ta = [_dot(attn_ref[p, :], wa_ref[...]) for p in parts]
    tb = [_dot(gm_ref[p, :], wb_ref[...]) for p in parts]
    merged = [(mg_ref[p, 0:D_MODEL].astype(F32) * ta[i] + mg_ref[p, D_MODEL:2 * D_MODEL].astype(F32) * tb[i])
              .astype(BF16) for i, p in enumerate(parts)]
    h = [x_ref[p, :] + _dot(merged[i], wo_ref[...]) for i, p in enumerate(parts)]
    hn = [hh * lax.rsqrt(jnp.mean(hh * hh, axis=-1, keepdims=True) + RMS_EPS) * gn_ref[...] for hh in h]
    hi = [v.astype(BF16) for v in hn]
    lo = [(v - hi[i].astype(F32)).astype(BF16) for i, v in enumerate(hn)]
    logits = [_dot(hi[i], wr_ref[0]) + _dot(lo[i], wr_ref[0]) + _dot(hi[i], wr_ref[1]) + br_ref[...]
              for i in range(len(parts))]
    routed = [_route(lg) for lg in logits]
    for i, p in enumerate(parts):
        h_ref[p, :] = h[i]
        hn_ref[p, :] = hi[i]
        idx_ref[p, :] = routed[i][0]
        wt_ref[p, :] = routed[i][1]


def _route(logits):
    lane = lax.broadcasted_iota(jnp.int32, logits.shape, 1)
    vals, idxs = [], []
    for _ in range(TOP_K):
        best = jnp.max(logits, axis=-1, keepdims=True)
        ix = jnp.min(jnp.where(logits == best, lane, LANE), axis=-1, keepdims=True)
        vals.append(best)
        idxs.append(ix)
        logits = jnp.where(lane == ix, -jnp.inf, logits)
    ex = [jnp.exp(v - vals[0]) for v in vals]
    inv = 1.0 / functools.reduce(lambda a, b: a + b, ex)
    idx_out = jnp.zeros(lane.shape, jnp.int32)
    wt_out = jnp.zeros(lane.shape, F32)
    for k in range(TOP_K):
        idx_out = jnp.where(lane == k, idxs[k], idx_out)
        wt_out = jnp.where(lane == k, ex[k] * inv, wt_out)
    return idx_out, wt_out


MERGE_SUB = 128


def merge_and_route(attn, gm, mg, x, wa, wb, wo, g_ffn, w_router, b_router, tm):
    m = x.shape[0]
    tm = min(tm, m)
    wr = jnp.zeros((D_MODEL, LANE), F32).at[:, :N_EXPERTS].set(w_router)
    wr_hi = wr.astype(BF16)
    wr = jnp.stack([wr_hi, (wr - wr_hi.astype(F32)).astype(BF16)])
    br = jnp.full((1, LANE), -jnp.inf, F32).at[0, :N_EXPERTS].set(b_router)
    row = lambda w: pl.BlockSpec((tm, w), lambda i: (i, 0))
    full = lambda a: pl.BlockSpec(a.shape, lambda i: (0,) * a.ndim, pipeline_mode=pl.Buffered(1))
    return pl.pallas_call(
        _merge_kernel, grid=(m // tm,), name="merge_route",
        in_specs=[row(ATTN_WIDTH), row(GM_WIDTH), row(2 * D_MODEL), row(D_MODEL), full(wa), full(wb), full(wo),
                  pl.BlockSpec((1, D_MODEL), lambda i: (0, 0)), full(wr), full(br)],
        out_specs=[row(D_MODEL), row(D_MODEL), row(LANE), row(LANE)],
        out_shape=[jax.ShapeDtypeStruct((m, D_MODEL), F32), jax.ShapeDtypeStruct((m, D_MODEL), BF16),
                   jax.ShapeDtypeStruct((m, LANE), jnp.int32), jax.ShapeDtypeStruct((m, LANE), F32)],
        compiler_params=_params("parallel"))(attn, gm, mg, x, wa, wb, wo, g_ffn.reshape(1, -1), wr, br)


def _expert_tile(te_ref, first_ref, nact_ref, nxt_ref, valid_ref, w_hbm, stage_ref, cache_ref, sem, o_ref, compute):
    j, r = pl.program_id(0), pl.program_id(1)
    tn = stage_ref.shape[-1]
    rows = stage_ref.shape[1]

    def copies(e, jj):
        cols = pl.ds(pl.multiple_of(jj * tn, tn), tn)
        band = rows // MOE_DMA_SPLIT
        return [pltpu.make_async_copy(w.at[e, c * band:(c + 1) * band, cols],
                                      stage_ref.at[i, c * band:(c + 1) * band, :], sem.at[i, c])
                for i, w in enumerate(w_hbm) for c in range(MOE_DMA_SPLIT)]

    @pl.when((j == 0) & (r == 0))
    def _():
        for c in copies(te_ref[0], 0):
            c.start()

    @pl.when(first_ref[r] == 1)
    def _():
        for c in copies(te_ref[r], j):
            c.wait()
        for i in range(len(w_hbm)):
            for r0 in range(0, rows, 256):
                cache_ref[i, r0:r0 + 256, :] = stage_ref[i, r0:r0 + 256, :].astype(cache_ref.dtype)
        wraps = nxt_ref[r] < 0
        e_next = jnp.where(wraps, te_ref[0], nxt_ref[r])
        j_next = jnp.where(wraps, j + 1, j)

        @pl.when(j_next < pl.num_programs(0))
        def _():
            for c in copies(e_next, j_next):
                c.start()

    tm = o_ref.shape[0]
    half = tm // 2

    @pl.when((r < nact_ref[0]) & (valid_ref[r] > half))
    def _():
        compute(tm)

    @pl.when((r < nact_ref[0]) & (valid_ref[r] <= half))
    def _():
        compute(half)
        o_ref[half:tm, :] = jnp.zeros((tm - half, o_ref.shape[1]), o_ref.dtype)

    @pl.when(r >= nact_ref[0])
    def _():
        o_ref[...] = jnp.zeros(o_ref.shape, o_ref.dtype)


def _moe_up_kernel(te_ref, first_ref, nact_ref, nxt_ref, valid_ref, x_ref, wg_hbm, wu_hbm, bg_ref, bu_ref, o_ref,
                   stage_ref, cache_ref, sem):
    def compute(m):
        x = x_ref[0:m, :]
        g = jnp.minimum(_dot(x, cache_ref[0]) + bg_ref[...], SWIGLU_LIMIT)
        u = jnp.clip(_dot(x, cache_ref[1]) + bu_ref[...], -SWIGLU_LIMIT, SWIGLU_LIMIT)
        o_ref[0:m, :] = ((u + 1.0) * (g * jax.nn.sigmoid(SWIGLU_ALPHA * g))).astype(o_ref.dtype)

    _expert_tile(te_ref, first_ref, nact_ref, nxt_ref, valid_ref, (wg_hbm, wu_hbm), stage_ref, cache_ref, sem, o_ref,
                 compute)


def _moe_down_kernel(te_ref, first_ref, nact_ref, nxt_ref, valid_ref, a_ref, wd_hbm, bd_ref, o_ref,
                     stage_ref, cache_ref, sem):
    def compute(m):
        o_ref[0:m, :] = (_dot(a_ref[0:m, :], cache_ref[0]) + bd_ref[...]).astype(o_ref.dtype)

    _expert_tile(te_ref, first_ref, nact_ref, nxt_ref, valid_ref, (wd_hbm,), stage_ref, cache_ref, sem, o_ref, compute)


def moe_experts(x_rows, tiles, w_gate, b_gate, w_up, b_up, w_down, b_down):
    rows = x_rows.shape[0]
    n_tiles = rows // MOE_TM
    row_map = lambda j, r, te, first, nact, nxt, valid: (jnp.minimum(r, nact[0] - 1), 0)
    out_map = lambda j, r, te, first, nact, nxt, valid: (r, j)
    b_spec = pl.BlockSpec((None, 1, MOE_TN), lambda j, r, te, first, nact, nxt, valid: (te[r], 0, j))
    hbm = pl.BlockSpec(memory_space=pl.ANY)

    def scratch(n_w, k):
        return [pltpu.VMEM((n_w, k, MOE_TN), F32), pltpu.VMEM((n_w, k, MOE_TN), BF16),
                pltpu.SemaphoreType.DMA((n_w, MOE_DMA_SPLIT))]

    act = pl.pallas_call(
        _moe_up_kernel, name="moe_up",
        grid_spec=pltpu.PrefetchScalarGridSpec(
            num_scalar_prefetch=5, grid=(D_FF // MOE_TN, n_tiles),
            in_specs=[pl.BlockSpec((MOE_TM, D_MODEL), row_map), hbm, hbm, b_spec, b_spec],
            out_specs=pl.BlockSpec((MOE_TM, MOE_TN), out_map),
            scratch_shapes=scratch(2, D_MODEL)),
        out_shape=jax.ShapeDtypeStruct((rows, D_FF), BF16),
        compiler_params=_params("arbitrary", "arbitrary"))(
            *tiles, x_rows, w_gate, w_up,
            b_gate.reshape(N_EXPERTS, 1, D_FF), b_up.reshape(N_EXPERTS, 1, D_FF))
    return pl.pallas_call(
        _moe_down_kernel, name="moe_down",
        grid_spec=pltpu.PrefetchScalarGridSpec(
            num_scalar_prefetch=5, grid=(D_MODEL // MOE_TN, n_tiles),
            in_specs=[pl.BlockSpec((MOE_TM, D_FF), row_map), hbm, b_spec],
            out_specs=pl.BlockSpec((MOE_TM, MOE_TN), out_map),
            scratch_shapes=scratch(1, D_FF)),
        out_shape=jax.ShapeDtypeStruct((rows, D_MODEL), BF16),
        compiler_params=_params("arbitrary", "arbitrary"))(
            *tiles, act, w_down, b_down.reshape(N_EXPERTS, 1, D_MODEL))


def route_rows(top_i, n_rows):
    n = top_i.shape[0]
    hit = top_i[:, :, None] == jnp.arange(N_EXPERTS, dtype=jnp.int32)[None, None, :]
    onehot = hit.astype(jnp.int32).sum(axis=1)
    before = jnp.cumsum(onehot, axis=0) - onehot
    counts = onehot.sum(axis=0)
    padded = (counts + MOE_TM - 1) // MOE_TM * MOE_TM
    ends = jnp.cumsum(padded)
    starts = ends - padded
    pos = jnp.sum(jnp.where(hit, (starts[None, :] + before)[:, None, :], 0), axis=-1)
    pair_id = jnp.arange(TOP_K, dtype=jnp.int32)[None, :] * n + jnp.arange(n, dtype=jnp.int32)[:, None]
    row_pair = jnp.arange(n_rows, dtype=jnp.int32).at[pos.reshape(-1)].set(pair_id.reshape(-1), unique_indices=True)
    n_tiles = n_rows // MOE_TM
    n_active = (ends[-1] // MOE_TM).astype(jnp.int32)
    tile_row = jnp.minimum(jnp.arange(n_tiles, dtype=jnp.int32), n_active - 1) * MOE_TM
    tile_expert = jnp.minimum((tile_row[:, None] >= ends[None, :]).astype(jnp.int32).sum(axis=-1), N_EXPERTS - 1)
    tile_first = jnp.concatenate([jnp.ones((1,), jnp.int32), (tile_expert[1:] != tile_expert[:-1]).astype(jnp.int32)])
    tile_id = jnp.arange(n_tiles, dtype=jnp.int32)
    start_id = jnp.where(tile_first == 1, tile_id, n_tiles)
    next_start = jnp.concatenate([lax.cummin(start_id[::-1])[::-1][1:], jnp.full((1,), n_tiles, jnp.int32)])
    tile_next = jnp.where(next_start < n_tiles, tile_expert[jnp.minimum(next_start, n_tiles - 1)], -1)
    tile_valid = jnp.clip((starts + counts)[tile_expert] - tile_id * MOE_TM, 0, MOE_TM)
    tiles = (tile_expert, tile_first, n_active.reshape(1), tile_next.astype(jnp.int32), tile_valid.astype(jnp.int32))
    return pos, row_pair, tiles


def _combine_kernel(h_ref, y_ref, w_ref, g_ref, o_ref):
    h = h_ref[...]
    w = w_ref[...]
    for k in range(TOP_K):
        h = h + w[:, k:k + 1] * y_ref[k].astype(F32)
    o_ref[...] = h * lax.rsqrt(jnp.mean(h * h, axis=-1, keepdims=True) + RMS_EPS) * g_ref[...]


def combine_and_norm(h, y4, wts, g, tm):
    m = h.shape[0]
    tm = min(tm, m)
    return pl.pallas_call(
        _combine_kernel, grid=(m // tm,), name="combine_norm",
        in_specs=[pl.BlockSpec((tm, D_MODEL), lambda i: (i, 0)), pl.BlockSpec((TOP_K, tm, D_MODEL), lambda i: (0, i, 0)),
                  pl.BlockSpec((tm, LANE), lambda i: (i, 0)), pl.BlockSpec((1, D_MODEL), lambda i: (0, 0))],
        out_specs=pl.BlockSpec((tm, D_MODEL), lambda i: (i, 0)),
        out_shape=jax.ShapeDtypeStruct((m, D_MODEL), F32), compiler_params=_params("parallel"))(
            h, y4, wts, g.reshape(1, -1))


def _in_weights(w_in):
    c0 = ATTN_WIDTH
    c1 = c0 + 3 * KV_COLS
    c2 = c1 + GATE_COLS
    c3 = c2 + 2 * GM_WIDTH
    wg = w_in[:, c1:c2].reshape(D_MODEL, N_KV_HEADS, HEADS_PER_KV, 3).transpose(0, 1, 3, 2)
    wg = jnp.pad(wg.reshape(D_MODEL, N_KV_HEADS, 3 * HEADS_PER_KV), ((0, 0), (0, 0), (0, LANE - 3 * HEADS_PER_KV)))
    cast = lambda w: w.astype(BF16)
    return (cast(w_in[:, :c0]), cast(w_in[:, c0:c1]), cast(wg.reshape(D_MODEL, N_KV_HEADS * LANE)),
            cast(w_in[:, c2:c3]), cast(w_in[:, c3:]))


def _project(a, weights, wide_dtype, q_scale):
    w_q, w_kv, w_ng, w_uv, w_mg = weights
    q = matmul(a, w_q, lambda z: z * np.float32(q_scale), wide_dtype, 1024, 1024, "proj_q")
    kv = matmul(a, w_kv, lambda z: z, F32, 1024, 3 * KV_COLS, "proj_kv")
    ng = matmul(a, w_ng, jax.nn.sigmoid, F32, 1024, N_KV_HEADS * LANE, "proj_gate")
    uv = matmul(a, w_uv, _gelu, wide_dtype, 1024, 1024, "proj_uv")
    mg = matmul(a, w_mg, jax.nn.sigmoid, BF16, 1024, 1024, "proj_merge")
    return q, kv, ng, uv, mg


def _head_rows(fn):
    out = fn(0)
    out = jnp.broadcast_to(out, (HEADS_PER_KV, out.shape[1]))
    row = lax.broadcasted_iota(jnp.int32, out.shape, 0)
    for h in range(1, HEADS_PER_KV):
        out = jnp.where(row == h, fn(h), out)
    return out


def _sample_cmp_win_kernel(rb_ref, q_ref, kvc_ref, ov_ref, sw_ref, new_ref, gate_ref, o_ref, idx_ref, *, past, n_cmp):
    n_rows = kvc_ref.shape[2]
    n_blk = ov_ref.shape[1]
    win = sw_ref.shape[-1]
    win_col = 2 * KV_COLS
    lane_blk = lax.broadcasted_iota(jnp.int32, (1, n_blk), 1)
    lane_out = lax.broadcasted_iota(jnp.int32, (1, LANE), 1)
    for g in range(N_KV_HEADS):
        heads = slice(g * HEADS_PER_KV, (g + 1) * HEADS_PER_KV)
        q = q_ref[heads, :]
        qb = q.astype(BF16)

        n = lax.broadcasted_iota(jnp.int32, (1, n_rows), 1)
        d = past - (n * CMP_STRIDE + CMP_BLOCK - 1)
        logit = _dot_nt(qb, kvc_ref[0, g].astype(BF16)) + _head_rows(
            lambda h: _rel_bias(d, rb_ref, g * HEADS_PER_KV + h))
        logit = jnp.where(n < n_cmp, logit, NEG_INF)
        p = jnp.exp(logit - jnp.max(logit, axis=-1, keepdims=True))
        p = p * (1.0 / jnp.sum(p, axis=-1, keepdims=True))
        o_cmp = _dot(p.astype(BF16), kvc_ref[1, g].astype(BF16))
        p_heads = jnp.broadcast_to(jnp.sum(p, axis=0, keepdims=True), (SUBLANE, n_rows))
        p_slc = _dot(p_heads, ov_ref[...], precision=lax.Precision.HIGHEST)[0:1, :]

        score = jnp.where((lane_blk == 0) | (lane_blk == n_blk - 1), jnp.inf, p_slc)
        picked = jnp.zeros((1, LANE), jnp.int32)
        for k in range(SEL_TOPK - 1):
            best = jnp.max(score, axis=-1, keepdims=True)
            ix = jnp.min(jnp.where(score == best, lane_blk, n_blk), axis=-1, keepdims=True)
            picked = jnp.where(lane_out == k, ix, picked)
            score = jnp.where(lane_blk == ix, -jnp.inf, score)
        idx_ref[g:g + 1, :] = picked

        dw = win - lax.broadcasted_iota(jnp.int32, (1, win), 1)
        lw = _dot(qb, sw_ref[0, g].astype(BF16)) + _head_rows(
            lambda h: _rel_bias(dw, rb_ref, g * HEADS_PER_KV + h))
        k_new = new_ref[:, win_col + g * HEAD_DIM:win_col + (g + 1) * HEAD_DIM]
        v_new = new_ref[:, win_col + KV_COLS // 2 + g * HEAD_DIM:win_col + KV_COLS // 2 + (g + 1) * HEAD_DIM]
        l_new = jnp.sum(q * k_new, axis=-1, keepdims=True) + _head_rows(
            lambda h: jnp.full((1, 1), rb_ref[0, g * HEADS_PER_KV + h], F32))
        m = jnp.maximum(jnp.max(lw, axis=-1, keepdims=True), l_new)
        pw = jnp.exp(lw - m)
        p_new = jnp.exp(l_new - m)
        o_win = ((_dot_nt(pw.astype(BF16), sw_ref[1, g].astype(BF16)) + p_new * v_new)
                 * (1.0 / (jnp.sum(pw, axis=-1, keepdims=True) + p_new)))
        o_ref[heads, :] = gate_ref[0, heads, :] * o_cmp + gate_ref[2, heads, :] * o_win


def sample_cmp_win(rel_bias, q, kvc, state_win, kv_new, gate, past):
    b, _, _, n_rows, _ = kvc.shape
    n_cmp = past // CMP_STRIDE - CMP_BLOCK // CMP_STRIDE + 1
    n_blk = past // SEL_BLOCK
    assert past % SEL_BLOCK == 0 and n_blk >= SEL_TOPK and n_cmp <= n_rows
    cs = np.arange(n_rows)[:, None] * CMP_STRIDE
    ss = np.arange(n_blk)[None, :] * SEL_BLOCK
    ov = np.maximum(np.minimum(cs + CMP_BLOCK, ss + SEL_BLOCK) - np.maximum(cs, ss), 0) / CMP_STRIDE
    ov = jnp.asarray((ov * (np.arange(n_rows) < n_cmp)[:, None]).astype(np.float32))
    win = state_win.shape[-1]
    return pl.pallas_call(
        functools.partial(_sample_cmp_win_kernel, past=past, n_cmp=n_cmp), grid=(b,), name="sample_cmp_win",
        in_specs=[pl.BlockSpec(memory_space=pltpu.SMEM),
                  pl.BlockSpec((None, N_HEADS, HEAD_DIM), lambda i: (i, 0, 0)),
                  pl.BlockSpec((None, 2, N_KV_HEADS, n_rows, HEAD_DIM), lambda i: (i, 0, 0, 0, 0)),
                  pl.BlockSpec(ov.shape, lambda i: (0, 0)),
                  pl.BlockSpec((None, 2, N_KV_HEADS, HEAD_DIM, win), lambda i: (i, 0, 0, 0, 0)),
                  pl.BlockSpec((None, 1, 3 * KV_COLS), lambda i: (i, 0, 0)),
                  pl.BlockSpec((None, 3, N_HEADS, 1), lambda i: (i, 0, 0, 0))],
        out_specs=[pl.BlockSpec((None, N_HEADS, HEAD_DIM), lambda i: (i, 0, 0)),
                   pl.BlockSpec((None, N_KV_HEADS, LANE), lambda i: (i, 0, 0))],
        out_shape=[jax.ShapeDtypeStruct((b, N_HEADS, HEAD_DIM), F32),
                   jax.ShapeDtypeStruct((b, N_KV_HEADS, LANE), jnp.int32)],
        compiler_params=_params("parallel"))(rel_bias, q, kvc, ov, state_win, kv_new, gate)


def _sample_slc_kernel(pt_ref, idx_ref, rb_ref, q_ref, new_ref, gate_ref, part_ref, *rest, past):
    blocks, o_ref = rest[:-1], rest[-1]
    b, g = pl.program_id(0), pl.program_id(1)
    q = q_ref[...]
    qb = q.astype(BF16)
    slc_col = KV_COLS
    half = KV_COLS // 2
    page = blocks[0].shape[-1]
    pos = lax.broadcasted_iota(jnp.int32, (1, page), 1)
    logits, values = [], []
    for k, blk_ref in enumerate(blocks):
        ix = idx_ref[(b * N_KV_HEADS + g) * SEL_TOPK + k]
        d = past - ((ix // SLC_PER_PAGE) * page + pos)
        lg = _dot(qb, blk_ref[0].astype(BF16)) + _head_rows(lambda h: _rel_bias(d, rb_ref, g * HEADS_PER_KV + h))
        logits.append(jnp.where(pos // SEL_BLOCK == ix % SLC_PER_PAGE, lg, NEG_INF))
        values.append(blk_ref[1].astype(BF16))
    new = new_ref[...]
    k_new = jnp.where(g == 0, new[:, slc_col:slc_col + HEAD_DIM], new[:, slc_col + HEAD_DIM:slc_col + 2 * HEAD_DIM])
    v_new = jnp.where(g == 0, new[:, slc_col + half:slc_col + half + HEAD_DIM],
                      new[:, slc_col + half + HEAD_DIM:slc_col + half + 2 * HEAD_DIM])
    l_new = jnp.sum(q * k_new, axis=-1, keepdims=True) + _head_rows(
        lambda h: jnp.full((1, 1), rb_ref[0, g * HEADS_PER_KV + h], F32))
    m = l_new
    for lg in logits:
        m = jnp.maximum(m, jnp.max(lg, axis=-1, keepdims=True))
    p_new = jnp.exp(l_new - m)
    denom = p_new
    out = p_new * v_new
    for lg, vv in zip(logits, values):
        p = jnp.exp(lg - m)
        denom = denom + jnp.sum(p, axis=-1, keepdims=True)
        out = out + _dot_nt(p.astype(BF16), vv)
    o_ref[...] = part_ref[...] + gate_ref[1] * (out * (1.0 / denom))


def sample_slc(page_table, picked, rel_bias, q, kv_new, gate, part, cache_slc, past):
    b = q.shape[0]
    page = cache_slc.shape[-1]
    n_pick = SEL_TOPK - 1

    def blk_map(k):
        def index(i, g, pt, idx, k=k):
            ix = idx[(i * N_KV_HEADS + g) * SEL_TOPK + k]
            return (pt[i, ix // SLC_PER_PAGE], 0, g, 0, 0)
        return index

    head_spec = pl.BlockSpec((None, HEADS_PER_KV, HEAD_DIM), lambda i, g, pt, idx: (i, g, 0))
    return pl.pallas_call(
        functools.partial(_sample_slc_kernel, past=past), name="sample_slc",
        grid_spec=pltpu.PrefetchScalarGridSpec(
            num_scalar_prefetch=2, grid=(b, N_KV_HEADS),
            in_specs=[pl.BlockSpec(memory_space=pltpu.SMEM), head_spec,
                      pl.BlockSpec((None, 1, 3 * KV_COLS), lambda i, g, pt, idx: (i, 0, 0)),
                      pl.BlockSpec((None, 3, HEADS_PER_KV, 1), lambda i, g, pt, idx: (i, 0, g, 0)), head_spec]
            + [pl.BlockSpec((None, 2, None, HEAD_DIM, page), blk_map(k)) for k in range(n_pick)],
            out_specs=head_spec),
        out_shape=jax.ShapeDtypeStruct((b, N_HEADS, HEAD_DIM), F32),
        compiler_params=_params("parallel", "parallel"))(
            page_table, picked, rel_bias, q, kv_new, gate, part, *([cache_slc] * n_pick))


def _gmlp_row_kernel(uv_ref, lng_ref, lnb_ref, w0_ref, b0_ref, vn_ref, o_ref):
    v = uv_ref[:, GM_WIDTH:2 * GM_WIDTH]
    mu = jnp.mean(v, axis=-1, keepdims=True)
    var = jnp.mean(jnp.square(v - mu), axis=-1, keepdims=True)
    vn = (v - mu) * lax.rsqrt(var + LN_EPS) * lng_ref[...] + lnb_ref[...]
    vn_ref[...] = vn
    o_ref[...] = (uv_ref[:, 0:GM_WIDTH] * (w0_ref[...] * vn + b0_ref[...])).astype(o_ref.dtype)


def gmlp_first_row(uv, ln_g, ln_b, w_s, b_s):
    m = uv.shape[0]
    w0 = jnp.repeat(w_s[:, 0, 0], GM_GROUP_DIM).reshape(1, GM_WIDTH)
    b0 = jnp.repeat(b_s[:, 0], GM_GROUP_DIM).reshape(1, GM_WIDTH)
    return pl.pallas_call(
        _gmlp_row_kernel, name="gmlp_row",
        out_shape=[jax.ShapeDtypeStruct((m, GM_WIDTH), F32), jax.ShapeDtypeStruct((m, GM_WIDTH), BF16)])(
            uv, ln_g.reshape(1, -1), ln_b.reshape(1, -1), w0, b0)


def _group_major(x, feat):
    b, t = x.shape[0], x.shape[1]
    f = jnp.broadcast_to(feat.astype(BF16)[None, None], (b, N_KV_HEADS, t, FEAT - HEAD_DIM))
    return jnp.concatenate([x.transpose(0, 2, 1, 3).astype(BF16), f], axis=-1)


def _value_tiles(x):
    b, t = x.shape[0], x.shape[1]
    xt = x.transpose(0, 2, 3, 1).astype(BF16)
    extra = jnp.zeros((b, N_KV_HEADS, V_ROWS - HEAD_DIM, t), BF16).at[:, :, 0].set(1)
    tiles = jnp.concatenate([xt, extra], axis=2).reshape(b, N_KV_HEADS, V_ROWS, t // TQ, TQ)
    return tiles.transpose(0, 1, 3, 2, 4)


def _prompt_mixer_a(q, kv, ng, rel_bias, pe, w1, w2, b, t):
    kvr = kv.reshape(b, t, 3, 2, N_KV_HEADS, HEAD_DIM)
    n_chunk = t // CMP_STRIDE
    chunks = kvr[:, :, 0].reshape(b, n_chunk, CMP_STRIDE, 2, N_KV_HEADS, HEAD_DIM).transpose(0, 3, 4, 1, 2, 5)
    kvc = compress_chunks(chunks.reshape(b, 2, N_KV_HEADS, n_chunk, CMP_STRIDE * HEAD_DIM).astype(BF16), w1, w2, pe)
    rc = -(-(T_PAD + max(n_chunk - 1, T_PAD * t // TQ)) // LANE) * LANE
    kvc = jnp.pad(kvc[:, :, :, :n_chunk - 1], ((0, 0), (0, 0), (0, 0), (T_PAD, rc - T_PAD - n_chunk + 1), (0, 0)))
    kc = jnp.pad(kvc[:, 0], ((0, 0), (0, 0), (0, 0), (0, FEAT - HEAD_DIM)))
    nf = FEAT - HEAD_DIM
    blocks = (jnp.arange(t)[:, None] // SEL_BLOCK == jnp.arange(nf)[None, :])
    tz, tc = bias_tables_t(rel_bias)
    gate_t = ng.reshape(b * t, N_KV_HEADS, LANE)[:, :, :4 * HEADS_PER_KV].reshape(b * t, -1).T
    return prompt_attention_t(
        q.T, gate_t, kc, kvc[:, 1].transpose(0, 1, 3, 2), tc, tz,
        _group_major(kvr[:, :, 1, 0], blocks), _value_tiles(kvr[:, :, 1, 1]),
        _group_major(kvr[:, :, 2, 0], jnp.zeros((t, nf), BF16)), _value_tiles(kvr[:, :, 2, 1]), b, t)


def _sample_mixer_a(q, kv, ng, rel_bias, pe, w1, w2, cache_cmp, cache_slc, state_win, page_table):
    b, n_pages = page_table.shape
    page = cache_cmp.shape[1]
    past = n_pages * page
    n_chunk = past // CMP_STRIDE
    rows = cache_cmp[page_table].reshape(b, n_chunk, CMP_STRIDE, 2, N_KV_HEADS, HEAD_DIM).transpose(0, 3, 4, 1, 2, 5)
    kvc = compress_chunks(rows.reshape(b, 2, N_KV_HEADS, n_chunk, CMP_STRIDE * HEAD_DIM).astype(BF16), w1, w2, pe)
    gate = ng.reshape(b, N_KV_HEADS, LANE)[:, :, :3 * HEADS_PER_KV].reshape(b, N_KV_HEADS, 3, HEADS_PER_KV)
    gate = gate.transpose(0, 2, 1, 3).reshape(b, 3, N_HEADS, 1)
    q3 = q.reshape(b, N_HEADS, HEAD_DIM)
    kv_new = kv.reshape(b, 1, 3 * KV_COLS)
    part, picked = sample_cmp_win(rel_bias, q3, kvc, state_win.transpose(0, 2, 3, 4, 1), kv_new, gate, past)
    picked = picked[:, :, :SEL_TOPK].reshape(-1)
    out = sample_slc(page_table, picked, rel_bias, q3, kv_new, gate, part, cache_slc.transpose(0, 2, 3, 4, 1), past)
    return out.reshape(b, ATTN_WIDTH).astype(BF16)


def kernel(x_prompt, x_sample, cache_cmp_kv, cache_slc_kv, state_win_kv, page_table, rel_bias, norm_mix, w_in, pe_cmp, w_cmp1, w_cmp2, gm_ln_g, gm_ln_b, gm_w_s, gm_b_s, w_br_attn, w_br_gmlp, w_out, norm_ffn, w_router, b_router, w_gate, b_gate, w_up, b_up, w_down, b_down, norm_final):
    b, t, _ = x_prompt.shape
    bs, ts, _ = x_sample.shape
    depth = norm_mix.shape[0]
    assert depth == 1 and ts == 1 and cache_slc_kv.shape[2] == SLC_PER_PAGE * SEL_BLOCK
    n_p, n_s = b * t, bs * ts
    kv_tail = (2, N_KV_HEADS, HEAD_DIM)
    lyr = 0

    xp = x_prompt.reshape(n_p, D_MODEL)
    xs = x_sample.reshape(n_s, D_MODEL)
    weights = _in_weights(w_in[lyr])
    qp, kvp, ngp, uvp, mgp = _project(rmsnorm_rows(xp, norm_mix[lyr], 512), weights, BF16, HEAD_DIM ** -0.5 * LOG2_E)
    qs, kvs, ngs, uvs, mgs = _project(rmsnorm_rows(xs, norm_mix[lyr], n_s), weights, F32, HEAD_DIM ** -0.5)

    attn_p = _prompt_mixer_a(qp, kvp, ngp, rel_bias, pe_cmp[lyr], w_cmp1[lyr], w_cmp2[lyr], b, t)
    attn_s = _sample_mixer_a(qs, kvs, ngs, rel_bias, pe_cmp[lyr], w_cmp1[lyr], w_cmp2[lyr], cache_cmp_kv[lyr],
                             cache_slc_kv[lyr], state_win_kv[lyr], page_table)

    gm_p = gmlp_chunks(uvp, gm_ln_g[lyr], gm_ln_b[lyr], gm_w_s[lyr], gm_b_s[lyr], 4 * CHUNK)
    vn_s, gm_s = gmlp_first_row(uvs, gm_ln_g[lyr], gm_ln_b[lyr], gm_w_s[lyr], gm_b_s[lyr])

    wa, wb, wo = w_br_attn[lyr].astype(BF16), w_br_gmlp[lyr].astype(BF16), w_out[lyr].astype(BF16)
    hp, hnp, idxp, wtp = merge_and_route(attn_p, gm_p, mgp, xp, wa, wb, wo, norm_ffn[lyr], w_router[lyr], b_router[lyr], 256)
    hs, hns, idxs, wts = merge_and_route(attn_s, gm_s, mgs, xs, wa, wb, wo, norm_ffn[lyr], w_router[lyr], b_router[lyr], 256)

    n_all = n_p + n_s
    n_rows = (n_all * TOP_K + N_EXPERTS * (MOE_TM - 1) + MOE_TM - 1) // MOE_TM * MOE_TM
    top_i = jnp.concatenate([idxp[:, :TOP_K], idxs[:, :TOP_K]], axis=0)
    pos, row_pair, tiles = route_rows(top_i, n_rows)
    x_rows = jnp.concatenate([hnp, hns] * -(-n_rows // n_all), axis=0)[row_pair]
    y_rows = moe_experts(x_rows, tiles, w_gate[lyr], b_gate[lyr], w_up[lyr], b_up[lyr], w_down[lyr], b_down[lyr])
    y_p = combine_and_norm(hp, y_rows[pos[:n_p].T], wtp, norm_final, 256)
    y_s = combine_and_norm(hs, y_rows[pos[n_p:].T], wts, norm_final, 256)

    kvp5 = kvp.reshape(b, t, 3, *kv_tail)
    kvs5 = kvs.reshape(bs, ts, 3, *kv_tail)
    win_buf = state_win_kv.shape[2]
    win_s = jnp.concatenate([state_win_kv[lyr], kvs5[:, :, 2]], axis=1)[:, ts:]
    return (y_p.reshape(b, t, D_MODEL), y_s.reshape(bs, ts, D_MODEL),
            kvp5[:, :, 0][None], kvs5[:, :, 0][None], kvp5[:, :, 1][None], kvs5[:, :, 1][None],
            kvp5[:, t - min(WINDOW, t):, 2][None], win_s[:, -win_buf:][None],
            vn_s.reshape(1, bs, ts, GM_WIDTH))
```

```python
import functools
import math

import jax
import jax.numpy as jnp
import numpy as np
from jax import lax
from jax.experimental import pallas as pl
from jax.experimental.pallas import tpu as pltpu

D_MODEL = 2048
N_HEADS = 16
HEAD_DIM = 64
N_KV_HEADS = 2
HEADS_PER_KV = N_HEADS // N_KV_HEADS
ATTN_WIDTH = N_HEADS * HEAD_DIM
CMP_BLOCK = 32
CMP_STRIDE = 16
CMP_HID = 2 * HEAD_DIM
SEL_BLOCK = 64
SEL_TOPK = 16
WINDOW = 512
GM_WIDTH = D_MODEL // 2
GM_GROUPS = 8
GM_GROUP_DIM = GM_WIDTH // GM_GROUPS
CHUNK = 128
N_BUCKETS = 32
REL_MAX_DIST = 128
N_EXPERTS = 32
TOP_K = 4
D_FF = D_MODEL
SWIGLU_LIMIT = 7.0
SWIGLU_ALPHA = 1.702
RMS_EPS = 1e-5
LN_EPS = 1e-5
NEG_INF = -1e30
LOG2_E = math.log2(math.e)
KV_COLS = 2 * N_KV_HEADS * HEAD_DIM
GATE_COLS = 3 * N_HEADS
SLC_PER_PAGE = 2

LANE = 128
SUBLANE = 8
VMEM_LIMIT = 56 * 1024 * 1024

MOE_TM = 256
MOE_TN = 1024
MOE_DMA_SPLIT = 4
FEAT = 2 * HEAD_DIM

BF16 = jnp.bfloat16
F32 = jnp.float32


def _bucket_upper_bounds():
    d = np.arange(0, 4 * REL_MAX_DIST)
    exact = N_BUCKETS // 2
    out = []
    for dt in (np.float32, np.float64):
        far = exact + (np.log(np.maximum(d, exact).astype(dt) / dt(exact)) / dt(math.log(REL_MAX_DIST / exact))
                       * dt(N_BUCKETS - exact)).astype(np.int32)
        out.append(np.where(d < exact, d, np.minimum(far, N_BUCKETS - 1)))
    assert (out[0] == out[1]).all() and (np.diff(out[0]) >= 0).all()
    b = out[0]
    return [int(d[b <= k].max()) for k in range(N_BUCKETS - 1)]


BUCKET_HI = _bucket_upper_bounds()
FAR_DIST = BUCKET_HI[-1] + 1


def _rel_bias(d, rb_ref, h):
    acc = jnp.full(d.shape, rb_ref[N_BUCKETS - 1, h], F32)
    for k in reversed(range(N_BUCKETS - 1)):
        acc = jnp.where(d <= BUCKET_HI[k], rb_ref[k, h], acc)
    return acc


def _dot(a, b, **kw):
    return jnp.dot(a, b, preferred_element_type=F32, **kw)


def _dot_nt(a, b):
    return lax.dot_general(a, b, (((1,), (1,)), ((), ())), preferred_element_type=F32)


def _gelu(x):
    return 0.5 * x * (1.0 + lax.erf(x * np.float32(math.sqrt(0.5))))


def _params(*sem):
    return pltpu.CompilerParams(dimension_semantics=sem, vmem_limit_bytes=VMEM_LIMIT)


def _rmsnorm_kernel(x_ref, g_ref, o_ref):
    x = x_ref[...]
    ms = jnp.mean(x * x, axis=-1, keepdims=True)
    o_ref[...] = (x * lax.rsqrt(ms + RMS_EPS) * g_ref[...]).astype(o_ref.dtype)


def rmsnorm_rows(x, g, tm):
    m, d = x.shape
    return pl.pallas_call(
        _rmsnorm_kernel, grid=(m // tm,), name="rmsnorm",
        in_specs=[pl.BlockSpec((tm, d), lambda i: (i, 0)), pl.BlockSpec((1, d), lambda i: (0, 0))],
        out_specs=pl.BlockSpec((tm, d), lambda i: (i, 0)),
        out_shape=jax.ShapeDtypeStruct((m, d), BF16), compiler_params=_params("parallel"))(x, g.reshape(1, d))


def _mm_kernel(a_ref, w_ref, o_ref, *, epilogue):
    o_ref[...] = epilogue(_dot(a_ref[...], w_ref[...])).astype(o_ref.dtype)


def matmul(a, w, epilogue, out_dtype, tm, tn, name):
    m, k = a.shape
    n = w.shape[1]
    tm, tn = min(tm, m), min(tn, n)
    return pl.pallas_call(
        functools.partial(_mm_kernel, epilogue=epilogue), grid=(m // tm, n // tn), name=name,
        in_specs=[pl.BlockSpec((tm, k), lambda i, j: (i, 0)), pl.BlockSpec((k, tn), lambda i, j: (0, j))],
        out_specs=pl.BlockSpec((tm, tn), lambda i, j: (i, j)),
        out_shape=jax.ShapeDtypeStruct((m, n), out_dtype), compiler_params=_params("parallel", "parallel"))(a, w)


def _compress_kernel(c_ref, w1_ref, w2_ref, pe_ref, o_ref):
    n = c_ref.shape[0]
    half = CMP_STRIDE * HEAD_DIM
    c = c_ref[...]
    first = _dot(c, w1_ref[0:half, :])
    second = _dot(c, w1_ref[half:2 * half, :])
    pe = _dot(pe_ref[...].astype(BF16), w1_ref[...])[0:1, :]
    hid = _gelu(first + pltpu.roll(second, n - 1, 0) + pe)
    o_ref[...] = _dot(hid.astype(BF16), w2_ref[...])


def compress_chunks(chunks, w1, w2, pe):
    b, _, g, n, width = chunks.shape
    pe_rows = jnp.broadcast_to(pe.reshape(2, 1, CMP_BLOCK * HEAD_DIM), (2, SUBLANE, CMP_BLOCK * HEAD_DIM))
    return pl.pallas_call(
        _compress_kernel, grid=(b, 2, g), name="compress",
        in_specs=[pl.BlockSpec((None, None, None, n, width), lambda i, s, j: (i, s, j, 0, 0)),
                  pl.BlockSpec((None, CMP_BLOCK * HEAD_DIM, CMP_HID), lambda i, s, j: (s, 0, 0)),
                  pl.BlockSpec((None, CMP_HID, HEAD_DIM), lambda i, s, j: (s, 0, 0)),
                  pl.BlockSpec((None, SUBLANE, CMP_BLOCK * HEAD_DIM), lambda i, s, j: (s, 0, 0))],
        out_specs=pl.BlockSpec((None, None, None, n, HEAD_DIM), lambda i, s, j: (i, s, j, 0, 0)),
        out_shape=jax.ShapeDtypeStruct((b, 2, g, n, HEAD_DIM), F32),
        compiler_params=_params("parallel", "parallel", "parallel"))(chunks, w1.astype(BF16), w2.astype(BF16), pe_rows)


def _sel_overlap_rows(n_rows, n_cmp, pad):
    nf = FEAT - HEAD_DIM
    cs = (np.arange(n_rows) - pad)[:, None] * CMP_STRIDE
    ss = np.arange(nf)[None, :] * SEL_BLOCK
    ov = np.minimum(cs + CMP_BLOCK, ss + SEL_BLOCK) - np.maximum(cs, ss)
    ov = np.maximum(ov, 0) / CMP_STRIDE
    valid = (np.arange(n_rows) >= pad) & (np.arange(n_rows) < pad + n_cmp)
    return (ov * valid[:, None]).astype(np.float32)


TQ = 256
T_PAD = TQ // CMP_STRIDE
T_NEAR = 2 * T_PAD
V_ROWS = HEAD_DIM + 16
assert FAR_DIST <= TQ and WINDOW % TQ == 0


def _bias_tables_t_kernel(rb_ref, tz_ref, tc_ref):
    j = lax.broadcasted_iota(jnp.int32, (TQ, TQ), 0)
    i = lax.broadcasted_iota(jnp.int32, (TQ, TQ), 1)
    cc = lax.broadcasted_iota(jnp.int32, (T_NEAR, TQ), 0)
    ic = lax.broadcasted_iota(jnp.int32, (T_NEAR, TQ), 1)
    d0 = i - j
    d1 = d0 + TQ
    dc = ic + (TQ - CMP_BLOCK + 1) - CMP_STRIDE * cc
    for h in range(N_HEADS):
        last = rb_ref[N_BUCKETS - 1, h]
        tz_ref[0, h] = jnp.where(d0 < 0, NEG_INF, (_rel_bias(d0, rb_ref, h) - last) * LOG2_E)
        tz_ref[1, h] = (_rel_bias(d1, rb_ref, h) - last) * LOG2_E
        tc_ref[h] = jnp.where(dc < 0, NEG_INF, (_rel_bias(dc, rb_ref, h) - last) * LOG2_E)


def bias_tables_t(rel_bias):
    return pl.pallas_call(
        _bias_tables_t_kernel, name="bias_tables",
        in_specs=[pl.BlockSpec(memory_space=pltpu.SMEM)],
        out_specs=[pl.BlockSpec(memory_space=pltpu.VMEM), pl.BlockSpec(memory_space=pltpu.VMEM)],
        out_shape=[jax.ShapeDtypeStruct((2, N_HEADS, TQ, TQ), F32),
                   jax.ShapeDtypeStruct((N_HEADS, T_NEAR, TQ), F32)],
        compiler_params=pltpu.CompilerParams(vmem_limit_bytes=VMEM_LIMIT))(rel_bias)


def _prompt_attn_t_kernel(qt_ref, gate_ref, kc_ref, vct_ref, ovt_ref, tc_ref, tz_ref, ks_ref, vst_ref, kw_ref, vwt_ref,
                          o_ref, qa_ref, lc_ref, m_ref, acc_ref, ot_ref):
    qi = pl.program_id(2)
    nf = FEAT - HEAD_DIM
    hp = HEADS_PER_KV
    for h in range(hp):
        qa_ref[h, 0:HEAD_DIM, :] = qt_ref[h * HEAD_DIM:(h + 1) * HEAD_DIM, :]
        qa_ref[h, HEAD_DIM:FEAT, :] = jnp.zeros((nf, TQ), BF16)
    t_lane = qi * TQ + lax.broadcasted_iota(jnp.int32, (1, TQ), 1)

    near0 = pl.multiple_of(qi * T_PAD, T_PAD)
    rc = kc_ref.shape[0]
    row = lax.broadcasted_iota(jnp.int32, (rc, 1), 0)
    row_ok = (row >= T_PAD) & (row < near0 + T_NEAR)
    has_block = t_lane >= CMP_BLOCK - 1
    kc = kc_ref[...].astype(BF16)
    vct = vct_ref[...].astype(BF16)
    for h in range(hp):
        lc_ref[h] = _dot(kc, qa_ref[h])
    for h in range(hp):
        lc_ref[h, pl.ds(near0, T_NEAR), :] = lc_ref[h, pl.ds(near0, T_NEAR), :] + tc_ref[h]
    p_heads = jnp.zeros((rc, TQ), F32)
    for h in range(hp):
        lc = jnp.where(row_ok, lc_ref[h], NEG_INF)
        p = jnp.exp2(lc - jnp.max(lc, axis=0, keepdims=True))
        p = p * jnp.where(has_block, 1.0 / jnp.sum(p, axis=0, keepdims=True), 0.0)
        p_heads = p_heads + p
        ot_ref[h * HEAD_DIM:(h + 1) * HEAD_DIM, :] = gate_ref[h:h + 1, :] * _dot(vct, p.astype(BF16))
    p_slc = _dot(ovt_ref[...], p_heads, precision=lax.Precision.HIGHEST)

    blk = lax.broadcasted_iota(jnp.int32, (nf, TQ), 0)
    jt = t_lane // SEL_BLOCK
    forced = (blk == 0) | (blk == jt) | (blk == jt - 1)
    score = jnp.where(forced, jnp.inf, jnp.where(blk <= jt, p_slc, -jnp.inf))
    n_grp = nf // SUBLANE
    groups = [score[SUBLANE * g:SUBLANE * (g + 1), :] for g in range(n_grp)]
    ranks = [jnp.zeros((SUBLANE, TQ), jnp.int32) for _ in range(n_grp)]
    sub = lax.broadcasted_iota(jnp.int32, (SUBLANE, TQ), 0)
    for c in range(nf):
        cg, cs = divmod(c, SUBLANE)
        other = jnp.broadcast_to(groups[cg][cs:cs + 1, :], (SUBLANE, TQ))
        for g in range(n_grp):
            if g > cg:
                beats = (other >= groups[g]).astype(jnp.int32)
            elif g < cg:
                beats = (other > groups[g]).astype(jnp.int32)
            else:
                beats = jnp.where(sub > cs, (other >= groups[g]).astype(jnp.int32), (other > groups[g]).astype(jnp.int32))
            ranks[g] = ranks[g] + beats
    feat = jnp.concatenate(
        [jnp.where((ranks[g] < SEL_TOPK) & (groups[g] > NEG_INF), 0.0, NEG_INF) for g in range(n_grp)], axis=0)
    feat = feat.astype(BF16)
    for h in range(hp):
        qa_ref[h, HEAD_DIM:FEAT, :] = feat

    def reset():
        m_ref[...] = jnp.full(m_ref.shape, -3e38, F32)
        acc_ref[...] = jnp.zeros(acc_ref.shape, F32)

    def step(k_ref, vt_ref, j, bias):
        k = k_ref[pl.ds(pl.multiple_of(j * TQ, TQ), TQ), :]
        vt = vt_ref[j]
        m_old = [m_ref[h] for h in range(hp)]
        acc_old = [acc_ref[h] for h in range(hp)]
        s = [_dot(k, qa_ref[h]) for h in range(hp)]
        if bias is not None:
            s = [s[h] + bias(h) for h in range(hp)]
        m_new = [jnp.maximum(m_old[h], jnp.max(s[h], axis=0, keepdims=True)) for h in range(hp)]
        pv = [_dot(vt, jnp.exp2(s[h] - m_new[h]).astype(BF16)) for h in range(hp)]
        for h in range(hp):
            acc_ref[h] = jnp.exp2(m_old[h] - m_new[h]) * acc_old[h] + pv[h]
            m_ref[h] = m_new[h]

    def add_result(branch):
        for h in range(hp):
            acc = acc_ref[h]
            o = acc[0:HEAD_DIM, :] * (1.0 / acc[HEAD_DIM:HEAD_DIM + 1, :])
            rows = slice(h * HEAD_DIM, (h + 1) * HEAD_DIM)
            ot_ref[rows, :] = ot_ref[rows, :] + gate_ref[branch * hp + h:branch * hp + h + 1, :] * o

    diag = lambda h: tz_ref[0, h]
    prev = lambda h: tz_ref[1, h]

    reset()

    def far_step(j, carry):
        step(ks_ref, vst_ref, j, None)
        return carry

    lax.fori_loop(0, jnp.maximum(qi - 1, 0), far_step, 0)

    @pl.when(qi >= 1)
    def _():
        step(ks_ref, vst_ref, qi - 1, prev)

    step(ks_ref, vst_ref, qi, diag)
    add_result(1)

    reset()
    n_back = WINDOW // TQ
    edge = jnp.where(lax.broadcasted_iota(jnp.int32, (TQ, TQ), 0) >= lax.broadcasted_iota(jnp.int32, (TQ, TQ), 1),
                     0.0, NEG_INF)
    for back in range(n_back, -1, -1):
        bias = (lambda h: edge) if back == n_back else prev if back == 1 else diag if back == 0 else None

        @pl.when(qi >= back)
        def _(back=back, bias=bias):
            step(kw_ref, vwt_ref, qi - back, bias)

    add_result(2)
    o_ref[...] = ot_ref[...].T.astype(o_ref.dtype)


def prompt_attention_t(qt, gate_t, kc, vct, tc, tz, ks, vst, kw, vwt, b, t):
    n_chunk = t // CMP_STRIDE
    rc = kc.shape[2]
    nf = FEAT - HEAD_DIM
    nq = t // TQ
    assert t // SEL_BLOCK <= nf and t % TQ == 0 and rc >= T_PAD * (nq + 1) and rc >= T_PAD + n_chunk - 1
    ovt = jnp.asarray(_sel_overlap_rows(rc, n_chunk - 1, T_PAD).T)
    hp = HEADS_PER_KV
    gw = hp * HEAD_DIM
    k_spec = pl.BlockSpec((None, None, t, FEAT), lambda i, g, qi: (i, g, 0, 0))
    v_spec = pl.BlockSpec((None, None, nq, V_ROWS, TQ), lambda i, g, qi: (i, g, 0, 0, 0))
    return pl.pallas_call(
        _prompt_attn_t_kernel, grid=(b, N_KV_HEADS, nq), name="prompt_attention",
        in_specs=[pl.BlockSpec((gw, TQ), lambda i, g, qi: (g, i * nq + qi)),
                  pl.BlockSpec((4 * hp, TQ), lambda i, g, qi: (g, i * nq + qi)),
                  pl.BlockSpec((None, None, rc, FEAT), lambda i, g, qi: (i, g, 0, 0)),
                  pl.BlockSpec((None, None, HEAD_DIM, rc), lambda i, g, qi: (i, g, 0, 0)),
                  pl.BlockSpec((nf, rc), lambda i, g, qi: (0, 0)),
                  pl.BlockSpec((hp, T_NEAR, TQ), lambda i, g, qi: (g, 0, 0)),
                  pl.BlockSpec((2, hp, TQ, TQ), lambda i, g, qi: (0, g, 0, 0)),
                  k_spec, v_spec, k_spec, v_spec],
        out_specs=pl.BlockSpec((TQ, gw), lambda i, g, qi: (i * nq + qi, g)),
        out_shape=jax.ShapeDtypeStruct((b * t, ATTN_WIDTH), BF16),
        scratch_shapes=[pltpu.VMEM((hp, FEAT, TQ), BF16), pltpu.VMEM((hp, rc, TQ), F32),
                        pltpu.VMEM((hp, 1, TQ), F32), pltpu.VMEM((hp, V_ROWS, TQ), F32), pltpu.VMEM((gw, TQ), F32)],
        compiler_params=_params("parallel", "parallel", "arbitrary"))(
            qt, gate_t, kc, vct, ovt, tc, tz, ks, vst, kw, vwt)


def _gmlp_kernel(u_ref, v_ref, lng_ref, lnb_ref, ws_ref, bs_ref, o_ref):
    v = v_ref[...].astype(F32)
    mu = jnp.mean(v, axis=-1, keepdims=True)
    var = jnp.mean(jnp.square(v - mu), axis=-1, keepdims=True)
    vn = ((v - mu) * lax.rsqrt(var + LN_EPS) * lng_ref[...] + lnb_ref[...]).astype(BF16)
    tri = (lax.broadcasted_iota(jnp.int32, (CHUNK, CHUNK), 0) >= lax.broadcasted_iota(jnp.int32, (CHUNK, CHUNK), 1))
    for g in range(GM_GROUPS):
        w = jnp.where(tri, ws_ref[g], 0.0).astype(BF16)
        cols = slice(g * GM_GROUP_DIM, (g + 1) * GM_GROUP_DIM)
        for c in range(u_ref.shape[0] // CHUNK):
            rows = slice(c * CHUNK, (c + 1) * CHUNK)
            s = _dot(w, vn[rows, cols]) + bs_ref[:, g:g + 1]
            o_ref[rows, cols] = (u_ref[rows, cols].astype(F32) * s).astype(o_ref.dtype)


def gmlp_chunks(uv, ln_g, ln_b, w_s, b_s, rows):
    m = uv.shape[0]
    return pl.pallas_call(
        _gmlp_kernel, grid=(m // rows,), name="gmlp",
        in_specs=[pl.BlockSpec((rows, GM_WIDTH), lambda i: (i, 0)), pl.BlockSpec((rows, GM_WIDTH), lambda i: (i, 1)),
                  pl.BlockSpec((1, GM_WIDTH), lambda i: (0, 0)), pl.BlockSpec((1, GM_WIDTH), lambda i: (0, 0)),
                  pl.BlockSpec((GM_GROUPS, CHUNK, CHUNK), lambda i: (0, 0, 0)),
                  pl.BlockSpec((CHUNK, GM_GROUPS), lambda i: (0, 0))],
        out_specs=pl.BlockSpec((rows, GM_WIDTH), lambda i: (i, 0)),
        out_shape=jax.ShapeDtypeStruct((m, GM_WIDTH), BF16),
        compiler_params=_params("parallel"))(uv, uv, ln_g.reshape(1, -1), ln_b.reshape(1, -1), w_s, b_s.T)


def _merge_kernel(attn_ref, gm_ref, mg_ref, x_ref, wa_ref, wb_ref, wo_ref, gn_ref, wr_ref, br_ref,
                  h_ref, hn_ref, idx_ref, wt_ref):
    tm = x_ref.shape[0]
    sub = min(MERGE_SUB, tm)
    parts = [slice(r0, r0 + sub) for r0 in range(0, tm, sub)]
    ta = [_dot(attn_ref[p, :], wa_ref[...]) for p in parts]
    tb = [_dot(gm_ref[p, :], wb_ref[...]) for p in parts]
    merged = [(mg_ref[p, 0:D_MODEL].astype(F32) * ta[i] + mg_ref[p, D_MODEL:2 * D_MODEL].astype(F32) * tb[i])
              .astype(BF16) for i, p in enumerate(parts)]
    h = [x_ref[p, :] + _dot(merged[i], wo_ref[...]) for i, p in enumerate(parts)]
    hn = [hh * lax.rsqrt(jnp.mean(hh * hh, axis=-1, keepdims=True) + RMS_EPS) * gn_ref[...] for hh in h]
    hi = [v.astype(BF16) for v in hn]
    lo = [(v - hi[i].astype(F32)).astype(BF16) for i, v in enumerate(hn)]
    logits = [_dot(hi[i], wr_ref[0]) + _dot(lo[i], wr_ref[0]) + _dot(hi[i], wr_ref[1]) + br_ref[...]
              for i in range(len(parts))]
    routed = [_route(lg) for lg in logits]
    for i, p in enumerate(parts):
        h_ref[p, :] = h[i]
        for c in range(hn_ref.shape[0]):
            hn_ref[c, p, :] = hi[i]
        idx_ref[p, :] = routed[i][0]
        wt_ref[p, :] = routed[i][1]


def _route(logits):
    lane = lax.broadcasted_iota(jnp.int32, logits.shape, 1)
    vals, idxs = [], []
    for _ in range(TOP_K):
        best = jnp.max(logits, axis=-1, keepdims=True)
        ix = jnp.min(jnp.where(logits == best, lane, LANE), axis=-1, keepdims=True)
        vals.append(best)
        idxs.append(ix)
        logits = jnp.where(lane == ix, -jnp.inf, logits)
    ex = [jnp.exp(v - vals[0]) for v in vals]
    inv = 1.0 / functools.reduce(lambda a, b: a + b, ex)
    idx_out = jnp.zeros(lane.shape, jnp.int32)
    wt_out = jnp.zeros(lane.shape, F32)
    for k in range(TOP_K):
        idx_out = jnp.where(lane == k, idxs[k], idx_out)
        wt_out = jnp.where(lane == k, ex[k] * inv, wt_out)
    return idx_out, wt_out


MERGE_SUB = 128


def merge_and_route(attn, gm, mg, x, wa, wb, wo, g_ffn, w_router, b_router, tm, copies):
    m = x.shape[0]
    tm = min(tm, m)
    wr = jnp.zeros((D_MODEL, LANE), F32).at[:, :N_EXPERTS].set(w_router)
    wr_hi = wr.astype(BF16)
    wr = jnp.stack([wr_hi, (wr - wr_hi.astype(F32)).astype(BF16)])
    br = jnp.full((1, LANE), -jnp.inf, F32).at[0, :N_EXPERTS].set(b_router)
    row = lambda w: pl.BlockSpec((tm, w), lambda i: (i, 0))
    full = lambda a: pl.BlockSpec(a.shape, lambda i: (0,) * a.ndim, pipeline_mode=pl.Buffered(1))
    return pl.pallas_call(
        _merge_kernel, grid=(m // tm,), name="merge_route",
        in_specs=[row(ATTN_WIDTH), row(GM_WIDTH), row(2 * D_MODEL), row(D_MODEL), full(wa), full(wb), full(wo),
                  pl.BlockSpec((1, D_MODEL), lambda i: (0, 0)), full(wr), full(br)],
        out_specs=[row(D_MODEL), pl.BlockSpec((copies, tm, D_MODEL), lambda i: (0, i, 0)), row(LANE), row(LANE)],
        out_shape=[jax.ShapeDtypeStruct((m, D_MODEL), F32), jax.ShapeDtypeStruct((copies, m, D_MODEL), BF16),
                   jax.ShapeDtypeStruct((m, LANE), jnp.int32), jax.ShapeDtypeStruct((m, LANE), F32)],
        compiler_params=_params("parallel"))(attn, gm, mg, x, wa, wb, wo, g_ffn.reshape(1, -1), wr, br)


def _expert_tile(te_ref, first_ref, nact_ref, nxt_ref, valid_ref, w_hbm, stage_ref, cache_ref, sem, o_ref, compute):
    j, r = pl.program_id(0), pl.program_id(1)
    tn = stage_ref.shape[-1]
    rows = stage_ref.shape[1]

    def copies(e, jj):
        cols = pl.ds(pl.multiple_of(jj * tn, tn), tn)
        band = rows // MOE_DMA_SPLIT
        return [pltpu.make_async_copy(w.at[e, c * band:(c + 1) * band, cols],
                                      stage_ref.at[i, c * band:(c + 1) * band, :], sem.at[i, c])
                for i, w in enumerate(w_hbm) for c in range(MOE_DMA_SPLIT)]

    @pl.when((j == 0) & (r == 0))
    def _():
        for c in copies(te_ref[0], 0):
            c.start()

    @pl.when(first_ref[r] == 1)
    def _():
        for c in copies(te_ref[r], j):
            c.wait()
        for i in range(len(w_hbm)):
            for r0 in range(0, rows, 256):
                cache_ref[i, r0:r0 + 256, :] = stage_ref[i, r0:r0 + 256, :].astype(cache_ref.dtype)
        wraps = nxt_ref[r] < 0
        e_next = jnp.where(wraps, te_ref[0], nxt_ref[r])
        j_next = jnp.where(wraps, j + 1, j)

        @pl.when(j_next < pl.num_programs(0))
        def _():
            for c in copies(e_next, j_next):
                c.start()

    tm = o_ref.shape[0]
    half = tm // 2

    @pl.when((r < nact_ref[0]) & (valid_ref[r] > half))
    def _():
        compute(tm)

    @pl.when((r < nact_ref[0]) & (valid_ref[r] <= half))
    def _():
        compute(half)
        o_ref[half:tm, :] = jnp.zeros((tm - half, o_ref.shape[1]), o_ref.dtype)

    @pl.when(r >= nact_ref[0])
    def _():
        o_ref[...] = jnp.zeros(o_ref.shape, o_ref.dtype)


def _moe_up_kernel(te_ref, first_ref, nact_ref, nxt_ref, valid_ref, x_ref, wg_hbm, wu_hbm, bg_ref, bu_ref, o_ref,
                   stage_ref, cache_ref, sem):
    def compute(m):
        x = x_ref[0:m, :]
        g = jnp.minimum(_dot(x, cache_ref[0]) + bg_ref[...], SWIGLU_LIMIT)
        u = jnp.clip(_dot(x, cache_ref[1]) + bu_ref[...], -SWIGLU_LIMIT, SWIGLU_LIMIT)
        o_ref[0:m, :] = ((u + 1.0) * (g * jax.nn.sigmoid(SWIGLU_ALPHA * g))).astype(o_ref.dtype)

    _expert_tile(te_ref, first_ref, nact_ref, nxt_ref, valid_ref, (wg_hbm, wu_hbm), stage_ref, cache_ref, sem, o_ref,
                 compute)


def _moe_down_kernel(te_ref, first_ref, nact_ref, nxt_ref, valid_ref, a_ref, wd_hbm, bd_ref, o_ref,
                     stage_ref, cache_ref, sem):
    def compute(m):
        o_ref[0:m, :] = (_dot(a_ref[0:m, :], cache_ref[0]) + bd_ref[...]).astype(o_ref.dtype)

    _expert_tile(te_ref, first_ref, nact_ref, nxt_ref, valid_ref, (wd_hbm,), stage_ref, cache_ref, sem, o_ref, compute)


def moe_experts(x_rows, tiles, w_gate, b_gate, w_up, b_up, w_down, b_down):
    rows = x_rows.shape[0]
    n_tiles = rows // MOE_TM
    row_map = lambda j, r, te, first, nact, nxt, valid: (jnp.minimum(r, nact[0] - 1), 0)
    out_map = lambda j, r, te, first, nact, nxt, valid: (r, j)
    b_spec = pl.BlockSpec((None, 1, MOE_TN), lambda j, r, te, first, nact, nxt, valid: (te[r], 0, j))
    hbm = pl.BlockSpec(memory_space=pl.ANY)

    def scratch(n_w, k):
        return [pltpu.VMEM((n_w, k, MOE_TN), F32), pltpu.VMEM((n_w, k, MOE_TN), BF16),
                pltpu.SemaphoreType.DMA((n_w, MOE_DMA_SPLIT))]

    act = pl.pallas_call(
        _moe_up_kernel, name="moe_up",
        grid_spec=pltpu.PrefetchScalarGridSpec(
            num_scalar_prefetch=5, grid=(D_FF // MOE_TN, n_tiles),
            in_specs=[pl.BlockSpec((MOE_TM, D_MODEL), row_map), hbm, hbm, b_spec, b_spec],
            out_specs=pl.BlockSpec((MOE_TM, MOE_TN), out_map),
            scratch_shapes=scratch(2, D_MODEL)),
        out_shape=jax.ShapeDtypeStruct((rows, D_FF), BF16),
        compiler_params=_params("arbitrary", "arbitrary"))(
            *tiles, x_rows, w_gate, w_up,
            b_gate.reshape(N_EXPERTS, 1, D_FF), b_up.reshape(N_EXPERTS, 1, D_FF))
    return pl.pallas_call(
        _moe_down_kernel, name="moe_down",
        grid_spec=pltpu.PrefetchScalarGridSpec(
            num_scalar_prefetch=5, grid=(D_MODEL // MOE_TN, n_tiles),
            in_specs=[pl.BlockSpec((MOE_TM, D_FF), row_map), hbm, b_spec],
            out_specs=pl.BlockSpec((MOE_TM, MOE_TN), out_map),
            scratch_shapes=scratch(1, D_FF)),
        out_shape=jax.ShapeDtypeStruct((rows, D_MODEL), BF16),
        compiler_params=_params("arbitrary", "arbitrary"))(
            *tiles, act, w_down, b_down.reshape(N_EXPERTS, 1, D_MODEL))


def route_rows(top_i, n_rows, n_stacked):
    n = top_i.shape[0]
    hit = top_i[:, :, None] == jnp.arange(N_EXPERTS, dtype=jnp.int32)[None, None, :]
    onehot = hit.astype(jnp.int32).sum(axis=1)
    before = jnp.cumsum(onehot, axis=0) - onehot
    counts = onehot.sum(axis=0)
    padded = (counts + MOE_TM - 1) // MOE_TM * MOE_TM
    ends = jnp.cumsum(padded)
    starts = ends - padded
    pos = jnp.sum(jnp.where(hit, (starts[None, :] + before)[:, None, :], 0), axis=-1)
    token = jnp.minimum(jnp.arange(n, dtype=jnp.int32), n_stacked - 1)
    pair_id = jnp.arange(TOP_K, dtype=jnp.int32)[None, :] * n_stacked + token[:, None]
    row_pair = jnp.arange(n_rows, dtype=jnp.int32).at[pos.reshape(-1)].set(pair_id.reshape(-1), unique_indices=True)
    n_tiles = n_rows // MOE_TM
    n_active = (ends[-1] // MOE_TM).astype(jnp.int32)
    tile_row = jnp.minimum(jnp.arange(n_tiles, dtype=jnp.int32), n_active - 1) * MOE_TM
    tile_expert = jnp.minimum((tile_row[:, None] >= ends[None, :]).astype(jnp.int32).sum(axis=-1), N_EXPERTS - 1)
    tile_first = jnp.concatenate([jnp.ones((1,), jnp.int32), (tile_expert[1:] != tile_expert[:-1]).astype(jnp.int32)])
    tile_id = jnp.arange(n_tiles, dtype=jnp.int32)
    start_id = jnp.where(tile_first == 1, tile_id, n_tiles)
    next_start = jnp.concatenate([lax.cummin(start_id[::-1])[::-1][1:], jnp.full((1,), n_tiles, jnp.int32)])
    tile_next = jnp.where(next_start < n_tiles, tile_expert[jnp.minimum(next_start, n_tiles - 1)], -1)
    tile_valid = jnp.clip((starts + counts)[tile_expert] - tile_id * MOE_TM, 0, MOE_TM)
    tiles = (tile_expert, tile_first, n_active.reshape(1), tile_next.astype(jnp.int32), tile_valid.astype(jnp.int32))
    return pos, row_pair, tiles


def _combine_kernel(h_ref, y_ref, w_ref, g_ref, o_ref):
    h = h_ref[...]
    w = w_ref[...]
    for k in range(TOP_K):
        h = h + w[:, k:k + 1] * y_ref[k].astype(F32)
    o_ref[...] = h * lax.rsqrt(jnp.mean(h * h, axis=-1, keepdims=True) + RMS_EPS) * g_ref[...]


def combine_and_norm(h, y4, wts, g, tm):
    m = h.shape[0]
    tm = min(tm, m)
    return pl.pallas_call(
        _combine_kernel, grid=(m // tm,), name="combine_norm",
        in_specs=[pl.BlockSpec((tm, D_MODEL), lambda i: (i, 0)), pl.BlockSpec((TOP_K, tm, D_MODEL), lambda i: (0, i, 0)),
                  pl.BlockSpec((tm, LANE), lambda i: (i, 0)), pl.BlockSpec((1, D_MODEL), lambda i: (0, 0))],
        out_specs=pl.BlockSpec((tm, D_MODEL), lambda i: (i, 0)),
        out_shape=jax.ShapeDtypeStruct((m, D_MODEL), F32), compiler_params=_params("parallel"))(
            h, y4, wts, g.reshape(1, -1))


def _in_weights(w_in):
    c0 = ATTN_WIDTH
    c1 = c0 + 3 * KV_COLS
    c2 = c1 + GATE_COLS
    c3 = c2 + 2 * GM_WIDTH
    wg = w_in[:, c1:c2].reshape(D_MODEL, N_KV_HEADS, HEADS_PER_KV, 3).transpose(0, 1, 3, 2)
    wg = jnp.pad(wg.reshape(D_MODEL, N_KV_HEADS, 3 * HEADS_PER_KV), ((0, 0), (0, 0), (0, LANE - 3 * HEADS_PER_KV)))
    cast = lambda w: w.astype(BF16)
    return (cast(w_in[:, :c0]), cast(w_in[:, c0:c1]), cast(wg.reshape(D_MODEL, N_KV_HEADS * LANE)),
            cast(w_in[:, c2:c3]), cast(w_in[:, c3:]))


def _project(a, weights, wide_dtype, q_scale):
    w_q, w_kv, w_ng, w_uv, w_mg = weights
    q = matmul(a, w_q, lambda z: z * np.float32(q_scale), wide_dtype, 1024, 1024, "proj_q")
    kv = matmul(a, w_kv, lambda z: z, F32, 1024, 3 * KV_COLS, "proj_kv")
    ng = matmul(a, w_ng, jax.nn.sigmoid, F32, 1024, N_KV_HEADS * LANE, "proj_gate")
    uv = matmul(a, w_uv, _gelu, wide_dtype, 1024, 1024, "proj_uv")
    mg = matmul(a, w_mg, jax.nn.sigmoid, BF16, 1024, 1024, "proj_merge")
    return q, kv, ng, uv, mg


def _head_rows(fn):
    out = fn(0)
    out = jnp.broadcast_to(out, (HEADS_PER_KV, out.shape[1]))
    row = lax.broadcasted_iota(jnp.int32, out.shape, 0)
    for h in range(1, HEADS_PER_KV):
        out = jnp.where(row == h, fn(h), out)
    return out


def _sample_cmp_win_kernel(rb_ref, q_ref, kvc_ref, ov_ref, sw_ref, new_ref, gate_ref, o_ref, idx_ref, *, past, n_cmp):
    n_rows = kvc_ref.shape[2]
    n_blk = ov_ref.shape[1]
    win = sw_ref.shape[-1]
    win_col = 2 * KV_COLS
    lane_blk = lax.broadcasted_iota(jnp.int32, (1, n_blk), 1)
    lane_out = lax.broadcasted_iota(jnp.int32, (1, LANE), 1)
    for g in range(N_KV_HEADS):
        heads = slice(g * HEADS_PER_KV, (g + 1) * HEADS_PER_KV)
        q = q_ref[heads, :]
        qb = q.astype(BF16)

        n = lax.broadcasted_iota(jnp.int32, (1, n_rows), 1)
        d = past - (n * CMP_STRIDE + CMP_BLOCK - 1)
        logit = _dot_nt(qb, kvc_ref[0, g].astype(BF16)) + _head_rows(
            lambda h: _rel_bias(d, rb_ref, g * HEADS_PER_KV + h))
        logit = jnp.where(n < n_cmp, logit, NEG_INF)
        p = jnp.exp(logit - jnp.max(logit, axis=-1, keepdims=True))
        p = p * (1.0 / jnp.sum(p, axis=-1, keepdims=True))
        o_cmp = _dot(p.astype(BF16), kvc_ref[1, g].astype(BF16))
        p_heads = jnp.broadcast_to(jnp.sum(p, axis=0, keepdims=True), (SUBLANE, n_rows))
        p_slc = _dot(p_heads, ov_ref[...], precision=lax.Precision.HIGHEST)[0:1, :]

        score = jnp.where((lane_blk == 0) | (lane_blk == n_blk - 1), jnp.inf, p_slc)
        picked = jnp.zeros((1, LANE), jnp.int32)
        for k in range(SEL_TOPK - 1):
            best = jnp.max(score, axis=-1, keepdims=True)
            ix = jnp.min(jnp.where(score == best, lane_blk, n_blk), axis=-1, keepdims=True)
            picked = jnp.where(lane_out == k, ix, picked)
            score = jnp.where(lane_blk == ix, -jnp.inf, score)
        idx_ref[g:g + 1, :] = picked

        dw = win - lax.broadcasted_iota(jnp.int32, (1, win), 1)
        lw = _dot(qb, sw_ref[0, g].astype(BF16)) + _head_rows(
            lambda h: _rel_bias(dw, rb_ref, g * HEADS_PER_KV + h))
        k_new = new_ref[:, win_col + g * HEAD_DIM:win_col + (g + 1) * HEAD_DIM]
        v_new = new_ref[:, win_col + KV_COLS // 2 + g * HEAD_DIM:win_col + KV_COLS // 2 + (g + 1) * HEAD_DIM]
        l_new = jnp.sum(q * k_new, axis=-1, keepdims=True) + _head_rows(
            lambda h: jnp.full((1, 1), rb_ref[0, g * HEADS_PER_KV + h], F32))
        m = jnp.maximum(jnp.max(lw, axis=-1, keepdims=True), l_new)
        pw = jnp.exp(lw - m)
        p_new = jnp.exp(l_new - m)
        o_win = ((_dot_nt(pw.astype(BF16), sw_ref[1, g].astype(BF16)) + p_new * v_new)
                 * (1.0 / (jnp.sum(pw, axis=-1, keepdims=True) + p_new)))
        o_ref[heads, :] = gate_ref[0, heads, :] * o_cmp + gate_ref[2, heads, :] * o_win


def sample_cmp_win(rel_bias, q, kvc, state_win, kv_new, gate, past):
    b, _, _, n_rows, _ = kvc.shape
    n_cmp = past // CMP_STRIDE - CMP_BLOCK // CMP_STRIDE + 1
    n_blk = past // SEL_BLOCK
    assert past % SEL_BLOCK == 0 and n_blk >= SEL_TOPK and n_cmp <= n_rows
    cs = np.arange(n_rows)[:, None] * CMP_STRIDE
    ss = np.arange(n_blk)[None, :] * SEL_BLOCK
    ov = np.maximum(np.minimum(cs + CMP_BLOCK, ss + SEL_BLOCK) - np.maximum(cs, ss), 0) / CMP_STRIDE
    ov = jnp.asarray((ov * (np.arange(n_rows) < n_cmp)[:, None]).astype(np.float32))
    win = state_win.shape[-1]
    return pl.pallas_call(
        functools.partial(_sample_cmp_win_kernel, past=past, n_cmp=n_cmp), grid=(b,), name="sample_cmp_win",
        in_specs=[pl.BlockSpec(memory_space=pltpu.SMEM),
                  pl.BlockSpec((None, N_HEADS, HEAD_DIM), lambda i: (i, 0, 0)),
                  pl.BlockSpec((None, 2, N_KV_HEADS, n_rows, HEAD_DIM), lambda i: (i, 0, 0, 0, 0)),
                  pl.BlockSpec(ov.shape, lambda i: (0, 0)),
                  pl.BlockSpec((None, 2, N_KV_HEADS, HEAD_DIM, win), lambda i: (i, 0, 0, 0, 0)),
                  pl.BlockSpec((None, 1, 3 * KV_COLS), lambda i: (i, 0, 0)),
                  pl.BlockSpec((None, 3, N_HEADS, 1), lambda i: (i, 0, 0, 0))],
        out_specs=[pl.BlockSpec((None, N_HEADS, HEAD_DIM), lambda i: (i, 0, 0)),
                   pl.BlockSpec((None, N_KV_HEADS, LANE), lambda i: (i, 0, 0))],
        out_shape=[jax.ShapeDtypeStruct((b, N_HEADS, HEAD_DIM), F32),
                   jax.ShapeDtypeStruct((b, N_KV_HEADS, LANE), jnp.int32)],
        compiler_params=_params("parallel"))(rel_bias, q, kvc, ov, state_win, kv_new, gate)


def _sample_slc_kernel(pt_ref, idx_ref, rb_ref, q_ref, new_ref, gate_ref, part_ref, *rest, past):
    blocks, o_ref = rest[:-1], rest[-1]
    b, g = pl.program_id(0), pl.program_id(1)
    q = q_ref[...]
    qb = q.astype(BF16)
    slc_col = KV_COLS
    half = KV_COLS // 2
    page = blocks[0].shape[-1]
    pos = lax.broadcasted_iota(jnp.int32, (1, page), 1)
    logits, values = [], []
    for k, blk_ref in enumerate(blocks):
        ix = idx_ref[(b * N_KV_HEADS + g) * SEL_TOPK + k]
        d = past - ((ix // SLC_PER_PAGE) * page + pos)
        lg = _dot(qb, blk_ref[0].astype(BF16)) + _head_rows(lambda h: _rel_bias(d, rb_ref, g * HEADS_PER_KV + h))
        logits.append(jnp.where(pos // SEL_BLOCK == ix % SLC_PER_PAGE, lg, NEG_INF))
        values.append(blk_ref[1].astype(BF16))
    new = new_ref[...]
    k_new = jnp.where(g == 0, new[:, slc_col:slc_col + HEAD_DIM], new[:, slc_col + HEAD_DIM:slc_col + 2 * HEAD_DIM])
    v_new = jnp.where(g == 0, new[:, slc_col + half:slc_col + half + HEAD_DIM],
                      new[:, slc_col + half + HEAD_DIM:slc_col + half + 2 * HEAD_DIM])
    l_new = jnp.sum(q * k_new, axis=-1, keepdims=True) + _head_rows(
        lambda h: jnp.full((1, 1), rb_ref[0, g * HEADS_PER_KV + h], F32))
    m = l_new
    for lg in logits:
        m = jnp.maximum(m, jnp.max(lg, axis=-1, keepdims=True))
    p_new = jnp.exp(l_new - m)
    denom = p_new
    out = p_new * v_new
    for lg, vv in zip(logits, values):
        p = jnp.exp(lg - m)
        denom = denom + jnp.sum(p, axis=-1, keepdims=True)
        out = out + _dot_nt(p.astype(BF16), vv)
    o_ref[...] = part_ref[...] + gate_ref[1] * (out * (1.0 / denom))


def sample_slc(page_table, picked, rel_bias, q, kv_new, gate, part, cache_slc, past):
    b = q.shape[0]
    page = cache_slc.shape[-1]
    n_pick = SEL_TOPK - 1

    def blk_map(k):
        def index(i, g, pt, idx, k=k):
            ix = idx[(i * N_KV_HEADS + g) * SEL_TOPK + k]
            return (pt[i, ix // SLC_PER_PAGE], 0, g, 0, 0)
        return index

    head_spec = pl.BlockSpec((None, HEADS_PER_KV, HEAD_DIM), lambda i, g, pt, idx: (i, g, 0))
    return pl.pallas_call(
        functools.partial(_sample_slc_kernel, past=past), name="sample_slc",
        grid_spec=pltpu.PrefetchScalarGridSpec(
            num_scalar_prefetch=2, grid=(b, N_KV_HEADS),
            in_specs=[pl.BlockSpec(memory_space=pltpu.SMEM), head_spec,
                      pl.BlockSpec((None, 1, 3 * KV_COLS), lambda i, g, pt, idx: (i, 0, 0)),
                      pl.BlockSpec((None, 3, HEADS_PER_KV, 1), lambda i, g, pt, idx: (i, 0, g, 0)), head_spec]
            + [pl.BlockSpec((None, 2, None, HEAD_DIM, page), blk_map(k)) for k in range(n_pick)],
            out_specs=head_spec),
        out_shape=jax.ShapeDtypeStruct((b, N_HEADS, HEAD_DIM), F32),
        compiler_params=_params("parallel", "parallel"))(
            page_table, picked, rel_bias, q, kv_new, gate, part, *([cache_slc] * n_pick))


def _gmlp_row_kernel(uv_ref, lng_ref, lnb_ref, w0_ref, b0_ref, vn_ref, o_ref):
    v = uv_ref[:, GM_WIDTH:2 * GM_WIDTH]
    mu = jnp.mean(v, axis=-1, keepdims=True)
    var = jnp.mean(jnp.square(v - mu), axis=-1, keepdims=True)
    vn = (v - mu) * lax.rsqrt(var + LN_EPS) * lng_ref[...] + lnb_ref[...]
    vn_ref[...] = vn
    o_ref[...] = (uv_ref[:, 0:GM_WIDTH] * (w0_ref[...] * vn + b0_ref[...])).astype(o_ref.dtype)


def gmlp_first_row(uv, ln_g, ln_b, w_s, b_s):
    m = uv.shape[0]
    w0 = jnp.repeat(w_s[:, 0, 0], GM_GROUP_DIM).reshape(1, GM_WIDTH)
    b0 = jnp.repeat(b_s[:, 0], GM_GROUP_DIM).reshape(1, GM_WIDTH)
    return pl.pallas_call(
        _gmlp_row_kernel, name="gmlp_row",
        out_shape=[jax.ShapeDtypeStruct((m, GM_WIDTH), F32), jax.ShapeDtypeStruct((m, GM_WIDTH), BF16)])(
            uv, ln_g.reshape(1, -1), ln_b.reshape(1, -1), w0, b0)


def _group_major(x, feat):
    b, t = x.shape[0], x.shape[1]
    f = jnp.broadcast_to(feat.astype(BF16)[None, None], (b, N_KV_HEADS, t, FEAT - HEAD_DIM))
    return jnp.concatenate([x.transpose(0, 2, 1, 3).astype(BF16), f], axis=-1)


def _value_tiles(x):
    b, t = x.shape[0], x.shape[1]
    xt = x.transpose(0, 2, 3, 1).astype(BF16)
    extra = jnp.zeros((b, N_KV_HEADS, V_ROWS - HEAD_DIM, t), BF16).at[:, :, 0].set(1)
    tiles = jnp.concatenate([xt, extra], axis=2).reshape(b, N_KV_HEADS, V_ROWS, t // TQ, TQ)
    return tiles.transpose(0, 1, 3, 2, 4)


def _prompt_mixer_a(q, kv, ng, rel_bias, pe, w1, w2, b, t):
    kvr = kv.reshape(b, t, 3, 2, N_KV_HEADS, HEAD_DIM)
    n_chunk = t // CMP_STRIDE
    chunks = kvr[:, :, 0].reshape(b, n_chunk, CMP_STRIDE, 2, N_KV_HEADS, HEAD_DIM).transpose(0, 3, 4, 1, 2, 5)
    kvc = compress_chunks(chunks.reshape(b, 2, N_KV_HEADS, n_chunk, CMP_STRIDE * HEAD_DIM).astype(BF16), w1, w2, pe)
    rc = -(-(T_PAD + max(n_chunk - 1, T_PAD * t // TQ)) // LANE) * LANE
    kvc = jnp.pad(kvc[:, :, :, :n_chunk - 1], ((0, 0), (0, 0), (0, 0), (T_PAD, rc - T_PAD - n_chunk + 1), (0, 0)))
    kc = jnp.pad(kvc[:, 0], ((0, 0), (0, 0), (0, 0), (0, FEAT - HEAD_DIM)))
    nf = FEAT - HEAD_DIM
    blocks = (jnp.arange(t)[:, None] // SEL_BLOCK == jnp.arange(nf)[None, :])
    tz, tc = bias_tables_t(rel_bias)
    gate_t = ng.reshape(b * t, N_KV_HEADS, LANE)[:, :, :4 * HEADS_PER_KV].reshape(b * t, -1).T
    return prompt_attention_t(
        q.T, gate_t, kc, kvc[:, 1].transpose(0, 1, 3, 2), tc, tz,
        _group_major(kvr[:, :, 1, 0], blocks), _value_tiles(kvr[:, :, 1, 1]),
        _group_major(kvr[:, :, 2, 0], jnp.zeros((t, nf), BF16)), _value_tiles(kvr[:, :, 2, 1]), b, t)


def _sample_mixer_a(q, kv, ng, rel_bias, pe, w1, w2, cache_cmp, cache_slc, state_win, page_table):
    b, n_pages = page_table.shape
    page = cache_cmp.shape[1]
    past = n_pages * page
    n_chunk = past // CMP_STRIDE
    rows = cache_cmp[page_table].reshape(b, n_chunk, CMP_STRIDE, 2, N_KV_HEADS, HEAD_DIM).transpose(0, 3, 4, 1, 2, 5)
    kvc = compress_chunks(rows.reshape(b, 2, N_KV_HEADS, n_chunk, CMP_STRIDE * HEAD_DIM).astype(BF16), w1, w2, pe)
    gate = ng.reshape(b, N_KV_HEADS, LANE)[:, :, :3 * HEADS_PER_KV].reshape(b, N_KV_HEADS, 3, HEADS_PER_KV)
    gate = gate.transpose(0, 2, 1, 3).reshape(b, 3, N_HEADS, 1)
    q3 = q.reshape(b, N_HEADS, HEAD_DIM)
    kv_new = kv.reshape(b, 1, 3 * KV_COLS)
    part, picked = sample_cmp_win(rel_bias, q3, kvc, state_win.transpose(0, 2, 3, 4, 1), kv_new, gate, past)
    picked = picked[:, :, :SEL_TOPK].reshape(-1)
    out = sample_slc(page_table, picked, rel_bias, q3, kv_new, gate, part, cache_slc.transpose(0, 2, 3, 4, 1), past)
    return out.reshape(b, ATTN_WIDTH).astype(BF16)


def kernel(x_prompt, x_sample, cache_cmp_kv, cache_slc_kv, state_win_kv, page_table, rel_bias, norm_mix, w_in, pe_cmp, w_cmp1, w_cmp2, gm_ln_g, gm_ln_b, gm_w_s, gm_b_s, w_br_attn, w_br_gmlp, w_out, norm_ffn, w_router, b_router, w_gate, b_gate, w_up, b_up, w_down, b_down, norm_final):
    b, t, _ = x_prompt.shape
    bs, ts, _ = x_sample.shape
    depth = norm_mix.shape[0]
    assert depth == 1 and ts == 1 and cache_slc_kv.shape[2] == SLC_PER_PAGE * SEL_BLOCK
    n_p, n_s = b * t, bs * ts
    kv_tail = (2, N_KV_HEADS, HEAD_DIM)
    lyr = 0

    xp = x_prompt.reshape(n_p, D_MODEL)
    xs = x_sample.reshape(n_s, D_MODEL)
    weights = _in_weights(w_in[lyr])
    qp, kvp, ngp, uvp, mgp = _project(rmsnorm_rows(xp, norm_mix[lyr], 512), weights, BF16, HEAD_DIM ** -0.5 * LOG2_E)
    qs, kvs, ngs, uvs, mgs = _project(rmsnorm_rows(xs, norm_mix[lyr], n_s), weights, F32, HEAD_DIM ** -0.5)

    attn_p = _prompt_mixer_a(qp, kvp, ngp, rel_bias, pe_cmp[lyr], w_cmp1[lyr], w_cmp2[lyr], b, t)
    attn_s = _sample_mixer_a(qs, kvs, ngs, rel_bias, pe_cmp[lyr], w_cmp1[lyr], w_cmp2[lyr], cache_cmp_kv[lyr],
                             cache_slc_kv[lyr], state_win_kv[lyr], page_table)

    gm_p = gmlp_chunks(uvp, gm_ln_g[lyr], gm_ln_b[lyr], gm_w_s[lyr], gm_b_s[lyr], 4 * CHUNK)
    vn_s, gm_s = gmlp_first_row(uvs, gm_ln_g[lyr], gm_ln_b[lyr], gm_w_s[lyr], gm_b_s[lyr])

    wa, wb, wo = w_br_attn[lyr].astype(BF16), w_br_gmlp[lyr].astype(BF16), w_out[lyr].astype(BF16)
    n_all = n_p + n_s
    n_rows = (n_all * TOP_K + N_EXPERTS * (MOE_TM - 1) + MOE_TM - 1) // MOE_TM * MOE_TM
    copies = -(-n_rows // n_p)
    router = (norm_ffn[lyr], w_router[lyr], b_router[lyr])
    hp, hnp, idxp, wtp = merge_and_route(attn_p, gm_p, mgp, xp, wa, wb, wo, *router, 256, copies)
    hs, hns, idxs, wts = merge_and_route(attn_s, gm_s, mgs, xs, wa, wb, wo, *router, 256, 1)
    top_i = jnp.concatenate([idxp[:, :TOP_K], idxs[:, :TOP_K]], axis=0)
    pos, row_pair, tiles = route_rows(top_i, n_rows, n_p)
    x_rows = hnp.reshape(copies * n_p, D_MODEL)[row_pair]
    x_rows = x_rows.at[pos[n_p:].reshape(-1)].set(jnp.repeat(hns[0], TOP_K, axis=0), unique_indices=True)
    y_rows = moe_experts(x_rows, tiles, w_gate[lyr], b_gate[lyr], w_up[lyr], b_up[lyr], w_down[lyr], b_down[lyr])
    y_p = combine_and_norm(hp, y_rows[pos[:n_p].T], wtp, norm_final, 256)
    y_s = combine_and_norm(hs, y_rows[pos[n_p:].T], wts, norm_final, 256)

    kvp5 = kvp.reshape(b, t, 3, *kv_tail)
    kvs5 = kvs.reshape(bs, ts, 3, *kv_tail)
    win_buf = state_win_kv.shape[2]
    win_s = jnp.concatenate([state_win_kv[lyr], kvs5[:, :, 2]], axis=1)[:, ts:]
    return (y_p.reshape(b, t, D_MODEL), y_s.reshape(bs, ts, D_MODEL),
            kvp5[:, :, 0][None], kvs5[:, :, 0][None], kvp5[:, :, 1][None], kvs5[:, :, 1][None],
            kvp5[:, t - min(WINDOW, t):, 2][None], win_s[:, -win_buf:][None],
            vn_s.reshape(1, bs, ts, GM_WIDTH))
```

```python
import functools
import math

import jax
import jax.numpy as jnp
import numpy as np
from jax import lax
from jax.experimental import pallas as pl
from jax.experimental.pallas import tpu as pltpu

D_MODEL = 2048
N_HEADS = 16
HEAD_DIM = 64
N_KV_HEADS = 2
HEADS_PER_KV = N_HEADS // N_KV_HEADS
ATTN_WIDTH = N_HEADS * HEAD_DIM
CMP_BLOCK = 32
CMP_STRIDE = 16
CMP_HID = 2 * HEAD_DIM
SEL_BLOCK = 64
SEL_TOPK = 16
WINDOW = 512
GM_WIDTH = D_MODEL // 2
GM_GROUPS = 8
GM_GROUP_DIM = GM_WIDTH // GM_GROUPS
CHUNK = 128
N_BUCKETS = 32
REL_MAX_DIST = 128
N_EXPERTS = 32
TOP_K = 4
D_FF = D_MODEL
SWIGLU_LIMIT = 7.0
SWIGLU_ALPHA = 1.702
RMS_EPS = 1e-5
LN_EPS = 1e-5
NEG_INF = -1e30
LOG2_E = math.log2(math.e)
KV_COLS = 2 * N_KV_HEADS * HEAD_DIM
GATE_COLS = 3 * N_HEADS
SLC_PER_PAGE = 2

LANE = 128
SUBLANE = 8
VMEM_LIMIT = 56 * 1024 * 1024

MOE_TM = 256
MOE_TN = 1024
MOE_DMA_SPLIT = 4
FEAT = 2 * HEAD_DIM

BF16 = jnp.bfloat16
F32 = jnp.float32


def _bucket_upper_bounds():
    d = np.arange(0, 4 * REL_MAX_DIST)
    exact = N_BUCKETS // 2
    out = []
    for dt in (np.float32, np.float64):
        far = exact + (np.log(np.maximum(d, exact).astype(dt) / dt(exact)) / dt(math.log(REL_MAX_DIST / exact))
                       * dt(N_BUCKETS - exact)).astype(np.int32)
        out.append(np.where(d < exact, d, np.minimum(far, N_BUCKETS - 1)))
    assert (out[0] == out[1]).all() and (np.diff(out[0]) >= 0).all()
    b = out[0]
    return [int(d[b <= k].max()) for k in range(N_BUCKETS - 1)]


BUCKET_HI = _bucket_upper_bounds()
FAR_DIST = BUCKET_HI[-1] + 1


def _rel_bias(d, rb_ref, h):
    acc = jnp.full(d.shape, rb_ref[N_BUCKETS - 1, h], F32)
    for k in reversed(range(N_BUCKETS - 1)):
        acc = jnp.where(d <= BUCKET_HI[k], rb_ref[k, h], acc)
    return acc


def _dot(a, b, **kw):
    return jnp.dot(a, b, preferred_element_type=F32, **kw)


def _dot_nt(a, b):
    return lax.dot_general(a, b, (((1,), (1,)), ((), ())), preferred_element_type=F32)


def _gelu(x):
    return 0.5 * x * (1.0 + lax.erf(x * np.float32(math.sqrt(0.5))))


def _params(*sem):
    return pltpu.CompilerParams(dimension_semantics=sem, vmem_limit_bytes=VMEM_LIMIT)


def _rmsnorm_kernel(x_ref, g_ref, o_ref):
    x = x_ref[...]
    ms = jnp.mean(x * x, axis=-1, keepdims=True)
    o_ref[...] = (x * lax.rsqrt(ms + RMS_EPS) * g_ref[...]).astype(o_ref.dtype)


def rmsnorm_rows(x, g, tm):
    m, d = x.shape
    return pl.pallas_call(
        _rmsnorm_kernel, grid=(m // tm,), name="rmsnorm",
        in_specs=[pl.BlockSpec((tm, d), lambda i: (i, 0)), pl.BlockSpec((1, d), lambda i: (0, 0))],
        out_specs=pl.BlockSpec((tm, d), lambda i: (i, 0)),
        out_shape=jax.ShapeDtypeStruct((m, d), BF16), compiler_params=_params("parallel"))(x, g.reshape(1, d))


def _mm_kernel(a_ref, w_ref, o_ref, *, epilogue):
    o_ref[...] = epilogue(_dot(a_ref[...], w_ref[...])).astype(o_ref.dtype)


def matmul(a, w, epilogue, out_dtype, tm, tn, name):
    m, k = a.shape
    n = w.shape[1]
    tm, tn = min(tm, m), min(tn, n)
    return pl.pallas_call(
        functools.partial(_mm_kernel, epilogue=epilogue), grid=(m // tm, n // tn), name=name,
        in_specs=[pl.BlockSpec((tm, k), lambda i, j: (i, 0)), pl.BlockSpec((k, tn), lambda i, j: (0, j))],
        out_specs=pl.BlockSpec((tm, tn), lambda i, j: (i, j)),
        out_shape=jax.ShapeDtypeStruct((m, n), out_dtype), compiler_params=_params("parallel", "parallel"))(a, w)


def _compress_kernel(c_ref, w1_ref, w2_ref, pe_ref, o_ref):
    n = c_ref.shape[0]
    half = CMP_STRIDE * HEAD_DIM
    c = c_ref[...]
    first = _dot(c, w1_ref[0:half, :])
    second = _dot(c, w1_ref[half:2 * half, :])
    pe = _dot(pe_ref[...].astype(BF16), w1_ref[...])[0:1, :]
    hid = _gelu(first + pltpu.roll(second, n - 1, 0) + pe)
    o_ref[...] = _dot(hid.astype(BF16), w2_ref[...])


def compress_chunks(chunks, w1, w2, pe):
    b, _, g, n, width = chunks.shape
    pe_rows = jnp.broadcast_to(pe.reshape(2, 1, CMP_BLOCK * HEAD_DIM), (2, SUBLANE, CMP_BLOCK * HEAD_DIM))
    return pl.pallas_call(
        _compress_kernel, grid=(b, 2, g), name="compress",
        in_specs=[pl.BlockSpec((None, None, None, n, width), lambda i, s, j: (i, s, j, 0, 0)),
                  pl.BlockSpec((None, CMP_BLOCK * HEAD_DIM, CMP_HID), lambda i, s, j: (s, 0, 0)),
                  pl.BlockSpec((None, CMP_HID, HEAD_DIM), lambda i, s, j: (s, 0, 0)),
                  pl.BlockSpec((None, SUBLANE, CMP_BLOCK * HEAD_DIM), lambda i, s, j: (s, 0, 0))],
        out_specs=pl.BlockSpec((None, None, None, n, HEAD_DIM), lambda i, s, j: (i, s, j, 0, 0)),
        out_shape=jax.ShapeDtypeStruct((b, 2, g, n, HEAD_DIM), F32),
        compiler_params=_params("parallel", "parallel", "parallel"))(chunks, w1.astype(BF16), w2.astype(BF16), pe_rows)


def _sel_overlap_rows(n_rows, n_cmp, pad):
    nf = FEAT - HEAD_DIM
    cs = (np.arange(n_rows) - pad)[:, None] * CMP_STRIDE
    ss = np.arange(nf)[None, :] * SEL_BLOCK
    ov = np.minimum(cs + CMP_BLOCK, ss + SEL_BLOCK) - np.maximum(cs, ss)
    ov = np.maximum(ov, 0) / CMP_STRIDE
    valid = (np.arange(n_rows) >= pad) & (np.arange(n_rows) < pad + n_cmp)
    return (ov * valid[:, None]).astype(np.float32)


TQ = 256
T_PAD = TQ // CMP_STRIDE
T_NEAR = 2 * T_PAD
V_ROWS = HEAD_DIM + 16
assert FAR_DIST <= TQ and WINDOW % TQ == 0


def _bias_tables_t_kernel(rb_ref, tz_ref, tc_ref):
    j = lax.broadcasted_iota(jnp.int32, (TQ, TQ), 0)
    i = lax.broadcasted_iota(jnp.int32, (TQ, TQ), 1)
    cc = lax.broadcasted_iota(jnp.int32, (T_NEAR, TQ), 0)
    ic = lax.broadcasted_iota(jnp.int32, (T_NEAR, TQ), 1)
    d0 = i - j
    d1 = d0 + TQ
    dc = ic + (TQ - CMP_BLOCK + 1) - CMP_STRIDE * cc
    for h in range(N_HEADS):
        last = rb_ref[N_BUCKETS - 1, h]
        tz_ref[0, h] = jnp.where(d0 < 0, NEG_INF, (_rel_bias(d0, rb_ref, h) - last) * LOG2_E)
        tz_ref[1, h] = (_rel_bias(d1, rb_ref, h) - last) * LOG2_E
        tc_ref[h] = jnp.where(dc < 0, NEG_INF, (_rel_bias(dc, rb_ref, h) - last) * LOG2_E)


def bias_tables_t(rel_bias):
    return pl.pallas_call(
        _bias_tables_t_kernel, name="bias_tables",
        in_specs=[pl.BlockSpec(memory_space=pltpu.SMEM)],
        out_specs=[pl.BlockSpec(memory_space=pltpu.VMEM), pl.BlockSpec(memory_space=pltpu.VMEM)],
        out_shape=[jax.ShapeDtypeStruct((2, N_HEADS, TQ, TQ), F32),
                   jax.ShapeDtypeStruct((N_HEADS, T_NEAR, TQ), F32)],
        compiler_params=pltpu.CompilerParams(vmem_limit_bytes=VMEM_LIMIT))(rel_bias)


def _prompt_attn_t_kernel(qt_ref, gate_ref, kc_ref, vct_ref, ovt_ref, tc_ref, tz_ref, ks_ref, vst_ref, kw_ref, vwt_ref,
                          o_ref, qa_ref, lc_ref, m_ref, acc_ref, ot_ref):
    qi = pl.program_id(2)
    nf = FEAT - HEAD_DIM
    hp = HEADS_PER_KV
    for h in range(hp):
        qa_ref[h, 0:HEAD_DIM, :] = qt_ref[h * HEAD_DIM:(h + 1) * HEAD_DIM, :]
        qa_ref[h, HEAD_DIM:FEAT, :] = jnp.zeros((nf, TQ), BF16)
    t_lane = qi * TQ + lax.broadcasted_iota(jnp.int32, (1, TQ), 1)

    near0 = pl.multiple_of(qi * T_PAD, T_PAD)
    rc = kc_ref.shape[0]
    row = lax.broadcasted_iota(jnp.int32, (rc, 1), 0)
    row_ok = (row >= T_PAD) & (row < near0 + T_NEAR)
    has_block = t_lane >= CMP_BLOCK - 1
    kc = kc_ref[...].astype(BF16)
    vct = vct_ref[...].astype(BF16)
    for h in range(hp):
        lc_ref[h] = _dot(kc, qa_ref[h])
    for h in range(hp):
        lc_ref[h, pl.ds(near0, T_NEAR), :] = lc_ref[h, pl.ds(near0, T_NEAR), :] + tc_ref[h]
    p_heads = jnp.zeros((rc, TQ), F32)
    for h in range(hp):
        lc = jnp.where(row_ok, lc_ref[h], NEG_INF)
        p = jnp.exp2(lc - jnp.max(lc, axis=0, keepdims=True))
        p = p * jnp.where(has_block, 1.0 / jnp.sum(p, axis=0, keepdims=True), 0.0)
        p_heads = p_heads + p
        ot_ref[h * HEAD_DIM:(h + 1) * HEAD_DIM, :] = gate_ref[h:h + 1, :] * _dot(vct, p.astype(BF16))
    p_slc = _dot(ovt_ref[...], p_heads, precision=lax.Precision.HIGHEST)

    blk = lax.broadcasted_iota(jnp.int32, (nf, TQ), 0)
    jt = t_lane // SEL_BLOCK
    forced = (blk == 0) | (blk == jt) | (blk == jt - 1)
    score = jnp.where(forced, jnp.inf, jnp.where(blk <= jt, p_slc, -jnp.inf))
    n_grp = nf // SUBLANE
    groups = [score[SUBLANE * g:SUBLANE * (g + 1), :] for g in range(n_grp)]
    ranks = [jnp.zeros((SUBLANE, TQ), jnp.int32) for _ in range(n_grp)]
    sub = lax.broadcasted_iota(jnp.int32, (SUBLANE, TQ), 0)
    for c in range(nf):
        cg, cs = divmod(c, SUBLANE)
        other = jnp.broadcast_to(groups[cg][cs:cs + 1, :], (SUBLANE, TQ))
        for g in range(n_grp):
            if g > cg:
                beats = (other >= groups[g]).astype(jnp.int32)
            elif g < cg:
                beats = (other > groups[g]).astype(jnp.int32)
            else:
                beats = jnp.where(sub > cs, (other >= groups[g]).astype(jnp.int32), (other > groups[g]).astype(jnp.int32))
            ranks[g] = ranks[g] + beats
    feat = jnp.concatenate(
        [jnp.where((ranks[g] < SEL_TOPK) & (groups[g] > NEG_INF), 0.0, NEG_INF) for g in range(n_grp)], axis=0)
    feat = feat.astype(BF16)
    for h in range(hp):
        qa_ref[h, HEAD_DIM:FEAT, :] = feat

    def reset():
        m_ref[...] = jnp.full(m_ref.shape, -3e38, F32)
        acc_ref[...] = jnp.zeros(acc_ref.shape, F32)

    def step(k_ref, vt_ref, j, bias):
        k = k_ref[pl.ds(pl.multiple_of(j * TQ, TQ), TQ), :]
        vt = vt_ref[j]
        m_old = [m_ref[h] for h in range(hp)]
        acc_old = [acc_ref[h] for h in range(hp)]
        s = [_dot(k, qa_ref[h]) for h in range(hp)]
        if bias is not None:
            s = [s[h] + bias(h) for h in range(hp)]
        m_new = [jnp.maximum(m_old[h], jnp.max(s[h], axis=0, keepdims=True)) for h in range(hp)]
        pv = [_dot(vt, jnp.exp2(s[h] - m_new[h]).astype(BF16)) for h in range(hp)]
        for h in range(hp):
            acc_ref[h] = jnp.exp2(m_old[h] - m_new[h]) * acc_old[h] + pv[h]
            m_ref[h] = m_new[h]

    def add_result(branch):
        for h in range(hp):
            acc = acc_ref[h]
            o = acc[0:HEAD_DIM, :] * (1.0 / acc[HEAD_DIM:HEAD_DIM + 1, :])
            rows = slice(h * HEAD_DIM, (h + 1) * HEAD_DIM)
            ot_ref[rows, :] = ot_ref[rows, :] + gate_ref[branch * hp + h:branch * hp + h + 1, :] * o

    diag = lambda h: tz_ref[0, h]
    prev = lambda h: tz_ref[1, h]

    reset()

    def far_step(j, carry):
        step(ks_ref, vst_ref, j, None)
        return carry

    lax.fori_loop(0, jnp.maximum(qi - 1, 0), far_step, 0)

    @pl.when(qi >= 1)
    def _():
        step(ks_ref, vst_ref, qi - 1, prev)

    step(ks_ref, vst_ref, qi, diag)
    add_result(1)

    reset()
    n_back = WINDOW // TQ
    edge = jnp.where(lax.broadcasted_iota(jnp.int32, (TQ, TQ), 0) >= lax.broadcasted_iota(jnp.int32, (TQ, TQ), 1),
                     0.0, NEG_INF)
    for back in range(n_back, -1, -1):
        bias = (lambda h: edge) if back == n_back else prev if back == 1 else diag if back == 0 else None

        @pl.when(qi >= back)
        def _(back=back, bias=bias):
            step(kw_ref, vwt_ref, qi - back, bias)

    add_result(2)
    o_ref[...] = ot_ref[...].T.astype(o_ref.dtype)


def prompt_attention_t(qt, gate_t, kc, vct, tc, tz, ks, vst, kw, vwt, b, t):
    n_chunk = t // CMP_STRIDE
    rc = kc.shape[2]
    nf = FEAT - HEAD_DIM
    nq = t // TQ
    assert t // SEL_BLOCK <= nf and t % TQ == 0 and rc >= T_PAD * (nq + 1) and rc >= T_PAD + n_chunk - 1
    ovt = jnp.asarray(_sel_overlap_rows(rc, n_chunk - 1, T_PAD).T)
    hp = HEADS_PER_KV
    gw = hp * HEAD_DIM
    k_spec = pl.BlockSpec((None, None, t, FEAT), lambda i, g, qi: (i, g, 0, 0))
    v_spec = pl.BlockSpec((None, None, nq, V_ROWS, TQ), lambda i, g, qi: (i, g, 0, 0, 0))
    return pl.pallas_call(
        _prompt_attn_t_kernel, grid=(b, N_KV_HEADS, nq), name="prompt_attention",
        in_specs=[pl.BlockSpec((gw, TQ), lambda i, g, qi: (g, i * nq + qi)),
                  pl.BlockSpec((4 * hp, TQ), lambda i, g, qi: (g, i * nq + qi)),
                  pl.BlockSpec((None, None, rc, FEAT), lambda i, g, qi: (i, g, 0, 0)),
                  pl.BlockSpec((None, None, HEAD_DIM, rc), lambda i, g, qi: (i, g, 0, 0)),
                  pl.BlockSpec((nf, rc), lambda i, g, qi: (0, 0)),
                  pl.BlockSpec((hp, T_NEAR, TQ), lambda i, g, qi: (g, 0, 0)),
                  pl.BlockSpec((2, hp, TQ, TQ), lambda i, g, qi: (0, g, 0, 0)),
                  k_spec, v_spec, k_spec, v_spec],
        out_specs=pl.BlockSpec((TQ, gw), lambda i, g, qi: (i * nq + qi, g)),
        out_shape=jax.ShapeDtypeStruct((b * t, ATTN_WIDTH), BF16),
        scratch_shapes=[pltpu.VMEM((hp, FEAT, TQ), BF16), pltpu.VMEM((hp, rc, TQ), F32),
                        pltpu.VMEM((hp, 1, TQ), F32), pltpu.VMEM((hp, V_ROWS, TQ), F32), pltpu.VMEM((gw, TQ), F32)],
        compiler_params=_params("parallel", "parallel", "arbitrary"))(
            qt, gate_t, kc, vct, ovt, tc, tz, ks, vst, kw, vwt)


def _gmlp_kernel(u_ref, v_ref, lng_ref, lnb_ref, ws_ref, bs_ref, o_ref):
    v = v_ref[...].astype(F32)
    mu = jnp.mean(v, axis=-1, keepdims=True)
    var = jnp.mean(jnp.square(v - mu), axis=-1, keepdims=True)
    vn = ((v - mu) * lax.rsqrt(var + LN_EPS) * lng_ref[...] + lnb_ref[...]).astype(BF16)
    tri = (lax.broadcasted_iota(jnp.int32, (CHUNK, CHUNK), 0) >= lax.broadcasted_iota(jnp.int32, (CHUNK, CHUNK), 1))
    for g in range(GM_GROUPS):
        w = jnp.where(tri, ws_ref[g], 0.0).astype(BF16)
        cols = slice(g * GM_GROUP_DIM, (g + 1) * GM_GROUP_DIM)
        for c in range(u_ref.shape[0] // CHUNK):
            rows = slice(c * CHUNK, (c + 1) * CHUNK)
            s = _dot(w, vn[rows, cols]) + bs_ref[:, g:g + 1]
            o_ref[rows, cols] = (u_ref[rows, cols].astype(F32) * s).astype(o_ref.dtype)


def gmlp_chunks(uv, ln_g, ln_b, w_s, b_s, rows):
    m = uv.shape[0]
    return pl.pallas_call(
        _gmlp_kernel, grid=(m // rows,), name="gmlp",
        in_specs=[pl.BlockSpec((rows, GM_WIDTH), lambda i: (i, 0)), pl.BlockSpec((rows, GM_WIDTH), lambda i: (i, 1)),
                  pl.BlockSpec((1, GM_WIDTH), lambda i: (0, 0)), pl.BlockSpec((1, GM_WIDTH), lambda i: (0, 0)),
                  pl.BlockSpec((GM_GROUPS, CHUNK, CHUNK), lambda i: (0, 0, 0)),
                  pl.BlockSpec((CHUNK, GM_GROUPS), lambda i: (0, 0))],
        out_specs=pl.BlockSpec((rows, GM_WIDTH), lambda i: (i, 0)),
        out_shape=jax.ShapeDtypeStruct((m, GM_WIDTH), BF16),
        compiler_params=_params("parallel"))(uv, uv, ln_g.reshape(1, -1), ln_b.reshape(1, -1), w_s, b_s.T)


def _merge_kernel(attn_ref, gm_ref, mg_ref, x_ref, wa_ref, wb_ref, wo_ref, gn_ref, wr_ref, br_ref,
                  h_ref, hn_ref, idx_ref, wt_ref):
    tm = x_ref.shape[0]
    sub = min(MERGE_SUB, tm)
    parts = [slice(r0, r0 + sub) for r0 in range(0, tm, sub)]
    ta = [_dot(attn_ref[p, :], wa_ref[...]) for p in parts]
    tb = [_dot(gm_ref[p, :], wb_ref[...]) for p in parts]
    merged = [(mg_ref[p, 0:D_MODEL].astype(F32) * ta[i] + mg_ref[p, D_MODEL:2 * D_MODEL].astype(F32) * tb[i])
              .astype(BF16) for i, p in enumerate(parts)]
    h = [x_ref[p, :] + _dot(merged[i], wo_ref[...]) for i, p in enumerate(parts)]
    hn = [hh * lax.rsqrt(jnp.mean(hh * hh, axis=-1, keepdims=True) + RMS_EPS) * gn_ref[...] for hh in h]
    hi = [v.astype(BF16) for v in hn]
    lo = [(v - hi[i].astype(F32)).astype(BF16) for i, v in enumerate(hn)]
    logits = [_dot(hi[i], wr_ref[0]) + _dot(lo[i], wr_ref[0]) + _dot(hi[i], wr_ref[1]) + br_ref[...]
              for i in range(len(parts))]
    routed = [_route(lg) for lg in logits]
    for i, p in enumerate(parts):
        h_ref[p, :] = h[i]
        for c in range(hn_ref.shape[0]):
            hn_ref[c, p, :] = hi[i]
        idx_ref[p, :] = routed[i][0]
        wt_ref[p, :] = routed[i][1]


def _route(logits):
    lane = lax.broadcasted_iota(jnp.int32, logits.shape, 1)
    vals, idxs = [], []
    for _ in range(TOP_K):
        best = jnp.max(logits, axis=-1, keepdims=True)
        ix = jnp.min(jnp.where(logits == best, lane, LANE), axis=-1, keepdims=True)
        vals.append(best)
        idxs.append(ix)
        logits = jnp.where(lane == ix, -jnp.inf, logits)
    ex = [jnp.exp(v - vals[0]) for v in vals]
    inv = 1.0 / functools.reduce(lambda a, b: a + b, ex)
    idx_out = jnp.zeros(lane.shape, jnp.int32)
    wt_out = jnp.zeros(lane.shape, F32)
    for k in range(TOP_K):
        idx_out = jnp.where(lane == k, idxs[k], idx_out)
        wt_out = jnp.where(lane == k, ex[k] * inv, wt_out)
    return idx_out, wt_out


MERGE_SUB = 128


def merge_and_route(attn, gm, mg, x, wa, wb, wo, g_ffn, w_router, b_router, tm, copies):
    m = x.shape[0]
    tm = min(tm, m)
    wr = jnp.zeros((D_MODEL, LANE), F32).at[:, :N_EXPERTS].set(w_router)
    wr_hi = wr.astype(BF16)
    wr = jnp.stack([wr_hi, (wr - wr_hi.astype(F32)).astype(BF16)])
    br = jnp.full((1, LANE), -jnp.inf, F32).at[0, :N_EXPERTS].set(b_router)
    row = lambda w: pl.BlockSpec((tm, w), lambda i: (i, 0))
    full = lambda a: pl.BlockSpec(a.shape, lambda i: (0,) * a.ndim, pipeline_mode=pl.Buffered(1))
    return pl.pallas_call(
        _merge_kernel, grid=(m // tm,), name="merge_route",
        in_specs=[row(ATTN_WIDTH), row(GM_WIDTH), row(2 * D_MODEL), row(D_MODEL), full(wa), full(wb), full(wo),
                  pl.BlockSpec((1, D_MODEL), lambda i: (0, 0)), full(wr), full(br)],
        out_specs=[row(D_MODEL), pl.BlockSpec((copies, tm, D_MODEL), lambda i: (0, i, 0)), row(LANE), row(LANE)],
        out_shape=[jax.ShapeDtypeStruct((m, D_MODEL), F32), jax.ShapeDtypeStruct((copies, m, D_MODEL), BF16),
                   jax.ShapeDtypeStruct((m, LANE), jnp.int32), jax.ShapeDtypeStruct((m, LANE), F32)],
        compiler_params=_params("parallel"))(attn, gm, mg, x, wa, wb, wo, g_ffn.reshape(1, -1), wr, br)


def _expert_tile(te_ref, last_ref, nact_ref, gidx_ref, valid_ref, ge_ref, ng_ref, w_hbm, stage_ref, cache_ref, sem,
                 o_ref, compute):
    j, r = pl.program_id(0), pl.program_id(1)
    tn = stage_ref.shape[-1]
    rows = stage_ref.shape[1]
    groups = ng_ref[0]
    slot = jnp.bitwise_and(j * groups + gidx_ref[r], 1)

    def copies(e, jj):
        cols = pl.ds(pl.multiple_of(jj * tn, tn), tn)
        band = rows // MOE_DMA_SPLIT
        return [pltpu.make_async_copy(w.at[e, c * band:(c + 1) * band, cols],
                                      stage_ref.at[i, c * band:(c + 1) * band, :], sem.at[i, c])
                for i, w in enumerate(w_hbm) for c in range(MOE_DMA_SPLIT)]

    def ahead(n):
        g, jj = gidx_ref[r] + n, j
        for _ in range(n):
            wraps = g >= groups
            g, jj = jnp.where(wraps, g - groups, g), jnp.where(wraps, jj + 1, jj)
        return ge_ref[g], jj, jj < pl.num_programs(0)

    def cast_into(dst):
        for i in range(len(w_hbm)):
            for r0 in range(0, rows, 256):
                cache_ref[dst, i, r0:r0 + 256, :] = stage_ref[i, r0:r0 + 256, :].astype(cache_ref.dtype)

    def start_if(e, jj, exists):
        @pl.when(exists)
        def _():
            for c in copies(e, jj):
                c.start()

    @pl.when((j == 0) & (r == 0))
    def _():
        for c in copies(ge_ref[0], 0):
            c.start()
        for c in copies(ge_ref[0], 0):
            c.wait()
        cast_into(0)
        start_if(*ahead(1))

    tm = o_ref.shape[0]
    half = tm // 2
    active = r < nact_ref[0]
    e1, j1, next_exists = ahead(1)
    hands_over = active & (last_ref[r] == 1) & next_exists

    def tile(m, prepare_next, s):
        if prepare_next:
            for c in copies(e1, j1):
                c.wait()
            cast_into(1 - s)
        compute(m, s)
        if m < tm:
            o_ref[m:tm, :] = jnp.zeros((tm - m, o_ref.shape[1]), o_ref.dtype)
        if prepare_next:
            start_if(*ahead(2))

    for s in (0, 1):
        for m, fits in ((tm, valid_ref[r] > half), (half, valid_ref[r] <= half)):
            here = fits & (slot == s)
            pl.when(hands_over & here)(functools.partial(tile, m, True, s))
            pl.when(active & jnp.logical_not(hands_over) & here)(functools.partial(tile, m, False, s))

    @pl.when(r >= nact_ref[0])
    def _():
        o_ref[...] = jnp.zeros(o_ref.shape, o_ref.dtype)


def _moe_up_kernel(te_ref, last_ref, nact_ref, gidx_ref, valid_ref, ge_ref, ng_ref, x_ref, wg_hbm, wu_hbm, bg_ref,
                   bu_ref, o_ref, stage_ref, cache_ref, sem):
    def compute(m, slot):
        x = x_ref[0:m, :]
        g = jnp.minimum(_dot(x, cache_ref[slot, 0]) + bg_ref[...], SWIGLU_LIMIT)
        u = jnp.clip(_dot(x, cache_ref[slot, 1]) + bu_ref[...], -SWIGLU_LIMIT, SWIGLU_LIMIT)
        o_ref[0:m, :] = ((u + 1.0) * (g * jax.nn.sigmoid(SWIGLU_ALPHA * g))).astype(o_ref.dtype)

    _expert_tile(te_ref, last_ref, nact_ref, gidx_ref, valid_ref, ge_ref, ng_ref, (wg_hbm, wu_hbm), stage_ref,
                 cache_ref, sem, o_ref, compute)


def _moe_down_kernel(te_ref, last_ref, nact_ref, gidx_ref, valid_ref, ge_ref, ng_ref, a_ref, wd_hbm, bd_ref, o_ref,
                     stage_ref, cache_ref, sem):
    def compute(m, slot):
        o_ref[0:m, :] = (_dot(a_ref[0:m, :], cache_ref[slot, 0]) + bd_ref[...]).astype(o_ref.dtype)

    _expert_tile(te_ref, last_ref, nact_ref, gidx_ref, valid_ref, ge_ref, ng_ref, (wd_hbm,), stage_ref, cache_ref, sem,
                 o_ref, compute)


def moe_experts(x_rows, tiles, w_gate, b_gate, w_up, b_up, w_down, b_down):
    rows = x_rows.shape[0]
    n_tiles = rows // MOE_TM
    row_map = lambda j, r, te, last, nact, *_: (jnp.minimum(r, nact[0] - 1), 0)
    out_map = lambda j, r, *_: (r, j)
    b_spec = pl.BlockSpec((None, 1, MOE_TN), lambda j, r, te, *_: (te[r], 0, j))
    hbm = pl.BlockSpec(memory_space=pl.ANY)

    def scratch(n_w, k):
        return [pltpu.VMEM((n_w, k, MOE_TN), F32), pltpu.VMEM((2, n_w, k, MOE_TN), BF16),
                pltpu.SemaphoreType.DMA((n_w, MOE_DMA_SPLIT))]

    act = pl.pallas_call(
        _moe_up_kernel, name="moe_up",
        grid_spec=pltpu.PrefetchScalarGridSpec(
            num_scalar_prefetch=7, grid=(D_FF // MOE_TN, n_tiles),
            in_specs=[pl.BlockSpec((MOE_TM, D_MODEL), row_map), hbm, hbm, b_spec, b_spec],
            out_specs=pl.BlockSpec((MOE_TM, MOE_TN), out_map),
            scratch_shapes=scratch(2, D_MODEL)),
        out_shape=jax.ShapeDtypeStruct((rows, D_FF), BF16),
        compiler_params=_params("arbitrary", "arbitrary"))(
            *tiles, x_rows, w_gate, w_up,
            b_gate.reshape(N_EXPERTS, 1, D_FF), b_up.reshape(N_EXPERTS, 1, D_FF))
    return pl.pallas_call(
        _moe_down_kernel, name="moe_down",
        grid_spec=pltpu.PrefetchScalarGridSpec(
            num_scalar_prefetch=7, grid=(D_MODEL // MOE_TN, n_tiles),
            in_specs=[pl.BlockSpec((MOE_TM, D_FF), row_map), hbm, b_spec],
            out_specs=pl.BlockSpec((MOE_TM, MOE_TN), out_map),
            scratch_shapes=scratch(1, D_FF)),
        out_shape=jax.ShapeDtypeStruct((rows, D_MODEL), BF16),
        compiler_params=_params("arbitrary", "arbitrary"))(
            *tiles, act, w_down, b_down.reshape(N_EXPERTS, 1, D_MODEL))


def route_rows(top_i, n_rows, n_stacked):
    n = top_i.shape[0]
    hit = top_i[:, :, None] == jnp.arange(N_EXPERTS, dtype=jnp.int32)[None, None, :]
    onehot = hit.astype(jnp.int32).sum(axis=1)
    before = jnp.cumsum(onehot, axis=0) - onehot
    counts = onehot.sum(axis=0)
    padded = (counts + MOE_TM - 1) // MOE_TM * MOE_TM
    ends = jnp.cumsum(padded)
    starts = ends - padded
    pos = jnp.sum(jnp.where(hit, (starts[None, :] + before)[:, None, :], 0), axis=-1)
    token = jnp.minimum(jnp.arange(n, dtype=jnp.int32), n_stacked - 1)
    pair_id = jnp.arange(TOP_K, dtype=jnp.int32)[None, :] * n_stacked + token[:, None]
    row_pair = jnp.arange(n_rows, dtype=jnp.int32).at[pos.reshape(-1)].set(pair_id.reshape(-1), unique_indices=True)
    n_tiles = n_rows // MOE_TM
    n_active = (ends[-1] // MOE_TM).astype(jnp.int32)
    tile_row = jnp.minimum(jnp.arange(n_tiles, dtype=jnp.int32), n_active - 1) * MOE_TM
    tile_expert = jnp.minimum((tile_row[:, None] >= ends[None, :]).astype(jnp.int32).sum(axis=-1), N_EXPERTS - 1)
    tile_id = jnp.arange(n_tiles, dtype=jnp.int32)
    tile_first = jnp.concatenate([jnp.ones((1,), jnp.int32), (tile_expert[1:] != tile_expert[:-1]).astype(jnp.int32)])
    tile_last = ((tile_id == n_active - 1) | (jnp.concatenate([tile_first[1:], jnp.zeros((1,), jnp.int32)]) == 1))
    tile_group = jnp.cumsum(tile_first) - 1
    used = counts > 0
    group_of_expert = jnp.cumsum(used.astype(jnp.int32)) - 1
    experts = jnp.arange(N_EXPERTS, dtype=jnp.int32)
    group_expert = jnp.sum(jnp.where(used[None, :] & (group_of_expert[None, :] == experts[:, None]), experts[None, :], 0),
                           axis=-1)
    tile_valid = jnp.clip((starts + counts)[tile_expert] - tile_id * MOE_TM, 0, MOE_TM)
    i32 = lambda a: a.astype(jnp.int32)
    tiles = (tile_expert, i32(tile_last), n_active.reshape(1), i32(tile_group), i32(tile_valid), i32(group_expert),
             i32(used.sum()).reshape(1))
    return pos, row_pair, tiles


def _combine_kernel(h_ref, y_ref, w_ref, g_ref, o_ref):
    h = h_ref[...]
    w = w_ref[...]
    for k in range(TOP_K):
        h = h + w[:, k:k + 1] * y_ref[k].astype(F32)
    o_ref[...] = h * lax.rsqrt(jnp.mean(h * h, axis=-1, keepdims=True) + RMS_EPS) * g_ref[...]


def combine_and_norm(h, y4, wts, g, tm):
    m = h.shape[0]
    tm = min(tm, m)
    return pl.pallas_call(
        _combine_kernel, grid=(m // tm,), name="combine_norm",
        in_specs=[pl.BlockSpec((tm, D_MODEL), lambda i: (i, 0)), pl.BlockSpec((TOP_K, tm, D_MODEL), lambda i: (0, i, 0)),
                  pl.BlockSpec((tm, LANE), lambda i: (i, 0)), pl.BlockSpec((1, D_MODEL), lambda i: (0, 0))],
        out_specs=pl.BlockSpec((tm, D_MODEL), lambda i: (i, 0)),
        out_shape=jax.ShapeDtypeStruct((m, D_MODEL), F32), compiler_params=_params("parallel"))(
            h, y4, wts, g.reshape(1, -1))


def _in_weights(w_in):
    c0 = ATTN_WIDTH
    c1 = c0 + 3 * KV_COLS
    c2 = c1 + GATE_COLS
    c3 = c2 + 2 * GM_WIDTH
    wg = w_in[:, c1:c2].reshape(D_MODEL, N_KV_HEADS, HEADS_PER_KV, 3).transpose(0, 1, 3, 2)
    wg = jnp.pad(wg.reshape(D_MODEL, N_KV_HEADS, 3 * HEADS_PER_KV), ((0, 0), (0, 0), (0, LANE - 3 * HEADS_PER_KV)))
    cast = lambda w: w.astype(BF16)
    return (cast(w_in[:, :c0]), cast(w_in[:, c0:c1]), cast(wg.reshape(D_MODEL, N_KV_HEADS * LANE)),
            cast(w_in[:, c2:c3]), cast(w_in[:, c3:]))


def _project(a, weights, wide_dtype, q_scale):
    w_q, w_kv, w_ng, w_uv, w_mg = weights
    q = matmul(a, w_q, lambda z: z * np.float32(q_scale), wide_dtype, 1024, 1024, "proj_q")
    kv = matmul(a, w_kv, lambda z: z, F32, 1024, 3 * KV_COLS, "proj_kv")
    ng = matmul(a, w_ng, jax.nn.sigmoid, F32, 1024, N_KV_HEADS * LANE, "proj_gate")
    uv = matmul(a, w_uv, _gelu, wide_dtype, 1024, 1024, "proj_uv")
    mg = matmul(a, w_mg, jax.nn.sigmoid, BF16, 1024, 1024, "proj_merge")
    return q, kv, ng, uv, mg


def _head_rows(fn):
    out = fn(0)
    out = jnp.broadcast_to(out, (HEADS_PER_KV, out.shape[1]))
    row = lax.broadcasted_iota(jnp.int32, out.shape, 0)
    for h in range(1, HEADS_PER_KV):
        out = jnp.where(row == h, fn(h), out)
    return out


def _sample_cmp_win_kernel(rb_ref, q_ref, kvc_ref, ov_ref, sw_ref, new_ref, gate_ref, o_ref, idx_ref, *, past, n_cmp):
    n_rows = kvc_ref.shape[2]
    n_blk = ov_ref.shape[1]
    win = sw_ref.shape[-1]
    win_col = 2 * KV_COLS
    lane_blk = lax.broadcasted_iota(jnp.int32, (1, n_blk), 1)
    lane_out = lax.broadcasted_iota(jnp.int32, (1, LANE), 1)
    for g in range(N_KV_HEADS):
        heads = slice(g * HEADS_PER_KV, (g + 1) * HEADS_PER_KV)
        q = q_ref[heads, :]
        qb = q.astype(BF16)

        n = lax.broadcasted_iota(jnp.int32, (1, n_rows), 1)
        d = past - (n * CMP_STRIDE + CMP_BLOCK - 1)
        logit = _dot_nt(qb, kvc_ref[0, g].astype(BF16)) + _head_rows(
            lambda h: _rel_bias(d, rb_ref, g * HEADS_PER_KV + h))
        logit = jnp.where(n < n_cmp, logit, NEG_INF)
        p = jnp.exp(logit - jnp.max(logit, axis=-1, keepdims=True))
        p = p * (1.0 / jnp.sum(p, axis=-1, keepdims=True))
        o_cmp = _dot(p.astype(BF16), kvc_ref[1, g].astype(BF16))
        p_heads = jnp.broadcast_to(jnp.sum(p, axis=0, keepdims=True), (SUBLANE, n_rows))
        p_slc = _dot(p_heads, ov_ref[...], precision=lax.Precision.HIGHEST)[0:1, :]

        score = jnp.where((lane_blk == 0) | (lane_blk == n_blk - 1), jnp.inf, p_slc)
        picked = jnp.zeros((1, LANE), jnp.int32)
        for k in range(SEL_TOPK - 1):
            best = jnp.max(score, axis=-1, keepdims=True)
            ix = jnp.min(jnp.where(score == best, lane_blk, n_blk), axis=-1, keepdims=True)
            picked = jnp.where(lane_out == k, ix, picked)
            score = jnp.where(lane_blk == ix, -jnp.inf, score)
        idx_ref[g:g + 1, :] = picked

        dw = win - lax.broadcasted_iota(jnp.int32, (1, win), 1)
        lw = _dot(qb, sw_ref[0, g].astype(BF16)) + _head_rows(
            lambda h: _rel_bias(dw, rb_ref, g * HEADS_PER_KV + h))
        k_new = new_ref[:, win_col + g * HEAD_DIM:win_col + (g + 1) * HEAD_DIM]
        v_new = new_ref[:, win_col + KV_COLS // 2 + g * HEAD_DIM:win_col + KV_COLS // 2 + (g + 1) * HEAD_DIM]
        l_new = jnp.sum(q * k_new, axis=-1, keepdims=True) + _head_rows(
            lambda h: jnp.full((1, 1), rb_ref[0, g * HEADS_PER_KV + h], F32))
        m = jnp.maximum(jnp.max(lw, axis=-1, keepdims=True), l_new)
        pw = jnp.exp(lw - m)
        p_new = jnp.exp(l_new - m)
        o_win = ((_dot_nt(pw.astype(BF16), sw_ref[1, g].astype(BF16)) + p_new * v_new)
                 * (1.0 / (jnp.sum(pw, axis=-1, keepdims=True) + p_new)))
        o_ref[heads, :] = gate_ref[0, heads, :] * o_cmp + gate_ref[2, heads, :] * o_win


def sample_cmp_win(rel_bias, q, kvc, state_win, kv_new, gate, past):
    b, _, _, n_rows, _ = kvc.shape
    n_cmp = past // CMP_STRIDE - CMP_BLOCK // CMP_STRIDE + 1
    n_blk = past // SEL_BLOCK
    assert past % SEL_BLOCK == 0 and n_blk >= SEL_TOPK and n_cmp <= n_rows
    cs = np.arange(n_rows)[:, None] * CMP_STRIDE
    ss = np.arange(n_blk)[None, :] * SEL_BLOCK
    ov = np.maximum(np.minimum(cs + CMP_BLOCK, ss + SEL_BLOCK) - np.maximum(cs, ss), 0) / CMP_STRIDE
    ov = jnp.asarray((ov * (np.arange(n_rows) < n_cmp)[:, None]).astype(np.float32))
    win = state_win.shape[-1]
    return pl.pallas_call(
        functools.partial(_sample_cmp_win_kernel, past=past, n_cmp=n_cmp), grid=(b,), name="sample_cmp_win",
        in_specs=[pl.BlockSpec(memory_space=pltpu.SMEM),
                  pl.BlockSpec((None, N_HEADS, HEAD_DIM), lambda i: (i, 0, 0)),
                  pl.BlockSpec((None, 2, N_KV_HEADS, n_rows, HEAD_DIM), lambda i: (i, 0, 0, 0, 0)),
                  pl.BlockSpec(ov.shape, lambda i: (0, 0)),
                  pl.BlockSpec((None, 2, N_KV_HEADS, HEAD_DIM, win), lambda i: (i, 0, 0, 0, 0)),
                  pl.BlockSpec((None, 1, 3 * KV_COLS), lambda i: (i, 0, 0)),
                  pl.BlockSpec((None, 3, N_HEADS, 1), lambda i: (i, 0, 0, 0))],
        out_specs=[pl.BlockSpec((None, N_HEADS, HEAD_DIM), lambda i: (i, 0, 0)),
                   pl.BlockSpec((None, N_KV_HEADS, LANE), lambda i: (i, 0, 0))],
        out_shape=[jax.ShapeDtypeStruct((b, N_HEADS, HEAD_DIM), F32),
                   jax.ShapeDtypeStruct((b, N_KV_HEADS, LANE), jnp.int32)],
        compiler_params=_params("parallel"))(rel_bias, q, kvc, ov, state_win, kv_new, gate)


def _sample_slc_kernel(pt_ref, idx_ref, rb_ref, q_ref, new_ref, gate_ref, part_ref, *rest, past):
    blocks, o_ref = rest[:-1], rest[-1]
    b, g = pl.program_id(0), pl.program_id(1)
    q = q_ref[...]
    qb = q.astype(BF16)
    slc_col = KV_COLS
    half = KV_COLS // 2
    page = blocks[0].shape[-1]
    pos = lax.broadcasted_iota(jnp.int32, (1, page), 1)
    logits, values = [], []
    for k, blk_ref in enumerate(blocks):
        ix = idx_ref[(b * N_KV_HEADS + g) * SEL_TOPK + k]
        d = past - ((ix // SLC_PER_PAGE) * page + pos)
        lg = _dot(qb, blk_ref[0].astype(BF16)) + _head_rows(lambda h: _rel_bias(d, rb_ref, g * HEADS_PER_KV + h))
        logits.append(jnp.where(pos // SEL_BLOCK == ix % SLC_PER_PAGE, lg, NEG_INF))
        values.append(blk_ref[1].astype(BF16))
    new = new_ref[...]
    k_new = jnp.where(g == 0, new[:, slc_col:slc_col + HEAD_DIM], new[:, slc_col + HEAD_DIM:slc_col + 2 * HEAD_DIM])
    v_new = jnp.where(g == 0, new[:, slc_col + half:slc_col + half + HEAD_DIM],
                      new[:, slc_col + half + HEAD_DIM:slc_col + half + 2 * HEAD_DIM])
    l_new = jnp.sum(q * k_new, axis=-1, keepdims=True) + _head_rows(
        lambda h: jnp.full((1, 1), rb_ref[0, g * HEADS_PER_KV + h], F32))
    m = l_new
    for lg in logits:
        m = jnp.maximum(m, jnp.max(lg, axis=-1, keepdims=True))
    p_new = jnp.exp(l_new - m)
    denom = p_new
    out = p_new * v_new
    for lg, vv in zip(logits, values):
        p = jnp.exp(lg - m)
        denom = denom + jnp.sum(p, axis=-1, keepdims=True)
        out = out + _dot_nt(p.astype(BF16), vv)
    o_ref[...] = part_ref[...] + gate_ref[1] * (out * (1.0 / denom))


def sample_slc(page_table, picked, rel_bias, q, kv_new, gate, part, cache_slc, past):
    b = q.shape[0]
    page = cache_slc.shape[-1]
    n_pick = SEL_TOPK - 1

    def blk_map(k):
        def index(i, g, pt, idx, k=k):
            ix = idx[(i * N_KV_HEADS + g) * SEL_TOPK + k]
            return (pt[i, ix // SLC_PER_PAGE], 0, g, 0, 0)
        return index

    head_spec = pl.BlockSpec((None, HEADS_PER_KV, HEAD_DIM), lambda i, g, pt, idx: (i, g, 0))
    return pl.pallas_call(
        functools.partial(_sample_slc_kernel, past=past), name="sample_slc",
        grid_spec=pltpu.PrefetchScalarGridSpec(
            num_scalar_prefetch=2, grid=(b, N_KV_HEADS),
            in_specs=[pl.BlockSpec(memory_space=pltpu.SMEM), head_spec,
                      pl.BlockSpec((None, 1, 3 * KV_COLS), lambda i, g, pt, idx: (i, 0, 0)),
                      pl.BlockSpec((None, 3, HEADS_PER_KV, 1), lambda i, g, pt, idx: (i, 0, g, 0)), head_spec]
            + [pl.BlockSpec((None, 2, None, HEAD_DIM, page), blk_map(k)) for k in range(n_pick)],
            out_specs=head_spec),
        out_shape=jax.ShapeDtypeStruct((b, N_HEADS, HEAD_DIM), F32),
        compiler_params=_params("parallel", "parallel"))(
            page_table, picked, rel_bias, q, kv_new, gate, part, *([cache_slc] * n_pick))


def _gmlp_row_kernel(uv_ref, lng_ref, lnb_ref, w0_ref, b0_ref, vn_ref, o_ref):
    v = uv_ref[:, GM_WIDTH:2 * GM_WIDTH]
    mu = jnp.mean(v, axis=-1, keepdims=True)
    var = jnp.mean(jnp.square(v - mu), axis=-1, keepdims=True)
    vn = (v - mu) * lax.rsqrt(var + LN_EPS) * lng_ref[...] + lnb_ref[...]
    vn_ref[...] = vn
    o_ref[...] = (uv_ref[:, 0:GM_WIDTH] * (w0_ref[...] * vn + b0_ref[...])).astype(o_ref.dtype)


def gmlp_first_row(uv, ln_g, ln_b, w_s, b_s):
    m = uv.shape[0]
    w0 = jnp.repeat(w_s[:, 0, 0], GM_GROUP_DIM).reshape(1, GM_WIDTH)
    b0 = jnp.repeat(b_s[:, 0], GM_GROUP_DIM).reshape(1, GM_WIDTH)
    return pl.pallas_call(
        _gmlp_row_kernel, name="gmlp_row",
        out_shape=[jax.ShapeDtypeStruct((m, GM_WIDTH), F32), jax.ShapeDtypeStruct((m, GM_WIDTH), BF16)])(
            uv, ln_g.reshape(1, -1), ln_b.reshape(1, -1), w0, b0)


def _group_major(x, feat):
    b, t = x.shape[0], x.shape[1]
    f = jnp.broadcast_to(feat.astype(BF16)[None, None], (b, N_KV_HEADS, t, FEAT - HEAD_DIM))
    return jnp.concatenate([x.transpose(0, 2, 1, 3).astype(BF16), f], axis=-1)


def _value_tiles(x):
    b, t = x.shape[0], x.shape[1]
    xt = x.transpose(0, 2, 3, 1).astype(BF16)
    extra = jnp.zeros((b, N_KV_HEADS, V_ROWS - HEAD_DIM, t), BF16).at[:, :, 0].set(1)
    tiles = jnp.concatenate([xt, extra], axis=2).reshape(b, N_KV_HEADS, V_ROWS, t // TQ, TQ)
    return tiles.transpose(0, 1, 3, 2, 4)


def _prompt_mixer_a(q, kv, ng, rel_bias, pe, w1, w2, b, t):
    kvr = kv.reshape(b, t, 3, 2, N_KV_HEADS, HEAD_DIM)
    n_chunk = t // CMP_STRIDE
    chunks = kvr[:, :, 0].reshape(b, n_chunk, CMP_STRIDE, 2, N_KV_HEADS, HEAD_DIM).transpose(0, 3, 4, 1, 2, 5)
    kvc = compress_chunks(chunks.reshape(b, 2, N_KV_HEADS, n_chunk, CMP_STRIDE * HEAD_DIM).astype(BF16), w1, w2, pe)
    rc = -(-(T_PAD + max(n_chunk - 1, T_PAD * t // TQ)) // LANE) * LANE
    kvc = jnp.pad(kvc[:, :, :, :n_chunk - 1], ((0, 0), (0, 0), (0, 0), (T_PAD, rc - T_PAD - n_chunk + 1), (0, 0)))
    kc = jnp.pad(kvc[:, 0], ((0, 0), (0, 0), (0, 0), (0, FEAT - HEAD_DIM)))
    nf = FEAT - HEAD_DIM
    blocks = (jnp.arange(t)[:, None] // SEL_BLOCK == jnp.arange(nf)[None, :])
    tz, tc = bias_tables_t(rel_bias)
    gate_t = ng.reshape(b * t, N_KV_HEADS, LANE)[:, :, :4 * HEADS_PER_KV].reshape(b * t, -1).T
    return prompt_attention_t(
        q.T, gate_t, kc, kvc[:, 1].transpose(0, 1, 3, 2), tc, tz,
        _group_major(kvr[:, :, 1, 0], blocks), _value_tiles(kvr[:, :, 1, 1]),
        _group_major(kvr[:, :, 2, 0], jnp.zeros((t, nf), BF16)), _value_tiles(kvr[:, :, 2, 1]), b, t)


def _sample_mixer_a(q, kv, ng, rel_bias, pe, w1, w2, cache_cmp, cache_slc, state_win, page_table):
    b, n_pages = page_table.shape
    page = cache_cmp.shape[1]
    past = n_pages * page
    n_chunk = past // CMP_STRIDE
    rows = cache_cmp[page_table].reshape(b, n_chunk, CMP_STRIDE, 2, N_KV_HEADS, HEAD_DIM).transpose(0, 3, 4, 1, 2, 5)
    kvc = compress_chunks(rows.reshape(b, 2, N_KV_HEADS, n_chunk, CMP_STRIDE * HEAD_DIM).astype(BF16), w1, w2, pe)
    gate = ng.reshape(b, N_KV_HEADS, LANE)[:, :, :3 * HEADS_PER_KV].reshape(b, N_KV_HEADS, 3, HEADS_PER_KV)
    gate = gate.transpose(0, 2, 1, 3).reshape(b, 3, N_HEADS, 1)
    q3 = q.reshape(b, N_HEADS, HEAD_DIM)
    kv_new = kv.reshape(b, 1, 3 * KV_COLS)
    part, picked = sample_cmp_win(rel_bias, q3, kvc, state_win.transpose(0, 2, 3, 4, 1), kv_new, gate, past)
    picked = picked[:, :, :SEL_TOPK].reshape(-1)
    out = sample_slc(page_table, picked, rel_bias, q3, kv_new, gate, part, cache_slc.transpose(0, 2, 3, 4, 1), past)
    return out.reshape(b, ATTN_WIDTH).astype(BF16)


def kernel(x_prompt, x_sample, cache_cmp_kv, cache_slc_kv, state_win_kv, page_table, rel_bias, norm_mix, w_in, pe_cmp, w_cmp1, w_cmp2, gm_ln_g, gm_ln_b, gm_w_s, gm_b_s, w_br_attn, w_br_gmlp, w_out, norm_ffn, w_router, b_router, w_gate, b_gate, w_up, b_up, w_down, b_down, norm_final):
    b, t, _ = x_prompt.shape
    bs, ts, _ = x_sample.shape
    depth = norm_mix.shape[0]
    assert depth == 1 and ts == 1 and cache_slc_kv.shape[2] == SLC_PER_PAGE * SEL_BLOCK
    n_p, n_s = b * t, bs * ts
    kv_tail = (2, N_KV_HEADS, HEAD_DIM)
    lyr = 0

    xp = x_prompt.reshape(n_p, D_MODEL)
    xs = x_sample.reshape(n_s, D_MODEL)
    weights = _in_weights(w_in[lyr])
    qp, kvp, ngp, uvp, mgp = _project(rmsnorm_rows(xp, norm_mix[lyr], 512), weights, BF16, HEAD_DIM ** -0.5 * LOG2_E)
    qs, kvs, ngs, uvs, mgs = _project(rmsnorm_rows(xs, norm_mix[lyr], n_s), weights, F32, HEAD_DIM ** -0.5)

    attn_p = _prompt_mixer_a(qp, kvp, ngp, rel_bias, pe_cmp[lyr], w_cmp1[lyr], w_cmp2[lyr], b, t)
    attn_s = _sample_mixer_a(qs, kvs, ngs, rel_bias, pe_cmp[lyr], w_cmp1[lyr], w_cmp2[lyr], cache_cmp_kv[lyr],
                             cache_slc_kv[lyr], state_win_kv[lyr], page_table)

    gm_p = gmlp_chunks(uvp, gm_ln_g[lyr], gm_ln_b[lyr], gm_w_s[lyr], gm_b_s[lyr], 4 * CHUNK)
    vn_s, gm_s = gmlp_first_row(uvs, gm_ln_g[lyr], gm_ln_b[lyr], gm_w_s[lyr], gm_b_s[lyr])

    wa, wb, wo = w_br_attn[lyr].astype(BF16), w_br_gmlp[lyr].astype(BF16), w_out[lyr].astype(BF16)
    n_all = n_p + n_s
    n_rows = (n_all * TOP_K + N_EXPERTS * (MOE_TM - 1) + MOE_TM - 1) // MOE_TM * MOE_TM
    copies = -(-n_rows // n_p)
    router = (norm_ffn[lyr], w_router[lyr], b_router[lyr])
    hp, hnp, idxp, wtp = merge_and_route(attn_p, gm_p, mgp, xp, wa, wb, wo, *router, 256, copies)
    hs, hns, idxs, wts = merge_and_route(attn_s, gm_s, mgs, xs, wa, wb, wo, *router, 256, 1)
    top_i = jnp.concatenate([idxp[:, :TOP_K], idxs[:, :TOP_K]], axis=0)
    pos, row_pair, tiles = route_rows(top_i, n_rows, n_p)
    x_rows = hnp.reshape(copies * n_p, D_MODEL)[row_pair]
    x_rows = x_rows.at[pos[n_p:].reshape(-1)].set(jnp.repeat(hns[0], TOP_K, axis=0), unique_indices=True)
    y_rows = moe_experts(x_rows, tiles, w_gate[lyr], b_gate[lyr], w_up[lyr], b_up[lyr], w_down[lyr], b_down[lyr])
    y_p = combine_and_norm(hp, y_rows[pos[:n_p].T], wtp, norm_final, 256)
    y_s = combine_and_norm(hs, y_rows[pos[n_p:].T], wts, norm_final, 256)

    kvp5 = kvp.reshape(b, t, 3, *kv_tail)
    kvs5 = kvs.reshape(bs, ts, 3, *kv_tail)
    win_buf = state_win_kv.shape[2]
    win_s = jnp.concatenate([state_win_kv[lyr], kvs5[:, :, 2]], axis=1)[:, ts:]
    return (y_p.reshape(b, t, D_MODEL), y_s.reshape(bs, ts, D_MODEL),
            kvp5[:, :, 0][None], kvs5[:, :, 0][None], kvp5[:, :, 1][None], kvs5[:, :, 1][None],
            kvp5[:, t - min(WINDOW, t):, 2][None], win_s[:, -win_buf:][None],
            vn_s.reshape(1, bs, ts, GM_WIDTH))
```

```python
import functools
import math

import jax
import jax.numpy as jnp
import numpy as np
from jax import lax
from jax.experimental import pallas as pl
from jax.experimental.pallas import tpu as pltpu

D_MODEL = 2048
N_HEADS = 16
HEAD_DIM = 64
N_KV_HEADS = 2
HEADS_PER_KV = N_HEADS // N_KV_HEADS
ATTN_WIDTH = N_HEADS * HEAD_DIM
CMP_BLOCK = 32
CMP_STRIDE = 16
CMP_HID = 2 * HEAD_DIM
SEL_BLOCK = 64
SEL_TOPK = 16
WINDOW = 512
GM_WIDTH = D_MODEL // 2
GM_GROUPS = 8
GM_GROUP_DIM = GM_WIDTH // GM_GROUPS
CHUNK = 128
N_BUCKETS = 32
REL_MAX_DIST = 128
N_EXPERTS = 32
TOP_K = 4
D_FF = D_MODEL
SWIGLU_LIMIT = 7.0
SWIGLU_ALPHA = 1.702
RMS_EPS = 1e-5
LN_EPS = 1e-5
NEG_INF = -1e30
LOG2_E = math.log2(math.e)
KV_COLS = 2 * N_KV_HEADS * HEAD_DIM
GATE_COLS = 3 * N_HEADS
SLC_PER_PAGE = 2

LANE = 128
SUBLANE = 8
VMEM_LIMIT = 56 * 1024 * 1024

MOE_TM = 256
MOE_TN = 1024
MOE_TN_DOWN = 2048
MOE_DMA_SPLIT = 4
FEAT = 2 * HEAD_DIM

BF16 = jnp.bfloat16
F32 = jnp.float32


def _bucket_upper_bounds():
    d = np.arange(0, 4 * REL_MAX_DIST)
    exact = N_BUCKETS // 2
    out = []
    for dt in (np.float32, np.float64):
        far = exact + (np.log(np.maximum(d, exact).astype(dt) / dt(exact)) / dt(math.log(REL_MAX_DIST / exact))
                       * dt(N_BUCKETS - exact)).astype(np.int32)
        out.append(np.where(d < exact, d, np.minimum(far, N_BUCKETS - 1)))
    assert (out[0] == out[1]).all() and (np.diff(out[0]) >= 0).all()
    b = out[0]
    return [int(d[b <= k].max()) for k in range(N_BUCKETS - 1)]


BUCKET_HI = _bucket_upper_bounds()
FAR_DIST = BUCKET_HI[-1] + 1


def _rel_bias(d, rb_ref, h):
    acc = jnp.full(d.shape, rb_ref[N_BUCKETS - 1, h], F32)
    for k in reversed(range(N_BUCKETS - 1)):
        acc = jnp.where(d <= BUCKET_HI[k], rb_ref[k, h], acc)
    return acc


def _dot(a, b, **kw):
    return jnp.dot(a, b, preferred_element_type=F32, **kw)


def _dot_nt(a, b):
    return lax.dot_general(a, b, (((1,), (1,)), ((), ())), preferred_element_type=F32)


def _gelu(x):
    return 0.5 * x * (1.0 + lax.erf(x * np.float32(math.sqrt(0.5))))


def _params(*sem):
    return pltpu.CompilerParams(dimension_semantics=sem, vmem_limit_bytes=VMEM_LIMIT)


def _rmsnorm_kernel(x_ref, g_ref, o_ref):
    x = x_ref[...]
    ms = jnp.mean(x * x, axis=-1, keepdims=True)
    o_ref[...] = (x * lax.rsqrt(ms + RMS_EPS) * g_ref[...]).astype(o_ref.dtype)


def rmsnorm_rows(x, g, tm):
    m, d = x.shape
    return pl.pallas_call(
        _rmsnorm_kernel, grid=(m // tm,), name="rmsnorm",
        in_specs=[pl.BlockSpec((tm, d), lambda i: (i, 0)), pl.BlockSpec((1, d), lambda i: (0, 0))],
        out_specs=pl.BlockSpec((tm, d), lambda i: (i, 0)),
        out_shape=jax.ShapeDtypeStruct((m, d), BF16), compiler_params=_params("parallel"))(x, g.reshape(1, d))


def _mm_kernel(a_ref, w_ref, o_ref, *, epilogue):
    o_ref[...] = epilogue(_dot(a_ref[...], w_ref[...])).astype(o_ref.dtype)


def matmul(a, w, epilogue, out_dtype, tm, tn, name):
    m, k = a.shape
    n = w.shape[1]
    tm, tn = min(tm, m), min(tn, n)
    return pl.pallas_call(
        functools.partial(_mm_kernel, epilogue=epilogue), grid=(m // tm, n // tn), name=name,
        in_specs=[pl.BlockSpec((tm, k), lambda i, j: (i, 0)), pl.BlockSpec((k, tn), lambda i, j: (0, j))],
        out_specs=pl.BlockSpec((tm, tn), lambda i, j: (i, j)),
        out_shape=jax.ShapeDtypeStruct((m, n), out_dtype), compiler_params=_params("parallel", "parallel"))(a, w)


def _compress_kernel(c_ref, w1_ref, w2_ref, pe_ref, o_ref):
    n = c_ref.shape[0]
    half = CMP_STRIDE * HEAD_DIM
    c = c_ref[...]
    first = _dot(c, w1_ref[0:half, :])
    second = _dot(c, w1_ref[half:2 * half, :])
    pe = _dot(pe_ref[...].astype(BF16), w1_ref[...])[0:1, :]
    hid = _gelu(first + pltpu.roll(second, n - 1, 0) + pe)
    o_ref[...] = _dot(hid.astype(BF16), w2_ref[...])


def compress_chunks(chunks, w1, w2, pe):
    b, _, g, n, width = chunks.shape
    pe_rows = jnp.broadcast_to(pe.reshape(2, 1, CMP_BLOCK * HEAD_DIM), (2, SUBLANE, CMP_BLOCK * HEAD_DIM))
    return pl.pallas_call(
        _compress_kernel, grid=(b, 2, g), name="compress",
        in_specs=[pl.BlockSpec((None, None, None, n, width), lambda i, s, j: (i, s, j, 0, 0)),
                  pl.BlockSpec((None, CMP_BLOCK * HEAD_DIM, CMP_HID), lambda i, s, j: (s, 0, 0)),
                  pl.BlockSpec((None, CMP_HID, HEAD_DIM), lambda i, s, j: (s, 0, 0)),
                  pl.BlockSpec((None, SUBLANE, CMP_BLOCK * HEAD_DIM), lambda i, s, j: (s, 0, 0))],
        out_specs=pl.BlockSpec((None, None, None, n, HEAD_DIM), lambda i, s, j: (i, s, j, 0, 0)),
        out_shape=jax.ShapeDtypeStruct((b, 2, g, n, HEAD_DIM), F32),
        compiler_params=_params("parallel", "parallel", "parallel"))(chunks, w1.astype(BF16), w2.astype(BF16), pe_rows)


def _sel_overlap_rows(n_rows, n_cmp, pad):
    nf = FEAT - HEAD_DIM
    cs = (np.arange(n_rows) - pad)[:, None] * CMP_STRIDE
    ss = np.arange(nf)[None, :] * SEL_BLOCK
    ov = np.minimum(cs + CMP_BLOCK, ss + SEL_BLOCK) - np.maximum(cs, ss)
    ov = np.maximum(ov, 0) / CMP_STRIDE
    valid = (np.arange(n_rows) >= pad) & (np.arange(n_rows) < pad + n_cmp)
    return (ov * valid[:, None]).astype(np.float32)


TQ = 256
T_PAD = TQ // CMP_STRIDE
T_NEAR = 2 * T_PAD
V_ROWS = HEAD_DIM + 16
assert FAR_DIST <= TQ and WINDOW % TQ == 0


def _bias_tables_t_kernel(rb_ref, tz_ref, tc_ref):
    j = lax.broadcasted_iota(jnp.int32, (TQ, TQ), 0)
    i = lax.broadcasted_iota(jnp.int32, (TQ, TQ), 1)
    cc = lax.broadcasted_iota(jnp.int32, (T_NEAR, TQ), 0)
    ic = lax.broadcasted_iota(jnp.int32, (T_NEAR, TQ), 1)
    d0 = i - j
    d1 = d0 + TQ
    dc = ic + (TQ - CMP_BLOCK + 1) - CMP_STRIDE * cc
    for h in range(N_HEADS):
        last = rb_ref[N_BUCKETS - 1, h]
        tz_ref[0, h] = jnp.where(d0 < 0, NEG_INF, (_rel_bias(d0, rb_ref, h) - last) * LOG2_E)
        tz_ref[1, h] = (_rel_bias(d1, rb_ref, h) - last) * LOG2_E
        tc_ref[h] = jnp.where(dc < 0, NEG_INF, (_rel_bias(dc, rb_ref, h) - last) * LOG2_E)


def bias_tables_t(rel_bias):
    return pl.pallas_call(
        _bias_tables_t_kernel, name="bias_tables",
        in_specs=[pl.BlockSpec(memory_space=pltpu.SMEM)],
        out_specs=[pl.BlockSpec(memory_space=pltpu.VMEM), pl.BlockSpec(memory_space=pltpu.VMEM)],
        out_shape=[jax.ShapeDtypeStruct((2, N_HEADS, TQ, TQ), F32),
                   jax.ShapeDtypeStruct((N_HEADS, T_NEAR, TQ), F32)],
        compiler_params=pltpu.CompilerParams(vmem_limit_bytes=VMEM_LIMIT))(rel_bias)


def _prompt_attn_t_kernel(qt_ref, gate_ref, kc_ref, vct_ref, ovt_ref, tc_ref, tz_ref, ks_ref, vst_ref, kw_ref, vwt_ref,
                          o_ref, qa_ref, lc_ref, m_ref, acc_ref, ot_ref):
    qi = pl.program_id(2)
    nf = FEAT - HEAD_DIM
    hp = HEADS_PER_KV
    for h in range(hp):
        qa_ref[h, 0:HEAD_DIM, :] = qt_ref[h * HEAD_DIM:(h + 1) * HEAD_DIM, :]
        qa_ref[h, HEAD_DIM:FEAT, :] = jnp.zeros((nf, TQ), BF16)
    t_lane = qi * TQ + lax.broadcasted_iota(jnp.int32, (1, TQ), 1)

    near0 = pl.multiple_of(qi * T_PAD, T_PAD)
    rc = kc_ref.shape[0]
    row = lax.broadcasted_iota(jnp.int32, (rc, 1), 0)
    row_ok = (row >= T_PAD) & (row < near0 + T_NEAR)
    has_block = t_lane >= CMP_BLOCK - 1
    kc = kc_ref[...].astype(BF16)
    vct = vct_ref[...].astype(BF16)
    for h in range(hp):
        lc_ref[h] = _dot(kc, qa_ref[h])
    for h in range(hp):
        lc_ref[h, pl.ds(near0, T_NEAR), :] = lc_ref[h, pl.ds(near0, T_NEAR), :] + tc_ref[h]
    p_heads = jnp.zeros((rc, TQ), F32)
    for h in range(hp):
        lc = jnp.where(row_ok, lc_ref[h], NEG_INF)
        p = jnp.exp2(lc - jnp.max(lc, axis=0, keepdims=True))
        p = p * jnp.where(has_block, 1.0 / jnp.sum(p, axis=0, keepdims=True), 0.0)
        p_heads = p_heads + p
        ot_ref[h * HEAD_DIM:(h + 1) * HEAD_DIM, :] = gate_ref[h:h + 1, :] * _dot(vct, p.astype(BF16))
    p_slc = _dot(ovt_ref[...], p_heads, precision=lax.Precision.HIGHEST)

    blk = lax.broadcasted_iota(jnp.int32, (nf, TQ), 0)
    jt = t_lane // SEL_BLOCK
    forced = (blk == 0) | (blk == jt) | (blk == jt - 1)
    score = jnp.where(forced, jnp.inf, jnp.where(blk <= jt, p_slc, -jnp.inf))
    n_grp = nf // SUBLANE
    groups = [score[SUBLANE * g:SUBLANE * (g + 1), :] for g in range(n_grp)]
    ranks = [jnp.zeros((SUBLANE, TQ), jnp.int32) for _ in range(n_grp)]
    sub = lax.broadcasted_iota(jnp.int32, (SUBLANE, TQ), 0)
    for c in range(nf):
        cg, cs = divmod(c, SUBLANE)
        other = jnp.broadcast_to(groups[cg][cs:cs + 1, :], (SUBLANE, TQ))
        for g in range(n_grp):
            if g > cg:
                beats = (other >= groups[g]).astype(jnp.int32)
            elif g < cg:
                beats = (other > groups[g]).astype(jnp.int32)
            else:
                beats = jnp.where(sub > cs, (other >= groups[g]).astype(jnp.int32), (other > groups[g]).astype(jnp.int32))
            ranks[g] = ranks[g] + beats
    feat = jnp.concatenate(
        [jnp.where((ranks[g] < SEL_TOPK) & (groups[g] > NEG_INF), 0.0, NEG_INF) for g in range(n_grp)], axis=0)
    feat = feat.astype(BF16)
    for h in range(hp):
        qa_ref[h, HEAD_DIM:FEAT, :] = feat

    def reset():
        m_ref[...] = jnp.full(m_ref.shape, -3e38, F32)
        acc_ref[...] = jnp.zeros(acc_ref.shape, F32)

    def step(k_ref, vt_ref, j, bias):
        k = k_ref[pl.ds(pl.multiple_of(j * TQ, TQ), TQ), :]
        vt = vt_ref[j]
        m_old = [m_ref[h] for h in range(hp)]
        acc_old = [acc_ref[h] for h in range(hp)]
        s = [_dot(k, qa_ref[h]) for h in range(hp)]
        if bias is not None:
            s = [s[h] + bias(h) for h in range(hp)]
        m_new = [jnp.maximum(m_old[h], jnp.max(s[h], axis=0, keepdims=True)) for h in range(hp)]
        pv = [_dot(vt, jnp.exp2(s[h] - m_new[h]).astype(BF16)) for h in range(hp)]
        for h in range(hp):
            acc_ref[h] = jnp.exp2(m_old[h] - m_new[h]) * acc_old[h] + pv[h]
            m_ref[h] = m_new[h]

    def add_result(branch):
        for h in range(hp):
            acc = acc_ref[h]
            o = acc[0:HEAD_DIM, :] * (1.0 / acc[HEAD_DIM:HEAD_DIM + 1, :])
            rows = slice(h * HEAD_DIM, (h + 1) * HEAD_DIM)
            ot_ref[rows, :] = ot_ref[rows, :] + gate_ref[branch * hp + h:branch * hp + h + 1, :] * o

    diag = lambda h: tz_ref[0, h]
    prev = lambda h: tz_ref[1, h]

    reset()

    def far_step(j, carry):
        step(ks_ref, vst_ref, j, None)
        return carry

    lax.fori_loop(0, jnp.maximum(qi - 1, 0), far_step, 0)

    @pl.when(qi >= 1)
    def _():
        step(ks_ref, vst_ref, qi - 1, prev)

    step(ks_ref, vst_ref, qi, diag)
    add_result(1)

    reset()
    n_back = WINDOW // TQ
    edge = jnp.where(lax.broadcasted_iota(jnp.int32, (TQ, TQ), 0) >= lax.broadcasted_iota(jnp.int32, (TQ, TQ), 1),
                     0.0, NEG_INF)
    for back in range(n_back, -1, -1):
        bias = (lambda h: edge) if back == n_back else prev if back == 1 else diag if back == 0 else None

        @pl.when(qi >= back)
        def _(back=back, bias=bias):
            step(kw_ref, vwt_ref, qi - back, bias)

    add_result(2)
    o_ref[...] = ot_ref[...].T.astype(o_ref.dtype)


def prompt_attention_t(qt, gate_t, kc, vct, tc, tz, ks, vst, kw, vwt, b, t):
    n_chunk = t // CMP_STRIDE
    rc = kc.shape[2]
    nf = FEAT - HEAD_DIM
    nq = t // TQ
    assert t // SEL_BLOCK <= nf and t % TQ == 0 and rc >= T_PAD * (nq + 1) and rc >= T_PAD + n_chunk - 1
    ovt = jnp.asarray(_sel_overlap_rows(rc, n_chunk - 1, T_PAD).T)
    hp = HEADS_PER_KV
    gw = hp * HEAD_DIM
    k_spec = pl.BlockSpec((None, None, t, FEAT), lambda i, g, qi: (i, g, 0, 0))
    v_spec = pl.BlockSpec((None, None, nq, V_ROWS, TQ), lambda i, g, qi: (i, g, 0, 0, 0))
    return pl.pallas_call(
        _prompt_attn_t_kernel, grid=(b, N_KV_HEADS, nq), name="prompt_attention",
        in_specs=[pl.BlockSpec((gw, TQ), lambda i, g, qi: (g, i * nq + qi)),
                  pl.BlockSpec((4 * hp, TQ), lambda i, g, qi: (g, i * nq + qi)),
                  pl.BlockSpec((None, None, rc, FEAT), lambda i, g, qi: (i, g, 0, 0)),
                  pl.BlockSpec((None, None, HEAD_DIM, rc), lambda i, g, qi: (i, g, 0, 0)),
                  pl.BlockSpec((nf, rc), lambda i, g, qi: (0, 0)),
                  pl.BlockSpec((hp, T_NEAR, TQ), lambda i, g, qi: (g, 0, 0)),
                  pl.BlockSpec((2, hp, TQ, TQ), lambda i, g, qi: (0, g, 0, 0)),
                  k_spec, v_spec, k_spec, v_spec],
        out_specs=pl.BlockSpec((TQ, gw), lambda i, g, qi: (i * nq + qi, g)),
        out_shape=jax.ShapeDtypeStruct((b * t, ATTN_WIDTH), BF16),
        scratch_shapes=[pltpu.VMEM((hp, FEAT, TQ), BF16), pltpu.VMEM((hp, rc, TQ), F32),
                        pltpu.VMEM((hp, 1, TQ), F32), pltpu.VMEM((hp, V_ROWS, TQ), F32), pltpu.VMEM((gw, TQ), F32)],
        compiler_params=_params("parallel", "parallel", "arbitrary"))(
            qt, gate_t, kc, vct, ovt, tc, tz, ks, vst, kw, vwt)


def _gmlp_kernel(u_ref, v_ref, lng_ref, lnb_ref, ws_ref, bs_ref, o_ref):
    v = v_ref[...].astype(F32)
    mu = jnp.mean(v, axis=-1, keepdims=True)
    var = jnp.mean(jnp.square(v - mu), axis=-1, keepdims=True)
    vn = ((v - mu) * lax.rsqrt(var + LN_EPS) * lng_ref[...] + lnb_ref[...]).astype(BF16)
    tri = (lax.broadcasted_iota(jnp.int32, (CHUNK, CHUNK), 0) >= lax.broadcasted_iota(jnp.int32, (CHUNK, CHUNK), 1))
    for g in range(GM_GROUPS):
        w = jnp.where(tri, ws_ref[g], 0.0).astype(BF16)
        cols = slice(g * GM_GROUP_DIM, (g + 1) * GM_GROUP_DIM)
        for c in range(u_ref.shape[0] // CHUNK):
            rows = slice(c * CHUNK, (c + 1) * CHUNK)
            s = _dot(w, vn[rows, cols]) + bs_ref[:, g:g + 1]
            o_ref[rows, cols] = (u_ref[rows, cols].astype(F32) * s).astype(o_ref.dtype)


def gmlp_chunks(uv, ln_g, ln_b, w_s, b_s, rows):
    m = uv.shape[0]
    return pl.pallas_call(
        _gmlp_kernel, grid=(m // rows,), name="gmlp",
        in_specs=[pl.BlockSpec((rows, GM_WIDTH), lambda i: (i, 0)), pl.BlockSpec((rows, GM_WIDTH), lambda i: (i, 1)),
                  pl.BlockSpec((1, GM_WIDTH), lambda i: (0, 0)), pl.BlockSpec((1, GM_WIDTH), lambda i: (0, 0)),
                  pl.BlockSpec((GM_GROUPS, CHUNK, CHUNK), lambda i: (0, 0, 0)),
                  pl.BlockSpec((CHUNK, GM_GROUPS), lambda i: (0, 0))],
        out_specs=pl.BlockSpec((rows, GM_WIDTH), lambda i: (i, 0)),
        out_shape=jax.ShapeDtypeStruct((m, GM_WIDTH), BF16),
        compiler_params=_params("parallel"))(uv, uv, ln_g.reshape(1, -1), ln_b.reshape(1, -1), w_s, b_s.T)


def _merge_kernel(attn_ref, gm_ref, mg_ref, x_ref, wa_ref, wb_ref, wo_ref, gn_ref, wr_ref, br_ref,
                  h_ref, hn_ref, idx_ref, wt_ref):
    tm = x_ref.shape[0]
    sub = min(MERGE_SUB, tm)
    parts = [slice(r0, r0 + sub) for r0 in range(0, tm, sub)]
    ta = [_dot(attn_ref[p, :], wa_ref[...]) for p in parts]
    tb = [_dot(gm_ref[p, :], wb_ref[...]) for p in parts]
    merged = [(mg_ref[p, 0:D_MODEL].astype(F32) * ta[i] + mg_ref[p, D_MODEL:2 * D_MODEL].astype(F32) * tb[i])
              .astype(BF16) for i, p in enumerate(parts)]
    h = [x_ref[p, :] + _dot(merged[i], wo_ref[...]) for i, p in enumerate(parts)]
    hn = [hh * lax.rsqrt(jnp.mean(hh * hh, axis=-1, keepdims=True) + RMS_EPS) * gn_ref[...] for hh in h]
    hi = [v.astype(BF16) for v in hn]
    lo = [(v - hi[i].astype(F32)).astype(BF16) for i, v in enumerate(hn)]
    logits = [_dot(hi[i], wr_ref[0]) + _dot(lo[i], wr_ref[0]) + _dot(hi[i], wr_ref[1]) + br_ref[...]
              for i in range(len(parts))]
    routed = [_route(lg) for lg in logits]
    for i, p in enumerate(parts):
        h_ref[p, :] = h[i]
        for c in range(hn_ref.shape[0]):
            hn_ref[c, p, :] = hi[i]
        idx_ref[p, :] = routed[i][0]
        wt_ref[p, :] = routed[i][1]


def _route(logits):
    lane = lax.broadcasted_iota(jnp.int32, logits.shape, 1)
    vals, idxs = [], []
    for _ in range(TOP_K):
        best = jnp.max(logits, axis=-1, keepdims=True)
        ix = jnp.min(jnp.where(logits == best, lane, LANE), axis=-1, keepdims=True)
        vals.append(best)
        idxs.append(ix)
        logits = jnp.where(lane == ix, -jnp.inf, logits)
    ex = [jnp.exp(v - vals[0]) for v in vals]
    inv = 1.0 / functools.reduce(lambda a, b: a + b, ex)
    idx_out = jnp.zeros(lane.shape, jnp.int32)
    wt_out = jnp.zeros(lane.shape, F32)
    for k in range(TOP_K):
        idx_out = jnp.where(lane == k, idxs[k], idx_out)
        wt_out = jnp.where(lane == k, ex[k] * inv, wt_out)
    return idx_out, wt_out


MERGE_SUB = 128


def merge_and_route(attn, gm, mg, x, wa, wb, wo, g_ffn, w_router, b_router, tm, copies):
    m = x.shape[0]
    tm = min(tm, m)
    wr = jnp.zeros((D_MODEL, LANE), F32).at[:, :N_EXPERTS].set(w_router)
    wr_hi = wr.astype(BF16)
    wr = jnp.stack([wr_hi, (wr - wr_hi.astype(F32)).astype(BF16)])
    br = jnp.full((1, LANE), -jnp.inf, F32).at[0, :N_EXPERTS].set(b_router)
    row = lambda w: pl.BlockSpec((tm, w), lambda i: (i, 0))
    full = lambda a: pl.BlockSpec(a.shape, lambda i: (0,) * a.ndim, pipeline_mode=pl.Buffered(1))
    return pl.pallas_call(
        _merge_kernel, grid=(m // tm,), name="merge_route",
        in_specs=[row(ATTN_WIDTH), row(GM_WIDTH), row(2 * D_MODEL), row(D_MODEL), full(wa), full(wb), full(wo),
                  pl.BlockSpec((1, D_MODEL), lambda i: (0, 0)), full(wr), full(br)],
        out_specs=[row(D_MODEL), pl.BlockSpec((copies, tm, D_MODEL), lambda i: (0, i, 0)), row(LANE), row(LANE)],
        out_shape=[jax.ShapeDtypeStruct((m, D_MODEL), F32), jax.ShapeDtypeStruct((copies, m, D_MODEL), BF16),
                   jax.ShapeDtypeStruct((m, LANE), jnp.int32), jax.ShapeDtypeStruct((m, LANE), F32)],
        compiler_params=_params("parallel"))(attn, gm, mg, x, wa, wb, wo, g_ffn.reshape(1, -1), wr, br)


def _expert_tile(te_ref, first_ref, nact_ref, nxt_ref, valid_ref, w_hbm, stage_ref, cache_ref, sem, o_ref, compute):
    j, r = pl.program_id(0), pl.program_id(1)
    tn = stage_ref.shape[-1]
    rows = stage_ref.shape[1]

    def copies(e, jj):
        cols = pl.ds(pl.multiple_of(jj * tn, tn), tn)
        band = rows // MOE_DMA_SPLIT
        return [pltpu.make_async_copy(w.at[e, c * band:(c + 1) * band, cols],
                                      stage_ref.at[i, c * band:(c + 1) * band, :], sem.at[i, c])
                for i, w in enumerate(w_hbm) for c in range(MOE_DMA_SPLIT)]

    @pl.when((j == 0) & (r == 0))
    def _():
        for c in copies(te_ref[0], 0):
            c.start()

    @pl.when(first_ref[r] == 1)
    def _():
        for c in copies(te_ref[r], j):
            c.wait()
        for i in range(len(w_hbm)):
            for r0 in range(0, rows, 256):
                cache_ref[i, r0:r0 + 256, :] = stage_ref[i, r0:r0 + 256, :].astype(cache_ref.dtype)
        wraps = nxt_ref[r] < 0
        e_next = jnp.where(wraps, te_ref[0], nxt_ref[r])
        j_next = jnp.where(wraps, j + 1, j)

        @pl.when(j_next < pl.num_programs(0))
        def _():
            for c in copies(e_next, j_next):
                c.start()

    tm = o_ref.shape[0]
    half = tm // 2

    @pl.when((r < nact_ref[0]) & (valid_ref[r] > half))
    def _():
        compute(tm)

    @pl.when((r < nact_ref[0]) & (valid_ref[r] <= half))
    def _():
        compute(half)
        o_ref[half:tm, :] = jnp.zeros((tm - half, o_ref.shape[1]), o_ref.dtype)

    @pl.when(r >= nact_ref[0])
    def _():
        o_ref[...] = jnp.zeros(o_ref.shape, o_ref.dtype)


def _moe_up_kernel(te_ref, first_ref, nact_ref, nxt_ref, valid_ref, x_ref, wg_hbm, wu_hbm, bg_ref, bu_ref, o_ref,
                   stage_ref, cache_ref, sem):
    def compute(m):
        x = x_ref[0:m, :]
        g = jnp.minimum(_dot(x, cache_ref[0]) + bg_ref[...], SWIGLU_LIMIT)
        u = jnp.clip(_dot(x, cache_ref[1]) + bu_ref[...], -SWIGLU_LIMIT, SWIGLU_LIMIT)
        o_ref[0:m, :] = ((u + 1.0) * (g * jax.nn.sigmoid(SWIGLU_ALPHA * g))).astype(o_ref.dtype)

    _expert_tile(te_ref, first_ref, nact_ref, nxt_ref, valid_ref, (wg_hbm, wu_hbm), stage_ref, cache_ref, sem, o_ref,
                 compute)


def _moe_down_kernel(te_ref, first_ref, nact_ref, nxt_ref, valid_ref, a_ref, wd_hbm, bd_ref, o_ref,
                     stage_ref, cache_ref, sem):
    def compute(m):
        o_ref[0:m, :] = (_dot(a_ref[0:m, :], cache_ref[0]) + bd_ref[...]).astype(o_ref.dtype)

    _expert_tile(te_ref, first_ref, nact_ref, nxt_ref, valid_ref, (wd_hbm,), stage_ref, cache_ref, sem, o_ref, compute)


def moe_experts(x_rows, tiles, w_gate, b_gate, w_up, b_up, w_down, b_down):
    rows = x_rows.shape[0]
    n_tiles = rows // MOE_TM
    row_map = lambda j, r, te, first, nact, nxt, valid: (jnp.minimum(r, nact[0] - 1), 0)
    out_map = lambda j, r, te, first, nact, nxt, valid: (r, j)
    b_spec = lambda tn: pl.BlockSpec((None, 1, tn), lambda j, r, te, first, nact, nxt, valid: (te[r], 0, j))
    hbm = pl.BlockSpec(memory_space=pl.ANY)

    def scratch(n_w, k, tn):
        return [pltpu.VMEM((n_w, k, tn), F32), pltpu.VMEM((n_w, k, tn), BF16),
                pltpu.SemaphoreType.DMA((n_w, MOE_DMA_SPLIT))]

    act = pl.pallas_call(
        _moe_up_kernel, name="moe_up",
        grid_spec=pltpu.PrefetchScalarGridSpec(
            num_scalar_prefetch=5, grid=(D_FF // MOE_TN, n_tiles),
            in_specs=[pl.BlockSpec((MOE_TM, D_MODEL), row_map), hbm, hbm, b_spec(MOE_TN), b_spec(MOE_TN)],
            out_specs=pl.BlockSpec((MOE_TM, MOE_TN), out_map),
            scratch_shapes=scratch(2, D_MODEL, MOE_TN)),
        out_shape=jax.ShapeDtypeStruct((rows, D_FF), BF16),
        compiler_params=_params("arbitrary", "arbitrary"))(
            *tiles, x_rows, w_gate, w_up,
            b_gate.reshape(N_EXPERTS, 1, D_FF), b_up.reshape(N_EXPERTS, 1, D_FF))
    return pl.pallas_call(
        _moe_down_kernel, name="moe_down",
        grid_spec=pltpu.PrefetchScalarGridSpec(
            num_scalar_prefetch=5, grid=(D_MODEL // MOE_TN_DOWN, n_tiles),
            in_specs=[pl.BlockSpec((MOE_TM, D_FF), row_map), hbm, b_spec(MOE_TN_DOWN)],
            out_specs=pl.BlockSpec((MOE_TM, MOE_TN_DOWN), out_map),
            scratch_shapes=scratch(1, D_FF, MOE_TN_DOWN)),
        out_shape=jax.ShapeDtypeStruct((rows, D_MODEL), BF16),
        compiler_params=_params("arbitrary", "arbitrary"))(
            *tiles, act, w_down, b_down.reshape(N_EXPERTS, 1, D_MODEL))


def route_rows(top_i, n_rows, n_stacked):
    n = top_i.shape[0]
    hit = top_i[:, :, None] == jnp.arange(N_EXPERTS, dtype=jnp.int32)[None, None, :]
    onehot = hit.astype(jnp.int32).sum(axis=1)
    before = jnp.cumsum(onehot, axis=0) - onehot
    counts = onehot.sum(axis=0)
    padded = (counts + MOE_TM - 1) // MOE_TM * MOE_TM
    ends = jnp.cumsum(padded)
    starts = ends - padded
    pos = jnp.sum(jnp.where(hit, (starts[None, :] + before)[:, None, :], 0), axis=-1)
    token = jnp.minimum(jnp.arange(n, dtype=jnp.int32), n_stacked - 1)
    pair_id = jnp.arange(TOP_K, dtype=jnp.int32)[None, :] * n_stacked + token[:, None]
    row_pair = jnp.arange(n_rows, dtype=jnp.int32).at[pos.reshape(-1)].set(pair_id.reshape(-1), unique_indices=True)
    n_tiles = n_rows // MOE_TM
    n_active = (ends[-1] // MOE_TM).astype(jnp.int32)
    tile_row = jnp.minimum(jnp.arange(n_tiles, dtype=jnp.int32), n_active - 1) * MOE_TM
    tile_expert = jnp.minimum((tile_row[:, None] >= ends[None, :]).astype(jnp.int32).sum(axis=-1), N_EXPERTS - 1)
    tile_first = jnp.concatenate([jnp.ones((1,), jnp.int32), (tile_expert[1:] != tile_expert[:-1]).astype(jnp.int32)])
    tile_id = jnp.arange(n_tiles, dtype=jnp.int32)
    start_id = jnp.where(tile_first == 1, tile_id, n_tiles)
    next_start = jnp.concatenate([lax.cummin(start_id[::-1])[::-1][1:], jnp.full((1,), n_tiles, jnp.int32)])
    tile_next = jnp.where(next_start < n_tiles, tile_expert[jnp.minimum(next_start, n_tiles - 1)], -1)
    tile_valid = jnp.clip((starts + counts)[tile_expert] - tile_id * MOE_TM, 0, MOE_TM)
    tiles = (tile_expert, tile_first, n_active.reshape(1), tile_next.astype(jnp.int32), tile_valid.astype(jnp.int32))
    return pos, row_pair, tiles


def _combine_kernel(h_ref, y_ref, w_ref, g_ref, o_ref):
    h = h_ref[...]
    w = w_ref[...]
    for k in range(TOP_K):
        h = h + w[:, k:k + 1] * y_ref[k].astype(F32)
    o_ref[...] = h * lax.rsqrt(jnp.mean(h * h, axis=-1, keepdims=True) + RMS_EPS) * g_ref[...]


def combine_and_norm(h, y4, wts, g, tm):
    m = h.shape[0]
    tm = min(tm, m)
    return pl.pallas_call(
        _combine_kernel, grid=(m // tm,), name="combine_norm",
        in_specs=[pl.BlockSpec((tm, D_MODEL), lambda i: (i, 0)), pl.BlockSpec((TOP_K, tm, D_MODEL), lambda i: (0, i, 0)),
                  pl.BlockSpec((tm, LANE), lambda i: (i, 0)), pl.BlockSpec((1, D_MODEL), lambda i: (0, 0))],
        out_specs=pl.BlockSpec((tm, D_MODEL), lambda i: (i, 0)),
        out_shape=jax.ShapeDtypeStruct((m, D_MODEL), F32), compiler_params=_params("parallel"))(
            h, y4, wts, g.reshape(1, -1))


def _in_weights(w_in):
    c0 = ATTN_WIDTH
    c1 = c0 + 3 * KV_COLS
    c2 = c1 + GATE_COLS
    c3 = c2 + 2 * GM_WIDTH
    wg = w_in[:, c1:c2].reshape(D_MODEL, N_KV_HEADS, HEADS_PER_KV, 3).transpose(0, 1, 3, 2)
    wg = jnp.pad(wg.reshape(D_MODEL, N_KV_HEADS, 3 * HEADS_PER_KV), ((0, 0), (0, 0), (0, LANE - 3 * HEADS_PER_KV)))
    cast = lambda w: w.astype(BF16)
    return (cast(w_in[:, :c0]), cast(w_in[:, c0:c1]), cast(wg.reshape(D_MODEL, N_KV_HEADS * LANE)),
            cast(w_in[:, c2:c3]), cast(w_in[:, c3:]))


def _project(a, weights, wide_dtype, q_scale):
    w_q, w_kv, w_ng, w_uv, w_mg = weights
    q = matmul(a, w_q, lambda z: z * np.float32(q_scale), wide_dtype, 1024, 1024, "proj_q")
    kv = matmul(a, w_kv, lambda z: z, F32, 1024, 3 * KV_COLS, "proj_kv")
    ng = matmul(a, w_ng, jax.nn.sigmoid, F32, 1024, N_KV_HEADS * LANE, "proj_gate")
    uv = matmul(a, w_uv, _gelu, wide_dtype, 1024, 1024, "proj_uv")
    mg = matmul(a, w_mg, jax.nn.sigmoid, BF16, 1024, 1024, "proj_merge")
    return q, kv, ng, uv, mg


def _head_rows(fn):
    out = fn(0)
    out = jnp.broadcast_to(out, (HEADS_PER_KV, out.shape[1]))
    row = lax.broadcasted_iota(jnp.int32, out.shape, 0)
    for h in range(1, HEADS_PER_KV):
        out = jnp.where(row == h, fn(h), out)
    return out


def _sample_cmp_win_kernel(rb_ref, q_ref, kvc_ref, ov_ref, sw_ref, new_ref, gate_ref, o_ref, idx_ref, *, past, n_cmp):
    n_rows = kvc_ref.shape[2]
    n_blk = ov_ref.shape[1]
    win = sw_ref.shape[-1]
    win_col = 2 * KV_COLS
    lane_blk = lax.broadcasted_iota(jnp.int32, (1, n_blk), 1)
    lane_out = lax.broadcasted_iota(jnp.int32, (1, LANE), 1)
    for g in range(N_KV_HEADS):
        heads = slice(g * HEADS_PER_KV, (g + 1) * HEADS_PER_KV)
        q = q_ref[heads, :]
        qb = q.astype(BF16)

        n = lax.broadcasted_iota(jnp.int32, (1, n_rows), 1)
        d = past - (n * CMP_STRIDE + CMP_BLOCK - 1)
        logit = _dot_nt(qb, kvc_ref[0, g].astype(BF16)) + _head_rows(
            lambda h: _rel_bias(d, rb_ref, g * HEADS_PER_KV + h))
        logit = jnp.where(n < n_cmp, logit, NEG_INF)
        p = jnp.exp(logit - jnp.max(logit, axis=-1, keepdims=True))
        p = p * (1.0 / jnp.sum(p, axis=-1, keepdims=True))
        o_cmp = _dot(p.astype(BF16), kvc_ref[1, g].astype(BF16))
        p_heads = jnp.broadcast_to(jnp.sum(p, axis=0, keepdims=True), (SUBLANE, n_rows))
        p_slc = _dot(p_heads, ov_ref[...], precision=lax.Precision.HIGHEST)[0:1, :]

        score = jnp.where((lane_blk == 0) | (lane_blk == n_blk - 1), jnp.inf, p_slc)
        picked = jnp.zeros((1, LANE), jnp.int32)
        for k in range(SEL_TOPK - 1):
            best = jnp.max(score, axis=-1, keepdims=True)
            ix = jnp.min(jnp.where(score == best, lane_blk, n_blk), axis=-1, keepdims=True)
            picked = jnp.where(lane_out == k, ix, picked)
            score = jnp.where(lane_blk == ix, -jnp.inf, score)
        idx_ref[g:g + 1, :] = picked

        dw = win - lax.broadcasted_iota(jnp.int32, (1, win), 1)
        lw = _dot(qb, sw_ref[0, g].astype(BF16)) + _head_rows(
            lambda h: _rel_bias(dw, rb_ref, g * HEADS_PER_KV + h))
        k_new = new_ref[:, win_col + g * HEAD_DIM:win_col + (g + 1) * HEAD_DIM]
        v_new = new_ref[:, win_col + KV_COLS // 2 + g * HEAD_DIM:win_col + KV_COLS // 2 + (g + 1) * HEAD_DIM]
        l_new = jnp.sum(q * k_new, axis=-1, keepdims=True) + _head_rows(
            lambda h: jnp.full((1, 1), rb_ref[0, g * HEADS_PER_KV + h], F32))
        m = jnp.maximum(jnp.max(lw, axis=-1, keepdims=True), l_new)
        pw = jnp.exp(lw - m)
        p_new = jnp.exp(l_new - m)
        o_win = ((_dot_nt(pw.astype(BF16), sw_ref[1, g].astype(BF16)) + p_new * v_new)
                 * (1.0 / (jnp.sum(pw, axis=-1, keepdims=True) + p_new)))
        o_ref[heads, :] = gate_ref[0, heads, :] * o_cmp + gate_ref[2, heads, :] * o_win


def sample_cmp_win(rel_bias, q, kvc, state_win, kv_new, gate, past):
    b, _, _, n_rows, _ = kvc.shape
    n_cmp = past // CMP_STRIDE - CMP_BLOCK // CMP_STRIDE + 1
    n_blk = past // SEL_BLOCK
    assert past % SEL_BLOCK == 0 and n_blk >= SEL_TOPK and n_cmp <= n_rows
    cs = np.arange(n_rows)[:, None] * CMP_STRIDE
    ss = np.arange(n_blk)[None, :] * SEL_BLOCK
    ov = np.maximum(np.minimum(cs + CMP_BLOCK, ss + SEL_BLOCK) - np.maximum(cs, ss), 0) / CMP_STRIDE
    ov = jnp.asarray((ov * (np.arange(n_rows) < n_cmp)[:, None]).astype(np.float32))
    win = state_win.shape[-1]
    return pl.pallas_call(
        functools.partial(_sample_cmp_win_kernel, past=past, n_cmp=n_cmp), grid=(b,), name="sample_cmp_win",
        in_specs=[pl.BlockSpec(memory_space=pltpu.SMEM),
                  pl.BlockSpec((None, N_HEADS, HEAD_DIM), lambda i: (i, 0, 0)),
                  pl.BlockSpec((None, 2, N_KV_HEADS, n_rows, HEAD_DIM), lambda i: (i, 0, 0, 0, 0)),
                  pl.BlockSpec(ov.shape, lambda i: (0, 0)),
                  pl.BlockSpec((None, 2, N_KV_HEADS, HEAD_DIM, win), lambda i: (i, 0, 0, 0, 0)),
                  pl.BlockSpec((None, 1, 3 * KV_COLS), lambda i: (i, 0, 0)),
                  pl.BlockSpec((None, 3, N_HEADS, 1), lambda i: (i, 0, 0, 0))],
        out_specs=[pl.BlockSpec((None, N_HEADS, HEAD_DIM), lambda i: (i, 0, 0)),
                   pl.BlockSpec((None, N_KV_HEADS, LANE), lambda i: (i, 0, 0))],
        out_shape=[jax.ShapeDtypeStruct((b, N_HEADS, HEAD_DIM), F32),
                   jax.ShapeDtypeStruct((b, N_KV_HEADS, LANE), jnp.int32)],
        compiler_params=_params("parallel"))(rel_bias, q, kvc, ov, state_win, kv_new, gate)


def _sample_slc_kernel(pt_ref, idx_ref, rb_ref, q_ref, new_ref, gate_ref, part_ref, *rest, past):
    blocks, o_ref = rest[:-1], rest[-1]
    b, g = pl.program_id(0), pl.program_id(1)
    q = q_ref[...]
    qb = q.astype(BF16)
    slc_col = KV_COLS
    half = KV_COLS // 2
    page = blocks[0].shape[-1]
    pos = lax.broadcasted_iota(jnp.int32, (1, page), 1)
    logits, values = [], []
    for k, blk_ref in enumerate(blocks):
        ix = idx_ref[(b * N_KV_HEADS + g) * SEL_TOPK + k]
        d = past - ((ix // SLC_PER_PAGE) * page + pos)
        lg = _dot(qb, blk_ref[0].astype(BF16)) + _head_rows(lambda h: _rel_bias(d, rb_ref, g * HEADS_PER_KV + h))
        logits.append(jnp.where(pos // SEL_BLOCK == ix % SLC_PER_PAGE, lg, NEG_INF))
        values.append(blk_ref[1].astype(BF16))
    new = new_ref[...]
    k_new = jnp.where(g == 0, new[:, slc_col:slc_col + HEAD_DIM], new[:, slc_col + HEAD_DIM:slc_col + 2 * HEAD_DIM])
    v_new = jnp.where(g == 0, new[:, slc_col + half:slc_col + half + HEAD_DIM],
                      new[:, slc_col + half + HEAD_DIM:slc_col + half + 2 * HEAD_DIM])
    l_new = jnp.sum(q * k_new, axis=-1, keepdims=True) + _head_rows(
        lambda h: jnp.full((1, 1), rb_ref[0, g * HEADS_PER_KV + h], F32))
    m = l_new
    for lg in logits:
        m = jnp.maximum(m, jnp.max(lg, axis=-1, keepdims=True))
    p_new = jnp.exp(l_new - m)
    denom = p_new
    out = p_new * v_new
    for lg, vv in zip(logits, values):
        p = jnp.exp(lg - m)
        denom = denom + jnp.sum(p, axis=-1, keepdims=True)
        out = out + _dot_nt(p.astype(BF16), vv)
    o_ref[...] = part_ref[...] + gate_ref[1] * (out * (1.0 / denom))


def sample_slc(page_table, picked, rel_bias, q, kv_new, gate, part, cache_slc, past):
    b = q.shape[0]
    page = cache_slc.shape[-1]
    n_pick = SEL_TOPK - 1

    def blk_map(k):
        def index(i, g, pt, idx, k=k):
            ix = idx[(i * N_KV_HEADS + g) * SEL_TOPK + k]
            return (pt[i, ix // SLC_PER_PAGE], 0, g, 0, 0)
        return index

    head_spec = pl.BlockSpec((None, HEADS_PER_KV, HEAD_DIM), lambda i, g, pt, idx: (i, g, 0))
    return pl.pallas_call(
        functools.partial(_sample_slc_kernel, past=past), name="sample_slc",
        grid_spec=pltpu.PrefetchScalarGridSpec(
            num_scalar_prefetch=2, grid=(b, N_KV_HEADS),
            in_specs=[pl.BlockSpec(memory_space=pltpu.SMEM), head_spec,
                      pl.BlockSpec((None, 1, 3 * KV_COLS), lambda i, g, pt, idx: (i, 0, 0)),
                      pl.BlockSpec((None, 3, HEADS_PER_KV, 1), lambda i, g, pt, idx: (i, 0, g, 0)), head_spec]
            + [pl.BlockSpec((None, 2, None, HEAD_DIM, page), blk_map(k)) for k in range(n_pick)],
            out_specs=head_spec),
        out_shape=jax.ShapeDtypeStruct((b, N_HEADS, HEAD_DIM), F32),
        compiler_params=_params("parallel", "parallel"))(
            page_table, picked, rel_bias, q, kv_new, gate, part, *([cache_slc] * n_pick))


def _gmlp_row_kernel(uv_ref, lng_ref, lnb_ref, w0_ref, b0_ref, vn_ref, o_ref):
    v = uv_ref[:, GM_WIDTH:2 * GM_WIDTH]
    mu = jnp.mean(v, axis=-1, keepdims=True)
    var = jnp.mean(jnp.square(v - mu), axis=-1, keepdims=True)
    vn = (v - mu) * lax.rsqrt(var + LN_EPS) * lng_ref[...] + lnb_ref[...]
    vn_ref[...] = vn
    o_ref[...] = (uv_ref[:, 0:GM_WIDTH] * (w0_ref[...] * vn + b0_ref[...])).astype(o_ref.dtype)


def gmlp_first_row(uv, ln_g, ln_b, w_s, b_s):
    m = uv.shape[0]
    w0 = jnp.repeat(w_s[:, 0, 0], GM_GROUP_DIM).reshape(1, GM_WIDTH)
    b0 = jnp.repeat(b_s[:, 0], GM_GROUP_DIM).reshape(1, GM_WIDTH)
    return pl.pallas_call(
        _gmlp_row_kernel, name="gmlp_row",
        out_shape=[jax.ShapeDtypeStruct((m, GM_WIDTH), F32), jax.ShapeDtypeStruct((m, GM_WIDTH), BF16)])(
            uv, ln_g.reshape(1, -1), ln_b.reshape(1, -1), w0, b0)


def _group_major(x, feat):
    b, t = x.shape[0], x.shape[1]
    f = jnp.broadcast_to(feat.astype(BF16)[None, None], (b, N_KV_HEADS, t, FEAT - HEAD_DIM))
    return jnp.concatenate([x.transpose(0, 2, 1, 3).astype(BF16), f], axis=-1)


def _value_tiles(x):
    b, t = x.shape[0], x.shape[1]
    xt = x.transpose(0, 2, 3, 1).astype(BF16)
    extra = jnp.zeros((b, N_KV_HEADS, V_ROWS - HEAD_DIM, t), BF16).at[:, :, 0].set(1)
    tiles = jnp.concatenate([xt, extra], axis=2).reshape(b, N_KV_HEADS, V_ROWS, t // TQ, TQ)
    return tiles.transpose(0, 1, 3, 2, 4)


def _prompt_mixer_a(q, kv, ng, rel_bias, pe, w1, w2, b, t):
    kvr = kv.reshape(b, t, 3, 2, N_KV_HEADS, HEAD_DIM)
    n_chunk = t // CMP_STRIDE
    chunks = kvr[:, :, 0].reshape(b, n_chunk, CMP_STRIDE, 2, N_KV_HEADS, HEAD_DIM).transpose(0, 3, 4, 1, 2, 5)
    kvc = compress_chunks(chunks.reshape(b, 2, N_KV_HEADS, n_chunk, CMP_STRIDE * HEAD_DIM).astype(BF16), w1, w2, pe)
    rc = -(-(T_PAD + max(n_chunk - 1, T_PAD * t // TQ)) // LANE) * LANE
    kvc = jnp.pad(kvc[:, :, :, :n_chunk - 1], ((0, 0), (0, 0), (0, 0), (T_PAD, rc - T_PAD - n_chunk + 1), (0, 0)))
    kc = jnp.pad(kvc[:, 0], ((0, 0), (0, 0), (0, 0), (0, FEAT - HEAD_DIM)))
    nf = FEAT - HEAD_DIM
    blocks = (jnp.arange(t)[:, None] // SEL_BLOCK == jnp.arange(nf)[None, :])
    tz, tc = bias_tables_t(rel_bias)
    gate_t = ng.reshape(b * t, N_KV_HEADS, LANE)[:, :, :4 * HEADS_PER_KV].reshape(b * t, -1).T
    return prompt_attention_t(
        q.T, gate_t, kc, kvc[:, 1].transpose(0, 1, 3, 2), tc, tz,
        _group_major(kvr[:, :, 1, 0], blocks), _value_tiles(kvr[:, :, 1, 1]),
        _group_major(kvr[:, :, 2, 0], jnp.zeros((t, nf), BF16)), _value_tiles(kvr[:, :, 2, 1]), b, t)


def _sample_mixer_a(q, kv, ng, rel_bias, pe, w1, w2, cache_cmp, cache_slc, state_win, page_table):
    b, n_pages = page_table.shape
    page = cache_cmp.shape[1]
    past = n_pages * page
    n_chunk = past // CMP_STRIDE
    rows = cache_cmp[page_table].reshape(b, n_chunk, CMP_STRIDE, 2, N_KV_HEADS, HEAD_DIM).transpose(0, 3, 4, 1, 2, 5)
    kvc = compress_chunks(rows.reshape(b, 2, N_KV_HEADS, n_chunk, CMP_STRIDE * HEAD_DIM).astype(BF16), w1, w2, pe)
    gate = ng.reshape(b, N_KV_HEADS, LANE)[:, :, :3 * HEADS_PER_KV].reshape(b, N_KV_HEADS, 3, HEADS_PER_KV)
    gate = gate.transpose(0, 2, 1, 3).reshape(b, 3, N_HEADS, 1)
    q3 = q.reshape(b, N_HEADS, HEAD_DIM)
    kv_new = kv.reshape(b, 1, 3 * KV_COLS)
    part, picked = sample_cmp_win(rel_bias, q3, kvc, state_win.transpose(0, 2, 3, 4, 1), kv_new, gate, past)
    picked = picked[:, :, :SEL_TOPK].reshape(-1)
    out = sample_slc(page_table, picked, rel_bias, q3, kv_new, gate, part, cache_slc.transpose(0, 2, 3, 4, 1), past)
    return out.reshape(b, ATTN_WIDTH).astype(BF16)


def kernel(x_prompt, x_sample, cache_cmp_kv, cache_slc_kv, state_win_kv, page_table, rel_bias, norm_mix, w_in, pe_cmp, w_cmp1, w_cmp2, gm_ln_g, gm_ln_b, gm_w_s, gm_b_s, w_br_attn, w_br_gmlp, w_out, norm_ffn, w_router, b_router, w_gate, b_gate, w_up, b_up, w_down, b_down, norm_final):
    b, t, _ = x_prompt.shape
    bs, ts, _ = x_sample.shape
    depth = norm_mix.shape[0]
    assert depth == 1 and ts == 1 and cache_slc_kv.shape[2] == SLC_PER_PAGE * SEL_BLOCK
    n_p, n_s = b * t, bs * ts
    kv_tail = (2, N_KV_HEADS, HEAD_DIM)
    lyr = 0

    xp = x_prompt.reshape(n_p, D_MODEL)
    xs = x_sample.reshape(n_s, D_MODEL)
    weights = _in_weights(w_in[lyr])
    qp, kvp, ngp, uvp, mgp = _project(rmsnorm_rows(xp, norm_mix[lyr], 512), weights, BF16, HEAD_DIM ** -0.5 * LOG2_E)
    qs, kvs, ngs, uvs, mgs = _project(rmsnorm_rows(xs, norm_mix[lyr], n_s), weights, F32, HEAD_DIM ** -0.5)

    attn_p = _prompt_mixer_a(qp, kvp, ngp, rel_bias, pe_cmp[lyr], w_cmp1[lyr], w_cmp2[lyr], b, t)
    attn_s = _sample_mixer_a(qs, kvs, ngs, rel_bias, pe_cmp[lyr], w_cmp1[lyr], w_cmp2[lyr], cache_cmp_kv[lyr],
                             cache_slc_kv[lyr], state_win_kv[lyr], page_table)

    gm_p = gmlp_chunks(uvp, gm_ln_g[lyr], gm_ln_b[lyr], gm_w_s[lyr], gm_b_s[lyr], 4 * CHUNK)
    vn_s, gm_s = gmlp_first_row(uvs, gm_ln_g[lyr], gm_ln_b[lyr], gm_w_s[lyr], gm_b_s[lyr])

    wa, wb, wo = w_br_attn[lyr].astype(BF16), w_br_gmlp[lyr].astype(BF16), w_out[lyr].astype(BF16)
    n_all = n_p + n_s
    n_rows = (n_all * TOP_K + N_EXPERTS * (MOE_TM - 1) + MOE_TM - 1) // MOE_TM * MOE_TM
    copies = -(-n_rows // n_p)
    router = (norm_ffn[lyr], w_router[lyr], b_router[lyr])
    hp, hnp, idxp, wtp = merge_and_route(attn_p, gm_p, mgp, xp, wa, wb, wo, *router, 256, copies)
    hs, hns, idxs, wts = merge_and_route(attn_s, gm_s, mgs, xs, wa, wb, wo, *router, 256, 1)
    top_i = jnp.concatenate([idxp[:, :TOP_K], idxs[:, :TOP_K]], axis=0)
    pos, row_pair, tiles = route_rows(top_i, n_rows, n_p)
    x_rows = hnp.reshape(copies * n_p, D_MODEL)[row_pair]
    x_rows = x_rows.at[pos[n_p:].reshape(-1)].set(jnp.repeat(hns[0], TOP_K, axis=0), unique_indices=True)
    y_rows = moe_experts(x_rows, tiles, w_gate[lyr], b_gate[lyr], w_up[lyr], b_up[lyr], w_down[lyr], b_down[lyr])
    y_p = combine_and_norm(hp, y_rows[pos[:n_p].T], wtp, norm_final, 256)
    y_s = combine_and_norm(hs, y_rows[pos[n_p:].T], wts, norm_final, 256)

    kvp5 = kvp.reshape(b, t, 3, *kv_tail)
    kvs5 = kvs.reshape(bs, ts, 3, *kv_tail)
    win_buf = state_win_kv.shape[2]
    win_s = jnp.concatenate([state_win_kv[lyr], kvs5[:, :, 2]], axis=1)[:, ts:]
    return (y_p.reshape(b, t, D_MODEL), y_s.reshape(bs, ts, D_MODEL),
            kvp5[:, :, 0][None], kvs5[:, :, 0][None], kvp5[:, :, 1][None], kvs5[:, :, 1][None],
            kvp5[:, t - min(WINDOW, t):, 2][None], win_s[:, -win_buf:][None],
            vn_s.reshape(1, bs, ts, GM_WIDTH))
```
